```python
import math
import jax, jax.numpy as jnp
from jax import lax
import numpy as np

D_MODEL = 1024
BATCH = 8
SEQ = 2048
DEPTH = 2

MEM_LEN = 256
BLOCK = 128
HEAD_DIM = 64
EPS = 1e-6
A_HEADS = 8
A_WIDTH = A_HEADS * HEAD_DIM
B_GROUPS = 4
B_GROUP_DIM = 128
B_WIDTH = B_GROUPS * B_GROUP_DIM
CHUNK = 128
EVEN_IN = 3 * A_WIDTH + A_HEADS + 2 * B_WIDTH
EVEN_SPLITS = [A_WIDTH, 2 * A_WIDTH, 3 * A_WIDTH, 3 * A_WIDTH + A_HEADS, 3 * A_WIDTH + A_HEADS + B_WIDTH]
C_WIDTH = 512
CONV_W = 3
D_HEADS = 4
D_WIDTH = D_HEADS * 2 * HEAD_DIM
ODD_IN = 3 * C_WIDTH + 3 * D_WIDTH
ODD_SPLITS = [C_WIDTH, 2 * C_WIDTH, 3 * C_WIDTH, 3 * C_WIDTH + D_WIDTH, 3 * C_WIDTH + 2 * D_WIDTH]
N_BUCKETS = 32
MAX_DIST = 128
X_HEADS = 4
X_HEAD_DIM = 128
X_WIDTH = X_HEADS * X_HEAD_DIM
FF_DENSE = 2816
N_EXPERTS = 8
TOP_K = 2
FF_EXPERT = 3584
N_EVEN = (DEPTH + 1) // 2
N_ODD = DEPTH // 2

kernel_name = "hybrid_fox_gmlp_shortconv_diffattn_moe"


def rmsnorm(x, g):
    xf = x.astype(jnp.float32)
    y = xf * lax.rsqrt(jnp.mean(xf * xf, axis=-1, keepdims=True) + EPS)
    return (y * g.astype(jnp.float32)).astype(x.dtype)


def t5_bucket(dist):
    n = jnp.maximum(dist, 0)
    max_exact = N_BUCKETS // 2
    nf = jnp.maximum(n, 1).astype(jnp.float32)
    large = max_exact + (jnp.log(nf / max_exact) / math.log(MAX_DIST / max_exact)
                         * (N_BUCKETS - max_exact)).astype(jnp.int32)
    large = jnp.minimum(large, N_BUCKETS - 1)
    return jnp.where(n < max_exact, n, large)


def fox_attention(q, k, v, log_f):
    seq = q.shape[2]
    c = jnp.cumsum(log_f, axis=-1)
    scale = HEAD_DIM ** -0.5
    outs = []
    for blk in range(seq // BLOCK):
        qs, qe = blk * BLOCK, (blk + 1) * BLOCK
        logits = jnp.einsum("bhqd,bhkd->bhqk", q[:, :, qs:qe], k[:, :, :qe]).astype(jnp.float32) * scale
        logits = logits + (c[:, :, qs:qe, None] - c[:, :, None, :qe])
        causal = (qs + jnp.arange(BLOCK))[:, None] >= jnp.arange(qe)[None, :]
        logits = jnp.where(causal, logits, -jnp.inf)
        p = jax.nn.softmax(logits, axis=-1).astype(v.dtype)
        outs.append(jnp.einsum("bhqk,bhkd->bhqd", p, v[:, :, :qe]))
    return jnp.concatenate(outs, axis=2)


def spatial_gating(u, v, sgu_norm, w_s, b_s):
    bsz, seq, _ = u.shape
    u = jax.nn.gelu(u).reshape(bsz, seq, B_GROUPS, B_GROUP_DIM)
    v = rmsnorm(jax.nn.gelu(v).reshape(bsz, seq, B_GROUPS, B_GROUP_DIM), sgu_norm)
    v = v.reshape(bsz, seq // CHUNK, CHUNK, B_GROUPS, B_GROUP_DIM)
    tri = jnp.tril(jnp.ones((CHUNK, CHUNK), dtype=bool))
    w = jnp.where(tri[None], w_s, jnp.zeros_like(w_s))
    mixed = jnp.einsum("gts,bcsgd->bctgd", w, v) + b_s.T[None, None, :, :, None]
    return (u * mixed.reshape(bsz, seq, B_GROUPS, B_GROUP_DIM)).reshape(bsz, seq, B_WIDTH)


def short_conv(b_gate, c_gate, x_in, conv_w):
    xc = c_gate * x_in
    y = lax.conv_general_dilated(xc, conv_w[:, None, :], window_strides=(1,),
                                 padding=[(CONV_W - 1, 0)],
                                 dimension_numbers=("NWC", "WIO", "NWC"),
                                 feature_group_count=C_WIDTH)
    return b_gate * y


def diff_attention(q, k, v, rel_bias, lam, lam_init, subln):
    seq = q.shape[3]
    scale = HEAD_DIM ** -0.5
    outs = []
    for blk in range(seq // BLOCK):
        qs, qe = blk * BLOCK, (blk + 1) * BLOCK
        dist = (qs + jnp.arange(BLOCK))[:, None] - jnp.arange(qe)[None, :]
        bias = rel_bias[t5_bucket(dist)].transpose(2, 0, 1).astype(jnp.float32)
        logits = jnp.einsum("bhiqd,bhikd->bhiqk", q[:, :, :, qs:qe], k[:, :, :, :qe]).astype(jnp.float32) * scale
        logits = jnp.where(dist >= 0, logits + bias[None, :, None], -jnp.inf)
        p = jax.nn.softmax(logits, axis=-1)
        p = p[:, :, 0] - lam * p[:, :, 1]
        outs.append(jnp.einsum("bhqk,bhkd->bhqd", p.astype(v.dtype), v[:, :, :qe]))
    o = jnp.concatenate(outs, axis=2)
    return rmsnorm(o, subln) * (1.0 - lam_init)


def even_mixer(h, w_in, b_f, sgu_norm, w_s, b_s, w_out):
    bsz, seq, _ = h.shape
    qa, ka, va, fa, ub, vb = jnp.split(h @ w_in, EVEN_SPLITS, axis=-1)
    heads = lambda t: t.reshape(bsz, seq, A_HEADS, HEAD_DIM).transpose(0, 2, 1, 3)
    log_f = jax.nn.log_sigmoid(fa.astype(jnp.float32) + b_f.astype(jnp.float32)).transpose(0, 2, 1)
    a = fox_attention(heads(qa), heads(ka), heads(va), log_f)
    a = a.transpose(0, 2, 1, 3).reshape(bsz, seq, A_WIDTH)
    g = spatial_gating(ub, vb, sgu_norm, w_s, b_s)
    return jnp.concatenate([a, g], axis=-1) @ w_out


def odd_mixer(h, w_in, conv_w, lam_q1, lam_k1, lam_q2, lam_k2, subln, w_out, rel_bias, lam_init):
    bsz, seq, _ = h.shape
    bg, cg, xi, qd, kd, vd = jnp.split(h @ w_in, ODD_SPLITS, axis=-1)
    c_out = short_conv(bg, cg, xi, conv_w)
    q = qd.reshape(bsz, seq, D_HEADS, 2, HEAD_DIM).transpose(0, 2, 3, 1, 4)
    k = kd.reshape(bsz, seq, D_HEADS, 2, HEAD_DIM).transpose(0, 2, 3, 1, 4)
    v = vd.reshape(bsz, seq, D_HEADS, 2 * HEAD_DIM).transpose(0, 2, 1, 3)
    lam = (jnp.exp(jnp.sum(lam_q1.astype(jnp.float32) * lam_k1.astype(jnp.float32)))
           - jnp.exp(jnp.sum(lam_q2.astype(jnp.float32) * lam_k2.astype(jnp.float32))) + lam_init)
    d_out = diff_attention(q, k, v, rel_bias, lam, lam_init, subln)
    d_out = d_out.transpose(0, 2, 1, 3).reshape(bsz, seq, D_WIDTH)
    return jnp.concatenate([c_out, d_out], axis=-1) @ w_out


def memory_cross_attention(h, mem_n, wq, wkv, wo):
    bsz, seq, _ = h.shape
    m = mem_n.shape[1]
    q = (h @ wq).reshape(bsz, seq, X_HEADS, X_HEAD_DIM)
    k, v = jnp.split(mem_n @ wkv, 2, axis=-1)
    k = k.reshape(bsz, m, X_HEADS, X_HEAD_DIM)
    v = v.reshape(bsz, m, X_HEADS, X_HEAD_DIM)
    logits = jnp.einsum("bshd,bmhd->bhsm", q, k).astype(jnp.float32) * (X_HEAD_DIM ** -0.5)
    p = jax.nn.softmax(logits, axis=-1).astype(v.dtype)
    o = jnp.einsum("bhsm,bmhd->bshd", p, v).reshape(bsz, seq, X_WIDTH)
    return o @ wo


def swiglu(h, w13, w2):
    gate, up = jnp.split(h @ w13, 2, axis=-1)
    return (jax.nn.silu(gate) * up) @ w2


def moe_swiglu(h, w_router, w13, w2):
    logits = (h @ w_router).astype(jnp.float32)
    top_v, top_i = lax.top_k(logits, TOP_K)
    wts = jax.nn.softmax(top_v, axis=-1)
    gate = jnp.sum(jax.nn.one_hot(top_i, N_EXPERTS, dtype=jnp.float32) * wts[..., None], axis=-2)
    y = jnp.zeros_like(h)
    for e in range(N_EXPERTS):
        y = y + gate[..., e:e + 1].astype(h.dtype) * swiglu(h, w13[e], w2[e])
    return y


def setup_inputs(seed: int = 0) -> dict:
    key = jax.random.key(seed)
    ks = iter(jax.random.split(key, 64))

    def nrm(shape, scale):
        return scale * jax.random.normal(next(ks), shape, jnp.float32)

    def gain(shape):
        return 1.0 + 0.05 * jax.random.normal(next(ks), shape, jnp.float32)

    d = D_MODEL
    return {
        "x": nrm((BATCH, SEQ, d), 1.0),
        "mem": nrm((BATCH, MEM_LEN, d), 1.0),
        "rel_bias": nrm((N_BUCKETS, D_HEADS), 0.5),
        "mem_norm": gain((d,)),
        "final_norm": gain((d,)),
        "ev_norm": gain((N_EVEN, d)),
        "ev_w_in": nrm((N_EVEN, d, EVEN_IN), d ** -0.5),
        "ev_b_f": 2.0 + nrm((N_EVEN, A_HEADS), 0.1),
        "ev_sgu_norm": gain((N_EVEN, B_GROUPS, B_GROUP_DIM)),
        "ev_w_s": nrm((N_EVEN, B_GROUPS, CHUNK, CHUNK), CHUNK ** -0.5),
        "ev_b_s": 1.0 + nrm((N_EVEN, B_GROUPS, CHUNK), 0.1),
        "ev_w_out": nrm((N_EVEN, A_WIDTH + B_WIDTH, d), (A_WIDTH + B_WIDTH) ** -0.5),
        "ffn_w13": nrm((N_EVEN, d, 2 * FF_DENSE), d ** -0.5),
        "ffn_w2": nrm((N_EVEN, FF_DENSE, d), FF_DENSE ** -0.5),
        "od_norm": gain((N_ODD, d)),
        "od_w_in": nrm((N_ODD, d, ODD_IN), d ** -0.5),
        "od_conv_w": nrm((N_ODD, CONV_W, C_WIDTH), CONV_W ** -0.5),
        "od_lam_q1": nrm((N_ODD, HEAD_DIM), 0.1),
        "od_lam_k1": nrm((N_ODD, HEAD_DIM), 0.1),
        "od_lam_q2": nrm((N_ODD, HEAD_DIM), 0.1),
        "od_lam_k2": nrm((N_ODD, HEAD_DIM), 0.1),
        "od_subln": gain((N_ODD, 2 * HEAD_DIM)),
        "od_w_out": nrm((N_ODD, C_WIDTH + D_WIDTH, d), (C_WIDTH + D_WIDTH) ** -0.5),
        "moe_router": nrm((N_ODD, d, N_EXPERTS), d ** -0.5),
        "moe_w13": nrm((N_ODD, N_EXPERTS, d, 2 * FF_EXPERT), d ** -0.5),
        "moe_w2": nrm((N_ODD, N_EXPERTS, FF_EXPERT, d), FF_EXPERT ** -0.5),
        "x_norm": gain((DEPTH, d)),
        "x_wq": nrm((DEPTH, d, X_WIDTH), d ** -0.5),
        "x_wkv": nrm((DEPTH, d, 2 * X_WIDTH), d ** -0.5),
        "x_wo": nrm((DEPTH, X_WIDTH, d), X_WIDTH ** -0.5),
        "ffn_norm": gain((DEPTH, d)),
    }


def reference(x, mem, rel_bias, mem_norm, final_norm,
              ev_norm, ev_w_in, ev_b_f, ev_sgu_norm, ev_w_s, ev_b_s, ev_w_out,
              ffn_w13, ffn_w2,
              od_norm, od_w_in, od_conv_w, od_lam_q1, od_lam_k1, od_lam_q2, od_lam_k2, od_subln, od_w_out,
              moe_router, moe_w13, moe_w2,
              x_norm, x_wq, x_wkv, x_wo, ffn_norm):
    mem_n = rmsnorm(mem, mem_norm)
    for layer in range(DEPTH):
        j = layer // 2
        if layer % 2 == 0:
            h = rmsnorm(x, ev_norm[j])
            x = x + even_mixer(h, ev_w_in[j], ev_b_f[j], ev_sgu_norm[j], ev_w_s[j], ev_b_s[j], ev_w_out[j])
        else:
            lam_init = 0.8 - 0.6 * math.exp(-0.3 * layer)
            h = rmsnorm(x, od_norm[j])
            x = x + odd_mixer(h, od_w_in[j], od_conv_w[j], od_lam_q1[j], od_lam_k1[j], od_lam_q2[j],
                              od_lam_k2[j], od_subln[j], od_w_out[j], rel_bias, lam_init)
        h = rmsnorm(x, x_norm[layer])
        x = x + memory_cross_attention(h, mem_n, x_wq[layer], x_wkv[layer], x_wo[layer])
        h = rmsnorm(x, ffn_norm[layer])
        if layer % 2 == 0:
            x = x + swiglu(h, ffn_w13[j], ffn_w2[j])
        else:
            x = x + moe_swiglu(h, moe_router[j], moe_w13[j], moe_w2[j])
    return rmsnorm(x, final_norm)
```

```python
import functools
import math

import jax
import jax.numpy as jnp
from jax import lax
from jax.experimental import pallas as pl
from jax.experimental.pallas import tpu as pltpu

F32 = jnp.float32
BF16 = jnp.bfloat16
EPS = 1e-6
HEAD_DIM = 64
LANES = 128
N_BUCKETS = 32
MAX_DIST = 128
TOP_K = 2
VMEM_LIMIT = 56 * 1024 * 1024


def _params(*sem):
    return pltpu.CompilerParams(dimension_semantics=sem, vmem_limit_bytes=VMEM_LIMIT)


def _rms(x, g):
    ms = jnp.mean(x * x, axis=-1, keepdims=True)
    return x * lax.rsqrt(ms + EPS) * g


def _dot(a, b):
    return jnp.dot(a, b, preferred_element_type=F32)


def _dot_nt(a, b):
    return lax.dot_general(a, b, (((1,), (1,)), ((), ())), preferred_element_type=F32)


def _norm_matmul_kernel(x_ref, g_ref, *refs, n_w, chunk):
    w_refs, o_refs = refs[:n_w], refs[n_w:]
    h = _rms(x_ref[...], g_ref[...]).astype(BF16)
    for w_ref, o_ref in zip(w_refs, o_refs):
        n = w_ref.shape[1]
        for c0 in range(0, n, chunk):
            c1 = min(c0 + chunk, n)
            o_ref[:, c0:c1] = _dot(h, w_ref[:, c0:c1]).astype(o_ref.dtype)


def norm_matmul(x, g, ws, out_dtypes, *, tm, name):
    t, d = x.shape
    in_specs = [pl.BlockSpec((tm, d), lambda i: (i, 0)), pl.BlockSpec((1, d), lambda i: (0, 0))]
    in_specs += [pl.BlockSpec(w.shape, lambda i: (0, 0)) for w in ws]
    out_specs = [pl.BlockSpec((tm, w.shape[1]), lambda i: (i, 0)) for w in ws]
    out_shape = [jax.ShapeDtypeStruct((t, w.shape[1]), dt) for w, dt in zip(ws, out_dtypes)]
    return pl.pallas_call(
        functools.partial(_norm_matmul_kernel, n_w=len(ws), chunk=512),
        grid=(t // tm,), in_specs=in_specs, out_specs=out_specs, out_shape=out_shape,
        compiler_params=_params("parallel"), name=name,
    )(x, g.reshape(1, d), *ws)


def _gate_kernel(g_ref, b_ref, c_ref):
    s = g_ref.shape[0]
    row = lax.broadcasted_iota(jnp.int32, (LANES, LANES), 0)
    col = lax.broadcasted_iota(jnp.int32, (LANES, LANES), 1)
    tri = (row >= col).astype(F32)
    carry = jnp.zeros((1, LANES), F32)
    for blk in range(s // LANES):
        z = g_ref[blk * LANES:(blk + 1) * LANES, :] + b_ref[...]
        log_f = jnp.minimum(z, 0.0) - jnp.log1p(jnp.exp(-jnp.abs(z)))
        cs = jnp.dot(tri, log_f, precision=lax.Precision.HIGHEST, preferred_element_type=F32) + carry
        c_ref[blk * LANES:(blk + 1) * LANES, :] = cs
        carry = cs[LANES - 1:LANES, :]


def gate_cumsum(g, b, *, seq):
    t = g.shape[0]
    return pl.pallas_call(
        _gate_kernel, grid=(t // seq,),
        in_specs=[pl.BlockSpec((seq, LANES), lambda i: (i, 0)), pl.BlockSpec((1, LANES), lambda i: (0, 0))],
        out_specs=pl.BlockSpec((seq, LANES), lambda i: (i, 0)),
        out_shape=jax.ShapeDtypeStruct((t, LANES), F32),
        compiler_params=_params("parallel"), name="gate_cumsum",
    )(g, b)


def _online_update(state, s, v):
    m, l, acc = state
    m_new = jnp.maximum(m, jnp.max(s, axis=1, keepdims=True))
    p = jnp.exp(s - m_new)
    a = jnp.exp(m - m_new)
    l = a * l + jnp.sum(p, axis=1, keepdims=True)
    acc = a * acc + _dot(p.astype(BF16), v)
    return m_new, l, acc


def _init_state(tq):
    return (jnp.full((tq, 1), -jnp.inf, F32), jnp.zeros((tq, 1), F32), jnp.zeros((tq, LANES), F32))


def _fox_kernel(q_ref, k_ref, v_ref, cq_ref, ck_ref, o_ref, *, tq):
    hp = pl.program_id(1)
    i = pl.program_id(2)
    lane = lax.broadcasted_iota(jnp.int32, (1, LANES), 1)
    row = lax.broadcasted_iota(jnp.int32, (tq, tq), 0)
    col = lax.broadcasted_iota(jnp.int32, (tq, tq), 1)
    causal = row >= col
    q = q_ref[...]
    cq_all = cq_ref[...]
    outs = []
    for hh in range(2):
        in_half = (lane >= HEAD_DIM * hh) & (lane < HEAD_DIM * (hh + 1))
        qm = jnp.where(in_half, q, jnp.zeros_like(q))
        cq = jnp.sum(jnp.where(lane == 2 * hp + hh, cq_all, 0.0), axis=1, keepdims=True)

        def tile(j, state, masked, qm=qm, cq=cq, hh=hh):
            start = pl.multiple_of(j * tq, tq)
            k = k_ref[pl.ds(start, tq), :]
            v = v_ref[pl.ds(start, tq), :]
            ck = ck_ref[0, 0, hh, pl.ds(j, 1), :]
            s = _dot_nt(qm, k) + (cq - ck)
            if masked:
                s = jnp.where(causal, s, -jnp.inf)
            return _online_update(state, s, v)

        state = lax.fori_loop(0, i, lambda j, st: tile(j, st, False), _init_state(tq))
        _, l, acc = tile(i, state, True)
        outs.append(acc / l)
    o_ref[...] = jnp.where(lane < HEAD_DIM, outs[0], outs[1]).astype(o_ref.dtype)


def fox_attention(qkv, c, ck, *, batch, seq, tq, q_col, k_col, v_col, n_pairs):
    t = batch * seq
    nq = seq // tq
    return pl.pallas_call(
        functools.partial(_fox_kernel, tq=tq),
        grid=(batch, n_pairs, nq),
        in_specs=[
            pl.BlockSpec((tq, LANES), lambda b, h, i: (b * nq + i, q_col + h)),
            pl.BlockSpec((seq, LANES), lambda b, h, i: (b, k_col + h)),
            pl.BlockSpec((seq, LANES), lambda b, h, i: (b, v_col + h)),
            pl.BlockSpec((tq, LANES), lambda b, h, i: (b * nq + i, 0)),
            pl.BlockSpec((1, 1, 2, nq, tq), lambda b, h, i: (b, h, 0, 0, 0)),
        ],
        out_specs=pl.BlockSpec((tq, LANES), lambda b, h, i: (b * nq + i, h)),
        out_shape=jax.ShapeDtypeStruct((t, n_pairs * LANES), BF16),
        compiler_params=_params("parallel", "parallel", "arbitrary"), name="fox_attention",
    )(qkv, qkv, qkv, c, ck)


def _sgu_kernel(u_ref, v_ref, norm_ref, ws_ref, bs_ref, o_ref, *, n_groups, chunk):
    tb = u_ref.shape[0]
    row = lax.broadcasted_iota(jnp.int32, (chunk, chunk), 0)
    col = lax.broadcasted_iota(jnp.int32, (chunk, chunk), 1)
    tri = row >= col
    for g in range(n_groups):
        w = jnp.where(tri, ws_ref[g], 0.0).astype(BF16)
        bias = bs_ref[:, g:g + 1]
        gain = norm_ref[g:g + 1, :]
        for c in range(tb // chunk):
            rs = slice(c * chunk, (c + 1) * chunk)
            cs = slice(g * LANES, (g + 1) * LANES)
            vn = _rms(jax.nn.gelu(v_ref[rs, cs].astype(F32)), gain)
            mixed = _dot(w, vn.astype(BF16)) + bias
            o_ref[rs, cs] = (jax.nn.gelu(u_ref[rs, cs].astype(F32)) * mixed).astype(o_ref.dtype)


def spatial_gating(main, sgu_norm, w_s, b_s, *, tb, u_col, v_col):
    t = main.shape[0]
    n_groups, chunk, _ = w_s.shape
    width = n_groups * LANES
    return pl.pallas_call(
        functools.partial(_sgu_kernel, n_groups=n_groups, chunk=chunk),
        grid=(t // tb,),
        in_specs=[
            pl.BlockSpec((tb, width), lambda i: (i, u_col)),
            pl.BlockSpec((tb, width), lambda i: (i, v_col)),
            pl.BlockSpec(sgu_norm.shape, lambda i: (0, 0)),
            pl.BlockSpec(w_s.shape, lambda i: (0, 0, 0)),
            pl.BlockSpec((chunk, n_groups), lambda i: (0, 0)),
        ],
        out_specs=pl.BlockSpec((tb, width), lambda i: (i, 0)),
        out_shape=jax.ShapeDtypeStruct((t, width), BF16),
        compiler_params=_params("parallel"), name="spatial_gating",
    )(main, main, sgu_norm, w_s, b_s.T)


def _proj_res_kernel(x_ref, a_ref, b_ref, wa_ref, wb_ref, o_ref):
    o_ref[...] = x_ref[...] + _dot(a_ref[...], wa_ref[...]) + _dot(b_ref[...], wb_ref[...])


def proj_residual(x, a, b, wa, wb, *, tm, name):
    t, d = x.shape
    return pl.pallas_call(
        _proj_res_kernel, grid=(t // tm,),
        in_specs=[
            pl.BlockSpec((tm, d), lambda i: (i, 0)),
            pl.BlockSpec((tm, a.shape[1]), lambda i: (i, 0)),
            pl.BlockSpec((tm, b.shape[1]), lambda i: (i, 0)),
            pl.BlockSpec(wa.shape, lambda i: (0, 0)),
            pl.BlockSpec(wb.shape, lambda i: (0, 0)),
        ],
        out_specs=pl.BlockSpec((tm, d), lambda i: (i, 0)),
        out_shape=jax.ShapeDtypeStruct((t, d), F32),
        compiler_params=_params("parallel"), name=name,
    )(x, a, b, wa, wb)


def _cross_kernel(x_ref, g_ref, wq_ref, kv_ref, wo_ref, o_ref, *, n_heads, dh):
    x = x_ref[...]
    h = _rms(x, g_ref[...]).astype(BF16)
    q = _dot(h, wq_ref[...]).astype(BF16)
    width = n_heads * dh
    outs = []
    for hd in range(n_heads):
        cs = slice(hd * dh, (hd + 1) * dh)
        s = _dot_nt(q[:, cs], kv_ref[:, cs]) * (dh ** -0.5)
        m = jnp.max(s, axis=1, keepdims=True)
        p = jnp.exp(s - m)
        p = p / jnp.sum(p, axis=1, keepdims=True)
        outs.append(_dot(p.astype(BF16), kv_ref[:, width + hd * dh:width + (hd + 1) * dh]).astype(BF16))
    o = jnp.concatenate(outs, axis=1)
    o_ref[...] = x + _dot(o, wo_ref[...])


def cross_attention(x, g, wq, kv, wo, *, tm, seq, mem_len, kv_col, n_heads, dh, name):
    t, d = x.shape
    per_b = seq // tm
    width = n_heads * dh
    return pl.pallas_call(
        functools.partial(_cross_kernel, n_heads=n_heads, dh=dh),
        grid=(t // tm,),
        in_specs=[
            pl.BlockSpec((tm, d), lambda i: (i, 0)),
            pl.BlockSpec((1, d), lambda i: (0, 0)),
            pl.BlockSpec(wq.shape, lambda i: (0, 0)),
            pl.BlockSpec((mem_len, 2 * width), lambda i: (i // per_b, kv_col)),
            pl.BlockSpec(wo.shape, lambda i: (0, 0)),
        ],
        out_specs=pl.BlockSpec((tm, d), lambda i: (i, 0)),
        out_shape=jax.ShapeDtypeStruct((t, d), F32),
        compiler_params=_params("parallel"), name=name,
    )(x, g.reshape(1, d), wq, kv, wo)


def _ffn_kernel(x_ref, g_ref, w1_ref, w3_ref, w2_ref, o_ref, h_s, acc_s):
    f = pl.program_id(1)

    @pl.when(f == 0)
    def _():
        h_s[...] = _rms(x_ref[...], g_ref[...]).astype(BF16)
        acc_s[...] = x_ref[...]

    h = h_s[...]
    gate = _dot(h, w1_ref[...])
    up = _dot(h, w3_ref[...])
    act = (gate * jax.nn.sigmoid(gate) * up).astype(BF16)
    acc_s[...] += _dot(act, w2_ref[...])

    @pl.when(f == pl.num_programs(1) - 1)
    def _():
        o_ref[...] = acc_s[...]


def ffn_swiglu(x, g, w13, w2, *, tm, tf, name):
    t, d = x.shape
    ff = w2.shape[0]
    nf = ff // tf
    return pl.pallas_call(
        _ffn_kernel, grid=(t // tm, nf),
        in_specs=[
            pl.BlockSpec((tm, d), lambda i, f: (i, 0)),
            pl.BlockSpec((1, d), lambda i, f: (0, 0)),
            pl.BlockSpec((d, tf), lambda i, f: (0, f)),
            pl.BlockSpec((d, tf), lambda i, f: (0, nf + f)),
            pl.BlockSpec((tf, d), lambda i, f: (f, 0)),
        ],
        out_specs=pl.BlockSpec((tm, d), lambda i, f: (i, 0)),
        out_shape=jax.ShapeDtypeStruct((t, d), F32),
        scratch_shapes=[pltpu.VMEM((tm, d), BF16), pltpu.VMEM((tm, d), F32)],
        compiler_params=_params("parallel", "arbitrary"), name=name,
    )(x, g.reshape(1, d), w13, w13, w2)


def _conv_kernel(bg_ref, cg_ref, xi_ref, w_ref, o_ref):
    s, width = o_ref.shape
    n_taps = w_ref.shape[0]
    xc = cg_ref[...].astype(F32) * xi_ref[...].astype(F32)
    row = lax.broadcasted_iota(jnp.int32, (s, width), 0)
    y = w_ref[n_taps - 1:n_taps, :] * xc
    for back in range(1, n_taps):
        shifted = jnp.where(row >= back, pltpu.roll(xc, back, axis=0), 0.0)
        y = y + w_ref[n_taps - 1 - back:n_taps - back, :] * shifted
    o_ref[...] = (bg_ref[...].astype(F32) * y).astype(o_ref.dtype)


def short_conv(main, conv_w, *, batch, seq):
    width = conv_w.shape[1]
    return pl.pallas_call(
        _conv_kernel, grid=(batch,),
        in_specs=[
            pl.BlockSpec((seq, width), lambda b: (b, 0)),
            pl.BlockSpec((seq, width), lambda b: (b, 1)),
            pl.BlockSpec((seq, width), lambda b: (b, 2)),
            pl.BlockSpec(conv_w.shape, lambda b: (0, 0)),
        ],
        out_specs=pl.BlockSpec((seq, width), lambda b: (b, 0)),
        out_shape=jax.ShapeDtypeStruct((batch * seq, width), BF16),
        compiler_params=_params("parallel"), name="short_conv",
    )(main, main, main, conv_w)


def _diff_prep_kernel(rb_ref, lq1_ref, lk1_ref, lq2_ref, lk2_ref, bias_ref, far_ref, lam_ref, *, tq, lam_init):
    n_heads = bias_ref.shape[0]
    row = lax.broadcasted_iota(jnp.int32, (tq, tq), 0)
    col = lax.broadcasted_iota(jnp.int32, (tq, tq), 1)
    max_exact = N_BUCKETS // 2
    for which in range(2):
        n = jnp.maximum(row - col + which * tq, 0)
        nf = jnp.maximum(n, 1).astype(F32)
        large = max_exact + (jnp.log(nf / max_exact) / math.log(MAX_DIST / max_exact)
                             * (N_BUCKETS - max_exact)).astype(jnp.int32)
        large = jnp.minimum(large, N_BUCKETS - 1)
        bucket = jnp.where(n < max_exact, n, large)
        for h in range(n_heads):
            b = jnp.zeros((tq, tq), F32)
            for kk in range(N_BUCKETS):
                b = jnp.where(bucket == kk, rb_ref[kk, h], b)
            bias_ref[h, which] = b
    for h in range(n_heads):
        far_ref[h] = jnp.full((1, LANES), rb_ref[N_BUCKETS - 1, h], F32)
    lam = (jnp.exp(jnp.sum(lq1_ref[...] * lk1_ref[...], axis=1, keepdims=True))
           - jnp.exp(jnp.sum(lq2_ref[...] * lk2_ref[...], axis=1, keepdims=True)) + lam_init)
    lam_ref[...] = jnp.broadcast_to(lam, (1, LANES))


def diff_prep(rel_bias, lq1, lk1, lq2, lk2, *, tq, lam_init):
    n_heads = rel_bias.shape[1]
    vec = lambda a: a.reshape(1, -1)
    vspec = pl.BlockSpec(memory_space=pltpu.VMEM)
    return pl.pallas_call(
        functools.partial(_diff_prep_kernel, tq=tq, lam_init=lam_init),
        in_specs=[pl.BlockSpec(memory_space=pltpu.SMEM), vspec, vspec, vspec, vspec],
        out_specs=[vspec, vspec, vspec],
        out_shape=[
            jax.ShapeDtypeStruct((n_heads, 2, tq, tq), F32),
            jax.ShapeDtypeStruct((n_heads, 1, LANES), F32),
            jax.ShapeDtypeStruct((1, LANES), F32),
        ],
        compiler_params=pltpu.CompilerParams(vmem_limit_bytes=VMEM_LIMIT), name="diff_prep",
    )(rel_bias, vec(lq1), vec(lk1), vec(lq2), vec(lk2))


def _diff_kernel(q_ref, k_ref, v_ref, bias_ref, far_ref, lam_ref, subln_ref, o_ref, *, tq, lam_init):
    i = pl.program_id(2)
    lane = lax.broadcasted_iota(jnp.int32, (1, LANES), 1)
    row = lax.broadcasted_iota(jnp.int32, (tq, tq), 0)
    col = lax.broadcasted_iota(jnp.int32, (tq, tq), 1)
    causal = row >= col
    q = q_ref[...]
    qms = [jnp.where((lane >= HEAD_DIM * sub) & (lane < HEAD_DIM * (sub + 1)), q, jnp.zeros_like(q))
           for sub in range(2)]
    far = far_ref[0][:, 0:1]

    def tile(j, states, bias, masked):
        start = pl.multiple_of(j * tq, tq)
        k = k_ref[pl.ds(start, tq), :]
        v = v_ref[pl.ds(start, tq), :]
        new = []
        for sub in range(2):
            s = _dot_nt(qms[sub], k) + bias
            if masked:
                s = jnp.where(causal, s, -jnp.inf)
            new.append(_online_update(states[sub], s, v))
        return tuple(new)

    states = (_init_state(tq), _init_state(tq))
    n_far = jnp.maximum(i - 1, 0)
    states = lax.fori_loop(0, n_far, lambda j, st: tile(j, st, far, False), states)
    states = lax.fori_loop(n_far, i, lambda j, st: tile(j, st, bias_ref[0, 1], False), states)
    (_, l0, acc0), (_, l1, acc1) = tile(i, states, bias_ref[0, 0], True)
    o = acc0 / l0 - lam_ref[...] * (acc1 / l1)
    o_ref[...] = (_rms(o, subln_ref[...]) * (1.0 - lam_init)).astype(o_ref.dtype)


def diff_attention(main, bias, far, lam, subln, *, batch, seq, tq, q_col, k_col, v_col, lam_init):
    t = batch * seq
    nq = seq // tq
    n_heads = bias.shape[0]
    return pl.pallas_call(
        functools.partial(_diff_kernel, tq=tq, lam_init=lam_init),
        grid=(batch, n_heads, nq),
        in_specs=[
            pl.BlockSpec((tq, LANES), lambda b, h, i: (b * nq + i, q_col + h)),
            pl.BlockSpec((seq, LANES), lambda b, h, i: (b, k_col + h)),
            pl.BlockSpec((seq, LANES), lambda b, h, i: (b, v_col + h)),
            pl.BlockSpec((1, 2, tq, tq), lambda b, h, i: (h, 0, 0, 0)),
            pl.BlockSpec((1, 1, LANES), lambda b, h, i: (h, 0, 0)),
            pl.BlockSpec((1, LANES), lambda b, h, i: (0, 0)),
            pl.BlockSpec((1, LANES), lambda b, h, i: (0, 0)),
        ],
        out_specs=pl.BlockSpec((tq, LANES), lambda b, h, i: (b * nq + i, h)),
        out_shape=jax.ShapeDtypeStruct((t, n_heads * LANES), BF16),
        compiler_params=_params("parallel", "parallel", "arbitrary"), name="diff_attention",
    )(main, main, main, bias, far, lam, subln.reshape(1, LANES))


def _router_kernel(x_ref, g_ref, wr_ref, meta_ref, wts_ref, cnt_ref, carry_s, *, n_exp):
    tm = x_ref.shape[0]

    @pl.when(pl.program_id(0) == 0)
    def _():
        carry_s[...] = jnp.zeros_like(carry_s)

    h = _rms(x_ref[...], g_ref[...])
    logits = jnp.dot(h, wr_ref[...], precision=lax.Precision.HIGHEST, preferred_element_type=F32)
    lane = lax.broadcasted_iota(jnp.int32, (tm, LANES), 1)
    lane_f = lane.astype(F32)
    logits = jnp.where(lane < n_exp, logits, -jnp.inf)
    m1 = jnp.max(logits, axis=1, keepdims=True)
    i1 = jnp.min(jnp.where(logits == m1, lane_f, float(LANES)), axis=1, keepdims=True)
    rest = jnp.where(lane_f == i1, -jnp.inf, logits)
    m2 = jnp.max(rest, axis=1, keepdims=True)
    i2 = jnp.min(jnp.where(rest == m2, lane_f, float(LANES)), axis=1, keepdims=True)
    e = jnp.exp(m2 - m1)
    w1 = 1.0 / (1.0 + e)
    w2 = e / (1.0 + e)
    sel1 = lane_f == i1
    sel2 = lane_f == i2
    onehot = jnp.where(sel1 | sel2, 1.0, 0.0)
    row = lax.broadcasted_iota(jnp.int32, (tm, tm), 0)
    col = lax.broadcasted_iota(jnp.int32, (tm, tm), 1)
    before = jnp.where(row > col, 1.0, 0.0).astype(BF16)
    rank = _dot(before, onehot.astype(BF16)) + carry_s[...]
    r1 = jnp.sum(jnp.where(sel1, rank, 0.0), axis=1, keepdims=True)
    r2 = jnp.sum(jnp.where(sel2, rank, 0.0), axis=1, keepdims=True)
    meta = jnp.where(lane == 0, i1, jnp.where(lane == 1, i2, jnp.where(lane == 2, r1, jnp.where(lane == 3, r2, 0.0))))
    meta_ref[...] = meta.astype(jnp.int32)
    wts_ref[...] = jnp.where(lane == 0, w1, jnp.where(lane == 1, w2, 0.0))
    carry_s[...] += jnp.sum(onehot, axis=0, keepdims=True)
    cnt_ref[...] = carry_s[...]


def route_tokens(x, g, wr, *, tm, n_exp):
    t, d = x.shape
    return pl.pallas_call(
        functools.partial(_router_kernel, n_exp=n_exp),
        grid=(t // tm,),
        in_specs=[
            pl.BlockSpec((tm, d), lambda i: (i, 0)),
            pl.BlockSpec((1, d), lambda i: (0, 0)),
            pl.BlockSpec((d, LANES), lambda i: (0, 0)),
        ],
        out_specs=[
            pl.BlockSpec((tm, LANES), lambda i: (i, 0)),
            pl.BlockSpec((tm, LANES), lambda i: (i, 0)),
            pl.BlockSpec((1, LANES), lambda i: (0, 0)),
        ],
        out_shape=[
            jax.ShapeDtypeStruct((t, LANES), jnp.int32),
            jax.ShapeDtypeStruct((t, LANES), F32),
            jax.ShapeDtypeStruct((1, LANES), F32),
        ],
        scratch_shapes=[pltpu.VMEM((1, LANES), F32)],
        compiler_params=_params("arbitrary"), name="moe_router",
    )(x, g.reshape(1, d), wr)


def _row_copy(src, dst, s_row, d_row, sem):
    return pltpu.make_async_copy(src.at[pl.ds(s_row, 1)], dst.at[pl.ds(d_row, 1)], sem)


def _dispatch_kernel(pos_ref, x_hbm, zeros_hbm, xs_hbm, sem, *, tm):
    del zeros_hbm
    base = pl.program_id(0) * tm
    n_tok = x_hbm.shape[0]

    def start(t, carry):
        for k in range(TOP_K):
            _row_copy(x_hbm, xs_hbm, base + t, pos_ref[k * n_tok + base + t], sem).start()
        return carry

    def wait(t, carry):
        for k in range(TOP_K):
            _row_copy(x_hbm, xs_hbm, base + t, pos_ref[k * n_tok + base + t], sem).wait()
        return carry

    lax.fori_loop(0, tm, start, 0)
    lax.fori_loop(0, tm, wait, 0)


def moe_dispatch(pos, x, n_rows, *, tm):
    t, d = x.shape
    zeros = jnp.zeros((n_rows, d), x.dtype)
    return pl.pallas_call(
        functools.partial(_dispatch_kernel, tm=tm),
        grid_spec=pltpu.PrefetchScalarGridSpec(
            num_scalar_prefetch=1, grid=(t // tm,),
            in_specs=[pl.BlockSpec(memory_space=pl.ANY), pl.BlockSpec(memory_space=pl.ANY)],
            out_specs=pl.BlockSpec(memory_space=pl.ANY),
            scratch_shapes=[pltpu.SemaphoreType.DMA(())],
        ),
        out_shape=jax.ShapeDtypeStruct((n_rows, d), x.dtype),
        input_output_aliases={2: 0},
        compiler_params=pltpu.CompilerParams(dimension_semantics=("arbitrary",), has_side_effects=True),
        name="moe_dispatch",
    )(pos, x, zeros)


def _expert_kernel(te_ref, nu_ref, xs_ref, g_ref, w1_ref, w3_ref, w2_ref, o_ref, h_s, acc_s):
    i = pl.program_id(0)
    f = pl.program_id(1)
    used = i < nu_ref[0]

    @pl.when(used & (f == 0))
    def _():
        h_s[...] = _rms(xs_ref[...], g_ref[...]).astype(BF16)
        acc_s[...] = jnp.zeros_like(acc_s)

    @pl.when(used)
    def _():
        h = h_s[...]
        gate = _dot(h, w1_ref[0])
        up = _dot(h, w3_ref[0])
        act = (gate * jax.nn.sigmoid(gate) * up).astype(BF16)
        acc_s[...] += _dot(act, w2_ref[0])

    last = f == pl.num_programs(1) - 1

    @pl.when(used & last)
    def _():
        o_ref[...] = acc_s[...]

    @pl.when(jnp.logical_not(used) & last)
    def _():
        o_ref[...] = jnp.zeros_like(o_ref)


def moe_experts(tile_expert, n_used, xs, g, w13, w2, *, tm, tf):
    n_rows, d = xs.shape
    ff = w2.shape[1]
    nf = ff // tf
    n_tiles = n_rows // tm

    def chunk(i, f, nu):
        return jnp.where(i < nu[0], f, nf - 1)

    return pl.pallas_call(
        _expert_kernel,
        grid_spec=pltpu.PrefetchScalarGridSpec(
            num_scalar_prefetch=2, grid=(n_tiles, nf),
            in_specs=[
                pl.BlockSpec((tm, d), lambda i, f, te, nu: (i, 0)),
                pl.BlockSpec((1, d), lambda i, f, te, nu: (0, 0)),
                pl.BlockSpec((1, d, tf), lambda i, f, te, nu: (te[i], 0, chunk(i, f, nu))),
                pl.BlockSpec((1, d, tf), lambda i, f, te, nu: (te[i], 0, nf + chunk(i, f, nu))),
                pl.BlockSpec((1, tf, d), lambda i, f, te, nu: (te[i], chunk(i, f, nu), 0)),
            ],
            out_specs=pl.BlockSpec((tm, d), lambda i, f, te, nu: (i, 0)),
            scratch_shapes=[pltpu.VMEM((tm, d), BF16), pltpu.VMEM((tm, d), F32)],
        ),
        out_shape=jax.ShapeDtypeStruct((n_rows, d), F32),
        compiler_params=_params("arbitrary", "arbitrary"), name="moe_experts",
    )(tile_expert, n_used, xs, g.reshape(1, d), w13, w13, w2)


def _combine_kernel(pos_ref, x_ref, wts_ref, g_ref, y_hbm, o_ref, rows_s, sem, *, tm):
    base = pl.program_id(0) * tm
    n_tok = pl.num_programs(0) * tm

    def start(t, carry):
        for k in range(TOP_K):
            _row_copy(y_hbm, rows_s.at[k], pos_ref[k * n_tok + base + t], t, sem).start()
        return carry

    def wait(t, carry):
        for k in range(TOP_K):
            _row_copy(y_hbm, rows_s.at[k], pos_ref[k * n_tok + base + t], t, sem).wait()
        return carry

    lax.fori_loop(0, tm, start, 0)
    lax.fori_loop(0, tm, wait, 0)
    wts = wts_ref[...]
    y = x_ref[...]
    for k in range(TOP_K):
        y = y + wts[:, k:k + 1] * rows_s[k]
    o_ref[...] = _rms(y, g_ref[...])


def moe_combine(pos, x, wts, g, y, *, tm):
    t, d = x.shape
    return pl.pallas_call(
        functools.partial(_combine_kernel, tm=tm),
        grid_spec=pltpu.PrefetchScalarGridSpec(
            num_scalar_prefetch=1, grid=(t // tm,),
            in_specs=[
                pl.BlockSpec((tm, d), lambda i, pos: (i, 0)),
                pl.BlockSpec((tm, LANES), lambda i, pos: (i, 0)),
                pl.BlockSpec((1, d), lambda i, pos: (0, 0)),
                pl.BlockSpec(memory_space=pl.ANY),
            ],
            out_specs=pl.BlockSpec((tm, d), lambda i, pos: (i, 0)),
            scratch_shapes=[pltpu.VMEM((TOP_K, tm, d), F32), pltpu.SemaphoreType.DMA(())],
        ),
        out_shape=jax.ShapeDtypeStruct((t, d), F32),
        compiler_params=_params("arbitrary"), name="moe_combine",
    )(pos, x, wts, g.reshape(1, d), y)


def _pad_cols(w, n):
    return jnp.pad(w, ((0, 0), (0, n - w.shape[1])))


def _even_mixer(x, norm, w_in, b_f, sgu_norm, w_s, b_s, w_out, *, batch, seq, tq):
    n_heads = b_f.shape[0]
    a_width = n_heads * HEAD_DIM
    b_width = w_s.shape[0] * LANES
    f0 = 3 * a_width
    scale = HEAD_DIM ** -0.5
    w_main = jnp.concatenate([w_in[:, :a_width] * scale, w_in[:, a_width:f0], w_in[:, f0 + n_heads:]], axis=1)
    w_gate = _pad_cols(w_in[:, f0:f0 + n_heads], LANES)
    main, gate = norm_matmul(x, norm, [w_main.astype(BF16), w_gate.astype(BF16)], [BF16, F32],
                             tm=512, name="even_in_proj")
    c = gate_cumsum(gate, _pad_cols(b_f.reshape(1, -1), LANES), seq=seq)
    n_pairs = a_width // LANES
    nq = seq // tq
    ck = c[:, :n_heads].reshape(batch, seq, n_pairs, 2).transpose(0, 2, 3, 1).reshape(batch, n_pairs, 2, nq, tq)
    a = fox_attention(main, c, ck, batch=batch, seq=seq, tq=tq, q_col=0, k_col=n_pairs, v_col=2 * n_pairs,
                      n_pairs=n_pairs)
    u_col = f0 // b_width
    g = spatial_gating(main, sgu_norm, w_s, b_s, tb=512, u_col=u_col, v_col=u_col + 1)
    w_out = w_out.astype(BF16)
    return proj_residual(x, a, g, w_out[:a_width], w_out[a_width:], tm=512, name="even_out_proj")


def _odd_mixer(x, norm, w_in, conv_w, lq1, lk1, lq2, lk2, subln, w_out, rel_bias, lam_init, *, batch, seq, tq):
    c_width = conv_w.shape[1]
    d_width = rel_bias.shape[1] * 2 * HEAD_DIM
    q0 = 3 * c_width
    scale = HEAD_DIM ** -0.5
    w_main = jnp.concatenate([w_in[:, :q0], w_in[:, q0:q0 + d_width] * scale, w_in[:, q0 + d_width:]], axis=1)
    (main,) = norm_matmul(x, norm, [w_main.astype(BF16)], [BF16], tm=512, name="odd_in_proj")
    c_out = short_conv(main, conv_w, batch=batch, seq=seq)
    bias, far, lam = diff_prep(rel_bias, lq1, lk1, lq2, lk2, tq=tq, lam_init=lam_init)
    q_col = q0 // LANES
    n_heads = rel_bias.shape[1]
    d_out = diff_attention(main, bias, far, lam, subln, batch=batch, seq=seq, tq=tq, q_col=q_col,
                           k_col=q_col + n_heads, v_col=q_col + 2 * n_heads, lam_init=lam_init)
    w_out = w_out.astype(BF16)
    return proj_residual(x, c_out, d_out, w_out[:c_width], w_out[c_width:], tm=512, name="odd_out_proj")


def _moe_layer(x, norm, w_router, w13, w2, final_norm, *, tm_expert):
    t, d = x.shape
    n_exp = w_router.shape[1]
    meta, wts, counts = route_tokens(x, norm, _pad_cols(w_router, LANES), tm=512, n_exp=n_exp)
    counts = counts[0, :n_exp].astype(jnp.int32)
    padded = (counts + tm_expert - 1) // tm_expert * tm_expert
    ends = jnp.cumsum(padded)
    offsets = ends - padded
    pos = jnp.concatenate([offsets[meta[:, k]] + meta[:, TOP_K + k] for k in range(TOP_K)]).astype(jnp.int32)
    n_rows = TOP_K * t + n_exp * tm_expert
    n_tiles = n_rows // tm_expert
    tile_start = jnp.arange(n_tiles, dtype=jnp.int32) * tm_expert
    tile_expert = jnp.minimum(jnp.sum(tile_start[:, None] >= ends[None, :], axis=1), n_exp - 1).astype(jnp.int32)
    n_used = (ends[-1:] // tm_expert).astype(jnp.int32)
    xs = moe_dispatch(pos, x, n_rows, tm=512)
    y = moe_experts(tile_expert, n_used, xs, norm, w13.astype(BF16), w2.astype(BF16), tm=tm_expert, tf=512)
    return moe_combine(pos, x, wts, final_norm, y, tm=256)


def kernel(x, mem, rel_bias, mem_norm, final_norm, ev_norm, ev_w_in, ev_b_f, ev_sgu_norm, ev_w_s, ev_b_s, ev_w_out, ffn_w13, ffn_w2, od_norm, od_w_in, od_conv_w, od_lam_q1, od_lam_k1, od_lam_q2, od_lam_k2, od_subln, od_w_out, moe_router, moe_w13, moe_w2, x_norm, x_wq, x_wkv, x_wo, ffn_norm):
    batch, seq, d = x.shape
    mem_len = mem.shape[1]
    depth = x_norm.shape[0]
    assert depth == 2 and ev_norm.shape[0] == 1 and od_norm.shape[0] == 1
    x_heads, x_dh = 4, 128
    xf = x.reshape(batch * seq, d)
    wkv = jnp.concatenate([x_wkv[layer] for layer in range(depth)], axis=1).astype(BF16)
    (kv,) = norm_matmul(mem.reshape(batch * mem_len, d), mem_norm, [wkv], [BF16], tm=512, name="mem_kv")

    def cross(xf, layer):
        return cross_attention(xf, x_norm[layer], x_wq[layer].astype(BF16), kv, x_wo[layer].astype(BF16),
                               tm=512, seq=seq, mem_len=mem_len, kv_col=layer, n_heads=x_heads, dh=x_dh,
                               name=f"cross_attention_{layer}")

    xf = _even_mixer(xf, ev_norm[0], ev_w_in[0], ev_b_f[0], ev_sgu_norm[0], ev_w_s[0], ev_b_s[0], ev_w_out[0],
                     batch=batch, seq=seq, tq=256)
    xf = cross(xf, 0)
    xf = ffn_swiglu(xf, ffn_norm[0], ffn_w13[0].astype(BF16), ffn_w2[0].astype(BF16), tm=1024, tf=256,
                    name="ffn_swiglu")
    lam_init = 0.8 - 0.6 * math.exp(-0.3 * 1)
    xf = _odd_mixer(xf, od_norm[0], od_w_in[0], od_conv_w[0], od_lam_q1[0], od_lam_k1[0], od_lam_q2[0],
                    od_lam_k2[0], od_subln[0], od_w_out[0], rel_bias, lam_init, batch=batch, seq=seq, tq=256)
    xf = cross(xf, 1)
    out = _moe_layer(xf, ffn_norm[1], moe_router[0], moe_w13[0], moe_w2[0], final_norm, tm_expert=512)
    return out.reshape(batch, seq, d)
```

```python
import functools
import math

import jax
import jax.numpy as jnp
from jax import lax
from jax.experimental import pallas as pl
from jax.experimental.pallas import tpu as pltpu

F32 = jnp.float32
BF16 = jnp.bfloat16
EPS = 1e-6
HEAD_DIM = 64
LANES = 128
N_BUCKETS = 32
MAX_DIST = 128
TOP_K = 2
VMEM_LIMIT = 56 * 1024 * 1024


def _params(*sem):
    return pltpu.CompilerParams(dimension_semantics=sem, vmem_limit_bytes=VMEM_LIMIT)


def _rms(x, g):
    ms = jnp.mean(x * x, axis=-1, keepdims=True)
    return x * lax.rsqrt(ms + EPS) * g


def _dot(a, b):
    return jnp.dot(a, b, preferred_element_type=F32)


def _dot_nt(a, b):
    return lax.dot_general(a, b, (((1,), (1,)), ((), ())), preferred_element_type=F32)


def _norm_matmul_kernel(x_ref, g_ref, *refs, n_w, chunk):
    w_refs, o_refs = refs[:n_w], refs[n_w:]
    h = _rms(x_ref[...], g_ref[...]).astype(BF16)
    for w_ref, o_ref in zip(w_refs, o_refs):
        n = w_ref.shape[1]
        for c0 in range(0, n, chunk):
            c1 = min(c0 + chunk, n)
            o_ref[:, c0:c1] = _dot(h, w_ref[:, c0:c1]).astype(o_ref.dtype)


def norm_matmul(x, g, ws, out_dtypes, *, tm, name):
    t, d = x.shape
    in_specs = [pl.BlockSpec((tm, d), lambda i: (i, 0)), pl.BlockSpec((1, d), lambda i: (0, 0))]
    in_specs += [pl.BlockSpec(w.shape, lambda i: (0, 0)) for w in ws]
    out_specs = [pl.BlockSpec((tm, w.shape[1]), lambda i: (i, 0)) for w in ws]
    out_shape = [jax.ShapeDtypeStruct((t, w.shape[1]), dt) for w, dt in zip(ws, out_dtypes)]
    return pl.pallas_call(
        functools.partial(_norm_matmul_kernel, n_w=len(ws), chunk=512),
        grid=(t // tm,), in_specs=in_specs, out_specs=out_specs, out_shape=out_shape,
        compiler_params=_params("parallel"), name=name,
    )(x, g.reshape(1, d), *ws)


def _gate_kernel(g_ref, b_ref, c_ref):
    s = g_ref.shape[0]
    row = lax.broadcasted_iota(jnp.int32, (LANES, LANES), 0)
    col = lax.broadcasted_iota(jnp.int32, (LANES, LANES), 1)
    tri = (row >= col).astype(F32)
    carry = jnp.zeros((1, LANES), F32)
    for blk in range(s // LANES):
        z = g_ref[blk * LANES:(blk + 1) * LANES, :] + b_ref[...]
        log_f = jnp.minimum(z, 0.0) - jnp.log1p(jnp.exp(-jnp.abs(z)))
        cs = jnp.dot(tri, log_f, precision=lax.Precision.HIGHEST, preferred_element_type=F32) + carry
        c_ref[blk * LANES:(blk + 1) * LANES, :] = cs
        carry = cs[LANES - 1:LANES, :]


def gate_cumsum(g, b, *, seq):
    t = g.shape[0]
    return pl.pallas_call(
        _gate_kernel, grid=(t // seq,),
        in_specs=[pl.BlockSpec((seq, LANES), lambda i: (i, 0)), pl.BlockSpec((1, LANES), lambda i: (0, 0))],
        out_specs=pl.BlockSpec((seq, LANES), lambda i: (i, 0)),
        out_shape=jax.ShapeDtypeStruct((t, LANES), F32),
        compiler_params=_params("parallel"), name="gate_cumsum",
    )(g, b)


def _online_update(state, s, v):
    m, l, acc = state
    m_new = jnp.maximum(m, jnp.max(s, axis=1, keepdims=True))
    p = jnp.exp(s - m_new)
    a = jnp.exp(m - m_new)
    l = a * l + jnp.sum(p, axis=1, keepdims=True)
    acc = a * acc + _dot(p.astype(BF16), v)
    return m_new, l, acc


def _init_state(tq):
    return (jnp.full((tq, 1), -jnp.inf, F32), jnp.zeros((tq, 1), F32), jnp.zeros((tq, LANES), F32))


def _fox_kernel(q_ref, k_ref, v_ref, cq_ref, ck_ref, o_ref, *, tq):
    hp = pl.program_id(1)
    i = pl.program_id(2)
    lane = lax.broadcasted_iota(jnp.int32, (1, LANES), 1)
    row = lax.broadcasted_iota(jnp.int32, (tq, tq), 0)
    col = lax.broadcasted_iota(jnp.int32, (tq, tq), 1)
    causal = row >= col
    q = q_ref[...]
    cq_all = cq_ref[...]
    outs = []
    for hh in range(2):
        in_half = (lane >= HEAD_DIM * hh) & (lane < HEAD_DIM * (hh + 1))
        qm = jnp.where(in_half, q, jnp.zeros_like(q))
        cq = jnp.sum(jnp.where(lane == 2 * hp + hh, cq_all, 0.0), axis=1, keepdims=True)

        def tile(j, state, masked, qm=qm, cq=cq, hh=hh):
            start = pl.multiple_of(j * tq, tq)
            k = k_ref[pl.ds(start, tq), :]
            v = v_ref[pl.ds(start, tq), :]
            ck = ck_ref[0, 0, hh, pl.ds(j, 1), :]
            s = _dot_nt(qm, k) + (cq - ck)
            if masked:
                s = jnp.where(causal, s, -jnp.inf)
            return _online_update(state, s, v)

        state = lax.fori_loop(0, i, lambda j, st: tile(j, st, False), _init_state(tq))
        _, l, acc = tile(i, state, True)
        outs.append(acc / l)
    o_ref[...] = jnp.where(lane < HEAD_DIM, outs[0], outs[1]).astype(o_ref.dtype)


def fox_attention(qkv, c, ck, *, batch, seq, tq, q_col, k_col, v_col, n_pairs):
    t = batch * seq
    nq = seq // tq
    return pl.pallas_call(
        functools.partial(_fox_kernel, tq=tq),
        grid=(batch, n_pairs, nq),
        in_specs=[
            pl.BlockSpec((tq, LANES), lambda b, h, i: (b * nq + i, q_col + h)),
            pl.BlockSpec((seq, LANES), lambda b, h, i: (b, k_col + h)),
            pl.BlockSpec((seq, LANES), lambda b, h, i: (b, v_col + h)),
            pl.BlockSpec((tq, LANES), lambda b, h, i: (b * nq + i, 0)),
            pl.BlockSpec((1, 1, 2, nq, tq), lambda b, h, i: (b, h, 0, 0, 0)),
        ],
        out_specs=pl.BlockSpec((tq, LANES), lambda b, h, i: (b * nq + i, h)),
        out_shape=jax.ShapeDtypeStruct((t, n_pairs * LANES), BF16),
        compiler_params=_params("parallel", "parallel", "arbitrary"), name="fox_attention",
    )(qkv, qkv, qkv, c, ck)


def _sgu_kernel(u_ref, v_ref, norm_ref, ws_ref, bs_ref, o_ref, *, n_groups, chunk):
    tb = u_ref.shape[0]
    row = lax.broadcasted_iota(jnp.int32, (chunk, chunk), 0)
    col = lax.broadcasted_iota(jnp.int32, (chunk, chunk), 1)
    tri = row >= col
    for g in range(n_groups):
        w = jnp.where(tri, ws_ref[g], 0.0).astype(BF16)
        bias = bs_ref[:, g:g + 1]
        gain = norm_ref[g:g + 1, :]
        for c in range(tb // chunk):
            rs = slice(c * chunk, (c + 1) * chunk)
            cs = slice(g * LANES, (g + 1) * LANES)
            vn = _rms(jax.nn.gelu(v_ref[rs, cs].astype(F32)), gain)
            mixed = _dot(w, vn.astype(BF16)) + bias
            o_ref[rs, cs] = (jax.nn.gelu(u_ref[rs, cs].astype(F32)) * mixed).astype(o_ref.dtype)


def spatial_gating(main, sgu_norm, w_s, b_s, *, tb, u_col, v_col):
    t = main.shape[0]
    n_groups, chunk, _ = w_s.shape
    width = n_groups * LANES
    return pl.pallas_call(
        functools.partial(_sgu_kernel, n_groups=n_groups, chunk=chunk),
        grid=(t // tb,),
        in_specs=[
            pl.BlockSpec((tb, width), lambda i: (i, u_col)),
            pl.BlockSpec((tb, width), lambda i: (i, v_col)),
            pl.BlockSpec(sgu_norm.shape, lambda i: (0, 0)),
            pl.BlockSpec(w_s.shape, lambda i: (0, 0, 0)),
            pl.BlockSpec((chunk, n_groups), lambda i: (0, 0)),
        ],
        out_specs=pl.BlockSpec((tb, width), lambda i: (i, 0)),
        out_shape=jax.ShapeDtypeStruct((t, width), BF16),
        compiler_params=_params("parallel"), name="spatial_gating",
    )(main, main, sgu_norm, w_s, b_s.T)


def _proj_res_kernel(x_ref, a_ref, b_ref, wa_ref, wb_ref, o_ref):
    o_ref[...] = x_ref[...] + _dot(a_ref[...], wa_ref[...]) + _dot(b_ref[...], wb_ref[...])


def proj_residual(x, a, b, wa, wb, *, tm, name):
    t, d = x.shape
    return pl.pallas_call(
        _proj_res_kernel, grid=(t // tm,),
        in_specs=[
            pl.BlockSpec((tm, d), lambda i: (i, 0)),
            pl.BlockSpec((tm, a.shape[1]), lambda i: (i, 0)),
            pl.BlockSpec((tm, b.shape[1]), lambda i: (i, 0)),
            pl.BlockSpec(wa.shape, lambda i: (0, 0)),
            pl.BlockSpec(wb.shape, lambda i: (0, 0)),
        ],
        out_specs=pl.BlockSpec((tm, d), lambda i: (i, 0)),
        out_shape=jax.ShapeDtypeStruct((t, d), F32),
        compiler_params=_params("parallel"), name=name,
    )(x, a, b, wa, wb)


def _cross_kernel(x_ref, g_ref, wq_ref, kv_ref, wo_ref, o_ref, *, n_heads, dh):
    x = x_ref[...]
    h = _rms(x, g_ref[...]).astype(BF16)
    q = _dot(h, wq_ref[...]).astype(BF16)
    width = n_heads * dh
    outs = []
    for hd in range(n_heads):
        cs = slice(hd * dh, (hd + 1) * dh)
        s = _dot_nt(q[:, cs], kv_ref[:, cs]) * (dh ** -0.5)
        m = jnp.max(s, axis=1, keepdims=True)
        p = jnp.exp(s - m)
        p = p / jnp.sum(p, axis=1, keepdims=True)
        outs.append(_dot(p.astype(BF16), kv_ref[:, width + hd * dh:width + (hd + 1) * dh]).astype(BF16))
    o = jnp.concatenate(outs, axis=1)
    o_ref[...] = x + _dot(o, wo_ref[...])


def cross_attention(x, g, wq, kv, wo, *, tm, seq, mem_len, kv_col, n_heads, dh, name):
    t, d = x.shape
    per_b = seq // tm
    width = n_heads * dh
    return pl.pallas_call(
        functools.partial(_cross_kernel, n_heads=n_heads, dh=dh),
        grid=(t // tm,),
        in_specs=[
            pl.BlockSpec((tm, d), lambda i: (i, 0)),
            pl.BlockSpec((1, d), lambda i: (0, 0)),
            pl.BlockSpec(wq.shape, lambda i: (0, 0)),
            pl.BlockSpec((mem_len, 2 * width), lambda i: (i // per_b, kv_col)),
            pl.BlockSpec(wo.shape, lambda i: (0, 0)),
        ],
        out_specs=pl.BlockSpec((tm, d), lambda i: (i, 0)),
        out_shape=jax.ShapeDtypeStruct((t, d), F32),
        compiler_params=_params("parallel"), name=name,
    )(x, g.reshape(1, d), wq, kv, wo)


def _ffn_kernel(x_ref, g_ref, w1_ref, w3_ref, w2_ref, o_ref, h_s, acc_s):
    f = pl.program_id(1)

    @pl.when(f == 0)
    def _():
        h_s[...] = _rms(x_ref[...], g_ref[...]).astype(BF16)
        acc_s[...] = x_ref[...]

    h = h_s[...]
    gate = _dot(h, w1_ref[...])
    up = _dot(h, w3_ref[...])
    act = (gate * jax.nn.sigmoid(gate) * up).astype(BF16)
    acc_s[...] += _dot(act, w2_ref[...])

    @pl.when(f == pl.num_programs(1) - 1)
    def _():
        o_ref[...] = acc_s[...]


def ffn_swiglu(x, g, w13, w2, *, tm, tf, name):
    t, d = x.shape
    ff = w2.shape[0]
    nf = ff // tf
    return pl.pallas_call(
        _ffn_kernel, grid=(t // tm, nf),
        in_specs=[
            pl.BlockSpec((tm, d), lambda i, f: (i, 0)),
            pl.BlockSpec((1, d), lambda i, f: (0, 0)),
            pl.BlockSpec((d, tf), lambda i, f: (0, f)),
            pl.BlockSpec((d, tf), lambda i, f: (0, nf + f)),
            pl.BlockSpec((tf, d), lambda i, f: (f, 0)),
        ],
        out_specs=pl.BlockSpec((tm, d), lambda i, f: (i, 0)),
        out_shape=jax.ShapeDtypeStruct((t, d), F32),
        scratch_shapes=[pltpu.VMEM((tm, d), BF16), pltpu.VMEM((tm, d), F32)],
        compiler_params=_params("parallel", "arbitrary"), name=name,
    )(x, g.reshape(1, d), w13, w13, w2)


def _conv_kernel(bg_ref, cg_ref, xi_ref, w_ref, o_ref):
    s, width = o_ref.shape
    n_taps = w_ref.shape[0]
    xc = cg_ref[...].astype(F32) * xi_ref[...].astype(F32)
    row = lax.broadcasted_iota(jnp.int32, (s, width), 0)
    y = w_ref[n_taps - 1:n_taps, :] * xc
    for back in range(1, n_taps):
        shifted = jnp.where(row >= back, pltpu.roll(xc, back, axis=0), 0.0)
        y = y + w_ref[n_taps - 1 - back:n_taps - back, :] * shifted
    o_ref[...] = (bg_ref[...].astype(F32) * y).astype(o_ref.dtype)


def short_conv(main, conv_w, *, batch, seq):
    width = conv_w.shape[1]
    return pl.pallas_call(
        _conv_kernel, grid=(batch,),
        in_specs=[
            pl.BlockSpec((seq, width), lambda b: (b, 0)),
            pl.BlockSpec((seq, width), lambda b: (b, 1)),
            pl.BlockSpec((seq, width), lambda b: (b, 2)),
            pl.BlockSpec(conv_w.shape, lambda b: (0, 0)),
        ],
        out_specs=pl.BlockSpec((seq, width), lambda b: (b, 0)),
        out_shape=jax.ShapeDtypeStruct((batch * seq, width), BF16),
        compiler_params=_params("parallel"), name="short_conv",
    )(main, main, main, conv_w)


def _diff_prep_kernel(rb_ref, lq1_ref, lk1_ref, lq2_ref, lk2_ref, bias_ref, far_ref, lam_ref, *, tq, lam_init):
    n_heads = bias_ref.shape[0]
    row = lax.broadcasted_iota(jnp.int32, (tq, tq), 0)
    col = lax.broadcasted_iota(jnp.int32, (tq, tq), 1)
    max_exact = N_BUCKETS // 2
    for which in range(2):
        n = jnp.maximum(row - col + which * tq, 0)
        nf = jnp.maximum(n, 1).astype(F32)
        large = max_exact + (jnp.log(nf / max_exact) / math.log(MAX_DIST / max_exact)
                             * (N_BUCKETS - max_exact)).astype(jnp.int32)
        large = jnp.minimum(large, N_BUCKETS - 1)
        bucket = jnp.where(n < max_exact, n, large)
        for h in range(n_heads):
            b = jnp.zeros((tq, tq), F32)
            for kk in range(N_BUCKETS):
                b = jnp.where(bucket == kk, rb_ref[kk, h], b)
            bias_ref[h, which] = b
    for h in range(n_heads):
        far_ref[h] = jnp.full((1, LANES), rb_ref[N_BUCKETS - 1, h], F32)
    lam = (jnp.exp(jnp.sum(lq1_ref[...] * lk1_ref[...], axis=1, keepdims=True))
           - jnp.exp(jnp.sum(lq2_ref[...] * lk2_ref[...], axis=1, keepdims=True)) + lam_init)
    lam_ref[...] = jnp.broadcast_to(lam, (1, LANES))


def diff_prep(rel_bias, lq1, lk1, lq2, lk2, *, tq, lam_init):
    n_heads = rel_bias.shape[1]
    vec = lambda a: a.reshape(1, -1)
    vspec = pl.BlockSpec(memory_space=pltpu.VMEM)
    return pl.pallas_call(
        functools.partial(_diff_prep_kernel, tq=tq, lam_init=lam_init),
        in_specs=[pl.BlockSpec(memory_space=pltpu.SMEM), vspec, vspec, vspec, vspec],
        out_specs=[vspec, vspec, vspec],
        out_shape=[
            jax.ShapeDtypeStruct((n_heads, 2, tq, tq), F32),
            jax.ShapeDtypeStruct((n_heads, 1, LANES), F32),
            jax.ShapeDtypeStruct((1, LANES), F32),
        ],
        compiler_params=pltpu.CompilerParams(vmem_limit_bytes=VMEM_LIMIT), name="diff_prep",
    )(rel_bias, vec(lq1), vec(lk1), vec(lq2), vec(lk2))


def _diff_kernel(q_ref, k_ref, v_ref, bias_ref, far_ref, lam_ref, subln_ref, o_ref, *, tq, lam_init):
    i = pl.program_id(2)
    lane = lax.broadcasted_iota(jnp.int32, (1, LANES), 1)
    row = lax.broadcasted_iota(jnp.int32, (tq, tq), 0)
    col = lax.broadcasted_iota(jnp.int32, (tq, tq), 1)
    causal = row >= col
    q = q_ref[...]
    qms = [jnp.where((lane >= HEAD_DIM * sub) & (lane < HEAD_DIM * (sub + 1)), q, jnp.zeros_like(q))
           for sub in range(2)]
    far = far_ref[0][:, 0:1]

    def tile(j, states, bias, masked):
        start = pl.multiple_of(j * tq, tq)
        k = k_ref[pl.ds(start, tq), :]
        v = v_ref[pl.ds(start, tq), :]
        new = []
        for sub in range(2):
            s = _dot_nt(qms[sub], k) + bias
            if masked:
                s = jnp.where(causal, s, -jnp.inf)
            new.append(_online_update(states[sub], s, v))
        return tuple(new)

    states = (_init_state(tq), _init_state(tq))
    n_far = jnp.maximum(i - 1, 0)
    states = lax.fori_loop(0, n_far, lambda j, st: tile(j, st, far, False), states)
    states = lax.fori_loop(n_far, i, lambda j, st: tile(j, st, bias_ref[0, 1], False), states)
    (_, l0, acc0), (_, l1, acc1) = tile(i, states, bias_ref[0, 0], True)
    o = acc0 / l0 - lam_ref[...] * (acc1 / l1)
    o_ref[...] = (_rms(o, subln_ref[...]) * (1.0 - lam_init)).astype(o_ref.dtype)


def diff_attention(main, bias, far, lam, subln, *, batch, seq, tq, q_col, k_col, v_col, lam_init):
    t = batch * seq
    nq = seq // tq
    n_heads = bias.shape[0]
    return pl.pallas_call(
        functools.partial(_diff_kernel, tq=tq, lam_init=lam_init),
        grid=(batch, n_heads, nq),
        in_specs=[
            pl.BlockSpec((tq, LANES), lambda b, h, i: (b * nq + i, q_col + h)),
            pl.BlockSpec((seq, LANES), lambda b, h, i: (b, k_col + h)),
            pl.BlockSpec((seq, LANES), lambda b, h, i: (b, v_col + h)),
            pl.BlockSpec((1, 2, tq, tq), lambda b, h, i: (h, 0, 0, 0)),
            pl.BlockSpec((1, 1, LANES), lambda b, h, i: (h, 0, 0)),
            pl.BlockSpec((1, LANES), lambda b, h, i: (0, 0)),
            pl.BlockSpec((1, LANES), lambda b, h, i: (0, 0)),
        ],
        out_specs=pl.BlockSpec((tq, LANES), lambda b, h, i: (b * nq + i, h)),
        out_shape=jax.ShapeDtypeStruct((t, n_heads * LANES), BF16),
        compiler_params=_params("parallel", "parallel", "arbitrary"), name="diff_attention",
    )(main, main, main, bias, far, lam, subln.reshape(1, LANES))


def _router_kernel(x_ref, g_ref, wr_ref, meta_ref, wts_ref, cnt_ref, carry_s, *, n_exp):
    tm = x_ref.shape[0]

    @pl.when(pl.program_id(0) == 0)
    def _():
        carry_s[...] = jnp.zeros_like(carry_s)

    h = _rms(x_ref[...], g_ref[...])
    logits = jnp.dot(h, wr_ref[...], precision=lax.Precision.HIGHEST, preferred_element_type=F32)
    lane = lax.broadcasted_iota(jnp.int32, (tm, LANES), 1)
    lane_f = lane.astype(F32)
    logits = jnp.where(lane < n_exp, logits, -jnp.inf)
    m1 = jnp.max(logits, axis=1, keepdims=True)
    i1 = jnp.min(jnp.where(logits == m1, lane_f, float(LANES)), axis=1, keepdims=True)
    rest = jnp.where(lane_f == i1, -jnp.inf, logits)
    m2 = jnp.max(rest, axis=1, keepdims=True)
    i2 = jnp.min(jnp.where(rest == m2, lane_f, float(LANES)), axis=1, keepdims=True)
    e = jnp.exp(m2 - m1)
    w1 = 1.0 / (1.0 + e)
    w2 = e / (1.0 + e)
    sel1 = lane_f == i1
    sel2 = lane_f == i2
    onehot = jnp.where(sel1 | sel2, 1.0, 0.0)
    row = lax.broadcasted_iota(jnp.int32, (tm, tm), 0)
    col = lax.broadcasted_iota(jnp.int32, (tm, tm), 1)
    before = jnp.where(row > col, 1.0, 0.0).astype(BF16)
    rank = _dot(before, onehot.astype(BF16)) + carry_s[...]
    r1 = jnp.sum(jnp.where(sel1, rank, 0.0), axis=1, keepdims=True)
    r2 = jnp.sum(jnp.where(sel2, rank, 0.0), axis=1, keepdims=True)
    meta = jnp.where(lane == 0, i1, jnp.where(lane == 1, i2, jnp.where(lane == 2, r1, jnp.where(lane == 3, r2, 0.0))))
    meta_ref[...] = meta.astype(jnp.int32)
    wts_ref[...] = jnp.where(lane == 0, w1, jnp.where(lane == 1, w2, 0.0))
    carry_s[...] += jnp.sum(onehot, axis=0, keepdims=True)
    cnt_ref[...] = carry_s[...]


def route_tokens(x, g, wr, *, tm, n_exp):
    t, d = x.shape
    return pl.pallas_call(
        functools.partial(_router_kernel, n_exp=n_exp),
        grid=(t // tm,),
        in_specs=[
            pl.BlockSpec((tm, d), lambda i: (i, 0)),
            pl.BlockSpec((1, d), lambda i: (0, 0)),
            pl.BlockSpec((d, LANES), lambda i: (0, 0)),
        ],
        out_specs=[
            pl.BlockSpec((tm, LANES), lambda i: (i, 0)),
            pl.BlockSpec((tm, LANES), lambda i: (i, 0)),
            pl.BlockSpec((1, LANES), lambda i: (0, 0)),
        ],
        out_shape=[
            jax.ShapeDtypeStruct((t, LANES), jnp.int32),
            jax.ShapeDtypeStruct((t, LANES), F32),
            jax.ShapeDtypeStruct((1, LANES), F32),
        ],
        scratch_shapes=[pltpu.VMEM((1, LANES), F32)],
        compiler_params=_params("arbitrary"), name="moe_router",
    )(x, g.reshape(1, d), wr)


def _row_copy(src, dst, s_row, d_row, sem):
    return pltpu.make_async_copy(src.at[pl.ds(s_row, 1)], dst.at[pl.ds(d_row, 1)], sem)


def _dispatch_kernel(pos_ref, x_ref, zeros_hbm, xs_hbm, sem, *, tm):
    del zeros_hbm
    base = pl.program_id(0) * tm
    n_tok = pl.num_programs(0) * tm

    def start(t, carry):
        for k in range(TOP_K):
            _row_copy(x_ref, xs_hbm, t, pos_ref[k * n_tok + base + t], sem).start()
        return carry

    def wait(t, carry):
        for k in range(TOP_K):
            _row_copy(x_ref, xs_hbm, t, pos_ref[k * n_tok + base + t], sem).wait()
        return carry

    lax.fori_loop(0, tm, start, 0)
    lax.fori_loop(0, tm, wait, 0)


def moe_dispatch(pos, x, n_rows, *, tm):
    t, d = x.shape
    zeros = jnp.zeros((n_rows, d), x.dtype)
    return pl.pallas_call(
        functools.partial(_dispatch_kernel, tm=tm),
        grid_spec=pltpu.PrefetchScalarGridSpec(
            num_scalar_prefetch=1, grid=(t // tm,),
            in_specs=[pl.BlockSpec((tm, d), lambda i, pos: (i, 0)), pl.BlockSpec(memory_space=pl.ANY)],
            out_specs=pl.BlockSpec(memory_space=pl.ANY),
            scratch_shapes=[pltpu.SemaphoreType.DMA(())],
        ),
        out_shape=jax.ShapeDtypeStruct((n_rows, d), x.dtype),
        input_output_aliases={2: 0},
        compiler_params=_params("arbitrary"),
        name="moe_dispatch",
    )(pos, x, zeros)


def _expert_kernel(te_ref, nu_ref, xs_ref, g_ref, w1_ref, w3_ref, w2_ref, o_ref, h_s, acc_s):
    i = pl.program_id(0)
    f = pl.program_id(1)
    used = i < nu_ref[0]

    @pl.when(used & (f == 0))
    def _():
        h_s[...] = _rms(xs_ref[...], g_ref[...]).astype(BF16)
        acc_s[...] = jnp.zeros_like(acc_s)

    @pl.when(used)
    def _():
        h = h_s[...]
        gate = _dot(h, w1_ref[0])
        up = _dot(h, w3_ref[0])
        act = (gate * jax.nn.sigmoid(gate) * up).astype(BF16)
        acc_s[...] += _dot(act, w2_ref[0])

    last = f == pl.num_programs(1) - 1

    @pl.when(used & last)
    def _():
        o_ref[...] = acc_s[...]

    @pl.when(jnp.logical_not(used) & last)
    def _():
        o_ref[...] = jnp.zeros_like(o_ref)


def moe_experts(tile_expert, n_used, xs, g, w13, w2, *, tm, tf):
    n_rows, d = xs.shape
    ff = w2.shape[1]
    nf = ff // tf
    n_tiles = n_rows // tm

    def chunk(i, f, nu):
        return jnp.where(i < nu[0], f, nf - 1)

    return pl.pallas_call(
        _expert_kernel,
        grid_spec=pltpu.PrefetchScalarGridSpec(
            num_scalar_prefetch=2, grid=(n_tiles, nf),
            in_specs=[
                pl.BlockSpec((tm, d), lambda i, f, te, nu: (i, 0)),
                pl.BlockSpec((1, d), lambda i, f, te, nu: (0, 0)),
                pl.BlockSpec((1, d, tf), lambda i, f, te, nu: (te[i], 0, chunk(i, f, nu))),
                pl.BlockSpec((1, d, tf), lambda i, f, te, nu: (te[i], 0, nf + chunk(i, f, nu))),
                pl.BlockSpec((1, tf, d), lambda i, f, te, nu: (te[i], chunk(i, f, nu), 0)),
            ],
            out_specs=pl.BlockSpec((tm, d), lambda i, f, te, nu: (i, 0)),
            scratch_shapes=[pltpu.VMEM((tm, d), BF16), pltpu.VMEM((tm, d), F32)],
        ),
        out_shape=jax.ShapeDtypeStruct((n_rows, d), F32),
        compiler_params=_params("arbitrary", "arbitrary"), name="moe_experts",
    )(tile_expert, n_used, xs, g.reshape(1, d), w13, w13, w2)


def _combine_kernel(pos_ref, x_ref, wts_ref, g_ref, y_hbm, o_ref, rows_s, sem, *, tm):
    base = pl.program_id(0) * tm
    n_tok = pl.num_programs(0) * tm

    def start(t, carry):
        for k in range(TOP_K):
            _row_copy(y_hbm, rows_s.at[k], pos_ref[k * n_tok + base + t], t, sem).start()
        return carry

    def wait(t, carry):
        for k in range(TOP_K):
            _row_copy(y_hbm, rows_s.at[k], pos_ref[k * n_tok + base + t], t, sem).wait()
        return carry

    lax.fori_loop(0, tm, start, 0)
    lax.fori_loop(0, tm, wait, 0)
    wts = wts_ref[...]
    y = x_ref[...]
    for k in range(TOP_K):
        y = y + wts[:, k:k + 1] * rows_s[k]
    o_ref[...] = _rms(y, g_ref[...])


def moe_combine(pos, x, wts, g, y, *, tm):
    t, d = x.shape
    return pl.pallas_call(
        functools.partial(_combine_kernel, tm=tm),
        grid_spec=pltpu.PrefetchScalarGridSpec(
            num_scalar_prefetch=1, grid=(t // tm,),
            in_specs=[
                pl.BlockSpec((tm, d), lambda i, pos: (i, 0)),
                pl.BlockSpec((tm, LANES), lambda i, pos: (i, 0)),
                pl.BlockSpec((1, d), lambda i, pos: (0, 0)),
                pl.BlockSpec(memory_space=pl.ANY),
            ],
            out_specs=pl.BlockSpec((tm, d), lambda i, pos: (i, 0)),
            scratch_shapes=[pltpu.VMEM((TOP_K, tm, d), F32), pltpu.SemaphoreType.DMA(())],
        ),
        out_shape=jax.ShapeDtypeStruct((t, d), F32),
        compiler_params=_params("arbitrary"), name="moe_combine",
    )(pos, x, wts, g.reshape(1, d), y)


def _pad_cols(w, n):
    return jnp.pad(w, ((0, 0), (0, n - w.shape[1])))


def _even_mixer(x, norm, w_in, b_f, sgu_norm, w_s, b_s, w_out, *, batch, seq, tq):
    n_heads = b_f.shape[0]
    a_width = n_heads * HEAD_DIM
    b_width = w_s.shape[0] * LANES
    f0 = 3 * a_width
    scale = HEAD_DIM ** -0.5
    w_main = jnp.concatenate([w_in[:, :a_width] * scale, w_in[:, a_width:f0], w_in[:, f0 + n_heads:]], axis=1)
    w_gate = _pad_cols(w_in[:, f0:f0 + n_heads], LANES)
    main, gate = norm_matmul(x, norm, [w_main.astype(BF16), w_gate.astype(BF16)], [BF16, F32],
                             tm=512, name="even_in_proj")
    c = gate_cumsum(gate, _pad_cols(b_f.reshape(1, -1), LANES), seq=seq)
    n_pairs = a_width // LANES
    nq = seq // tq
    ck = c[:, :n_heads].reshape(batch, seq, n_pairs, 2).transpose(0, 2, 3, 1).reshape(batch, n_pairs, 2, nq, tq)
    a = fox_attention(main, c, ck, batch=batch, seq=seq, tq=tq, q_col=0, k_col=n_pairs, v_col=2 * n_pairs,
                      n_pairs=n_pairs)
    u_col = f0 // b_width
    g = spatial_gating(main, sgu_norm, w_s, b_s, tb=512, u_col=u_col, v_col=u_col + 1)
    w_out = w_out.astype(BF16)
    return proj_residual(x, a, g, w_out[:a_width], w_out[a_width:], tm=512, name="even_out_proj")


def _odd_mixer(x, norm, w_in, conv_w, lq1, lk1, lq2, lk2, subln, w_out, rel_bias, lam_init, *, batch, seq, tq):
    c_width = conv_w.shape[1]
    d_width = rel_bias.shape[1] * 2 * HEAD_DIM
    q0 = 3 * c_width
    scale = HEAD_DIM ** -0.5
    w_main = jnp.concatenate([w_in[:, :q0], w_in[:, q0:q0 + d_width] * scale, w_in[:, q0 + d_width:]], axis=1)
    (main,) = norm_matmul(x, norm, [w_main.astype(BF16)], [BF16], tm=512, name="odd_in_proj")
    c_out = short_conv(main, conv_w, batch=batch, seq=seq)
    bias, far, lam = diff_prep(rel_bias, lq1, lk1, lq2, lk2, tq=tq, lam_init=lam_init)
    q_col = q0 // LANES
    n_heads = rel_bias.shape[1]
    d_out = diff_attention(main, bias, far, lam, subln, batch=batch, seq=seq, tq=tq, q_col=q_col,
                           k_col=q_col + n_heads, v_col=q_col + 2 * n_heads, lam_init=lam_init)
    w_out = w_out.astype(BF16)
    return proj_residual(x, c_out, d_out, w_out[:c_width], w_out[c_width:], tm=512, name="odd_out_proj")


def _moe_layer(x, norm, w_router, w13, w2, final_norm, *, tm_expert):
    t, d = x.shape
    n_exp = w_router.shape[1]
    meta, wts, counts = route_tokens(x, norm, _pad_cols(w_router, LANES), tm=512, n_exp=n_exp)
    counts = counts[0, :n_exp].astype(jnp.int32)
    padded = (counts + tm_expert - 1) // tm_expert * tm_expert
    ends = jnp.cumsum(padded)
    offsets = ends - padded
    pos = jnp.concatenate([offsets[meta[:, k]] + meta[:, TOP_K + k] for k in range(TOP_K)]).astype(jnp.int32)
    n_rows = TOP_K * t + n_exp * tm_expert
    n_tiles = n_rows // tm_expert
    tile_start = jnp.arange(n_tiles, dtype=jnp.int32) * tm_expert
    tile_expert = jnp.minimum(jnp.sum(tile_start[:, None] >= ends[None, :], axis=1), n_exp - 1).astype(jnp.int32)
    n_used = (ends[-1:] // tm_expert).astype(jnp.int32)
    xs = moe_dispatch(pos, x, n_rows, tm=512)
    y = moe_experts(tile_expert, n_used, xs, norm, w13.astype(BF16), w2.astype(BF16), tm=tm_expert, tf=512)
    return moe_combine(pos, x, wts, final_norm, y, tm=256)


def kernel(x, mem, rel_bias, mem_norm, final_norm, ev_norm, ev_w_in, ev_b_f, ev_sgu_norm, ev_w_s, ev_b_s, ev_w_out, ffn_w13, ffn_w2, od_norm, od_w_in, od_conv_w, od_lam_q1, od_lam_k1, od_lam_q2, od_lam_k2, od_subln, od_w_out, moe_router, moe_w13, moe_w2, x_norm, x_wq, x_wkv, x_wo, ffn_norm):
    batch, seq, d = x.shape
    mem_len = mem.shape[1]
    depth = x_norm.shape[0]
    assert depth == 2 and ev_norm.shape[0] == 1 and od_norm.shape[0] == 1
    x_heads, x_dh = 4, 128
    xf = x.reshape(batch * seq, d)
    wkv = jnp.concatenate([x_wkv[layer] for layer in range(depth)], axis=1).astype(BF16)
    (kv,) = norm_matmul(mem.reshape(batch * mem_len, d), mem_norm, [wkv], [BF16], tm=512, name="mem_kv")

    def cross(xf, layer):
        return cross_attention(xf, x_norm[layer], x_wq[layer].astype(BF16), kv, x_wo[layer].astype(BF16),
                               tm=512, seq=seq, mem_len=mem_len, kv_col=layer, n_heads=x_heads, dh=x_dh,
                               name=f"cross_attention_{layer}")

    xf = _even_mixer(xf, ev_norm[0], ev_w_in[0], ev_b_f[0], ev_sgu_norm[0], ev_w_s[0], ev_b_s[0], ev_w_out[0],
                     batch=batch, seq=seq, tq=256)
    xf = cross(xf, 0)
    xf = ffn_swiglu(xf, ffn_norm[0], ffn_w13[0].astype(BF16), ffn_w2[0].astype(BF16), tm=1024, tf=256,
                    name="ffn_swiglu")
    lam_init = 0.8 - 0.6 * math.exp(-0.3 * 1)
    xf = _odd_mixer(xf, od_norm[0], od_w_in[0], od_conv_w[0], od_lam_q1[0], od_lam_k1[0], od_lam_q2[0],
                    od_lam_k2[0], od_subln[0], od_w_out[0], rel_bias, lam_init, batch=batch, seq=seq, tq=256)
    xf = cross(xf, 1)
    out = _moe_layer(xf, ffn_norm[1], moe_router[0], moe_w13[0], moe_w2[0], final_norm, tm_expert=512)
    return out.reshape(batch, seq, d)
```

```python
import functools
import math

import jax
import jax.numpy as jnp
from jax import lax
from jax.experimental import pallas as pl
from jax.experimental.pallas import tpu as pltpu

F32 = jnp.float32
BF16 = jnp.bfloat16
EPS = 1e-6
HEAD_DIM = 64
LANES = 128
N_BUCKETS = 32
MAX_DIST = 128
TOP_K = 2
LOG2E = 1.4426950408889634
ATTN_TILE = 512
VMEM_LIMIT = 56 * 1024 * 1024


def _params(*sem):
    return pltpu.CompilerParams(dimension_semantics=sem, vmem_limit_bytes=VMEM_LIMIT)


def _rms(x, g):
    ms = jnp.mean(x * x, axis=-1, keepdims=True)
    return x * lax.rsqrt(ms + EPS) * g


def _dot(a, b):
    return jnp.dot(a, b, preferred_element_type=F32)


def _dot_nt(a, b):
    return lax.dot_general(a, b, (((1,), (1,)), ((), ())), preferred_element_type=F32)


def _norm_matmul_kernel(x_ref, g_ref, *refs, n_w, chunk, scaled):
    w_refs, o_refs = refs[:n_w], refs[n_w:]
    h = _rms(x_ref[...], g_ref[...]).astype(BF16)
    s0, s1, scale = scaled
    for k, (w_ref, o_ref) in enumerate(zip(w_refs, o_refs)):
        n = w_ref.shape[1]
        for c0 in range(0, n, chunk):
            c1 = min(c0 + chunk, n)
            y = _dot(h, w_ref[:, c0:c1])
            if k == 0 and s0 <= c0 and c1 <= s1:
                y = y * scale
            o_ref[:, c0:c1] = y.astype(o_ref.dtype)


def norm_matmul(x, g, ws, out_dtypes, *, tm, name, scaled=(0, 0, 1.0)):
    t, d = x.shape
    chunk = 512
    assert scaled[0] % chunk == 0 and scaled[1] % chunk == 0
    in_specs = [pl.BlockSpec((tm, d), lambda i: (i, 0)), pl.BlockSpec((1, d), lambda i: (0, 0))]
    in_specs += [pl.BlockSpec(w.shape, lambda i: (0, 0)) for w in ws]
    out_specs = [pl.BlockSpec((tm, w.shape[1]), lambda i: (i, 0)) for w in ws]
    out_shape = [jax.ShapeDtypeStruct((t, w.shape[1]), dt) for w, dt in zip(ws, out_dtypes)]
    return pl.pallas_call(
        functools.partial(_norm_matmul_kernel, n_w=len(ws), chunk=chunk, scaled=scaled),
        grid=(t // tm,), in_specs=in_specs, out_specs=out_specs, out_shape=out_shape,
        compiler_params=_params("parallel"), name=name,
    )(x, g.reshape(1, d), *ws)


def _gate_kernel(g_ref, b_ref, c_ref):
    s = g_ref.shape[0]
    row = lax.broadcasted_iota(jnp.int32, (LANES, LANES), 0)
    col = lax.broadcasted_iota(jnp.int32, (LANES, LANES), 1)
    tri = (row >= col).astype(F32)
    carry = jnp.zeros((1, LANES), F32)
    for blk in range(s // LANES):
        z = g_ref[blk * LANES:(blk + 1) * LANES, :] + b_ref[...]
        log_f = jnp.minimum(z, 0.0) - jnp.log1p(jnp.exp(-jnp.abs(z)))
        cs = jnp.dot(tri, log_f, precision=lax.Precision.HIGHEST, preferred_element_type=F32) + carry
        c_ref[blk * LANES:(blk + 1) * LANES, :] = cs
        carry = cs[LANES - 1:LANES, :]


def gate_cumsum(g, b, *, seq):
    t = g.shape[0]
    return pl.pallas_call(
        _gate_kernel, grid=(t // seq,),
        in_specs=[pl.BlockSpec((seq, LANES), lambda i: (i, 0)), pl.BlockSpec((1, LANES), lambda i: (0, 0))],
        out_specs=pl.BlockSpec((seq, LANES), lambda i: (i, 0)),
        out_shape=jax.ShapeDtypeStruct((t, LANES), F32),
        compiler_params=_params("parallel"), name="gate_cumsum",
    )(g, b)


def _split3(x):
    hi = x.astype(BF16).astype(F32)
    rest = x - hi
    mid = rest.astype(BF16).astype(F32)
    lo = (rest - mid).astype(BF16).astype(F32)
    return hi, mid, lo


def _augment(x, in_half, lane, base, pieces, pieces_first):
    n = len(pieces)
    p0, o0 = (base, base + n) if pieces_first else (base + n, base)
    aug = jnp.where((lane >= o0) & (lane < o0 + n), 1.0, 0.0)
    for idx, piece in enumerate(pieces):
        aug = jnp.where(lane == p0 + idx, piece, aug)
    return jnp.where(in_half, x, aug.astype(x.dtype))


def _halves(lane):
    return [(lane >= HEAD_DIM * hh) & (lane < HEAD_DIM * (hh + 1)) for hh in range(2)]


def _causal(tq):
    row = lax.broadcasted_iota(jnp.int32, (tq, tq), 0)
    col = lax.broadcasted_iota(jnp.int32, (tq, tq), 1)
    return row >= col


def _causal_attention(qa, ka_s, v_ref, bias_ref, i, tq, m_s, l_s, acc_s):
    n_chunks = tq // LANES

    def scores(h, j, near, mask):
        start = pl.multiple_of(j * tq, tq)
        s = _dot_nt(qa[h], ka_s[h, pl.ds(start, tq), :])
        if near is not None:
            s = s + bias_ref[0, near]
        if mask is not None:
            s = jnp.where(mask, s, -jnp.inf)
        return [s[:, c * LANES:(c + 1) * LANES] for c in range(n_chunks)]

    def max_tile(j, near, mask):
        for h in range(2):
            m = m_s[h]
            for chunk in scores(h, j, near, mask):
                m = jnp.maximum(m, chunk)
            m_s[h] = m

    def sum_tile(j, near, mask):
        v = v_ref[pl.ds(pl.multiple_of(j * tq, tq), tq), :]
        for h in range(2):
            m = m_s[h]
            ps = [jnp.exp2(chunk - m) for chunk in scores(h, j, near, mask)]
            l_s[h] += functools.reduce(lambda a, b: a + b, ps)
            acc_s[h] += _dot(jnp.concatenate(ps, axis=1).astype(BF16), v)

    def key_tiles(fn):
        def run(lo, hi, near):
            def body(j, carry):
                fn(j, near, None)
                return carry
            lax.fori_loop(lo, hi, body, 0)

        if bias_ref is None:
            run(0, i, None)
            fn(i, None, _causal(tq))
        else:
            n_far = jnp.maximum(i - 1, 0)
            run(0, n_far, None)
            run(n_far, i, 1)
            fn(i, 0, _causal(tq))

    m_s[...] = jnp.full(m_s.shape, -jnp.inf, F32)
    key_tiles(max_tile)
    for h in range(2):
        m_s[h] = jnp.broadcast_to(jnp.max(m_s[h], axis=1, keepdims=True), (tq, LANES))
    l_s[...] = jnp.zeros(l_s.shape, F32)
    acc_s[...] = jnp.zeros(acc_s.shape, F32)
    key_tiles(sum_tile)


def _normalised(l_s, acc_s, h):
    return acc_s[h] / jnp.sum(l_s[h], axis=1, keepdims=True)


def _fox_kernel(q_ref, k_ref, v_ref, c_ref, o_ref, ka_s, m_s, l_s, acc_s, *, tq):
    hp = pl.program_id(1)
    i = pl.program_id(2)
    lane = lax.broadcasted_iota(jnp.int32, (1, LANES), 1)
    halves = _halves(lane)

    def decay(c, hh):
        return jnp.sum(jnp.where(lane == 2 * hp + hh, c, 0.0), axis=1, keepdims=True) * LOG2E

    @pl.when(i == 0)
    def _():
        k = k_ref[...]
        c_all = c_ref[...]
        for hh in range(2):
            ka_s[hh] = _augment(k, halves[hh], lane, HEAD_DIM * (1 - hh), _split3(-decay(c_all, hh)), True)

    q = q_ref[...]
    c_q = c_ref[pl.ds(pl.multiple_of(i * tq, tq), tq), :]
    qa = [_augment(q, halves[hh], lane, HEAD_DIM * (1 - hh), _split3(decay(c_q, hh)), False) for hh in range(2)]
    _causal_attention(qa, ka_s, v_ref, None, i, tq, m_s, l_s, acc_s)
    out = jnp.where(lane < HEAD_DIM, _normalised(l_s, acc_s, 0), _normalised(l_s, acc_s, 1))
    o_ref[...] = out.astype(o_ref.dtype)


def _attn_scratch(seq, tq):
    return [pltpu.VMEM((2, seq, LANES), BF16)] + [pltpu.VMEM((2, tq, LANES), F32)] * 3


def fox_attention(qkv, c, *, batch, seq, tq, q_col, k_col, v_col, n_pairs):
    t = batch * seq
    nq = seq // tq
    return pl.pallas_call(
        functools.partial(_fox_kernel, tq=tq),
        grid=(batch, n_pairs, nq),
        in_specs=[
            pl.BlockSpec((tq, LANES), lambda b, h, i: (b * nq + i, q_col + h)),
            pl.BlockSpec((seq, LANES), lambda b, h, i: (b, k_col + h)),
            pl.BlockSpec((seq, LANES), lambda b, h, i: (b, v_col + h)),
            pl.BlockSpec((seq, LANES), lambda b, h, i: (b, 0)),
        ],
        out_specs=pl.BlockSpec((tq, LANES), lambda b, h, i: (b * nq + i, h)),
        out_shape=jax.ShapeDtypeStruct((t, n_pairs * LANES), BF16),
        scratch_shapes=_attn_scratch(seq, tq),
        compiler_params=_params("parallel", "parallel", "arbitrary"), name="fox_attention",
    )(qkv, qkv, qkv, c)


def _sgu_kernel(u_ref, v_ref, norm_ref, ws_ref, bs_ref, o_ref, *, n_groups, chunk):
    tb = u_ref.shape[0]
    row = lax.broadcasted_iota(jnp.int32, (chunk, chunk), 0)
    col = lax.broadcasted_iota(jnp.int32, (chunk, chunk), 1)
    tri = row >= col
    for g in range(n_groups):
        w = jnp.where(tri, ws_ref[g], 0.0).astype(BF16)
        bias = bs_ref[:, g:g + 1]
        gain = norm_ref[g:g + 1, :]
        for c in range(tb // chunk):
            rs = slice(c * chunk, (c + 1) * chunk)
            cs = slice(g * LANES, (g + 1) * LANES)
            vn = _rms(jax.nn.gelu(v_ref[rs, cs].astype(F32)), gain)
            mixed = _dot(w, vn.astype(BF16)) + bias
            o_ref[rs, cs] = (jax.nn.gelu(u_ref[rs, cs].astype(F32)) * mixed).astype(o_ref.dtype)


def spatial_gating(main, sgu_norm, w_s, b_s, *, tb, u_col, v_col):
    t = main.shape[0]
    n_groups, chunk, _ = w_s.shape
    width = n_groups * LANES
    return pl.pallas_call(
        functools.partial(_sgu_kernel, n_groups=n_groups, chunk=chunk),
        grid=(t // tb,),
        in_specs=[
            pl.BlockSpec((tb, width), lambda i: (i, u_col)),
            pl.BlockSpec((tb, width), lambda i: (i, v_col)),
            pl.BlockSpec(sgu_norm.shape, lambda i: (0, 0)),
            pl.BlockSpec(w_s.shape, lambda i: (0, 0, 0)),
            pl.BlockSpec((chunk, n_groups), lambda i: (0, 0)),
        ],
        out_specs=pl.BlockSpec((tb, width), lambda i: (i, 0)),
        out_shape=jax.ShapeDtypeStruct((t, width), BF16),
        compiler_params=_params("parallel"), name="spatial_gating",
    )(main, main, sgu_norm, w_s, b_s.T)


def _proj_res_kernel(x_ref, a_ref, b_ref, wa_ref, wb_ref, o_ref):
    o_ref[...] = x_ref[...] + _dot(a_ref[...], wa_ref[...]) + _dot(b_ref[...], wb_ref[...])


def proj_residual(x, a, b, wa, wb, *, tm, name):
    t, d = x.shape
    return pl.pallas_call(
        _proj_res_kernel, grid=(t // tm,),
        in_specs=[
            pl.BlockSpec((tm, d), lambda i: (i, 0)),
            pl.BlockSpec((tm, a.shape[1]), lambda i: (i, 0)),
            pl.BlockSpec((tm, b.shape[1]), lambda i: (i, 0)),
            pl.BlockSpec(wa.shape, lambda i: (0, 0)),
            pl.BlockSpec(wb.shape, lambda i: (0, 0)),
        ],
        out_specs=pl.BlockSpec((tm, d), lambda i: (i, 0)),
        out_shape=jax.ShapeDtypeStruct((t, d), F32),
        compiler_params=_params("parallel"), name=name,
    )(x, a, b, wa, wb)


def _cross_kernel(x_ref, g_ref, wq_ref, kv_ref, wo_ref, o_ref, *, n_heads, dh):
    x = x_ref[...]
    h = _rms(x, g_ref[...]).astype(BF16)
    q = _dot(h, wq_ref[...]).astype(BF16)
    width = n_heads * dh
    outs = []
    for hd in range(n_heads):
        cs = slice(hd * dh, (hd + 1) * dh)
        s = _dot_nt(q[:, cs], kv_ref[:, cs]) * (dh ** -0.5)
        m = jnp.max(s, axis=1, keepdims=True)
        p = jnp.exp(s - m)
        p = p / jnp.sum(p, axis=1, keepdims=True)
        outs.append(_dot(p.astype(BF16), kv_ref[:, width + hd * dh:width + (hd + 1) * dh]).astype(BF16))
    o = jnp.concatenate(outs, axis=1)
    o_ref[...] = x + _dot(o, wo_ref[...])


def cross_attention(x, g, wq, kv, wo, *, tm, seq, mem_len, kv_col, n_heads, dh, name):
    t, d = x.shape
    per_b = seq // tm
    width = n_heads * dh
    return pl.pallas_call(
        functools.partial(_cross_kernel, n_heads=n_heads, dh=dh),
        grid=(t // tm,),
        in_specs=[
            pl.BlockSpec((tm, d), lambda i: (i, 0)),
            pl.BlockSpec((1, d), lambda i: (0, 0)),
            pl.BlockSpec(wq.shape, lambda i: (0, 0)),
            pl.BlockSpec((mem_len, 2 * width), lambda i: (i // per_b, kv_col)),
            pl.BlockSpec(wo.shape, lambda i: (0, 0)),
        ],
        out_specs=pl.BlockSpec((tm, d), lambda i: (i, 0)),
        out_shape=jax.ShapeDtypeStruct((t, d), F32),
        compiler_params=_params("parallel"), name=name,
    )(x, g.reshape(1, d), wq, kv, wo)


def _ffn_kernel(x_ref, g_ref, w1_ref, w3_ref, w2_ref, o_ref, h_s, acc_s):
    f = pl.program_id(1)

    @pl.when(f == 0)
    def _():
        h_s[...] = _rms(x_ref[...], g_ref[...]).astype(BF16)
        acc_s[...] = x_ref[...]

    h = h_s[...]
    gate = _dot(h, w1_ref[...])
    up = _dot(h, w3_ref[...])
    act = (gate * jax.nn.sigmoid(gate) * up).astype(BF16)
    acc_s[...] += _dot(act, w2_ref[...])

    @pl.when(f == pl.num_programs(1) - 1)
    def _():
        o_ref[...] = acc_s[...]


def ffn_swiglu(x, g, w13, w2, *, tm, tf, name):
    t, d = x.shape
    ff = w2.shape[0]
    nf = ff // tf
    return pl.pallas_call(
        _ffn_kernel, grid=(t // tm, nf),
        in_specs=[
            pl.BlockSpec((tm, d), lambda i, f: (i, 0)),
            pl.BlockSpec((1, d), lambda i, f: (0, 0)),
            pl.BlockSpec((d, tf), lambda i, f: (0, f)),
            pl.BlockSpec((d, tf), lambda i, f: (0, nf + f)),
            pl.BlockSpec((tf, d), lambda i, f: (f, 0)),
        ],
        out_specs=pl.BlockSpec((tm, d), lambda i, f: (i, 0)),
        out_shape=jax.ShapeDtypeStruct((t, d), F32),
        scratch_shapes=[pltpu.VMEM((tm, d), BF16), pltpu.VMEM((tm, d), F32)],
        compiler_params=_params("parallel", "arbitrary"), name=name,
    )(x, g.reshape(1, d), w13, w13, w2)


def _conv_kernel(bg_ref, cg_ref, xi_ref, w_ref, o_ref):
    s, width = o_ref.shape
    n_taps = w_ref.shape[0]
    xc = cg_ref[...].astype(F32) * xi_ref[...].astype(F32)
    row = lax.broadcasted_iota(jnp.int32, (s, width), 0)
    y = w_ref[n_taps - 1:n_taps, :] * xc
    for back in range(1, n_taps):
        shifted = jnp.where(row >= back, pltpu.roll(xc, back, axis=0), 0.0)
        y = y + w_ref[n_taps - 1 - back:n_taps - back, :] * shifted
    o_ref[...] = (bg_ref[...].astype(F32) * y).astype(o_ref.dtype)


def short_conv(main, conv_w, *, batch, seq):
    width = conv_w.shape[1]
    return pl.pallas_call(
        _conv_kernel, grid=(batch,),
        in_specs=[
            pl.BlockSpec((seq, width), lambda b: (b, 0)),
            pl.BlockSpec((seq, width), lambda b: (b, 1)),
            pl.BlockSpec((seq, width), lambda b: (b, 2)),
            pl.BlockSpec(conv_w.shape, lambda b: (0, 0)),
        ],
        out_specs=pl.BlockSpec((seq, width), lambda b: (b, 0)),
        out_shape=jax.ShapeDtypeStruct((batch * seq, width), BF16),
        compiler_params=_params("parallel"), name="short_conv",
    )(main, main, main, conv_w)


def _diff_prep_kernel(rb_ref, lq1_ref, lk1_ref, lq2_ref, lk2_ref, bias_ref, far_ref, lam_ref, *, tq, lam_init):
    n_heads = bias_ref.shape[0]
    row = lax.broadcasted_iota(jnp.int32, (tq, tq), 0)
    col = lax.broadcasted_iota(jnp.int32, (tq, tq), 1)
    max_exact = N_BUCKETS // 2
    for which in range(2):
        n = jnp.maximum(row - col + which * tq, 0)
        nf = jnp.maximum(n, 1).astype(F32)
        large = max_exact + (jnp.log(nf / max_exact) / math.log(MAX_DIST / max_exact)
                             * (N_BUCKETS - max_exact)).astype(jnp.int32)
        large = jnp.minimum(large, N_BUCKETS - 1)
        bucket = jnp.where(n < max_exact, n, large)
        for h in range(n_heads):
            b = jnp.zeros((tq, tq), F32)
            for kk in range(N_BUCKETS):
                b = jnp.where(bucket == kk, rb_ref[kk, h], b)
            bias_ref[h, which] = (b - rb_ref[N_BUCKETS - 1, h]) * LOG2E
    for h in range(n_heads):
        far_ref[h] = jnp.full((1, LANES), rb_ref[N_BUCKETS - 1, h], F32) * LOG2E
    lam = (jnp.exp(jnp.sum(lq1_ref[...] * lk1_ref[...], axis=1, keepdims=True))
           - jnp.exp(jnp.sum(lq2_ref[...] * lk2_ref[...], axis=1, keepdims=True)) + lam_init)
    lam_ref[...] = jnp.broadcast_to(lam, (1, LANES))


def diff_prep(rel_bias, lq1, lk1, lq2, lk2, *, tq, lam_init):
    n_heads = rel_bias.shape[1]
    vec = lambda a: a.reshape(1, -1)
    vspec = pl.BlockSpec(memory_space=pltpu.VMEM)
    return pl.pallas_call(
        functools.partial(_diff_prep_kernel, tq=tq, lam_init=lam_init),
        in_specs=[pl.BlockSpec(memory_space=pltpu.SMEM), vspec, vspec, vspec, vspec],
        out_specs=[vspec, vspec, vspec],
        out_shape=[
            jax.ShapeDtypeStruct((n_heads, 2, tq, tq), F32),
            jax.ShapeDtypeStruct((n_heads, 1, LANES), F32),
            jax.ShapeDtypeStruct((1, LANES), F32),
        ],
        compiler_params=pltpu.CompilerParams(vmem_limit_bytes=VMEM_LIMIT), name="diff_prep",
    )(rel_bias, vec(lq1), vec(lk1), vec(lq2), vec(lk2))


def _diff_kernel(q_ref, k_ref, v_ref, bias_ref, far_ref, lam_ref, subln_ref, o_ref, ka_s, m_s, l_s, acc_s, *,
                 tq, lam_init):
    i = pl.program_id(2)
    lane = lax.broadcasted_iota(jnp.int32, (1, LANES), 1)
    halves = _halves(lane)

    @pl.when(i == 0)
    def _():
        k = k_ref[...]
        far = _split3(far_ref[0])
        for sub in range(2):
            ka_s[sub] = _augment(k, halves[sub], lane, HEAD_DIM * (1 - sub), far, True)

    q = q_ref[...]
    zero = jnp.zeros((1, LANES), F32)
    qa = [_augment(q, halves[sub], lane, HEAD_DIM * (1 - sub), (zero,) * 3, False) for sub in range(2)]
    _causal_attention(qa, ka_s, v_ref, bias_ref, i, tq, m_s, l_s, acc_s)
    o = _normalised(l_s, acc_s, 0) - lam_ref[...] * _normalised(l_s, acc_s, 1)
    o_ref[...] = (_rms(o, subln_ref[...]) * (1.0 - lam_init)).astype(o_ref.dtype)


def diff_attention(main, bias, far, lam, subln, *, batch, seq, tq, q_col, k_col, v_col, lam_init):
    t = batch * seq
    nq = seq // tq
    n_heads = bias.shape[0]
    return pl.pallas_call(
        functools.partial(_diff_kernel, tq=tq, lam_init=lam_init),
        grid=(batch, n_heads, nq),
        in_specs=[
            pl.BlockSpec((tq, LANES), lambda b, h, i: (b * nq + i, q_col + h)),
            pl.BlockSpec((seq, LANES), lambda b, h, i: (b, k_col + h)),
            pl.BlockSpec((seq, LANES), lambda b, h, i: (b, v_col + h)),
            pl.BlockSpec((1, 2, tq, tq), lambda b, h, i: (h, 0, 0, 0)),
            pl.BlockSpec((1, 1, LANES), lambda b, h, i: (h, 0, 0)),
            pl.BlockSpec((1, LANES), lambda b, h, i: (0, 0)),
            pl.BlockSpec((1, LANES), lambda b, h, i: (0, 0)),
        ],
        out_specs=pl.BlockSpec((tq, LANES), lambda b, h, i: (b * nq + i, h)),
        out_shape=jax.ShapeDtypeStruct((t, n_heads * LANES), BF16),
        scratch_shapes=_attn_scratch(seq, tq),
        compiler_params=_params("parallel", "parallel", "arbitrary"), name="diff_attention",
    )(main, main, main, bias, far, lam, subln.reshape(1, LANES))


def _router_kernel(x_ref, g_ref, wr_ref, meta_ref, wts_ref, cnt_ref, carry_s, *, n_exp):
    tm = x_ref.shape[0]

    @pl.when(pl.program_id(0) == 0)
    def _():
        carry_s[...] = jnp.zeros_like(carry_s)

    h = _rms(x_ref[...], g_ref[...])
    logits = jnp.dot(h, wr_ref[...], precision=lax.Precision.HIGHEST, preferred_element_type=F32)
    lane = lax.broadcasted_iota(jnp.int32, (tm, LANES), 1)
    lane_f = lane.astype(F32)
    logits = jnp.where(lane < n_exp, logits, -jnp.inf)
    m1 = jnp.max(logits, axis=1, keepdims=True)
    i1 = jnp.min(jnp.where(logits == m1, lane_f, float(LANES)), axis=1, keepdims=True)
    rest = jnp.where(lane_f == i1, -jnp.inf, logits)
    m2 = jnp.max(rest, axis=1, keepdims=True)
    i2 = jnp.min(jnp.where(rest == m2, lane_f, float(LANES)), axis=1, keepdims=True)
    e = jnp.exp(m2 - m1)
    w1 = 1.0 / (1.0 + e)
    w2 = e / (1.0 + e)
    sel1 = lane_f == i1
    sel2 = lane_f == i2
    onehot = jnp.where(sel1 | sel2, 1.0, 0.0)
    row = lax.broadcasted_iota(jnp.int32, (tm, tm), 0)
    col = lax.broadcasted_iota(jnp.int32, (tm, tm), 1)
    before = jnp.where(row > col, 1.0, 0.0).astype(BF16)
    rank = _dot(before, onehot.astype(BF16)) + carry_s[...]
    r1 = jnp.sum(jnp.where(sel1, rank, 0.0), axis=1, keepdims=True)
    r2 = jnp.sum(jnp.where(sel2, rank, 0.0), axis=1, keepdims=True)
    meta = jnp.where(lane == 0, i1, jnp.where(lane == 1, i2, jnp.where(lane == 2, r1, jnp.where(lane == 3, r2, 0.0))))
    meta_ref[...] = meta.astype(jnp.int32)
    wts_ref[...] = jnp.where(lane == 0, w1, jnp.where(lane == 1, w2, 0.0))
    carry_s[...] += jnp.sum(onehot, axis=0, keepdims=True)
    cnt_ref[...] = carry_s[...]


def route_tokens(x, g, wr, *, tm, n_exp):
    t, d = x.shape
    return pl.pallas_call(
        functools.partial(_router_kernel, n_exp=n_exp),
        grid=(t // tm,),
        in_specs=[
            pl.BlockSpec((tm, d), lambda i: (i, 0)),
            pl.BlockSpec((1, d), lambda i: (0, 0)),
            pl.BlockSpec((d, LANES), lambda i: (0, 0)),
        ],
        out_specs=[
            pl.BlockSpec((tm, LANES), lambda i: (i, 0)),
            pl.BlockSpec((tm, LANES), lambda i: (i, 0)),
            pl.BlockSpec((1, LANES), lambda i: (0, 0)),
        ],
        out_shape=[
            jax.ShapeDtypeStruct((t, LANES), jnp.int32),
            jax.ShapeDtypeStruct((t, LANES), F32),
            jax.ShapeDtypeStruct((1, LANES), F32),
        ],
        scratch_shapes=[pltpu.VMEM((1, LANES), F32)],
        compiler_params=_params("arbitrary"), name="moe_router",
    )(x, g.reshape(1, d), wr)


def _row_copy(src, dst, s_row, d_row, sem):
    return pltpu.make_async_copy(src.at[pl.ds(s_row, 1)], dst.at[pl.ds(d_row, 1)], sem)


def _dispatch_kernel(pos_ref, x_ref, zeros_hbm, xs_hbm, sem, *, tm):
    del zeros_hbm
    base = pl.program_id(0) * tm
    n_tok = pl.num_programs(0) * tm

    def start(t, carry):
        for k in range(TOP_K):
            _row_copy(x_ref, xs_hbm, t, pos_ref[k * n_tok + base + t], sem).start()
        return carry

    def wait(t, carry):
        for k in range(TOP_K):
            _row_copy(x_ref, xs_hbm, t, pos_ref[k * n_tok + base + t], sem).wait()
        return carry

    lax.fori_loop(0, tm, start, 0)
    lax.fori_loop(0, tm, wait, 0)


def moe_dispatch(pos, x, n_rows, *, tm):
    t, d = x.shape
    zeros = jnp.zeros((n_rows, d), x.dtype)
    return pl.pallas_call(
        functools.partial(_dispatch_kernel, tm=tm),
        grid_spec=pltpu.PrefetchScalarGridSpec(
            num_scalar_prefetch=1, grid=(t // tm,),
            in_specs=[pl.BlockSpec((tm, d), lambda i, pos: (i, 0)), pl.BlockSpec(memory_space=pl.ANY)],
            out_specs=pl.BlockSpec(memory_space=pl.ANY),
            scratch_shapes=[pltpu.SemaphoreType.DMA(())],
        ),
        out_shape=jax.ShapeDtypeStruct((n_rows, d), x.dtype),
        input_output_aliases={2: 0},
        compiler_params=_params("arbitrary"),
        name="moe_dispatch",
    )(pos, x, zeros)


def _expert_kernel(te_ref, nu_ref, xs_ref, g_ref, w1_ref, w3_ref, w2_ref, o_ref, h_s, acc_s):
    i = pl.program_id(0)
    f = pl.program_id(1)
    used = i < nu_ref[0]

    @pl.when(used & (f == 0))
    def _():
        h_s[...] = _rms(xs_ref[...], g_ref[...]).astype(BF16)
        acc_s[...] = jnp.zeros_like(acc_s)

    @pl.when(used)
    def _():
        h = h_s[...]
        gate = _dot(h, w1_ref[0])
        up = _dot(h, w3_ref[0])
        act = (gate * jax.nn.sigmoid(gate) * up).astype(BF16)
        acc_s[...] += _dot(act, w2_ref[0])

    last = f == pl.num_programs(1) - 1

    @pl.when(used & last)
    def _():
        o_ref[...] = acc_s[...]

    @pl.when(jnp.logical_not(used) & last)
    def _():
        o_ref[...] = jnp.zeros_like(o_ref)


def moe_experts(tile_expert, n_used, xs, g, w13, w2, *, tm, tf):
    n_rows, d = xs.shape
    ff = w2.shape[1]
    nf = ff // tf
    n_tiles = n_rows // tm

    def chunk(i, f, nu):
        return jnp.where(i < nu[0], f, nf - 1)

    return pl.pallas_call(
        _expert_kernel,
        grid_spec=pltpu.PrefetchScalarGridSpec(
            num_scalar_prefetch=2, grid=(n_tiles, nf),
            in_specs=[
                pl.BlockSpec((tm, d), lambda i, f, te, nu: (i, 0)),
                pl.BlockSpec((1, d), lambda i, f, te, nu: (0, 0)),
                pl.BlockSpec((1, d, tf), lambda i, f, te, nu: (te[i], 0, chunk(i, f, nu))),
                pl.BlockSpec((1, d, tf), lambda i, f, te, nu: (te[i], 0, nf + chunk(i, f, nu))),
                pl.BlockSpec((1, tf, d), lambda i, f, te, nu: (te[i], chunk(i, f, nu), 0)),
            ],
            out_specs=pl.BlockSpec((tm, d), lambda i, f, te, nu: (i, 0)),
            scratch_shapes=[pltpu.VMEM((tm, d), BF16), pltpu.VMEM((tm, d), F32)],
        ),
        out_shape=jax.ShapeDtypeStruct((n_rows, d), F32),
        compiler_params=_params("arbitrary", "arbitrary"), name="moe_experts",
    )(tile_expert, n_used, xs, g.reshape(1, d), w13, w13, w2)


def _combine_kernel(pos_ref, x_ref, wts_ref, g_ref, y_hbm, o_ref, rows_s, sem, *, tm):
    base = pl.program_id(0) * tm
    n_tok = pl.num_programs(0) * tm

    def start(t, carry):
        for k in range(TOP_K):
            _row_copy(y_hbm, rows_s.at[k], pos_ref[k * n_tok + base + t], t, sem).start()
        return carry

    def wait(t, carry):
        for k in range(TOP_K):
            _row_copy(y_hbm, rows_s.at[k], pos_ref[k * n_tok + base + t], t, sem).wait()
        return carry

    lax.fori_loop(0, tm, start, 0)
    lax.fori_loop(0, tm, wait, 0)
    wts = wts_ref[...]
    y = x_ref[...]
    for k in range(TOP_K):
        y = y + wts[:, k:k + 1] * rows_s[k]
    o_ref[...] = _rms(y, g_ref[...])


def moe_combine(pos, x, wts, g, y, *, tm):
    t, d = x.shape
    return pl.pallas_call(
        functools.partial(_combine_kernel, tm=tm),
        grid_spec=pltpu.PrefetchScalarGridSpec(
            num_scalar_prefetch=1, grid=(t // tm,),
            in_specs=[
                pl.BlockSpec((tm, d), lambda i, pos: (i, 0)),
                pl.BlockSpec((tm, LANES), lambda i, pos: (i, 0)),
                pl.BlockSpec((1, d), lambda i, pos: (0, 0)),
                pl.BlockSpec(memory_space=pl.ANY),
            ],
            out_specs=pl.BlockSpec((tm, d), lambda i, pos: (i, 0)),
            scratch_shapes=[pltpu.VMEM((TOP_K, tm, d), F32), pltpu.SemaphoreType.DMA(())],
        ),
        out_shape=jax.ShapeDtypeStruct((t, d), F32),
        compiler_params=_params("arbitrary"), name="moe_combine",
    )(pos, x, wts, g.reshape(1, d), y)


def _pad_cols(w, n):
    return jnp.pad(w, ((0, 0), (0, n - w.shape[1])))


def _even_mixer(x, norm, w_in, b_f, sgu_norm, w_s, b_s, w_out, *, batch, seq, tq):
    n_heads = b_f.shape[0]
    a_width = n_heads * HEAD_DIM
    b_width = w_s.shape[0] * LANES
    f0 = 3 * a_width
    w_main = jnp.concatenate([w_in[:, :f0], w_in[:, f0 + n_heads:]], axis=1)
    w_gate = _pad_cols(w_in[:, f0:f0 + n_heads], LANES)
    main, gate = norm_matmul(x, norm, [w_main.astype(BF16), w_gate.astype(BF16)], [BF16, F32],
                             tm=512, name="even_in_proj", scaled=(0, a_width, LOG2E * HEAD_DIM ** -0.5))
    c = gate_cumsum(gate, _pad_cols(b_f.reshape(1, -1), LANES), seq=seq)
    n_pairs = a_width // LANES
    a = fox_attention(main, c, batch=batch, seq=seq, tq=tq, q_col=0, k_col=n_pairs, v_col=2 * n_pairs,
                      n_pairs=n_pairs)
    u_col = f0 // b_width
    g = spatial_gating(main, sgu_norm, w_s, b_s, tb=512, u_col=u_col, v_col=u_col + 1)
    w_out = w_out.astype(BF16)
    return proj_residual(x, a, g, w_out[:a_width], w_out[a_width:], tm=512, name="even_out_proj")


def _odd_mixer(x, norm, w_in, conv_w, lq1, lk1, lq2, lk2, subln, w_out, rel_bias, lam_init, *, batch, seq, tq):
    c_width = conv_w.shape[1]
    d_width = rel_bias.shape[1] * 2 * HEAD_DIM
    q0 = 3 * c_width
    (main,) = norm_matmul(x, norm, [w_in.astype(BF16)], [BF16], tm=512, name="odd_in_proj",
                          scaled=(q0, q0 + d_width, LOG2E * HEAD_DIM ** -0.5))
    c_out = short_conv(main, conv_w, batch=batch, seq=seq)
    bias, far, lam = diff_prep(rel_bias, lq1, lk1, lq2, lk2, tq=tq, lam_init=lam_init)
    q_col = q0 // LANES
    n_heads = rel_bias.shape[1]
    d_out = diff_attention(main, bias, far, lam, subln, batch=batch, seq=seq, tq=tq, q_col=q_col,
                           k_col=q_col + n_heads, v_col=q_col + 2 * n_heads, lam_init=lam_init)
    w_out = w_out.astype(BF16)
    return proj_residual(x, c_out, d_out, w_out[:c_width], w_out[c_width:], tm=512, name="odd_out_proj")


def _moe_layer(x, norm, w_router, w13, w2, final_norm, *, tm_expert):
    t, d = x.shape
    n_exp = w_router.shape[1]
    meta, wts, counts = route_tokens(x, norm, _pad_cols(w_router, LANES), tm=512, n_exp=n_exp)
    counts = counts[0, :n_exp].astype(jnp.int32)
    padded = (counts + tm_expert - 1) // tm_expert * tm_expert
    ends = jnp.cumsum(padded)
    offsets = ends - padded
    pos = jnp.concatenate([offsets[meta[:, k]] + meta[:, TOP_K + k] for k in range(TOP_K)]).astype(jnp.int32)
    n_rows = TOP_K * t + n_exp * tm_expert
    n_tiles = n_rows // tm_expert
    tile_start = jnp.arange(n_tiles, dtype=jnp.int32) * tm_expert
    tile_expert = jnp.minimum(jnp.sum(tile_start[:, None] >= ends[None, :], axis=1), n_exp - 1).astype(jnp.int32)
    n_used = (ends[-1:] // tm_expert).astype(jnp.int32)
    xs = moe_dispatch(pos, x, n_rows, tm=512)
    y = moe_experts(tile_expert, n_used, xs, norm, w13.astype(BF16), w2.astype(BF16), tm=tm_expert, tf=512)
    return moe_combine(pos, x, wts, final_norm, y, tm=256)


def kernel(x, mem, rel_bias, mem_norm, final_norm, ev_norm, ev_w_in, ev_b_f, ev_sgu_norm, ev_w_s, ev_b_s, ev_w_out, ffn_w13, ffn_w2, od_norm, od_w_in, od_conv_w, od_lam_q1, od_lam_k1, od_lam_q2, od_lam_k2, od_subln, od_w_out, moe_router, moe_w13, moe_w2, x_norm, x_wq, x_wkv, x_wo, ffn_norm):
    batch, seq, d = x.shape
    mem_len = mem.shape[1]
    depth = x_norm.shape[0]
    assert depth == 2 and ev_norm.shape[0] == 1 and od_norm.shape[0] == 1
    x_heads, x_dh = 4, 128
    xf = x.reshape(batch * seq, d)
    wkv = jnp.concatenate([x_wkv[layer] for layer in range(depth)], axis=1).astype(BF16)
    (kv,) = norm_matmul(mem.reshape(batch * mem_len, d), mem_norm, [wkv], [BF16], tm=512, name="mem_kv")

    def cross(xf, layer):
        return cross_attention(xf, x_norm[layer], x_wq[layer].astype(BF16), kv, x_wo[layer].astype(BF16),
                               tm=512, seq=seq, mem_len=mem_len, kv_col=layer, n_heads=x_heads, dh=x_dh,
                               name=f"cross_attention_{layer}")

    xf = _even_mixer(xf, ev_norm[0], ev_w_in[0], ev_b_f[0], ev_sgu_norm[0], ev_w_s[0], ev_b_s[0], ev_w_out[0],
                     batch=batch, seq=seq, tq=ATTN_TILE)
    xf = cross(xf, 0)
    xf = ffn_swiglu(xf, ffn_norm[0], ffn_w13[0].astype(BF16), ffn_w2[0].astype(BF16), tm=1024, tf=256,
                    name="ffn_swiglu")
    lam_init = 0.8 - 0.6 * math.exp(-0.3 * 1)
    xf = _odd_mixer(xf, od_norm[0], od_w_in[0], od_conv_w[0], od_lam_q1[0], od_lam_k1[0], od_lam_q2[0],
                    od_lam_k2[0], od_subln[0], od_w_out[0], rel_bias, lam_init, batch=batch, seq=seq, tq=ATTN_TILE)
    xf = cross(xf, 1)
    out = _moe_layer(xf, ffn_norm[1], moe_router[0], moe_w13[0], moe_w2[0], final_norm, tm_expert=512)
    return out.reshape(batch, seq, d)
```

```python
import functools
import math

import jax
import jax.numpy as jnp
from jax import lax
from jax.experimental import pallas as pl
from jax.experimental.pallas import tpu as pltpu

F32 = jnp.float32
BF16 = jnp.bfloat16
EPS = 1e-6
HEAD_DIM = 64
LANES = 128
N_BUCKETS = 32
MAX_DIST = 128
TOP_K = 2
LOG2E = 1.4426950408889634
ATTN_TILE = 512
VMEM_LIMIT = 56 * 1024 * 1024


def _params(*sem):
    return pltpu.CompilerParams(dimension_semantics=sem, vmem_limit_bytes=VMEM_LIMIT)


def _rms(x, g):
    ms = jnp.mean(x * x, axis=-1, keepdims=True)
    return x * lax.rsqrt(ms + EPS) * g


def _dot(a, b):
    return jnp.dot(a, b, preferred_element_type=F32)


def _dot_nt(a, b):
    return lax.dot_general(a, b, (((1,), (1,)), ((), ())), preferred_element_type=F32)


def _norm_matmul_kernel(x_ref, g_ref, *refs, n_w, chunk, scaled):
    w_refs, o_refs = refs[:n_w], refs[n_w:]
    h = _rms(x_ref[...], g_ref[...]).astype(BF16)
    s0, s1, scale = scaled
    for k, (w_ref, o_ref) in enumerate(zip(w_refs, o_refs)):
        n = w_ref.shape[1]
        for c0 in range(0, n, chunk):
            c1 = min(c0 + chunk, n)
            y = _dot(h, w_ref[:, c0:c1])
            if k == 0 and s0 <= c0 and c1 <= s1:
                y = y * scale
            o_ref[:, c0:c1] = y.astype(o_ref.dtype)


def norm_matmul(x, g, ws, out_dtypes, *, tm, name, scaled=(0, 0, 1.0)):
    t, d = x.shape
    chunk = 512
    assert scaled[0] % chunk == 0 and scaled[1] % chunk == 0
    in_specs = [pl.BlockSpec((tm, d), lambda i: (i, 0)), pl.BlockSpec((1, d), lambda i: (0, 0))]
    in_specs += [pl.BlockSpec(w.shape, lambda i: (0, 0)) for w in ws]
    out_specs = [pl.BlockSpec((tm, w.shape[1]), lambda i: (i, 0)) for w in ws]
    out_shape = [jax.ShapeDtypeStruct((t, w.shape[1]), dt) for w, dt in zip(ws, out_dtypes)]
    return pl.pallas_call(
        functools.partial(_norm_matmul_kernel, n_w=len(ws), chunk=chunk, scaled=scaled),
        grid=(t // tm,), in_specs=in_specs, out_specs=out_specs, out_shape=out_shape,
        compiler_params=_params("parallel"), name=name,
    )(x, g.reshape(1, d), *ws)


def _gate_kernel(g_ref, b_ref, c_ref):
    s = g_ref.shape[0]
    row = lax.broadcasted_iota(jnp.int32, (LANES, LANES), 0)
    col = lax.broadcasted_iota(jnp.int32, (LANES, LANES), 1)
    tri = (row >= col).astype(F32)
    carry = jnp.zeros((1, LANES), F32)
    for blk in range(s // LANES):
        z = g_ref[blk * LANES:(blk + 1) * LANES, :] + b_ref[...]
        log_f = jnp.minimum(z, 0.0) - jnp.log1p(jnp.exp(-jnp.abs(z)))
        cs = jnp.dot(tri, log_f, precision=lax.Precision.HIGHEST, preferred_element_type=F32) + carry
        c_ref[blk * LANES:(blk + 1) * LANES, :] = cs
        carry = cs[LANES - 1:LANES, :]


def gate_cumsum(g, b, *, seq):
    t = g.shape[0]
    return pl.pallas_call(
        _gate_kernel, grid=(t // seq,),
        in_specs=[pl.BlockSpec((seq, LANES), lambda i: (i, 0)), pl.BlockSpec((1, LANES), lambda i: (0, 0))],
        out_specs=pl.BlockSpec((seq, LANES), lambda i: (i, 0)),
        out_shape=jax.ShapeDtypeStruct((t, LANES), F32),
        compiler_params=_params("parallel"), name="gate_cumsum",
    )(g, b)


def _split3(x):
    hi = x.astype(BF16).astype(F32)
    rest = x - hi
    mid = rest.astype(BF16).astype(F32)
    lo = (rest - mid).astype(BF16).astype(F32)
    return hi, mid, lo


def _augment(x, in_half, lane, base, pieces, pieces_first):
    n = len(pieces)
    p0, o0 = (base, base + n) if pieces_first else (base + n, base)
    aug = jnp.where((lane >= o0) & (lane < o0 + n), 1.0, 0.0)
    for idx, piece in enumerate(pieces):
        aug = jnp.where(lane == p0 + idx, piece, aug)
    return jnp.where(in_half, x, aug.astype(x.dtype))


def _halves(lane):
    return [(lane >= HEAD_DIM * hh) & (lane < HEAD_DIM * (hh + 1)) for hh in range(2)]


def _causal(tq):
    row = lax.broadcasted_iota(jnp.int32, (tq, tq), 0)
    col = lax.broadcasted_iota(jnp.int32, (tq, tq), 1)
    return row >= col


def _causal_attention(qa, ka_s, v_ref, bias_ref, i, tq, m_s, l_s, acc_s):
    n_chunks = tq // LANES

    def scores(h, j, near, mask):
        start = pl.multiple_of(j * tq, tq)
        s = _dot_nt(qa[h], ka_s[h, pl.ds(start, tq), :])
        if near is not None:
            s = s + bias_ref[0, near]
        if mask is not None:
            s = jnp.where(mask, s, -jnp.inf)
        return [s[:, c * LANES:(c + 1) * LANES] for c in range(n_chunks)]

    def max_tile(j, near, mask):
        for h in range(2):
            m = m_s[h]
            for chunk in scores(h, j, near, mask):
                m = jnp.maximum(m, chunk)
            m_s[h] = m

    def sum_tile(j, near, mask):
        v = v_ref[pl.ds(pl.multiple_of(j * tq, tq), tq), :]
        for h in range(2):
            m = m_s[h]
            ps = [jnp.exp2(chunk - m) for chunk in scores(h, j, near, mask)]
            l_s[h] += functools.reduce(lambda a, b: a + b, ps)
            acc_s[h] += _dot(jnp.concatenate(ps, axis=1).astype(BF16), v)

    def key_tiles(fn):
        def run(lo, hi, near):
            def body(j, carry):
                fn(j, near, None)
                return carry
            lax.fori_loop(lo, hi, body, 0)

        if bias_ref is None:
            run(0, i, None)
            fn(i, None, _causal(tq))
        else:
            n_far = jnp.maximum(i - 1, 0)
            run(0, n_far, None)
            run(n_far, i, 1)
            fn(i, 0, _causal(tq))

    m_s[...] = jnp.full(m_s.shape, -jnp.inf, F32)
    key_tiles(max_tile)
    for h in range(2):
        m_s[h] = jnp.broadcast_to(jnp.max(m_s[h], axis=1, keepdims=True), (tq, LANES))
    l_s[...] = jnp.zeros(l_s.shape, F32)
    acc_s[...] = jnp.zeros(acc_s.shape, F32)
    key_tiles(sum_tile)


def _normalised(l_s, acc_s, h):
    return acc_s[h] / jnp.sum(l_s[h], axis=1, keepdims=True)


def _fox_kernel(q_ref, k_ref, v_ref, c_ref, o_ref, ka_s, m_s, l_s, acc_s, *, tq):
    hp = pl.program_id(1)
    i = pl.program_id(2)
    lane = lax.broadcasted_iota(jnp.int32, (1, LANES), 1)
    halves = _halves(lane)

    def decay(c, hh):
        return jnp.sum(jnp.where(lane == 2 * hp + hh, c, 0.0), axis=1, keepdims=True) * LOG2E

    @pl.when(i == 0)
    def _():
        k = k_ref[...]
        c_all = c_ref[...]
        for hh in range(2):
            ka_s[hh] = _augment(k, halves[hh], lane, HEAD_DIM * (1 - hh), _split3(-decay(c_all, hh)), True)

    q = q_ref[...]
    c_q = c_ref[pl.ds(pl.multiple_of(i * tq, tq), tq), :]
    qa = [_augment(q, halves[hh], lane, HEAD_DIM * (1 - hh), _split3(decay(c_q, hh)), False) for hh in range(2)]
    _causal_attention(qa, ka_s, v_ref, None, i, tq, m_s, l_s, acc_s)
    out = jnp.where(lane < HEAD_DIM, _normalised(l_s, acc_s, 0), _normalised(l_s, acc_s, 1))
    o_ref[...] = out.astype(o_ref.dtype)


def _attn_scratch(seq, tq):
    return [pltpu.VMEM((2, seq, LANES), BF16)] + [pltpu.VMEM((2, tq, LANES), F32)] * 3


def fox_attention(qkv, c, *, batch, seq, tq, q_col, k_col, v_col, n_pairs):
    t = batch * seq
    nq = seq // tq
    return pl.pallas_call(
        functools.partial(_fox_kernel, tq=tq),
        grid=(batch, n_pairs, nq),
        in_specs=[
            pl.BlockSpec((tq, LANES), lambda b, h, i: (b * nq + i, q_col + h)),
            pl.BlockSpec((seq, LANES), lambda b, h, i: (b, k_col + h)),
            pl.BlockSpec((seq, LANES), lambda b, h, i: (b, v_col + h)),
            pl.BlockSpec((seq, LANES), lambda b, h, i: (b, 0)),
        ],
        out_specs=pl.BlockSpec((tq, LANES), lambda b, h, i: (b * nq + i, h)),
        out_shape=jax.ShapeDtypeStruct((t, n_pairs * LANES), BF16),
        scratch_shapes=_attn_scratch(seq, tq),
        compiler_params=_params("parallel", "parallel", "arbitrary"), name="fox_attention",
    )(qkv, qkv, qkv, c)


def _sgu_kernel(u_ref, v_ref, norm_ref, ws_ref, bs_ref, o_ref, *, n_groups, chunk):
    tb = u_ref.shape[0]
    row = lax.broadcasted_iota(jnp.int32, (chunk, chunk), 0)
    col = lax.broadcasted_iota(jnp.int32, (chunk, chunk), 1)
    tri = row >= col
    for g in range(n_groups):
        w = jnp.where(tri, ws_ref[g], 0.0).astype(BF16)
        bias = bs_ref[:, g:g + 1]
        gain = norm_ref[g:g + 1, :]
        for c in range(tb // chunk):
            rs = slice(c * chunk, (c + 1) * chunk)
            cs = slice(g * LANES, (g + 1) * LANES)
            vn = _rms(jax.nn.gelu(v_ref[rs, cs].astype(F32)), gain)
            mixed = _dot(w, vn.astype(BF16)) + bias
            o_ref[rs, cs] = (jax.nn.gelu(u_ref[rs, cs].astype(F32)) * mixed).astype(o_ref.dtype)


def spatial_gating(main, sgu_norm, w_s, b_s, *, tb, u_col, v_col):
    t = main.shape[0]
    n_groups, chunk, _ = w_s.shape
    width = n_groups * LANES
    return pl.pallas_call(
        functools.partial(_sgu_kernel, n_groups=n_groups, chunk=chunk),
        grid=(t // tb,),
        in_specs=[
            pl.BlockSpec((tb, width), lambda i: (i, u_col)),
            pl.BlockSpec((tb, width), lambda i: (i, v_col)),
            pl.BlockSpec(sgu_norm.shape, lambda i: (0, 0)),
            pl.BlockSpec(w_s.shape, lambda i: (0, 0, 0)),
            pl.BlockSpec((chunk, n_groups), lambda i: (0, 0)),
        ],
        out_specs=pl.BlockSpec((tb, width), lambda i: (i, 0)),
        out_shape=jax.ShapeDtypeStruct((t, width), BF16),
        compiler_params=_params("parallel"), name="spatial_gating",
    )(main, main, sgu_norm, w_s, b_s.T)


def _proj_res_kernel(x_ref, a_ref, b_ref, wa_ref, wb_ref, o_ref):
    o_ref[...] = x_ref[...] + _dot(a_ref[...], wa_ref[...]) + _dot(b_ref[...], wb_ref[...])


def proj_residual(x, a, b, wa, wb, *, tm, name):
    t, d = x.shape
    return pl.pallas_call(
        _proj_res_kernel, grid=(t // tm,),
        in_specs=[
            pl.BlockSpec((tm, d), lambda i: (i, 0)),
            pl.BlockSpec((tm, a.shape[1]), lambda i: (i, 0)),
            pl.BlockSpec((tm, b.shape[1]), lambda i: (i, 0)),
            pl.BlockSpec(wa.shape, lambda i: (0, 0)),
            pl.BlockSpec(wb.shape, lambda i: (0, 0)),
        ],
        out_specs=pl.BlockSpec((tm, d), lambda i: (i, 0)),
        out_shape=jax.ShapeDtypeStruct((t, d), F32),
        compiler_params=_params("parallel"), name=name,
    )(x, a, b, wa, wb)


def _cross_kernel(x_ref, g_ref, wq_ref, kv_ref, wo_ref, o_ref, *, n_heads, dh):
    x = x_ref[...]
    h = _rms(x, g_ref[...]).astype(BF16)
    q = _dot(h, wq_ref[...]).astype(BF16)
    width = n_heads * dh
    outs = []
    for hd in range(n_heads):
        cs = slice(hd * dh, (hd + 1) * dh)
        s = _dot_nt(q[:, cs], kv_ref[:, cs]) * (dh ** -0.5)
        m = jnp.max(s, axis=1, keepdims=True)
        p = jnp.exp(s - m)
        p = p / jnp.sum(p, axis=1, keepdims=True)
        outs.append(_dot(p.astype(BF16), kv_ref[:, width + hd * dh:width + (hd + 1) * dh]).astype(BF16))
    o = jnp.concatenate(outs, axis=1)
    o_ref[...] = x + _dot(o, wo_ref[...])


def cross_attention(x, g, wq, kv, wo, *, tm, seq, mem_len, kv_col, n_heads, dh, name):
    t, d = x.shape
    per_b = seq // tm
    width = n_heads * dh
    return pl.pallas_call(
        functools.partial(_cross_kernel, n_heads=n_heads, dh=dh),
        grid=(t // tm,),
        in_specs=[
            pl.BlockSpec((tm, d), lambda i: (i, 0)),
            pl.BlockSpec((1, d), lambda i: (0, 0)),
            pl.BlockSpec(wq.shape, lambda i: (0, 0)),
            pl.BlockSpec((mem_len, 2 * width), lambda i: (i // per_b, kv_col)),
            pl.BlockSpec(wo.shape, lambda i: (0, 0)),
        ],
        out_specs=pl.BlockSpec((tm, d), lambda i: (i, 0)),
        out_shape=jax.ShapeDtypeStruct((t, d), F32),
        compiler_params=_params("parallel"), name=name,
    )(x, g.reshape(1, d), wq, kv, wo)


def _chunk_weights(w13, w2, tf):
    *lead, d, ff2 = w13.shape
    n = ff2 // 2 // tf
    w13 = jnp.moveaxis(w13.astype(BF16).reshape(*lead, d, 2 * n, tf), -2, -3)
    return w13, w2.astype(BF16).reshape(*lead, n, tf, d)


def _swiglu_chunks(h_s, w13, w2, o_ref, gu_s):
    n = w2.shape[0]

    def project(c, slot):
        gu_s[slot, 0] = _dot(h_s[...], w13[c])
        gu_s[slot, 1] = _dot(h_s[...], w13[n + c])

    def consume(c, slot):
        gate = gu_s[slot, 0]
        act = (gate * jax.nn.sigmoid(gate) * gu_s[slot, 1]).astype(BF16)
        o_ref[...] += _dot(act, w2[c])

    def pair(k, carry):
        c = 2 * k
        project(c + 1, 1)
        consume(c, 0)
        project(c + 2, 0)
        consume(c + 1, 1)
        return carry

    project(0, 0)
    lax.fori_loop(0, (n - 1) // 2, pair, 0)
    if n % 2 == 0:
        project(n - 1, 1)
        consume(n - 2, 0)
        consume(n - 1, 1)
    else:
        consume(n - 1, 0)


def _swiglu_scratch(tm, d, tf):
    return [pltpu.VMEM((tm, d), BF16), pltpu.VMEM((2, 2, tm, tf), F32)]


def _ffn_kernel(x_ref, g_ref, w13_ref, w2_ref, o_ref, h_s, gu_s):
    h_s[...] = _rms(x_ref[...], g_ref[...]).astype(BF16)
    o_ref[...] = x_ref[...]
    _swiglu_chunks(h_s, w13_ref, w2_ref, o_ref, gu_s)


def ffn_swiglu(x, g, w13, w2, *, tm, name):
    t, d = x.shape
    tf = w2.shape[1]
    resident = pl.Buffered(1)
    return pl.pallas_call(
        _ffn_kernel, grid=(t // tm,),
        in_specs=[
            pl.BlockSpec((tm, d), lambda i: (i, 0)),
            pl.BlockSpec((1, d), lambda i: (0, 0)),
            pl.BlockSpec(w13.shape, lambda i: (0, 0, 0), pipeline_mode=resident),
            pl.BlockSpec(w2.shape, lambda i: (0, 0, 0), pipeline_mode=resident),
        ],
        out_specs=pl.BlockSpec((tm, d), lambda i: (i, 0)),
        out_shape=jax.ShapeDtypeStruct((t, d), F32),
        scratch_shapes=_swiglu_scratch(tm, d, tf),
        compiler_params=_params("parallel"), name=name,
    )(x, g.reshape(1, d), w13, w2)


def _conv_kernel(bg_ref, cg_ref, xi_ref, w_ref, o_ref):
    s, width = o_ref.shape
    n_taps = w_ref.shape[0]
    xc = cg_ref[...].astype(F32) * xi_ref[...].astype(F32)
    row = lax.broadcasted_iota(jnp.int32, (s, width), 0)
    y = w_ref[n_taps - 1:n_taps, :] * xc
    for back in range(1, n_taps):
        shifted = jnp.where(row >= back, pltpu.roll(xc, back, axis=0), 0.0)
        y = y + w_ref[n_taps - 1 - back:n_taps - back, :] * shifted
    o_ref[...] = (bg_ref[...].astype(F32) * y).astype(o_ref.dtype)


def short_conv(main, conv_w, *, batch, seq):
    width = conv_w.shape[1]
    return pl.pallas_call(
        _conv_kernel, grid=(batch,),
        in_specs=[
            pl.BlockSpec((seq, width), lambda b: (b, 0)),
            pl.BlockSpec((seq, width), lambda b: (b, 1)),
            pl.BlockSpec((seq, width), lambda b: (b, 2)),
            pl.BlockSpec(conv_w.shape, lambda b: (0, 0)),
        ],
        out_specs=pl.BlockSpec((seq, width), lambda b: (b, 0)),
        out_shape=jax.ShapeDtypeStruct((batch * seq, width), BF16),
        compiler_params=_params("parallel"), name="short_conv",
    )(main, main, main, conv_w)


def _diff_prep_kernel(rb_ref, lq1_ref, lk1_ref, lq2_ref, lk2_ref, bias_ref, far_ref, lam_ref, *, tq, lam_init):
    n_heads = bias_ref.shape[0]
    strip = 32
    row = lax.broadcasted_iota(jnp.int32, (strip, tq), 0)
    col = lax.broadcasted_iota(jnp.int32, (strip, tq), 1)
    max_exact = N_BUCKETS // 2

    def fill(r, carry):
        r0 = pl.multiple_of(r * strip, strip)
        for which in range(2):
            n = jnp.maximum(row + r0 - col + which * tq, 0)
            nf = jnp.maximum(n, 1).astype(F32)
            large = max_exact + (jnp.log(nf / max_exact) / math.log(MAX_DIST / max_exact)
                                 * (N_BUCKETS - max_exact)).astype(jnp.int32)
            large = jnp.minimum(large, N_BUCKETS - 1)
            bucket = jnp.where(n < max_exact, n, large)
            for h in range(n_heads):
                b = jnp.zeros((strip, tq), F32)
                for kk in range(N_BUCKETS):
                    b = jnp.where(bucket == kk, rb_ref[kk, h], b)
                bias_ref[h, which, pl.ds(r0, strip), :] = (b - rb_ref[N_BUCKETS - 1, h]) * LOG2E
        return carry

    lax.fori_loop(0, tq // strip, fill, 0)
    for h in range(n_heads):
        far_ref[h] = jnp.full((1, LANES), rb_ref[N_BUCKETS - 1, h], F32) * LOG2E
    lam = (jnp.exp(jnp.sum(lq1_ref[...] * lk1_ref[...], axis=1, keepdims=True))
           - jnp.exp(jnp.sum(lq2_ref[...] * lk2_ref[...], axis=1, keepdims=True)) + lam_init)
    lam_ref[...] = jnp.broadcast_to(lam, (1, LANES))


def diff_prep(rel_bias, lq1, lk1, lq2, lk2, *, tq, lam_init):
    n_heads = rel_bias.shape[1]
    vec = lambda a: a.reshape(1, -1)
    vspec = pl.BlockSpec(memory_space=pltpu.VMEM)
    return pl.pallas_call(
        functools.partial(_diff_prep_kernel, tq=tq, lam_init=lam_init),
        in_specs=[pl.BlockSpec(memory_space=pltpu.SMEM), vspec, vspec, vspec, vspec],
        out_specs=[vspec, vspec, vspec],
        out_shape=[
            jax.ShapeDtypeStruct((n_heads, 2, tq, tq), F32),
            jax.ShapeDtypeStruct((n_heads, 1, LANES), F32),
            jax.ShapeDtypeStruct((1, LANES), F32),
        ],
        compiler_params=pltpu.CompilerParams(vmem_limit_bytes=VMEM_LIMIT), name="diff_prep",
    )(rel_bias, vec(lq1), vec(lk1), vec(lq2), vec(lk2))


def _diff_kernel(q_ref, k_ref, v_ref, bias_ref, far_ref, lam_ref, subln_ref, o_ref, ka_s, m_s, l_s, acc_s, *,
                 tq, lam_init):
    i = pl.program_id(2)
    lane = lax.broadcasted_iota(jnp.int32, (1, LANES), 1)
    halves = _halves(lane)

    @pl.when(i == 0)
    def _():
        k = k_ref[...]
        far = _split3(far_ref[0])
        for sub in range(2):
            ka_s[sub] = _augment(k, halves[sub], lane, HEAD_DIM * (1 - sub), far, True)

    q = q_ref[...]
    zero = jnp.zeros((1, LANES), F32)
    qa = [_augment(q, halves[sub], lane, HEAD_DIM * (1 - sub), (zero,) * 3, False) for sub in range(2)]
    _causal_attention(qa, ka_s, v_ref, bias_ref, i, tq, m_s, l_s, acc_s)
    o = _normalised(l_s, acc_s, 0) - lam_ref[...] * _normalised(l_s, acc_s, 1)
    o_ref[...] = (_rms(o, subln_ref[...]) * (1.0 - lam_init)).astype(o_ref.dtype)


def diff_attention(main, bias, far, lam, subln, *, batch, seq, tq, q_col, k_col, v_col, lam_init):
    t = batch * seq
    nq = seq // tq
    n_heads = bias.shape[0]
    return pl.pallas_call(
        functools.partial(_diff_kernel, tq=tq, lam_init=lam_init),
        grid=(batch, n_heads, nq),
        in_specs=[
            pl.BlockSpec((tq, LANES), lambda b, h, i: (b * nq + i, q_col + h)),
            pl.BlockSpec((seq, LANES), lambda b, h, i: (b, k_col + h)),
            pl.BlockSpec((seq, LANES), lambda b, h, i: (b, v_col + h)),
            pl.BlockSpec((1, 2, tq, tq), lambda b, h, i: (h, 0, 0, 0)),
            pl.BlockSpec((1, 1, LANES), lambda b, h, i: (h, 0, 0)),
            pl.BlockSpec((1, LANES), lambda b, h, i: (0, 0)),
            pl.BlockSpec((1, LANES), lambda b, h, i: (0, 0)),
        ],
        out_specs=pl.BlockSpec((tq, LANES), lambda b, h, i: (b * nq + i, h)),
        out_shape=jax.ShapeDtypeStruct((t, n_heads * LANES), BF16),
        scratch_shapes=_attn_scratch(seq, tq),
        compiler_params=_params("parallel", "parallel", "arbitrary"), name="diff_attention",
    )(main, main, main, bias, far, lam, subln.reshape(1, LANES))


def _router_kernel(x_ref, g_ref, wr_ref, meta_ref, wts_ref, cnt_ref, carry_s, *, n_exp):
    tm = x_ref.shape[0]

    @pl.when(pl.program_id(0) == 0)
    def _():
        carry_s[...] = jnp.zeros_like(carry_s)

    h = _rms(x_ref[...], g_ref[...])
    logits = jnp.dot(h, wr_ref[...], precision=lax.Precision.HIGHEST, preferred_element_type=F32)
    lane = lax.broadcasted_iota(jnp.int32, (tm, LANES), 1)
    lane_f = lane.astype(F32)
    logits = jnp.where(lane < n_exp, logits, -jnp.inf)
    m1 = jnp.max(logits, axis=1, keepdims=True)
    i1 = jnp.min(jnp.where(logits == m1, lane_f, float(LANES)), axis=1, keepdims=True)
    rest = jnp.where(lane_f == i1, -jnp.inf, logits)
    m2 = jnp.max(rest, axis=1, keepdims=True)
    i2 = jnp.min(jnp.where(rest == m2, lane_f, float(LANES)), axis=1, keepdims=True)
    e = jnp.exp(m2 - m1)
    w1 = 1.0 / (1.0 + e)
    w2 = e / (1.0 + e)
    sel1 = lane_f == i1
    sel2 = lane_f == i2
    onehot = jnp.where(sel1 | sel2, 1.0, 0.0)
    row = lax.broadcasted_iota(jnp.int32, (tm, tm), 0)
    col = lax.broadcasted_iota(jnp.int32, (tm, tm), 1)
    before = jnp.where(row > col, 1.0, 0.0).astype(BF16)
    rank = _dot(before, onehot.astype(BF16)) + carry_s[...]
    r1 = jnp.sum(jnp.where(sel1, rank, 0.0), axis=1, keepdims=True)
    r2 = jnp.sum(jnp.where(sel2, rank, 0.0), axis=1, keepdims=True)
    meta = jnp.where(lane == 0, i1, jnp.where(lane == 1, i2, jnp.where(lane == 2, r1, jnp.where(lane == 3, r2, 0.0))))
    meta_ref[...] = meta.astype(jnp.int32)
    wts_ref[...] = jnp.where(lane == 0, w1, jnp.where(lane == 1, w2, 0.0))
    carry_s[...] += jnp.sum(onehot, axis=0, keepdims=True)
    cnt_ref[...] = carry_s[...]


def route_tokens(x, g, wr, *, tm, n_exp):
    t, d = x.shape
    return pl.pallas_call(
        functools.partial(_router_kernel, n_exp=n_exp),
        grid=(t // tm,),
        in_specs=[
            pl.BlockSpec((tm, d), lambda i: (i, 0)),
            pl.BlockSpec((1, d), lambda i: (0, 0)),
            pl.BlockSpec((d, LANES), lambda i: (0, 0)),
        ],
        out_specs=[
            pl.BlockSpec((tm, LANES), lambda i: (i, 0)),
            pl.BlockSpec((tm, LANES), lambda i: (i, 0)),
            pl.BlockSpec((1, LANES), lambda i: (0, 0)),
        ],
        out_shape=[
            jax.ShapeDtypeStruct((t, LANES), jnp.int32),
            jax.ShapeDtypeStruct((t, LANES), F32),
            jax.ShapeDtypeStruct((1, LANES), F32),
        ],
        scratch_shapes=[pltpu.VMEM((1, LANES), F32)],
        compiler_params=_params("arbitrary"), name="moe_router",
    )(x, g.reshape(1, d), wr)


def _row_copy(src, dst, s_row, d_row, sem):
    return pltpu.make_async_copy(src.at[pl.ds(s_row, 1)], dst.at[pl.ds(d_row, 1)], sem)


def _dispatch_kernel(pos_ref, x_ref, zeros_hbm, xs_hbm, sem, *, tm):
    del zeros_hbm
    base = pl.program_id(0) * tm
    n_tok = pl.num_programs(0) * tm

    def start(t, carry):
        for k in range(TOP_K):
            _row_copy(x_ref, xs_hbm, t, pos_ref[k * n_tok + base + t], sem).start()
        return carry

    def wait(t, carry):
        for k in range(TOP_K):
            _row_copy(x_ref, xs_hbm, t, pos_ref[k * n_tok + base + t], sem).wait()
        return carry

    lax.fori_loop(0, tm, start, 0)
    lax.fori_loop(0, tm, wait, 0)


def moe_dispatch(pos, x, n_rows, *, tm):
    t, d = x.shape
    zeros = jnp.zeros((n_rows, d), x.dtype)
    return pl.pallas_call(
        functools.partial(_dispatch_kernel, tm=tm),
        grid_spec=pltpu.PrefetchScalarGridSpec(
            num_scalar_prefetch=1, grid=(t // tm,),
            in_specs=[pl.BlockSpec((tm, d), lambda i, pos: (i, 0)), pl.BlockSpec(memory_space=pl.ANY)],
            out_specs=pl.BlockSpec(memory_space=pl.ANY),
            scratch_shapes=[pltpu.SemaphoreType.DMA(())],
        ),
        out_shape=jax.ShapeDtypeStruct((n_rows, d), x.dtype),
        input_output_aliases={2: 0},
        compiler_params=_params("arbitrary"),
        name="moe_dispatch",
    )(pos, x, zeros)


def _expert_kernel(te_ref, nu_ref, xs_ref, g_ref, w13_ref, w2_ref, o_ref, h_s, gu_s):
    del te_ref
    used = pl.program_id(0) < nu_ref[0]
    o_ref[...] = jnp.zeros_like(o_ref)

    @pl.when(used)
    def _():
        h_s[...] = _rms(xs_ref[...], g_ref[...]).astype(BF16)
        _swiglu_chunks(h_s, w13_ref.at[0], w2_ref.at[0], o_ref, gu_s)


def moe_experts(tile_expert, n_used, xs, g, w13, w2, *, tm):
    n_rows, d = xs.shape
    tf = w2.shape[2]
    resident = pl.Buffered(1)
    return pl.pallas_call(
        _expert_kernel,
        grid_spec=pltpu.PrefetchScalarGridSpec(
            num_scalar_prefetch=2, grid=(n_rows // tm,),
            in_specs=[
                pl.BlockSpec((tm, d), lambda i, te, nu: (i, 0)),
                pl.BlockSpec((1, d), lambda i, te, nu: (0, 0)),
                pl.BlockSpec((1,) + w13.shape[1:], lambda i, te, nu: (te[i], 0, 0, 0), pipeline_mode=resident),
                pl.BlockSpec((1,) + w2.shape[1:], lambda i, te, nu: (te[i], 0, 0, 0), pipeline_mode=resident),
            ],
            out_specs=pl.BlockSpec((tm, d), lambda i, te, nu: (i, 0)),
            scratch_shapes=_swiglu_scratch(tm, d, tf),
        ),
        out_shape=jax.ShapeDtypeStruct((n_rows, d), F32),
        compiler_params=_params("arbitrary"), name="moe_experts",
    )(tile_expert, n_used, xs, g.reshape(1, d), w13, w2)


def _combine_kernel(pos_ref, x_ref, wts_ref, g_ref, y_hbm, o_ref, rows_s, sem, *, tm):
    base = pl.program_id(0) * tm
    n_tok = pl.num_programs(0) * tm

    def start(t, carry):
        for k in range(TOP_K):
            _row_copy(y_hbm, rows_s.at[k], pos_ref[k * n_tok + base + t], t, sem).start()
        return carry

    def wait(t, carry):
        for k in range(TOP_K):
            _row_copy(y_hbm, rows_s.at[k], pos_ref[k * n_tok + base + t], t, sem).wait()
        return carry

    lax.fori_loop(0, tm, start, 0)
    lax.fori_loop(0, tm, wait, 0)
    wts = wts_ref[...]
    y = x_ref[...]
    for k in range(TOP_K):
        y = y + wts[:, k:k + 1] * rows_s[k]
    o_ref[...] = _rms(y, g_ref[...])


def moe_combine(pos, x, wts, g, y, *, tm):
    t, d = x.shape
    return pl.pallas_call(
        functools.partial(_combine_kernel, tm=tm),
        grid_spec=pltpu.PrefetchScalarGridSpec(
            num_scalar_prefetch=1, grid=(t // tm,),
            in_specs=[
                pl.BlockSpec((tm, d), lambda i, pos: (i, 0)),
                pl.BlockSpec((tm, LANES), lambda i, pos: (i, 0)),
                pl.BlockSpec((1, d), lambda i, pos: (0, 0)),
                pl.BlockSpec(memory_space=pl.ANY),
            ],
            out_specs=pl.BlockSpec((tm, d), lambda i, pos: (i, 0)),
            scratch_shapes=[pltpu.VMEM((TOP_K, tm, d), F32), pltpu.SemaphoreType.DMA(())],
        ),
        out_shape=jax.ShapeDtypeStruct((t, d), F32),
        compiler_params=_params("arbitrary"), name="moe_combine",
    )(pos, x, wts, g.reshape(1, d), y)


def _pad_cols(w, n):
    return jnp.pad(w, ((0, 0), (0, n - w.shape[1])))


def _even_mixer(x, norm, w_in, b_f, sgu_norm, w_s, b_s, w_out, *, batch, seq, tq):
    n_heads = b_f.shape[0]
    a_width = n_heads * HEAD_DIM
    b_width = w_s.shape[0] * LANES
    f0 = 3 * a_width
    w_main = jnp.concatenate([w_in[:, :f0], w_in[:, f0 + n_heads:]], axis=1)
    w_gate = _pad_cols(w_in[:, f0:f0 + n_heads], LANES)
    main, gate = norm_matmul(x, norm, [w_main.astype(BF16), w_gate.astype(BF16)], [BF16, F32],
                             tm=512, name="even_in_proj", scaled=(0, a_width, LOG2E * HEAD_DIM ** -0.5))
    c = gate_cumsum(gate, _pad_cols(b_f.reshape(1, -1), LANES), seq=seq)
    n_pairs = a_width // LANES
    a = fox_attention(main, c, batch=batch, seq=seq, tq=tq, q_col=0, k_col=n_pairs, v_col=2 * n_pairs,
                      n_pairs=n_pairs)
    u_col = f0 // b_width
    g = spatial_gating(main, sgu_norm, w_s, b_s, tb=512, u_col=u_col, v_col=u_col + 1)
    w_out = w_out.astype(BF16)
    return proj_residual(x, a, g, w_out[:a_width], w_out[a_width:], tm=512, name="even_out_proj")


def _odd_mixer(x, norm, w_in, conv_w, lq1, lk1, lq2, lk2, subln, w_out, rel_bias, lam_init, *, batch, seq, tq):
    c_width = conv_w.shape[1]
    d_width = rel_bias.shape[1] * 2 * HEAD_DIM
    q0 = 3 * c_width
    (main,) = norm_matmul(x, norm, [w_in.astype(BF16)], [BF16], tm=512, name="odd_in_proj",
                          scaled=(q0, q0 + d_width, LOG2E * HEAD_DIM ** -0.5))
    c_out = short_conv(main, conv_w, batch=batch, seq=seq)
    bias, far, lam = diff_prep(rel_bias, lq1, lk1, lq2, lk2, tq=tq, lam_init=lam_init)
    q_col = q0 // LANES
    n_heads = rel_bias.shape[1]
    d_out = diff_attention(main, bias, far, lam, subln, batch=batch, seq=seq, tq=tq, q_col=q_col,
                           k_col=q_col + n_heads, v_col=q_col + 2 * n_heads, lam_init=lam_init)
    w_out = w_out.astype(BF16)
    return proj_residual(x, c_out, d_out, w_out[:c_width], w_out[c_width:], tm=512, name="odd_out_proj")


def _moe_layer(x, norm, w_router, w13, w2, final_norm, *, tm_expert):
    t, d = x.shape
    n_exp = w_router.shape[1]
    meta, wts, counts = route_tokens(x, norm, _pad_cols(w_router, LANES), tm=512, n_exp=n_exp)
    counts = counts[0, :n_exp].astype(jnp.int32)
    padded = (counts + tm_expert - 1) // tm_expert * tm_expert
    ends = jnp.cumsum(padded)
    offsets = ends - padded
    pos = jnp.concatenate([offsets[meta[:, k]] + meta[:, TOP_K + k] for k in range(TOP_K)]).astype(jnp.int32)
    n_rows = TOP_K * t + n_exp * tm_expert
    n_tiles = n_rows // tm_expert
    tile_start = jnp.arange(n_tiles, dtype=jnp.int32) * tm_expert
    tile_expert = jnp.minimum(jnp.sum(tile_start[:, None] >= ends[None, :], axis=1), n_exp - 1).astype(jnp.int32)
    n_used = (ends[-1:] // tm_expert).astype(jnp.int32)
    xs = moe_dispatch(pos, x, n_rows, tm=512)
    y = moe_experts(tile_expert, n_used, xs, norm, *_chunk_weights(w13, w2, 512), tm=tm_expert)
    return moe_combine(pos, x, wts, final_norm, y, tm=256)


def kernel(x, mem, rel_bias, mem_norm, final_norm, ev_norm, ev_w_in, ev_b_f, ev_sgu_norm, ev_w_s, ev_b_s, ev_w_out, ffn_w13, ffn_w2, od_norm, od_w_in, od_conv_w, od_lam_q1, od_lam_k1, od_lam_q2, od_lam_k2, od_subln, od_w_out, moe_router, moe_w13, moe_w2, x_norm, x_wq, x_wkv, x_wo, ffn_norm):
    batch, seq, d = x.shape
    mem_len = mem.shape[1]
    depth = x_norm.shape[0]
    assert depth == 2 and ev_norm.shape[0] == 1 and od_norm.shape[0] == 1
    x_heads, x_dh = 4, 128
    xf = x.reshape(batch * seq, d)
    wkv = jnp.concatenate([x_wkv[layer] for layer in range(depth)], axis=1).astype(BF16)
    (kv,) = norm_matmul(mem.reshape(batch * mem_len, d), mem_norm, [wkv], [BF16], tm=512, name="mem_kv")

    def cross(xf, layer):
        return cross_attention(xf, x_norm[layer], x_wq[layer].astype(BF16), kv, x_wo[layer].astype(BF16),
                               tm=512, seq=seq, mem_len=mem_len, kv_col=layer, n_heads=x_heads, dh=x_dh,
                               name=f"cross_attention_{layer}")

    xf = _even_mixer(xf, ev_norm[0], ev_w_in[0], ev_b_f[0], ev_sgu_norm[0], ev_w_s[0], ev_b_s[0], ev_w_out[0],
                     batch=batch, seq=seq, tq=ATTN_TILE)
    xf = cross(xf, 0)
    xf = ffn_swiglu(xf, ffn_norm[0], *_chunk_weights(ffn_w13[0], ffn_w2[0], 256), tm=512, name="ffn_swiglu")
    lam_init = 0.8 - 0.6 * math.exp(-0.3 * 1)
    xf = _odd_mixer(xf, od_norm[0], od_w_in[0], od_conv_w[0], od_lam_q1[0], od_lam_k1[0], od_lam_q2[0],
                    od_lam_k2[0], od_subln[0], od_w_out[0], rel_bias, lam_init, batch=batch, seq=seq, tq=ATTN_TILE)
    xf = cross(xf, 1)
    out = _moe_layer(xf, ffn_norm[1], moe_router[0], moe_w13[0], moe_w2[0], final_norm, tm_expert=512)
    return out.reshape(batch, seq, d)
```

```python
import functools
import math

import jax
import jax.numpy as jnp
from jax import lax
from jax.experimental import pallas as pl
from jax.experimental.pallas import tpu as pltpu

F32 = jnp.float32
BF16 = jnp.bfloat16
EPS = 1e-6
HEAD_DIM = 64
LANES = 128
N_BUCKETS = 32
MAX_DIST = 128
TOP_K = 2
LOG2E = 1.4426950408889634
ATTN_TILE = 512
VMEM_LIMIT = 56 * 1024 * 1024


def _params(*sem):
    return pltpu.CompilerParams(dimension_semantics=sem, vmem_limit_bytes=VMEM_LIMIT)


def _rms(x, g):
    ms = jnp.mean(x * x, axis=-1, keepdims=True)
    return x * lax.rsqrt(ms + EPS) * g


def _dot(a, b):
    return jnp.dot(a, b, preferred_element_type=F32)


def _dot_nt(a, b):
    return lax.dot_general(a, b, (((1,), (1,)), ((), ())), preferred_element_type=F32)


def _norm_matmul_kernel(x_ref, g_ref, *refs, n_w, chunk, scaled):
    w_refs, o_refs = refs[:n_w], refs[n_w:]
    h = _rms(x_ref[...], g_ref[...]).astype(BF16)
    s0, s1, scale = scaled
    for k, (w_ref, o_ref) in enumerate(zip(w_refs, o_refs)):
        n = w_ref.shape[1]
        for c0 in range(0, n, chunk):
            c1 = min(c0 + chunk, n)
            y = _dot(h, w_ref[:, c0:c1])
            if k == 0 and s0 <= c0 and c1 <= s1:
                y = y * scale
            o_ref[:, c0:c1] = y.astype(o_ref.dtype)


def norm_matmul(x, g, ws, out_dtypes, *, tm, name, scaled=(0, 0, 1.0)):
    t, d = x.shape
    chunk = 512
    assert scaled[0] % chunk == 0 and scaled[1] % chunk == 0
    in_specs = [pl.BlockSpec((tm, d), lambda i: (i, 0)), pl.BlockSpec((1, d), lambda i: (0, 0))]
    in_specs += [pl.BlockSpec(w.shape, lambda i: (0, 0)) for w in ws]
    out_specs = [pl.BlockSpec((tm, w.shape[1]), lambda i: (i, 0)) for w in ws]
    out_shape = [jax.ShapeDtypeStruct((t, w.shape[1]), dt) for w, dt in zip(ws, out_dtypes)]
    return pl.pallas_call(
        functools.partial(_norm_matmul_kernel, n_w=len(ws), chunk=chunk, scaled=scaled),
        grid=(t // tm,), in_specs=in_specs, out_specs=out_specs, out_shape=out_shape,
        compiler_params=_params("parallel"), name=name,
    )(x, g.reshape(1, d), *ws)


def _gate_kernel(g_ref, b_ref, c_ref):
    s = g_ref.shape[0]
    row = lax.broadcasted_iota(jnp.int32, (LANES, LANES), 0)
    col = lax.broadcasted_iota(jnp.int32, (LANES, LANES), 1)
    tri = (row >= col).astype(F32)
    carry = jnp.zeros((1, LANES), F32)
    for blk in range(s // LANES):
        z = g_ref[blk * LANES:(blk + 1) * LANES, :] + b_ref[...]
        log_f = jnp.minimum(z, 0.0) - jnp.log1p(jnp.exp(-jnp.abs(z)))
        cs = jnp.dot(tri, log_f, precision=lax.Precision.HIGHEST, preferred_element_type=F32) + carry
        c_ref[blk * LANES:(blk + 1) * LANES, :] = cs
        carry = cs[LANES - 1:LANES, :]


def gate_cumsum(g, b, *, seq):
    t = g.shape[0]
    return pl.pallas_call(
        _gate_kernel, grid=(t // seq,),
        in_specs=[pl.BlockSpec((seq, LANES), lambda i: (i, 0)), pl.BlockSpec((1, LANES), lambda i: (0, 0))],
        out_specs=pl.BlockSpec((seq, LANES), lambda i: (i, 0)),
        out_shape=jax.ShapeDtypeStruct((t, LANES), F32),
        compiler_params=_params("parallel"), name="gate_cumsum",
    )(g, b)


def _split3(x):
    hi = x.astype(BF16).astype(F32)
    rest = x - hi
    mid = rest.astype(BF16).astype(F32)
    lo = (rest - mid).astype(BF16).astype(F32)
    return hi, mid, lo


def _augment(x, in_half, lane, base, pieces, pieces_first):
    n = len(pieces)
    p0, o0 = (base, base + n) if pieces_first else (base + n, base)
    aug = jnp.where((lane >= o0) & (lane < o0 + n), 1.0, 0.0)
    for idx, piece in enumerate(pieces):
        aug = jnp.where(lane == p0 + idx, piece, aug)
    return jnp.where(in_half, x, aug.astype(x.dtype))


def _halves(lane):
    return [(lane >= HEAD_DIM * hh) & (lane < HEAD_DIM * (hh + 1)) for hh in range(2)]


def _causal(tq):
    row = lax.broadcasted_iota(jnp.int32, (tq, tq), 0)
    col = lax.broadcasted_iota(jnp.int32, (tq, tq), 1)
    return row >= col


def _causal_attention(qa, ka_s, v_ref, bias_ref, i, tq, m_s, l_s, acc_s):
    n_chunks = tq // LANES

    def scores(h, j, near, mask):
        start = pl.multiple_of(j * tq, tq)
        s = _dot_nt(qa[h], ka_s[h, pl.ds(start, tq), :])
        if near is not None:
            s = s + bias_ref[0, near]
        if mask is not None:
            s = jnp.where(mask, s, -jnp.inf)
        return [s[:, c * LANES:(c + 1) * LANES] for c in range(n_chunks)]

    def max_tile(j, near, mask):
        for h in range(2):
            m = m_s[h]
            for chunk in scores(h, j, near, mask):
                m = jnp.maximum(m, chunk)
            m_s[h] = m

    def sum_tile(j, near, mask):
        v = v_ref[pl.ds(pl.multiple_of(j * tq, tq), tq), :]
        for h in range(2):
            m = m_s[h]
            ps = [jnp.exp2(chunk - m) for chunk in scores(h, j, near, mask)]
            l_s[h] += functools.reduce(lambda a, b: a + b, ps)
            acc_s[h] += _dot(jnp.concatenate(ps, axis=1).astype(BF16), v)

    def key_tiles(fn):
        def run(lo, hi, near):
            def body(j, carry):
                fn(j, near, None)
                return carry
            lax.fori_loop(lo, hi, body, 0)

        if bias_ref is None:
            run(0, i, None)
            fn(i, None, _causal(tq))
        else:
            n_far = jnp.maximum(i - 1, 0)
            run(0, n_far, None)
            run(n_far, i, 1)
            fn(i, 0, _causal(tq))

    m_s[...] = jnp.full(m_s.shape, -jnp.inf, F32)
    key_tiles(max_tile)
    for h in range(2):
        m_s[h] = jnp.broadcast_to(jnp.max(m_s[h], axis=1, keepdims=True), (tq, LANES))
    l_s[...] = jnp.zeros(l_s.shape, F32)
    acc_s[...] = jnp.zeros(acc_s.shape, F32)
    key_tiles(sum_tile)


def _normalised(l_s, acc_s, h):
    return acc_s[h] / jnp.sum(l_s[h], axis=1, keepdims=True)


def _fox_kernel(q_ref, k_ref, v_ref, c_ref, o_ref, ka_s, m_s, l_s, acc_s, *, tq):
    hp = pl.program_id(1)
    i = pl.program_id(2)
    lane = lax.broadcasted_iota(jnp.int32, (1, LANES), 1)
    halves = _halves(lane)

    def decay(c, hh):
        return jnp.sum(jnp.where(lane == 2 * hp + hh, c, 0.0), axis=1, keepdims=True) * LOG2E

    @pl.when(i == 0)
    def _():
        k = k_ref[...]
        c_all = c_ref[...]
        for hh in range(2):
            ka_s[hh] = _augment(k, halves[hh], lane, HEAD_DIM * (1 - hh), _split3(-decay(c_all, hh)), True)

    q = q_ref[...]
    c_q = c_ref[pl.ds(pl.multiple_of(i * tq, tq), tq), :]
    qa = [_augment(q, halves[hh], lane, HEAD_DIM * (1 - hh), _split3(decay(c_q, hh)), False) for hh in range(2)]
    _causal_attention(qa, ka_s, v_ref, None, i, tq, m_s, l_s, acc_s)
    out = jnp.where(lane < HEAD_DIM, _normalised(l_s, acc_s, 0), _normalised(l_s, acc_s, 1))
    o_ref[...] = out.astype(o_ref.dtype)


def _attn_scratch(seq, tq):
    return [pltpu.VMEM((2, seq, LANES), BF16)] + [pltpu.VMEM((2, tq, LANES), F32)] * 3


def fox_attention(qkv, c, *, batch, seq, tq, q_col, k_col, v_col, n_pairs):
    t = batch * seq
    nq = seq // tq
    return pl.pallas_call(
        functools.partial(_fox_kernel, tq=tq),
        grid=(batch, n_pairs, nq),
        in_specs=[
            pl.BlockSpec((tq, LANES), lambda b, h, i: (b * nq + i, q_col + h)),
            pl.BlockSpec((seq, LANES), lambda b, h, i: (b, k_col + h)),
            pl.BlockSpec((seq, LANES), lambda b, h, i: (b, v_col + h)),
            pl.BlockSpec((seq, LANES), lambda b, h, i: (b, 0)),
        ],
        out_specs=pl.BlockSpec((tq, LANES), lambda b, h, i: (b * nq + i, h)),
        out_shape=jax.ShapeDtypeStruct((t, n_pairs * LANES), BF16),
        scratch_shapes=_attn_scratch(seq, tq),
        compiler_params=_params("parallel", "parallel", "arbitrary"), name="fox_attention",
    )(qkv, qkv, qkv, c)


def _sgu_kernel(u_ref, v_ref, norm_ref, ws_ref, bs_ref, o_ref, *, n_groups, chunk):
    tb = u_ref.shape[0]
    row = lax.broadcasted_iota(jnp.int32, (chunk, chunk), 0)
    col = lax.broadcasted_iota(jnp.int32, (chunk, chunk), 1)
    tri = row >= col
    for g in range(n_groups):
        w = jnp.where(tri, ws_ref[g], 0.0).astype(BF16)
        bias = bs_ref[:, g:g + 1]
        gain = norm_ref[g:g + 1, :]
        for c in range(tb // chunk):
            rs = slice(c * chunk, (c + 1) * chunk)
            cs = slice(g * LANES, (g + 1) * LANES)
            vn = _rms(jax.nn.gelu(v_ref[rs, cs].astype(F32)), gain)
            mixed = _dot(w, vn.astype(BF16)) + bias
            o_ref[rs, cs] = (jax.nn.gelu(u_ref[rs, cs].astype(F32)) * mixed).astype(o_ref.dtype)


def spatial_gating(main, sgu_norm, w_s, b_s, *, tb, u_col, v_col):
    t = main.shape[0]
    n_groups, chunk, _ = w_s.shape
    width = n_groups * LANES
    return pl.pallas_call(
        functools.partial(_sgu_kernel, n_groups=n_groups, chunk=chunk),
        grid=(t // tb,),
        in_specs=[
            pl.BlockSpec((tb, width), lambda i: (i, u_col)),
            pl.BlockSpec((tb, width), lambda i: (i, v_col)),
            pl.BlockSpec(sgu_norm.shape, lambda i: (0, 0)),
            pl.BlockSpec(w_s.shape, lambda i: (0, 0, 0)),
            pl.BlockSpec((chunk, n_groups), lambda i: (0, 0)),
        ],
        out_specs=pl.BlockSpec((tb, width), lambda i: (i, 0)),
        out_shape=jax.ShapeDtypeStruct((t, width), BF16),
        compiler_params=_params("parallel"), name="spatial_gating",
    )(main, main, sgu_norm, w_s, b_s.T)


def _proj_res_kernel(x_ref, a_ref, b_ref, wa_ref, wb_ref, o_ref):
    o_ref[...] = x_ref[...] + _dot(a_ref[...], wa_ref[...]) + _dot(b_ref[...], wb_ref[...])


def proj_residual(x, a, b, wa, wb, *, tm, name):
    t, d = x.shape
    return pl.pallas_call(
        _proj_res_kernel, grid=(t // tm,),
        in_specs=[
            pl.BlockSpec((tm, d), lambda i: (i, 0)),
            pl.BlockSpec((tm, a.shape[1]), lambda i: (i, 0)),
            pl.BlockSpec((tm, b.shape[1]), lambda i: (i, 0)),
            pl.BlockSpec(wa.shape, lambda i: (0, 0)),
            pl.BlockSpec(wb.shape, lambda i: (0, 0)),
        ],
        out_specs=pl.BlockSpec((tm, d), lambda i: (i, 0)),
        out_shape=jax.ShapeDtypeStruct((t, d), F32),
        compiler_params=_params("parallel"), name=name,
    )(x, a, b, wa, wb)


def _cross_kernel(x_ref, g_ref, wq_ref, kv_ref, wo_ref, o_ref, *, n_heads, dh):
    x = x_ref[...]
    h = _rms(x, g_ref[...]).astype(BF16)
    q = _dot(h, wq_ref[...]).astype(BF16)
    width = n_heads * dh
    outs = []
    for hd in range(n_heads):
        cs = slice(hd * dh, (hd + 1) * dh)
        s = _dot_nt(q[:, cs], kv_ref[:, cs]) * (dh ** -0.5)
        m = jnp.max(s, axis=1, keepdims=True)
        p = jnp.exp(s - m)
        p = p / jnp.sum(p, axis=1, keepdims=True)
        outs.append(_dot(p.astype(BF16), kv_ref[:, width + hd * dh:width + (hd + 1) * dh]).astype(BF16))
    o = jnp.concatenate(outs, axis=1)
    o_ref[...] = x + _dot(o, wo_ref[...])


def cross_attention(x, g, wq, kv, wo, *, tm, seq, mem_len, kv_col, n_heads, dh, name):
    t, d = x.shape
    per_b = seq // tm
    width = n_heads * dh
    return pl.pallas_call(
        functools.partial(_cross_kernel, n_heads=n_heads, dh=dh),
        grid=(t // tm,),
        in_specs=[
            pl.BlockSpec((tm, d), lambda i: (i, 0)),
            pl.BlockSpec((1, d), lambda i: (0, 0)),
            pl.BlockSpec(wq.shape, lambda i: (0, 0)),
            pl.BlockSpec((mem_len, 2 * width), lambda i: (i // per_b, kv_col)),
            pl.BlockSpec(wo.shape, lambda i: (0, 0)),
        ],
        out_specs=pl.BlockSpec((tm, d), lambda i: (i, 0)),
        out_shape=jax.ShapeDtypeStruct((t, d), F32),
        compiler_params=_params("parallel"), name=name,
    )(x, g.reshape(1, d), wq, kv, wo)


def _chunk_weights(w13, w2, tf):
    *lead, ff, d = w2.shape
    return w13.astype(BF16), w2.astype(BF16).reshape(*lead, ff // tf, tf, d)


def _swiglu_chunks(h_s, w13, w2, o_ref, gu_s):
    n, tf, _ = w2.shape

    def project(c, slot):
        for part in range(2):
            cols = pl.ds(pl.multiple_of((part * n + c) * tf, tf), tf)
            gu_s[slot, part] = _dot(h_s[...], w13[:, cols])

    def consume(c, slot):
        gate = gu_s[slot, 0]
        act = (gate * jax.nn.sigmoid(gate) * gu_s[slot, 1]).astype(BF16)
        o_ref[...] += _dot(act, w2[c])

    def pair(k, carry):
        c = 2 * k
        project(c + 1, 1)
        consume(c, 0)
        project(c + 2, 0)
        consume(c + 1, 1)
        return carry

    project(0, 0)
    lax.fori_loop(0, (n - 1) // 2, pair, 0)
    if n % 2 == 0:
        project(n - 1, 1)
        consume(n - 2, 0)
        consume(n - 1, 1)
    else:
        consume(n - 1, 0)


def _swiglu_scratch(tm, d, tf):
    return [pltpu.VMEM((tm, d), BF16), pltpu.VMEM((2, 2, tm, tf), F32)]


def _ffn_kernel(x_ref, g_ref, w13_ref, w2_ref, o_ref, h_s, gu_s):
    h_s[...] = _rms(x_ref[...], g_ref[...]).astype(BF16)
    o_ref[...] = x_ref[...]
    _swiglu_chunks(h_s, w13_ref, w2_ref, o_ref, gu_s)


def ffn_swiglu(x, g, w13, w2, *, tm, name):
    t, d = x.shape
    tf = w2.shape[1]
    resident = pl.Buffered(1)
    return pl.pallas_call(
        _ffn_kernel, grid=(t // tm,),
        in_specs=[
            pl.BlockSpec((tm, d), lambda i: (i, 0)),
            pl.BlockSpec((1, d), lambda i: (0, 0)),
            pl.BlockSpec(w13.shape, lambda i: (0, 0), pipeline_mode=resident),
            pl.BlockSpec(w2.shape, lambda i: (0, 0, 0), pipeline_mode=resident),
        ],
        out_specs=pl.BlockSpec((tm, d), lambda i: (i, 0)),
        out_shape=jax.ShapeDtypeStruct((t, d), F32),
        scratch_shapes=_swiglu_scratch(tm, d, tf),
        compiler_params=_params("parallel"), name=name,
    )(x, g.reshape(1, d), w13, w2)


def _conv_kernel(bg_ref, cg_ref, xi_ref, w_ref, o_ref):
    s, width = o_ref.shape
    n_taps = w_ref.shape[0]
    xc = cg_ref[...].astype(F32) * xi_ref[...].astype(F32)
    row = lax.broadcasted_iota(jnp.int32, (s, width), 0)
    y = w_ref[n_taps - 1:n_taps, :] * xc
    for back in range(1, n_taps):
        shifted = jnp.where(row >= back, pltpu.roll(xc, back, axis=0), 0.0)
        y = y + w_ref[n_taps - 1 - back:n_taps - back, :] * shifted
    o_ref[...] = (bg_ref[...].astype(F32) * y).astype(o_ref.dtype)


def short_conv(main, conv_w, *, batch, seq):
    width = conv_w.shape[1]
    return pl.pallas_call(
        _conv_kernel, grid=(batch,),
        in_specs=[
            pl.BlockSpec((seq, width), lambda b: (b, 0)),
            pl.BlockSpec((seq, width), lambda b: (b, 1)),
            pl.BlockSpec((seq, width), lambda b: (b, 2)),
            pl.BlockSpec(conv_w.shape, lambda b: (0, 0)),
        ],
        out_specs=pl.BlockSpec((seq, width), lambda b: (b, 0)),
        out_shape=jax.ShapeDtypeStruct((batch * seq, width), BF16),
        compiler_params=_params("parallel"), name="short_conv",
    )(main, main, main, conv_w)


def _diff_prep_kernel(rb_ref, lq1_ref, lk1_ref, lq2_ref, lk2_ref, bias_ref, far_ref, lam_ref, *, tq, lam_init):
    n_heads = bias_ref.shape[0]
    strip = 32
    row = lax.broadcasted_iota(jnp.int32, (strip, tq), 0)
    col = lax.broadcasted_iota(jnp.int32, (strip, tq), 1)
    max_exact = N_BUCKETS // 2

    def fill(r, carry):
        r0 = pl.multiple_of(r * strip, strip)
        for which in range(2):
            n = jnp.maximum(row + r0 - col + which * tq, 0)
            nf = jnp.maximum(n, 1).astype(F32)
            large = max_exact + (jnp.log(nf / max_exact) / math.log(MAX_DIST / max_exact)
                                 * (N_BUCKETS - max_exact)).astype(jnp.int32)
            large = jnp.minimum(large, N_BUCKETS - 1)
            bucket = jnp.where(n < max_exact, n, large)
            for h in range(n_heads):
                b = jnp.zeros((strip, tq), F32)
                for kk in range(N_BUCKETS):
                    b = jnp.where(bucket == kk, rb_ref[kk, h], b)
                bias_ref[h, which, pl.ds(r0, strip), :] = (b - rb_ref[N_BUCKETS - 1, h]) * LOG2E
        return carry

    lax.fori_loop(0, tq // strip, fill, 0)
    for h in range(n_heads):
        far_ref[h] = jnp.full((1, LANES), rb_ref[N_BUCKETS - 1, h], F32) * LOG2E
    lam = (jnp.exp(jnp.sum(lq1_ref[...] * lk1_ref[...], axis=1, keepdims=True))
           - jnp.exp(jnp.sum(lq2_ref[...] * lk2_ref[...], axis=1, keepdims=True)) + lam_init)
    lam_ref[...] = jnp.broadcast_to(lam, (1, LANES))


def diff_prep(rel_bias, lq1, lk1, lq2, lk2, *, tq, lam_init):
    n_heads = rel_bias.shape[1]
    vec = lambda a: a.reshape(1, -1)
    vspec = pl.BlockSpec(memory_space=pltpu.VMEM)
    return pl.pallas_call(
        functools.partial(_diff_prep_kernel, tq=tq, lam_init=lam_init),
        in_specs=[pl.BlockSpec(memory_space=pltpu.SMEM), vspec, vspec, vspec, vspec],
        out_specs=[vspec, vspec, vspec],
        out_shape=[
            jax.ShapeDtypeStruct((n_heads, 2, tq, tq), F32),
            jax.ShapeDtypeStruct((n_heads, 1, LANES), F32),
            jax.ShapeDtypeStruct((1, LANES), F32),
        ],
        compiler_params=pltpu.CompilerParams(vmem_limit_bytes=VMEM_LIMIT), name="diff_prep",
    )(rel_bias, vec(lq1), vec(lk1), vec(lq2), vec(lk2))


def _diff_kernel(q_ref, k_ref, v_ref, bias_ref, far_ref, lam_ref, subln_ref, o_ref, ka_s, m_s, l_s, acc_s, *,
                 tq, lam_init):
    i = pl.program_id(2)
    lane = lax.broadcasted_iota(jnp.int32, (1, LANES), 1)
    halves = _halves(lane)

    @pl.when(i == 0)
    def _():
        k = k_ref[...]
        far = _split3(far_ref[0])
        for sub in range(2):
            ka_s[sub] = _augment(k, halves[sub], lane, HEAD_DIM * (1 - sub), far, True)

    q = q_ref[...]
    zero = jnp.zeros((1, LANES), F32)
    qa = [_augment(q, halves[sub], lane, HEAD_DIM * (1 - sub), (zero,) * 3, False) for sub in range(2)]
    _causal_attention(qa, ka_s, v_ref, bias_ref, i, tq, m_s, l_s, acc_s)
    o = _normalised(l_s, acc_s, 0) - lam_ref[...] * _normalised(l_s, acc_s, 1)
    o_ref[...] = (_rms(o, subln_ref[...]) * (1.0 - lam_init)).astype(o_ref.dtype)


def diff_attention(main, bias, far, lam, subln, *, batch, seq, tq, q_col, k_col, v_col, lam_init):
    t = batch * seq
    nq = seq // tq
    n_heads = bias.shape[0]
    return pl.pallas_call(
        functools.partial(_diff_kernel, tq=tq, lam_init=lam_init),
        grid=(batch, n_heads, nq),
        in_specs=[
            pl.BlockSpec((tq, LANES), lambda b, h, i: (b * nq + i, q_col + h)),
            pl.BlockSpec((seq, LANES), lambda b, h, i: (b, k_col + h)),
            pl.BlockSpec((seq, LANES), lambda b, h, i: (b, v_col + h)),
            pl.BlockSpec((1, 2, tq, tq), lambda b, h, i: (h, 0, 0, 0)),
            pl.BlockSpec((1, 1, LANES), lambda b, h, i: (h, 0, 0)),
            pl.BlockSpec((1, LANES), lambda b, h, i: (0, 0)),
            pl.BlockSpec((1, LANES), lambda b, h, i: (0, 0)),
        ],
        out_specs=pl.BlockSpec((tq, LANES), lambda b, h, i: (b * nq + i, h)),
        out_shape=jax.ShapeDtypeStruct((t, n_heads * LANES), BF16),
        scratch_shapes=_attn_scratch(seq, tq),
        compiler_params=_params("parallel", "parallel", "arbitrary"), name="diff_attention",
    )(main, main, main, bias, far, lam, subln.reshape(1, LANES))


def _router_kernel(x_ref, g_ref, wr_ref, meta_ref, wts_ref, cnt_ref, carry_s, *, n_exp):
    tm = x_ref.shape[0]

    @pl.when(pl.program_id(0) == 0)
    def _():
        carry_s[...] = jnp.zeros_like(carry_s)

    h = _rms(x_ref[...], g_ref[...])
    logits = jnp.dot(h, wr_ref[...], precision=lax.Precision.HIGHEST, preferred_element_type=F32)
    lane = lax.broadcasted_iota(jnp.int32, (tm, LANES), 1)
    lane_f = lane.astype(F32)
    logits = jnp.where(lane < n_exp, logits, -jnp.inf)
    m1 = jnp.max(logits, axis=1, keepdims=True)
    i1 = jnp.min(jnp.where(logits == m1, lane_f, float(LANES)), axis=1, keepdims=True)
    rest = jnp.where(lane_f == i1, -jnp.inf, logits)
    m2 = jnp.max(rest, axis=1, keepdims=True)
    i2 = jnp.min(jnp.where(rest == m2, lane_f, float(LANES)), axis=1, keepdims=True)
    e = jnp.exp(m2 - m1)
    w1 = 1.0 / (1.0 + e)
    w2 = e / (1.0 + e)
    sel1 = lane_f == i1
    sel2 = lane_f == i2
    onehot = jnp.where(sel1 | sel2, 1.0, 0.0)
    row = lax.broadcasted_iota(jnp.int32, (tm, tm), 0)
    col = lax.broadcasted_iota(jnp.int32, (tm, tm), 1)
    before = jnp.where(row > col, 1.0, 0.0).astype(BF16)
    rank = _dot(before, onehot.astype(BF16)) + carry_s[...]
    r1 = jnp.sum(jnp.where(sel1, rank, 0.0), axis=1, keepdims=True)
    r2 = jnp.sum(jnp.where(sel2, rank, 0.0), axis=1, keepdims=True)
    meta = jnp.where(lane == 0, i1, jnp.where(lane == 1, i2, jnp.where(lane == 2, r1, jnp.where(lane == 3, r2, 0.0))))
    meta_ref[...] = meta.astype(jnp.int32)
    wts_ref[...] = jnp.where(lane == 0, w1, jnp.where(lane == 1, w2, 0.0))
    carry_s[...] += jnp.sum(onehot, axis=0, keepdims=True)
    cnt_ref[...] = carry_s[...]


def route_tokens(x, g, wr, *, tm, n_exp):
    t, d = x.shape
    return pl.pallas_call(
        functools.partial(_router_kernel, n_exp=n_exp),
        grid=(t // tm,),
        in_specs=[
            pl.BlockSpec((tm, d), lambda i: (i, 0)),
            pl.BlockSpec((1, d), lambda i: (0, 0)),
            pl.BlockSpec((d, LANES), lambda i: (0, 0)),
        ],
        out_specs=[
            pl.BlockSpec((tm, LANES), lambda i: (i, 0)),
            pl.BlockSpec((tm, LANES), lambda i: (i, 0)),
            pl.BlockSpec((1, LANES), lambda i: (0, 0)),
        ],
        out_shape=[
            jax.ShapeDtypeStruct((t, LANES), jnp.int32),
            jax.ShapeDtypeStruct((t, LANES), F32),
            jax.ShapeDtypeStruct((1, LANES), F32),
        ],
        scratch_shapes=[pltpu.VMEM((1, LANES), F32)],
        compiler_params=_params("arbitrary"), name="moe_router",
    )(x, g.reshape(1, d), wr)


def _row_copy(src, dst, s_row, d_row, sem):
    return pltpu.make_async_copy(src.at[pl.ds(s_row, 1)], dst.at[pl.ds(d_row, 1)], sem)


def _dispatch_kernel(pos_ref, x_ref, zeros_hbm, xs_hbm, sem, *, tm):
    del zeros_hbm
    base = pl.program_id(0) * tm
    n_tok = pl.num_programs(0) * tm

    def start(t, carry):
        for k in range(TOP_K):
            _row_copy(x_ref, xs_hbm, t, pos_ref[k * n_tok + base + t], sem).start()
        return carry

    def wait(t, carry):
        for k in range(TOP_K):
            _row_copy(x_ref, xs_hbm, t, pos_ref[k * n_tok + base + t], sem).wait()
        return carry

    lax.fori_loop(0, tm, start, 0)
    lax.fori_loop(0, tm, wait, 0)


def moe_dispatch(pos, x, n_rows, *, tm):
    t, d = x.shape
    zeros = jnp.zeros((n_rows, d), x.dtype)
    return pl.pallas_call(
        functools.partial(_dispatch_kernel, tm=tm),
        grid_spec=pltpu.PrefetchScalarGridSpec(
            num_scalar_prefetch=1, grid=(t // tm,),
            in_specs=[pl.BlockSpec((tm, d), lambda i, pos: (i, 0)), pl.BlockSpec(memory_space=pl.ANY)],
            out_specs=pl.BlockSpec(memory_space=pl.ANY),
            scratch_shapes=[pltpu.SemaphoreType.DMA(())],
        ),
        out_shape=jax.ShapeDtypeStruct((n_rows, d), x.dtype),
        input_output_aliases={2: 0},
        compiler_params=_params("arbitrary"),
        name="moe_dispatch",
    )(pos, x, zeros)


def _expert_kernel(te_ref, nu_ref, xs_ref, g_ref, w13_ref, w2_ref, o_ref, h_s, gu_s):
    del te_ref
    used = pl.program_id(0) < nu_ref[0]
    o_ref[...] = jnp.zeros_like(o_ref)

    @pl.when(used)
    def _():
        h_s[...] = _rms(xs_ref[...], g_ref[...]).astype(BF16)
        _swiglu_chunks(h_s, w13_ref.at[0], w2_ref.at[0], o_ref, gu_s)


def moe_experts(tile_expert, n_used, xs, g, w13, w2, *, tm):
    n_rows, d = xs.shape
    tf = w2.shape[2]
    resident = pl.Buffered(1)
    return pl.pallas_call(
        _expert_kernel,
        grid_spec=pltpu.PrefetchScalarGridSpec(
            num_scalar_prefetch=2, grid=(n_rows // tm,),
            in_specs=[
                pl.BlockSpec((tm, d), lambda i, te, nu: (i, 0)),
                pl.BlockSpec((1, d), lambda i, te, nu: (0, 0)),
                pl.BlockSpec((1,) + w13.shape[1:], lambda i, te, nu: (te[i], 0, 0), pipeline_mode=resident),
                pl.BlockSpec((1,) + w2.shape[1:], lambda i, te, nu: (te[i], 0, 0, 0), pipeline_mode=resident),
            ],
            out_specs=pl.BlockSpec((tm, d), lambda i, te, nu: (i, 0)),
            scratch_shapes=_swiglu_scratch(tm, d, tf),
        ),
        out_shape=jax.ShapeDtypeStruct((n_rows, d), F32),
        compiler_params=_params("arbitrary"), name="moe_experts",
    )(tile_expert, n_used, xs, g.reshape(1, d), w13, w2)


def _combine_kernel(pos_ref, x_ref, wts_ref, g_ref, y_hbm, o_ref, rows_s, sem, *, tm):
    base = pl.program_id(0) * tm
    n_tok = pl.num_programs(0) * tm

    def start(t, carry):
        for k in range(TOP_K):
            _row_copy(y_hbm, rows_s.at[k], pos_ref[k * n_tok + base + t], t, sem).start()
        return carry

    def wait(t, carry):
        for k in range(TOP_K):
            _row_copy(y_hbm, rows_s.at[k], pos_ref[k * n_tok + base + t], t, sem).wait()
        return carry

    lax.fori_loop(0, tm, start, 0)
    lax.fori_loop(0, tm, wait, 0)
    wts = wts_ref[...]
    y = x_ref[...]
    for k in range(TOP_K):
        y = y + wts[:, k:k + 1] * rows_s[k]
    o_ref[...] = _rms(y, g_ref[...])


def moe_combine(pos, x, wts, g, y, *, tm):
    t, d = x.shape
    return pl.pallas_call(
        functools.partial(_combine_kernel, tm=tm),
        grid_spec=pltpu.PrefetchScalarGridSpec(
            num_scalar_prefetch=1, grid=(t // tm,),
            in_specs=[
                pl.BlockSpec((tm, d), lambda i, pos: (i, 0)),
                pl.BlockSpec((tm, LANES), lambda i, pos: (i, 0)),
                pl.BlockSpec((1, d), lambda i, pos: (0, 0)),
                pl.BlockSpec(memory_space=pl.ANY),
            ],
            out_specs=pl.BlockSpec((tm, d), lambda i, pos: (i, 0)),
            scratch_shapes=[pltpu.VMEM((TOP_K, tm, d), F32), pltpu.SemaphoreType.DMA(())],
        ),
        out_shape=jax.ShapeDtypeStruct((t, d), F32),
        compiler_params=_params("arbitrary"), name="moe_combine",
    )(pos, x, wts, g.reshape(1, d), y)


def _pad_cols(w, n):
    return jnp.pad(w, ((0, 0), (0, n - w.shape[1])))


def _even_mixer(x, norm, w_in, b_f, sgu_norm, w_s, b_s, w_out, *, batch, seq, tq):
    n_heads = b_f.shape[0]
    a_width = n_heads * HEAD_DIM
    b_width = w_s.shape[0] * LANES
    f0 = 3 * a_width
    w_main = jnp.concatenate([w_in[:, :f0], w_in[:, f0 + n_heads:]], axis=1)
    w_gate = _pad_cols(w_in[:, f0:f0 + n_heads], LANES)
    main, gate = norm_matmul(x, norm, [w_main.astype(BF16), w_gate.astype(BF16)], [BF16, F32],
                             tm=512, name="even_in_proj", scaled=(0, a_width, LOG2E * HEAD_DIM ** -0.5))
    c = gate_cumsum(gate, _pad_cols(b_f.reshape(1, -1), LANES), seq=seq)
    n_pairs = a_width // LANES
    a = fox_attention(main, c, batch=batch, seq=seq, tq=tq, q_col=0, k_col=n_pairs, v_col=2 * n_pairs,
                      n_pairs=n_pairs)
    u_col = f0 // b_width
    g = spatial_gating(main, sgu_norm, w_s, b_s, tb=512, u_col=u_col, v_col=u_col + 1)
    w_out = w_out.astype(BF16)
    return proj_residual(x, a, g, w_out[:a_width], w_out[a_width:], tm=512, name="even_out_proj")


def _odd_mixer(x, norm, w_in, conv_w, lq1, lk1, lq2, lk2, subln, w_out, rel_bias, lam_init, *, batch, seq, tq):
    c_width = conv_w.shape[1]
    d_width = rel_bias.shape[1] * 2 * HEAD_DIM
    q0 = 3 * c_width
    (main,) = norm_matmul(x, norm, [w_in.astype(BF16)], [BF16], tm=512, name="odd_in_proj",
                          scaled=(q0, q0 + d_width, LOG2E * HEAD_DIM ** -0.5))
    c_out = short_conv(main, conv_w, batch=batch, seq=seq)
    bias, far, lam = diff_prep(rel_bias, lq1, lk1, lq2, lk2, tq=tq, lam_init=lam_init)
    q_col = q0 // LANES
    n_heads = rel_bias.shape[1]
    d_out = diff_attention(main, bias, far, lam, subln, batch=batch, seq=seq, tq=tq, q_col=q_col,
                           k_col=q_col + n_heads, v_col=q_col + 2 * n_heads, lam_init=lam_init)
    w_out = w_out.astype(BF16)
    return proj_residual(x, c_out, d_out, w_out[:c_width], w_out[c_width:], tm=512, name="odd_out_proj")


def _moe_layer(x, norm, w_router, w13, w2, final_norm, *, tm_expert):
    t, d = x.shape
    n_exp = w_router.shape[1]
    meta, wts, counts = route_tokens(x, norm, _pad_cols(w_router, LANES), tm=512, n_exp=n_exp)
    counts = counts[0, :n_exp].astype(jnp.int32)
    padded = (counts + tm_expert - 1) // tm_expert * tm_expert
    ends = jnp.cumsum(padded)
    offsets = ends - padded
    pos = jnp.concatenate([offsets[meta[:, k]] + meta[:, TOP_K + k] for k in range(TOP_K)]).astype(jnp.int32)
    n_rows = TOP_K * t + n_exp * tm_expert
    n_tiles = n_rows // tm_expert
    tile_start = jnp.arange(n_tiles, dtype=jnp.int32) * tm_expert
    tile_expert = jnp.minimum(jnp.sum(tile_start[:, None] >= ends[None, :], axis=1), n_exp - 1).astype(jnp.int32)
    n_used = (ends[-1:] // tm_expert).astype(jnp.int32)
    xs = moe_dispatch(pos, x, n_rows, tm=512)
    y = moe_experts(tile_expert, n_used, xs, norm, *_chunk_weights(w13, w2, 512), tm=tm_expert)
    return moe_combine(pos, x, wts, final_norm, y, tm=256)


def kernel(x, mem, rel_bias, mem_norm, final_norm, ev_norm, ev_w_in, ev_b_f, ev_sgu_norm, ev_w_s, ev_b_s, ev_w_out, ffn_w13, ffn_w2, od_norm, od_w_in, od_conv_w, od_lam_q1, od_lam_k1, od_lam_q2, od_lam_k2, od_subln, od_w_out, moe_router, moe_w13, moe_w2, x_norm, x_wq, x_wkv, x_wo, ffn_norm):
    batch, seq, d = x.shape
    mem_len = mem.shape[1]
    depth = x_norm.shape[0]
    assert depth == 2 and ev_norm.shape[0] == 1 and od_norm.shape[0] == 1
    x_heads, x_dh = 4, 128
    xf = x.reshape(batch * seq, d)
    wkv = jnp.concatenate([x_wkv[layer] for layer in range(depth)], axis=1).astype(BF16)
    (kv,) = norm_matmul(mem.reshape(batch * mem_len, d), mem_norm, [wkv], [BF16], tm=512, name="mem_kv")

    def cross(xf, layer):
        return cross_attention(xf, x_norm[layer], x_wq[layer].astype(BF16), kv, x_wo[layer].astype(BF16),
                               tm=512, seq=seq, mem_len=mem_len, kv_col=layer, n_heads=x_heads, dh=x_dh,
                               name=f"cross_attention_{layer}")

    xf = _even_mixer(xf, ev_norm[0], ev_w_in[0], ev_b_f[0], ev_sgu_norm[0], ev_w_s[0], ev_b_s[0], ev_w_out[0],
                     batch=batch, seq=seq, tq=ATTN_TILE)
    xf = cross(xf, 0)
    xf = ffn_swiglu(xf, ffn_norm[0], *_chunk_weights(ffn_w13[0], ffn_w2[0], 256), tm=512, name="ffn_swiglu")
    lam_init = 0.8 - 0.6 * math.exp(-0.3 * 1)
    xf = _odd_mixer(xf, od_norm[0], od_w_in[0], od_conv_w[0], od_lam_q1[0], od_lam_k1[0], od_lam_q2[0],
                    od_lam_k2[0], od_subln[0], od_w_out[0], rel_bias, lam_init, batch=batch, seq=seq, tq=ATTN_TILE)
    xf = cross(xf, 1)
    out = _moe_layer(xf, ffn_norm[1], moe_router[0], moe_w13[0], moe_w2[0], final_norm, tm_expert=512)
    return out.reshape(batch, seq, d)
```

```python
import functools
import math

import jax
import jax.numpy as jnp
from jax import lax
from jax.experimental import pallas as pl
from jax.experimental.pallas import tpu as pltpu

F32 = jnp.float32
BF16 = jnp.bfloat16
EPS = 1e-6
HEAD_DIM = 64
LANES = 128
SUBLANES = 8
BF16_ROWS = 16
N_BUCKETS = 32
MAX_DIST = 128
TOP_K = 2
LOG2E = 1.4426950408889634
ATTN_TILE = 512
VMEM_LIMIT = 56 * 1024 * 1024


def _params(*sem):
    return pltpu.CompilerParams(dimension_semantics=sem, vmem_limit_bytes=VMEM_LIMIT)


def _rms(x, g):
    ms = jnp.mean(x * x, axis=-1, keepdims=True)
    return x * lax.rsqrt(ms + EPS) * g


def _dot(a, b):
    return jnp.dot(a, b, preferred_element_type=F32)


def _dot_nt(a, b):
    return lax.dot_general(a, b, (((1,), (1,)), ((), ())), preferred_element_type=F32)


def _norm_matmul_kernel(x_ref, g_ref, *refs, n_w, chunk, scaled):
    w_refs, o_refs = refs[:n_w], refs[n_w:]
    h = _rms(x_ref[...], g_ref[...]).astype(BF16)
    s0, s1, scale = scaled
    for k, (w_ref, o_ref) in enumerate(zip(w_refs, o_refs)):
        n = w_ref.shape[1]
        for c0 in range(0, n, chunk):
            c1 = min(c0 + chunk, n)
            y = _dot(h, w_ref[:, c0:c1])
            if k == 0 and s0 <= c0 and c1 <= s1:
                y = y * scale
            o_ref[:, c0:c1] = y.astype(o_ref.dtype)


def norm_matmul(x, g, ws, out_dtypes, *, tm, name, scaled=(0, 0, 1.0)):
    t, d = x.shape
    chunk = 512
    assert scaled[0] % chunk == 0 and scaled[1] % chunk == 0
    in_specs = [pl.BlockSpec((tm, d), lambda i: (i, 0)), pl.BlockSpec((1, d), lambda i: (0, 0))]
    in_specs += [pl.BlockSpec(w.shape, lambda i: (0, 0)) for w in ws]
    out_specs = [pl.BlockSpec((tm, w.shape[1]), lambda i: (i, 0)) for w in ws]
    out_shape = [jax.ShapeDtypeStruct((t, w.shape[1]), dt) for w, dt in zip(ws, out_dtypes)]
    return pl.pallas_call(
        functools.partial(_norm_matmul_kernel, n_w=len(ws), chunk=chunk, scaled=scaled),
        grid=(t // tm,), in_specs=in_specs, out_specs=out_specs, out_shape=out_shape,
        compiler_params=_params("parallel"), name=name,
    )(x, g.reshape(1, d), *ws)


def _gate_kernel(g_ref, b_ref, c_ref):
    s = g_ref.shape[0]
    row = lax.broadcasted_iota(jnp.int32, (LANES, LANES), 0)
    col = lax.broadcasted_iota(jnp.int32, (LANES, LANES), 1)
    tri = (row >= col).astype(F32)
    carry = jnp.zeros((1, LANES), F32)
    for blk in range(s // LANES):
        z = g_ref[blk * LANES:(blk + 1) * LANES, :] + b_ref[...]
        log_f = jnp.minimum(z, 0.0) - jnp.log1p(jnp.exp(-jnp.abs(z)))
        cs = jnp.dot(tri, log_f, precision=lax.Precision.HIGHEST, preferred_element_type=F32) + carry
        c_ref[blk * LANES:(blk + 1) * LANES, :] = cs
        carry = cs[LANES - 1:LANES, :]


def gate_cumsum(g, b, *, seq):
    t = g.shape[0]
    return pl.pallas_call(
        _gate_kernel, grid=(t // seq,),
        in_specs=[pl.BlockSpec((seq, LANES), lambda i: (i, 0)), pl.BlockSpec((1, LANES), lambda i: (0, 0))],
        out_specs=pl.BlockSpec((seq, LANES), lambda i: (i, 0)),
        out_shape=jax.ShapeDtypeStruct((t, LANES), F32),
        compiler_params=_params("parallel"), name="gate_cumsum",
    )(g, b)


def _split3(x):
    hi = x.astype(BF16).astype(F32)
    rest = x - hi
    mid = rest.astype(BF16).astype(F32)
    lo = (rest - mid).astype(BF16).astype(F32)
    return hi, mid, lo


def _augment(x, in_half, lane, base, pieces, pieces_first):
    n = len(pieces)
    p0, o0 = (base, base + n) if pieces_first else (base + n, base)
    aug = jnp.where((lane >= o0) & (lane < o0 + n), 1.0, 0.0)
    for idx, piece in enumerate(pieces):
        aug = jnp.where(lane == p0 + idx, piece, aug)
    return jnp.where(in_half, x, aug.astype(x.dtype))


def _halves(lane):
    return [(lane >= HEAD_DIM * hh) & (lane < HEAD_DIM * (hh + 1)) for hh in range(2)]


def _causal(tq):
    row = lax.broadcasted_iota(jnp.int32, (tq, tq), 0)
    col = lax.broadcasted_iota(jnp.int32, (tq, tq), 1)
    return row >= col


def _causal_attention(qa, ka_s, v_ref, bias_ref, i, tq, m_s, l_s, acc_s):
    n_chunks = tq // LANES

    def scores(h, j, near, mask):
        start = pl.multiple_of(j * tq, tq)
        s = _dot_nt(qa[h], ka_s[h, pl.ds(start, tq), :])
        if near is not None:
            s = s + bias_ref[0, near]
        if mask is not None:
            s = jnp.where(mask, s, -jnp.inf)
        return [s[:, c * LANES:(c + 1) * LANES] for c in range(n_chunks)]

    def max_tile(j, near, mask):
        for h in range(2):
            m = m_s[h]
            for chunk in scores(h, j, near, mask):
                m = jnp.maximum(m, chunk)
            m_s[h] = m

    def sum_tile(j, near, mask):
        v = v_ref[pl.ds(pl.multiple_of(j * tq, tq), tq), :]
        for h in range(2):
            m = m_s[h]
            ps = [jnp.exp2(chunk - m) for chunk in scores(h, j, near, mask)]
            l_s[h] += functools.reduce(lambda a, b: a + b, ps)
            acc_s[h] += _dot(jnp.concatenate(ps, axis=1).astype(BF16), v)

    def key_tiles(fn):
        def run(lo, hi, near):
            def body(j, carry):
                fn(j, near, None)
                return carry
            lax.fori_loop(lo, hi, body, 0)

        if bias_ref is None:
            run(0, i, None)
            fn(i, None, _causal(tq))
        else:
            n_far = jnp.maximum(i - 1, 0)
            run(0, n_far, None)
            run(n_far, i, 1)
            fn(i, 0, _causal(tq))

    m_s[...] = jnp.full(m_s.shape, -jnp.inf, F32)
    key_tiles(max_tile)
    for h in range(2):
        m_s[h] = jnp.broadcast_to(jnp.max(m_s[h], axis=1, keepdims=True), (tq, LANES))
    l_s[...] = jnp.zeros(l_s.shape, F32)
    acc_s[...] = jnp.zeros(acc_s.shape, F32)
    key_tiles(sum_tile)


def _normalised(l_s, acc_s, h):
    return acc_s[h] / jnp.sum(l_s[h], axis=1, keepdims=True)


def _fox_kernel(q_ref, k_ref, v_ref, c_ref, o_ref, ka_s, m_s, l_s, acc_s, *, tq):
    hp = pl.program_id(1)
    i = pl.program_id(2)
    lane = lax.broadcasted_iota(jnp.int32, (1, LANES), 1)
    halves = _halves(lane)

    def decay(c, hh):
        return jnp.sum(jnp.where(lane == 2 * hp + hh, c, 0.0), axis=1, keepdims=True) * LOG2E

    @pl.when(i == 0)
    def _():
        k = k_ref[...]
        c_all = c_ref[...]
        for hh in range(2):
            ka_s[hh] = _augment(k, halves[hh], lane, HEAD_DIM * (1 - hh), _split3(-decay(c_all, hh)), True)

    q = q_ref[...]
    c_q = c_ref[pl.ds(pl.multiple_of(i * tq, tq), tq), :]
    qa = [_augment(q, halves[hh], lane, HEAD_DIM * (1 - hh), _split3(decay(c_q, hh)), False) for hh in range(2)]
    _causal_attention(qa, ka_s, v_ref, None, i, tq, m_s, l_s, acc_s)
    out = jnp.where(lane < HEAD_DIM, _normalised(l_s, acc_s, 0), _normalised(l_s, acc_s, 1))
    o_ref[...] = out.astype(o_ref.dtype)


def _attn_scratch(seq, tq):
    return [pltpu.VMEM((2, seq, LANES), BF16)] + [pltpu.VMEM((2, tq, LANES), F32)] * 3


def fox_attention(qkv, c, *, batch, seq, tq, q_col, k_col, v_col, n_pairs):
    t = batch * seq
    nq = seq // tq
    return pl.pallas_call(
        functools.partial(_fox_kernel, tq=tq),
        grid=(batch, n_pairs, nq),
        in_specs=[
            pl.BlockSpec((tq, LANES), lambda b, h, i: (b * nq + i, q_col + h)),
            pl.BlockSpec((seq, LANES), lambda b, h, i: (b, k_col + h)),
            pl.BlockSpec((seq, LANES), lambda b, h, i: (b, v_col + h)),
            pl.BlockSpec((seq, LANES), lambda b, h, i: (b, 0)),
        ],
        out_specs=pl.BlockSpec((tq, LANES), lambda b, h, i: (b * nq + i, h)),
        out_shape=jax.ShapeDtypeStruct((t, n_pairs * LANES), BF16),
        scratch_shapes=_attn_scratch(seq, tq),
        compiler_params=_params("parallel", "parallel", "arbitrary"), name="fox_attention",
    )(qkv, qkv, qkv, c)


def _sgu_kernel(u_ref, v_ref, norm_ref, ws_ref, bs_ref, o_ref, *, n_groups, chunk):
    tb = u_ref.shape[0]
    row = lax.broadcasted_iota(jnp.int32, (chunk, chunk), 0)
    col = lax.broadcasted_iota(jnp.int32, (chunk, chunk), 1)
    tri = row >= col
    for g in range(n_groups):
        w = jnp.where(tri, ws_ref[g], 0.0).astype(BF16)
        bias = bs_ref[:, g:g + 1]
        gain = norm_ref[g:g + 1, :]
        for c in range(tb // chunk):
            rs = slice(c * chunk, (c + 1) * chunk)
            cs = slice(g * LANES, (g + 1) * LANES)
            vn = _rms(jax.nn.gelu(v_ref[rs, cs].astype(F32)), gain)
            mixed = _dot(w, vn.astype(BF16)) + bias
            o_ref[rs, cs] = (jax.nn.gelu(u_ref[rs, cs].astype(F32)) * mixed).astype(o_ref.dtype)


def spatial_gating(main, sgu_norm, w_s, b_s, *, tb, u_col, v_col):
    t = main.shape[0]
    n_groups, chunk, _ = w_s.shape
    width = n_groups * LANES
    return pl.pallas_call(
        functools.partial(_sgu_kernel, n_groups=n_groups, chunk=chunk),
        grid=(t // tb,),
        in_specs=[
            pl.BlockSpec((tb, width), lambda i: (i, u_col)),
            pl.BlockSpec((tb, width), lambda i: (i, v_col)),
            pl.BlockSpec(sgu_norm.shape, lambda i: (0, 0)),
            pl.BlockSpec(w_s.shape, lambda i: (0, 0, 0)),
            pl.BlockSpec((chunk, n_groups), lambda i: (0, 0)),
        ],
        out_specs=pl.BlockSpec((tb, width), lambda i: (i, 0)),
        out_shape=jax.ShapeDtypeStruct((t, width), BF16),
        compiler_params=_params("parallel"), name="spatial_gating",
    )(main, main, sgu_norm, w_s, b_s.T)


def _proj_res_kernel(x_ref, a_ref, b_ref, wa_ref, wb_ref, o_ref):
    o_ref[...] = x_ref[...] + _dot(a_ref[...], wa_ref[...]) + _dot(b_ref[...], wb_ref[...])


def proj_residual(x, a, b, wa, wb, *, tm, name):
    t, d = x.shape
    return pl.pallas_call(
        _proj_res_kernel, grid=(t // tm,),
        in_specs=[
            pl.BlockSpec((tm, d), lambda i: (i, 0)),
            pl.BlockSpec((tm, a.shape[1]), lambda i: (i, 0)),
            pl.BlockSpec((tm, b.shape[1]), lambda i: (i, 0)),
            pl.BlockSpec(wa.shape, lambda i: (0, 0)),
            pl.BlockSpec(wb.shape, lambda i: (0, 0)),
        ],
        out_specs=pl.BlockSpec((tm, d), lambda i: (i, 0)),
        out_shape=jax.ShapeDtypeStruct((t, d), F32),
        compiler_params=_params("parallel"), name=name,
    )(x, a, b, wa, wb)


def _cross_kernel(x_ref, g_ref, wq_ref, kv_ref, wo_ref, o_ref, *, n_heads, dh):
    x = x_ref[...]
    h = _rms(x, g_ref[...]).astype(BF16)
    q = _dot(h, wq_ref[...]).astype(BF16)
    width = n_heads * dh
    outs = []
    for hd in range(n_heads):
        cs = slice(hd * dh, (hd + 1) * dh)
        s = _dot_nt(q[:, cs], kv_ref[:, cs]) * (dh ** -0.5)
        m = jnp.max(s, axis=1, keepdims=True)
        p = jnp.exp(s - m)
        p = p / jnp.sum(p, axis=1, keepdims=True)
        outs.append(_dot(p.astype(BF16), kv_ref[:, width + hd * dh:width + (hd + 1) * dh]).astype(BF16))
    o = jnp.concatenate(outs, axis=1)
    o_ref[...] = x + _dot(o, wo_ref[...])


def cross_attention(x, g, wq, kv, wo, *, tm, seq, mem_len, kv_col, n_heads, dh, name):
    t, d = x.shape
    per_b = seq // tm
    width = n_heads * dh
    return pl.pallas_call(
        functools.partial(_cross_kernel, n_heads=n_heads, dh=dh),
        grid=(t // tm,),
        in_specs=[
            pl.BlockSpec((tm, d), lambda i: (i, 0)),
            pl.BlockSpec((1, d), lambda i: (0, 0)),
            pl.BlockSpec(wq.shape, lambda i: (0, 0)),
            pl.BlockSpec((mem_len, 2 * width), lambda i: (i // per_b, kv_col)),
            pl.BlockSpec(wo.shape, lambda i: (0, 0)),
        ],
        out_specs=pl.BlockSpec((tm, d), lambda i: (i, 0)),
        out_shape=jax.ShapeDtypeStruct((t, d), F32),
        compiler_params=_params("parallel"), name=name,
    )(x, g.reshape(1, d), wq, kv, wo)


def _chunk_weights(w13, w2, tf):
    *lead, ff, d = w2.shape
    return w13.astype(BF16), w2.astype(BF16).reshape(*lead, ff // tf, tf, d)


def _swiglu_chunks(h_s, w13, w2, o_ref, gu_s):
    n, tf, _ = w2.shape

    def project(c, slot):
        for part in range(2):
            cols = pl.ds(pl.multiple_of((part * n + c) * tf, tf), tf)
            gu_s[slot, part] = _dot(h_s[...], w13[:, cols])

    def consume(c, slot):
        gate = gu_s[slot, 0]
        act = (gate * jax.nn.sigmoid(gate) * gu_s[slot, 1]).astype(BF16)
        o_ref[...] += _dot(act, w2[c])

    def pair(k, carry):
        c = 2 * k
        project(c + 1, 1)
        consume(c, 0)
        project(c + 2, 0)
        consume(c + 1, 1)
        return carry

    project(0, 0)
    lax.fori_loop(0, (n - 1) // 2, pair, 0)
    if n % 2 == 0:
        project(n - 1, 1)
        consume(n - 2, 0)
        consume(n - 1, 1)
    else:
        consume(n - 1, 0)


def _swiglu_scratch(tm, d, tf):
    return [pltpu.VMEM((tm, d), BF16), pltpu.VMEM((2, 2, tm, tf), F32)]


def _ffn_kernel(x_ref, g_ref, w13_ref, w2_ref, o_ref, h_s, gu_s):
    h_s[...] = _rms(x_ref[...], g_ref[...]).astype(BF16)
    o_ref[...] = x_ref[...]
    _swiglu_chunks(h_s, w13_ref, w2_ref, o_ref, gu_s)


def ffn_swiglu(x, g, w13, w2, *, tm, name):
    t, d = x.shape
    tf = w2.shape[1]
    resident = pl.Buffered(1)
    return pl.pallas_call(
        _ffn_kernel, grid=(t // tm,),
        in_specs=[
            pl.BlockSpec((tm, d), lambda i: (i, 0)),
            pl.BlockSpec((1, d), lambda i: (0, 0)),
            pl.BlockSpec(w13.shape, lambda i: (0, 0), pipeline_mode=resident),
            pl.BlockSpec(w2.shape, lambda i: (0, 0, 0), pipeline_mode=resident),
        ],
        out_specs=pl.BlockSpec((tm, d), lambda i: (i, 0)),
        out_shape=jax.ShapeDtypeStruct((t, d), F32),
        scratch_shapes=_swiglu_scratch(tm, d, tf),
        compiler_params=_params("parallel"), name=name,
    )(x, g.reshape(1, d), w13, w2)


def _conv_kernel(bg_ref, cg_ref, xi_ref, w_ref, o_ref):
    s, width = o_ref.shape
    n_taps = w_ref.shape[0]
    xc = cg_ref[...].astype(F32) * xi_ref[...].astype(F32)
    row = lax.broadcasted_iota(jnp.int32, (s, width), 0)
    y = w_ref[n_taps - 1:n_taps, :] * xc
    for back in range(1, n_taps):
        shifted = jnp.where(row >= back, pltpu.roll(xc, back, axis=0), 0.0)
        y = y + w_ref[n_taps - 1 - back:n_taps - back, :] * shifted
    o_ref[...] = (bg_ref[...].astype(F32) * y).astype(o_ref.dtype)


def short_conv(main, conv_w, *, batch, seq):
    width = conv_w.shape[1]
    return pl.pallas_call(
        _conv_kernel, grid=(batch,),
        in_specs=[
            pl.BlockSpec((seq, width), lambda b: (b, 0)),
            pl.BlockSpec((seq, width), lambda b: (b, 1)),
            pl.BlockSpec((seq, width), lambda b: (b, 2)),
            pl.BlockSpec(conv_w.shape, lambda b: (0, 0)),
        ],
        out_specs=pl.BlockSpec((seq, width), lambda b: (b, 0)),
        out_shape=jax.ShapeDtypeStruct((batch * seq, width), BF16),
        compiler_params=_params("parallel"), name="short_conv",
    )(main, main, main, conv_w)


def _diff_prep_kernel(rb_ref, lq1_ref, lk1_ref, lq2_ref, lk2_ref, bias_ref, far_ref, lam_ref, *, tq, lam_init):
    n_heads = bias_ref.shape[0]
    strip = 32
    row = lax.broadcasted_iota(jnp.int32, (strip, tq), 0)
    col = lax.broadcasted_iota(jnp.int32, (strip, tq), 1)
    max_exact = N_BUCKETS // 2

    def fill(r, carry):
        r0 = pl.multiple_of(r * strip, strip)
        for which in range(2):
            n = jnp.maximum(row + r0 - col + which * tq, 0)
            nf = jnp.maximum(n, 1).astype(F32)
            large = max_exact + (jnp.log(nf / max_exact) / math.log(MAX_DIST / max_exact)
                                 * (N_BUCKETS - max_exact)).astype(jnp.int32)
            large = jnp.minimum(large, N_BUCKETS - 1)
            bucket = jnp.where(n < max_exact, n, large)
            for h in range(n_heads):
                b = jnp.zeros((strip, tq), F32)
                for kk in range(N_BUCKETS):
                    b = jnp.where(bucket == kk, rb_ref[kk, h], b)
                bias_ref[h, which, pl.ds(r0, strip), :] = (b - rb_ref[N_BUCKETS - 1, h]) * LOG2E
        return carry

    lax.fori_loop(0, tq // strip, fill, 0)
    for h in range(n_heads):
        far_ref[h] = jnp.full((1, LANES), rb_ref[N_BUCKETS - 1, h], F32) * LOG2E
    lam = (jnp.exp(jnp.sum(lq1_ref[...] * lk1_ref[...], axis=1, keepdims=True))
           - jnp.exp(jnp.sum(lq2_ref[...] * lk2_ref[...], axis=1, keepdims=True)) + lam_init)
    lam_ref[...] = jnp.broadcast_to(lam, (1, LANES))


def diff_prep(rel_bias, lq1, lk1, lq2, lk2, *, tq, lam_init):
    n_heads = rel_bias.shape[1]
    vec = lambda a: a.reshape(1, -1)
    vspec = pl.BlockSpec(memory_space=pltpu.VMEM)
    return pl.pallas_call(
        functools.partial(_diff_prep_kernel, tq=tq, lam_init=lam_init),
        in_specs=[pl.BlockSpec(memory_space=pltpu.SMEM), vspec, vspec, vspec, vspec],
        out_specs=[vspec, vspec, vspec],
        out_shape=[
            jax.ShapeDtypeStruct((n_heads, 2, tq, tq), F32),
            jax.ShapeDtypeStruct((n_heads, 1, LANES), F32),
            jax.ShapeDtypeStruct((1, LANES), F32),
        ],
        compiler_params=pltpu.CompilerParams(vmem_limit_bytes=VMEM_LIMIT), name="diff_prep",
    )(rel_bias, vec(lq1), vec(lk1), vec(lq2), vec(lk2))


def _diff_kernel(q_ref, k_ref, v_ref, bias_ref, far_ref, lam_ref, subln_ref, o_ref, ka_s, m_s, l_s, acc_s, *,
                 tq, lam_init):
    i = pl.program_id(2)
    lane = lax.broadcasted_iota(jnp.int32, (1, LANES), 1)
    halves = _halves(lane)

    @pl.when(i == 0)
    def _():
        k = k_ref[...]
        far = _split3(far_ref[0])
        for sub in range(2):
            ka_s[sub] = _augment(k, halves[sub], lane, HEAD_DIM * (1 - sub), far, True)

    q = q_ref[...]
    zero = jnp.zeros((1, LANES), F32)
    qa = [_augment(q, halves[sub], lane, HEAD_DIM * (1 - sub), (zero,) * 3, False) for sub in range(2)]
    _causal_attention(qa, ka_s, v_ref, bias_ref, i, tq, m_s, l_s, acc_s)
    o = _normalised(l_s, acc_s, 0) - lam_ref[...] * _normalised(l_s, acc_s, 1)
    o_ref[...] = (_rms(o, subln_ref[...]) * (1.0 - lam_init)).astype(o_ref.dtype)


def diff_attention(main, bias, far, lam, subln, *, batch, seq, tq, q_col, k_col, v_col, lam_init):
    t = batch * seq
    nq = seq // tq
    n_heads = bias.shape[0]
    return pl.pallas_call(
        functools.partial(_diff_kernel, tq=tq, lam_init=lam_init),
        grid=(batch, n_heads, nq),
        in_specs=[
            pl.BlockSpec((tq, LANES), lambda b, h, i: (b * nq + i, q_col + h)),
            pl.BlockSpec((seq, LANES), lambda b, h, i: (b, k_col + h)),
            pl.BlockSpec((seq, LANES), lambda b, h, i: (b, v_col + h)),
            pl.BlockSpec((1, 2, tq, tq), lambda b, h, i: (h, 0, 0, 0)),
            pl.BlockSpec((1, 1, LANES), lambda b, h, i: (h, 0, 0)),
            pl.BlockSpec((1, LANES), lambda b, h, i: (0, 0)),
            pl.BlockSpec((1, LANES), lambda b, h, i: (0, 0)),
        ],
        out_specs=pl.BlockSpec((tq, LANES), lambda b, h, i: (b * nq + i, h)),
        out_shape=jax.ShapeDtypeStruct((t, n_heads * LANES), BF16),
        scratch_shapes=_attn_scratch(seq, tq),
        compiler_params=_params("parallel", "parallel", "arbitrary"), name="diff_attention",
    )(main, main, main, bias, far, lam, subln.reshape(1, LANES))


def _router_kernel(x_ref, g_ref, wr_ref, meta_ref, wts_ref, before_ref, cnt_ref, carry_s, *, n_exp):
    tm = x_ref.shape[0]

    @pl.when(pl.program_id(0) == 0)
    def _():
        carry_s[...] = jnp.zeros_like(carry_s)

    before_ref[...] = jnp.broadcast_to(carry_s[...], before_ref.shape)

    h = _rms(x_ref[...], g_ref[...])
    logits = jnp.dot(h, wr_ref[...], precision=lax.Precision.HIGHEST, preferred_element_type=F32)
    lane = lax.broadcasted_iota(jnp.int32, (tm, LANES), 1)
    lane_f = lane.astype(F32)
    logits = jnp.where(lane < n_exp, logits, -jnp.inf)
    m1 = jnp.max(logits, axis=1, keepdims=True)
    i1 = jnp.min(jnp.where(logits == m1, lane_f, float(LANES)), axis=1, keepdims=True)
    rest = jnp.where(lane_f == i1, -jnp.inf, logits)
    m2 = jnp.max(rest, axis=1, keepdims=True)
    i2 = jnp.min(jnp.where(rest == m2, lane_f, float(LANES)), axis=1, keepdims=True)
    e = jnp.exp(m2 - m1)
    w1 = 1.0 / (1.0 + e)
    w2 = e / (1.0 + e)
    sel1 = lane_f == i1
    sel2 = lane_f == i2
    onehot = jnp.where(sel1 | sel2, 1.0, 0.0)
    row = lax.broadcasted_iota(jnp.int32, (tm, tm), 0)
    col = lax.broadcasted_iota(jnp.int32, (tm, tm), 1)
    before = jnp.where(row > col, 1.0, 0.0).astype(BF16)
    local = _dot(before, onehot.astype(BF16))
    rank = local + carry_s[...]
    pick = lambda sel, val: jnp.sum(jnp.where(sel, val, 0.0), axis=1, keepdims=True)
    fields = [i1, i2, pick(sel1, rank), pick(sel2, rank), pick(sel1, local), pick(sel2, local)]
    meta = jnp.zeros((tm, LANES), F32)
    for idx, field in enumerate(fields):
        meta = jnp.where(lane == idx, field, meta)
    meta_ref[...] = meta.astype(jnp.int32)
    wts_ref[...] = jnp.where(lane == 0, w1, jnp.where(lane == 1, w2, 0.0))
    carry_s[...] += jnp.sum(onehot, axis=0, keepdims=True)
    cnt_ref[...] = carry_s[...]


def route_tokens(x, g, wr, *, tm, n_exp):
    t, d = x.shape
    return pl.pallas_call(
        functools.partial(_router_kernel, n_exp=n_exp),
        grid=(t // tm,),
        in_specs=[
            pl.BlockSpec((tm, d), lambda i: (i, 0)),
            pl.BlockSpec((1, d), lambda i: (0, 0)),
            pl.BlockSpec((d, LANES), lambda i: (0, 0)),
        ],
        out_specs=[
            pl.BlockSpec((tm, LANES), lambda i: (i, 0)),
            pl.BlockSpec((tm, LANES), lambda i: (i, 0)),
            pl.BlockSpec((SUBLANES, LANES), lambda i: (i, 0)),
            pl.BlockSpec((1, LANES), lambda i: (0, 0)),
        ],
        out_shape=[
            jax.ShapeDtypeStruct((t, LANES), jnp.int32),
            jax.ShapeDtypeStruct((t, LANES), F32),
            jax.ShapeDtypeStruct((t // tm * SUBLANES, LANES), F32),
            jax.ShapeDtypeStruct((1, LANES), F32),
        ],
        scratch_shapes=[pltpu.VMEM((1, LANES), F32)],
        compiler_params=_params("arbitrary"), name="moe_router",
    )(x, g.reshape(1, d), wr)


def _row_copy(src, dst, s_row, d_row, sem):
    return pltpu.make_async_copy(src.at[pl.ds(s_row, 1)], dst.at[pl.ds(d_row, 1)], sem)


def _dispatch_kernel(pos_ref, x_ref, zeros_hbm, xs_hbm, sem, *, tm):
    del zeros_hbm
    base = pl.program_id(0) * tm
    n_tok = pl.num_programs(0) * tm

    def start(t, carry):
        for k in range(TOP_K):
            _row_copy(x_ref, xs_hbm, t, pos_ref[k * n_tok + base + t], sem).start()
        return carry

    def wait(t, carry):
        for k in range(TOP_K):
            _row_copy(x_ref, xs_hbm, t, pos_ref[k * n_tok + base + t], sem).wait()
        return carry

    lax.fori_loop(0, tm, start, 0)
    lax.fori_loop(0, tm, wait, 0)


def moe_dispatch(pos, x, n_rows, *, tm):
    t, d = x.shape
    zeros = jnp.zeros((n_rows, d), x.dtype)
    return pl.pallas_call(
        functools.partial(_dispatch_kernel, tm=tm),
        grid_spec=pltpu.PrefetchScalarGridSpec(
            num_scalar_prefetch=1, grid=(t // tm,),
            in_specs=[pl.BlockSpec((tm, d), lambda i, pos: (i, 0)), pl.BlockSpec(memory_space=pl.ANY)],
            out_specs=pl.BlockSpec(memory_space=pl.ANY),
            scratch_shapes=[pltpu.SemaphoreType.DMA(())],
        ),
        out_shape=jax.ShapeDtypeStruct((n_rows, d), x.dtype),
        input_output_aliases={2: 0},
        compiler_params=_params("arbitrary"),
        name="moe_dispatch",
    )(pos, x, zeros)


def _expert_kernel(te_ref, nu_ref, xs_ref, g_ref, w13_ref, w2_ref, o_ref, h_s, gu_s, acc_s):
    del te_ref
    used = pl.program_id(0) < nu_ref[0]

    @pl.when(used)
    def _():
        h_s[...] = _rms(xs_ref[...], g_ref[...]).astype(BF16)
        acc_s[...] = jnp.zeros_like(acc_s)
        _swiglu_chunks(h_s, w13_ref.at[0], w2_ref.at[0], acc_s, gu_s)
        o_ref[...] = acc_s[...].astype(o_ref.dtype)

    @pl.when(jnp.logical_not(used))
    def _():
        o_ref[...] = jnp.zeros_like(o_ref)


def moe_experts(tile_expert, n_used, xs, g, w13, w2, *, tm):
    n_rows, d = xs.shape
    tf = w2.shape[2]
    resident = pl.Buffered(1)
    return pl.pallas_call(
        _expert_kernel,
        grid_spec=pltpu.PrefetchScalarGridSpec(
            num_scalar_prefetch=2, grid=(n_rows // tm,),
            in_specs=[
                pl.BlockSpec((tm, d), lambda i, te, nu: (i, 0)),
                pl.BlockSpec((1, d), lambda i, te, nu: (0, 0)),
                pl.BlockSpec((1,) + w13.shape[1:], lambda i, te, nu: (te[i], 0, 0), pipeline_mode=resident),
                pl.BlockSpec((1,) + w2.shape[1:], lambda i, te, nu: (te[i], 0, 0, 0), pipeline_mode=resident),
            ],
            out_specs=pl.BlockSpec((tm, d), lambda i, te, nu: (i, 0)),
            scratch_shapes=_swiglu_scratch(tm, d, tf) + [pltpu.VMEM((tm, d), F32)],
        ),
        out_shape=jax.ShapeDtypeStruct((n_rows, d), BF16),
        compiler_params=_params("arbitrary"), name="moe_experts",
    )(tile_expert, n_used, xs, g.reshape(1, d), w13, w2)


def _combine_kernel(start_ref, shift_ref, tail_ref, x_ref, wts_ref, meta_ref, g_ref, y_hbm, o_ref, blk_s, sems, *,
                    tm, n_exp):
    j = pl.program_id(0)
    slot = j % 2
    rows = tm + BF16_ROWS

    def block_copy(step, e, buf):
        first = pl.multiple_of(start_ref[step * n_exp + e], BF16_ROWS)
        return pltpu.make_async_copy(y_hbm.at[pl.ds(first, rows)], blk_s.at[buf, e], sems.at[buf])

    def fetch(step, buf):
        for e in range(n_exp):
            block_copy(step, e, buf).start()

    @pl.when(j == 0)
    def _():
        fetch(0, 0)

    @pl.when(j + 1 < pl.num_programs(0))
    def _():
        fetch(j + 1, 1 - slot)

    for e in range(n_exp):
        block_copy(j, e, slot).wait()

    meta = meta_ref[...]
    wts = wts_ref[...]
    col = lax.broadcasted_iota(jnp.int32, (tm, tm), 1)
    tail_col = lax.broadcasted_iota(jnp.int32, (tm, BF16_ROWS), 1) + tm

    def pick(e):
        sel = [meta[:, k:k + 1] == e for k in range(TOP_K)]
        idx = jnp.where(sel[0], meta[:, 4:5], jnp.where(sel[1], meta[:, 5:6], -2 * rows)) + shift_ref[j * n_exp + e]
        w = jnp.where(sel[0], wts[:, 0:1], jnp.where(sel[1], wts[:, 1:2], 0.0))
        return idx, w

    acc = x_ref[...]
    for e in range(n_exp):
        idx, w = pick(e)
        onehot = jnp.where(col == idx, 1.0, 0.0).astype(BF16)
        acc = acc + w * _dot(onehot, blk_s[slot, e, 0:tm, :])
    o_ref[...] = acc

    for e in range(n_exp):
        @pl.when(tail_ref[j * n_exp + e] != 0)
        def _(e=e):
            idx, w = pick(e)
            onehot_tail = jnp.where(tail_col == idx, 1.0, 0.0).astype(BF16)
            o_ref[...] += w * _dot(onehot_tail, blk_s[slot, e, tm:rows, :])

    o_ref[...] = _rms(o_ref[...], g_ref[...])


def moe_combine(starts, shifts, tails, x, wts, meta, g, y, *, tm, n_exp):
    t, d = x.shape
    n_pre = 3
    return pl.pallas_call(
        functools.partial(_combine_kernel, tm=tm, n_exp=n_exp),
        grid_spec=pltpu.PrefetchScalarGridSpec(
            num_scalar_prefetch=n_pre, grid=(t // tm,),
            in_specs=[
                pl.BlockSpec((tm, d), lambda i, *_: (i, 0)),
                pl.BlockSpec((tm, LANES), lambda i, *_: (i, 0)),
                pl.BlockSpec((tm, LANES), lambda i, *_: (i, 0)),
                pl.BlockSpec((1, d), lambda i, *_: (0, 0)),
                pl.BlockSpec(memory_space=pl.ANY),
            ],
            out_specs=pl.BlockSpec((tm, d), lambda i, *_: (i, 0)),
            scratch_shapes=[pltpu.VMEM((2, n_exp, tm + BF16_ROWS, d), BF16), pltpu.SemaphoreType.DMA((2,))],
        ),
        out_shape=jax.ShapeDtypeStruct((t, d), F32),
        compiler_params=_params("arbitrary"), name="moe_combine",
    )(starts, shifts, tails, x, wts, meta, g.reshape(1, d), y)


def _pad_cols(w, n):
    return jnp.pad(w, ((0, 0), (0, n - w.shape[1])))


def _even_mixer(x, norm, w_in, b_f, sgu_norm, w_s, b_s, w_out, *, batch, seq, tq):
    n_heads = b_f.shape[0]
    a_width = n_heads * HEAD_DIM
    b_width = w_s.shape[0] * LANES
    f0 = 3 * a_width
    w_main = jnp.concatenate([w_in[:, :f0], w_in[:, f0 + n_heads:]], axis=1)
    w_gate = _pad_cols(w_in[:, f0:f0 + n_heads], LANES)
    main, gate = norm_matmul(x, norm, [w_main.astype(BF16), w_gate.astype(BF16)], [BF16, F32],
                             tm=512, name="even_in_proj", scaled=(0, a_width, LOG2E * HEAD_DIM ** -0.5))
    c = gate_cumsum(gate, _pad_cols(b_f.reshape(1, -1), LANES), seq=seq)
    n_pairs = a_width // LANES
    a = fox_attention(main, c, batch=batch, seq=seq, tq=tq, q_col=0, k_col=n_pairs, v_col=2 * n_pairs,
                      n_pairs=n_pairs)
    u_col = f0 // b_width
    g = spatial_gating(main, sgu_norm, w_s, b_s, tb=512, u_col=u_col, v_col=u_col + 1)
    w_out = w_out.astype(BF16)
    return proj_residual(x, a, g, w_out[:a_width], w_out[a_width:], tm=512, name="even_out_proj")


def _odd_mixer(x, norm, w_in, conv_w, lq1, lk1, lq2, lk2, subln, w_out, rel_bias, lam_init, *, batch, seq, tq):
    c_width = conv_w.shape[1]
    d_width = rel_bias.shape[1] * 2 * HEAD_DIM
    q0 = 3 * c_width
    (main,) = norm_matmul(x, norm, [w_in.astype(BF16)], [BF16], tm=512, name="odd_in_proj",
                          scaled=(q0, q0 + d_width, LOG2E * HEAD_DIM ** -0.5))
    c_out = short_conv(main, conv_w, batch=batch, seq=seq)
    bias, far, lam = diff_prep(rel_bias, lq1, lk1, lq2, lk2, tq=tq, lam_init=lam_init)
    q_col = q0 // LANES
    n_heads = rel_bias.shape[1]
    d_out = diff_attention(main, bias, far, lam, subln, batch=batch, seq=seq, tq=tq, q_col=q_col,
                           k_col=q_col + n_heads, v_col=q_col + 2 * n_heads, lam_init=lam_init)
    w_out = w_out.astype(BF16)
    return proj_residual(x, c_out, d_out, w_out[:c_width], w_out[c_width:], tm=512, name="odd_out_proj")


def _moe_layer(x, norm, w_router, w13, w2, final_norm, *, tm_expert):
    t, d = x.shape
    n_exp = w_router.shape[1]
    tm_route = 256
    meta, wts, before, counts = route_tokens(x, norm, _pad_cols(w_router, LANES), tm=tm_route, n_exp=n_exp)
    counts = counts[0, :n_exp].astype(jnp.int32)
    padded = (counts + tm_expert - 1) // tm_expert * tm_expert
    ends = jnp.cumsum(padded)
    offsets = ends - padded
    pos = jnp.concatenate([offsets[meta[:, k]] + meta[:, TOP_K + k] for k in range(TOP_K)]).astype(jnp.int32)
    n_rows = TOP_K * t + (n_exp + 1) * tm_expert
    n_tiles = n_rows // tm_expert
    tile_start = jnp.arange(n_tiles, dtype=jnp.int32) * tm_expert
    tile_expert = jnp.minimum(jnp.sum(tile_start[:, None] >= ends[None, :], axis=1), n_exp - 1).astype(jnp.int32)
    n_used = (ends[-1:] // tm_expert).astype(jnp.int32)
    xs = moe_dispatch(pos, x, n_rows, tm=512)
    y = moe_experts(tile_expert, n_used, xs, norm, *_chunk_weights(w13, w2, 512), tm=tm_expert)
    before = before[::SUBLANES, :n_exp].astype(jnp.int32)
    in_tile = jnp.concatenate([before[1:], counts[None]]) - before
    first = offsets[None, :] + before
    starts = first // BF16_ROWS * BF16_ROWS
    shifts = first - starts
    tails = (shifts + in_tile > tm_route).astype(jnp.int32)
    flat = lambda a: a.reshape(-1).astype(jnp.int32)
    return moe_combine(flat(starts), flat(shifts), flat(tails), x, wts, meta, final_norm, y, tm=tm_route,
                       n_exp=n_exp)


def kernel(x, mem, rel_bias, mem_norm, final_norm, ev_norm, ev_w_in, ev_b_f, ev_sgu_norm, ev_w_s, ev_b_s, ev_w_out, ffn_w13, ffn_w2, od_norm, od_w_in, od_conv_w, od_lam_q1, od_lam_k1, od_lam_q2, od_lam_k2, od_subln, od_w_out, moe_router, moe_w13, moe_w2, x_norm, x_wq, x_wkv, x_wo, ffn_norm):
    batch, seq, d = x.shape
    mem_len = mem.shape[1]
    depth = x_norm.shape[0]
    assert depth == 2 and ev_norm.shape[0] == 1 and od_norm.shape[0] == 1
    x_heads, x_dh = 4, 128
    xf = x.reshape(batch * seq, d)
    wkv = jnp.concatenate([x_wkv[layer] for layer in range(depth)], axis=1).astype(BF16)
    (kv,) = norm_matmul(mem.reshape(batch * mem_len, d), mem_norm, [wkv], [BF16], tm=512, name="mem_kv")

    def cross(xf, layer):
        return cross_attention(xf, x_norm[layer], x_wq[layer].astype(BF16), kv, x_wo[layer].astype(BF16),
                               tm=512, seq=seq, mem_len=mem_len, kv_col=layer, n_heads=x_heads, dh=x_dh,
                               name=f"cross_attention_{layer}")

    xf = _even_mixer(xf, ev_norm[0], ev_w_in[0], ev_b_f[0], ev_sgu_norm[0], ev_w_s[0], ev_b_s[0], ev_w_out[0],
                     batch=batch, seq=seq, tq=ATTN_TILE)
    xf = cross(xf, 0)
    xf = ffn_swiglu(xf, ffn_norm[0], *_chunk_weights(ffn_w13[0], ffn_w2[0], 256), tm=512, name="ffn_swiglu")
    lam_init = 0.8 - 0.6 * math.exp(-0.3 * 1)
    xf = _odd_mixer(xf, od_norm[0], od_w_in[0], od_conv_w[0], od_lam_q1[0], od_lam_k1[0], od_lam_q2[0],
                    od_lam_k2[0], od_subln[0], od_w_out[0], rel_bias, lam_init, batch=batch, seq=seq, tq=ATTN_TILE)
    xf = cross(xf, 1)
    out = _moe_layer(xf, ffn_norm[1], moe_router[0], moe_w13[0], moe_w2[0], final_norm, tm_expert=512)
    return out.reshape(batch, seq, d)
```

```python
import functools
import math

import jax
import jax.numpy as jnp
from jax import lax
from jax.experimental import pallas as pl
from jax.experimental.pallas import tpu as pltpu

F32 = jnp.float32
BF16 = jnp.bfloat16
EPS = 1e-6
HEAD_DIM = 64
LANES = 128
SUBLANES = 8
BF16_ROWS = 16
N_BUCKETS = 32
MAX_DIST = 128
TOP_K = 2
LOG2E = 1.4426950408889634
ATTN_TILE = 512
VMEM_LIMIT = 56 * 1024 * 1024


def _params(*sem):
    return pltpu.CompilerParams(dimension_semantics=sem, vmem_limit_bytes=VMEM_LIMIT)


def _rms(x, g):
    ms = jnp.mean(x * x, axis=-1, keepdims=True)
    return x * lax.rsqrt(ms + EPS) * g


def _dot(a, b):
    return jnp.dot(a, b, preferred_element_type=F32)


def _dot_nt(a, b):
    return lax.dot_general(a, b, (((1,), (1,)), ((), ())), preferred_element_type=F32)


def _norm_matmul_kernel(x_ref, g_ref, *refs, n_w, chunk, scaled):
    w_refs, o_refs = refs[:n_w], refs[n_w:]
    h = _rms(x_ref[...], g_ref[...]).astype(BF16)
    s0, s1, scale = scaled
    for k, (w_ref, o_ref) in enumerate(zip(w_refs, o_refs)):
        n = w_ref.shape[1]
        for c0 in range(0, n, chunk):
            c1 = min(c0 + chunk, n)
            y = _dot(h, w_ref[:, c0:c1])
            if k == 0 and s0 <= c0 and c1 <= s1:
                y = y * scale
            o_ref[:, c0:c1] = y.astype(o_ref.dtype)


def norm_matmul(x, g, ws, out_dtypes, *, tm, name, scaled=(0, 0, 1.0)):
    t, d = x.shape
    chunk = 512
    assert scaled[0] % chunk == 0 and scaled[1] % chunk == 0
    in_specs = [pl.BlockSpec((tm, d), lambda i: (i, 0)), pl.BlockSpec((1, d), lambda i: (0, 0))]
    in_specs += [pl.BlockSpec(w.shape, lambda i: (0, 0)) for w in ws]
    out_specs = [pl.BlockSpec((tm, w.shape[1]), lambda i: (i, 0)) for w in ws]
    out_shape = [jax.ShapeDtypeStruct((t, w.shape[1]), dt) for w, dt in zip(ws, out_dtypes)]
    return pl.pallas_call(
        functools.partial(_norm_matmul_kernel, n_w=len(ws), chunk=chunk, scaled=scaled),
        grid=(t // tm,), in_specs=in_specs, out_specs=out_specs, out_shape=out_shape,
        compiler_params=_params("parallel"), name=name,
    )(x, g.reshape(1, d), *ws)


def _gate_kernel(g_ref, b_ref, c_ref):
    s = g_ref.shape[0]
    row = lax.broadcasted_iota(jnp.int32, (LANES, LANES), 0)
    col = lax.broadcasted_iota(jnp.int32, (LANES, LANES), 1)
    tri = (row >= col).astype(F32)
    carry = jnp.zeros((1, LANES), F32)
    for blk in range(s // LANES):
        z = g_ref[blk * LANES:(blk + 1) * LANES, :] + b_ref[...]
        log_f = jnp.minimum(z, 0.0) - jnp.log1p(jnp.exp(-jnp.abs(z)))
        cs = jnp.dot(tri, log_f, precision=lax.Precision.HIGHEST, preferred_element_type=F32) + carry
        c_ref[blk * LANES:(blk + 1) * LANES, :] = cs
        carry = cs[LANES - 1:LANES, :]


def gate_cumsum(g, b, *, seq):
    t = g.shape[0]
    return pl.pallas_call(
        _gate_kernel, grid=(t // seq,),
        in_specs=[pl.BlockSpec((seq, LANES), lambda i: (i, 0)), pl.BlockSpec((1, LANES), lambda i: (0, 0))],
        out_specs=pl.BlockSpec((seq, LANES), lambda i: (i, 0)),
        out_shape=jax.ShapeDtypeStruct((t, LANES), F32),
        compiler_params=_params("parallel"), name="gate_cumsum",
    )(g, b)


def _split3(x):
    hi = x.astype(BF16).astype(F32)
    rest = x - hi
    mid = rest.astype(BF16).astype(F32)
    lo = (rest - mid).astype(BF16).astype(F32)
    return hi, mid, lo


def _augment(x, in_half, lane, base, pieces, pieces_first):
    n = len(pieces)
    p0, o0 = (base, base + n) if pieces_first else (base + n, base)
    aug = jnp.where((lane >= o0) & (lane < o0 + n), 1.0, 0.0)
    for idx, piece in enumerate(pieces):
        aug = jnp.where(lane == p0 + idx, piece, aug)
    return jnp.where(in_half, x, aug.astype(x.dtype))


def _halves(lane):
    return [(lane >= HEAD_DIM * hh) & (lane < HEAD_DIM * (hh + 1)) for hh in range(2)]


def _causal_attention(qa, ka_s, v_ref, bias_ref, i, tq, m_s, l_s, acc_s):
    n_chunks = tq // LANES

    def causal():
        row = lax.broadcasted_iota(jnp.int32, (tq, tq), 0)
        col = lax.broadcasted_iota(jnp.int32, (tq, tq), 1)
        return row >= col

    def scores(h, j, near, mask):
        start = pl.multiple_of(j * tq, tq)
        s = _dot_nt(qa[h], ka_s[h, pl.ds(start, tq), :])
        if near is not None:
            s = s + bias_ref[0, near]
        if mask is not None:
            s = jnp.where(mask, s, -jnp.inf)
        return [s[:, c * LANES:(c + 1) * LANES] for c in range(n_chunks)]

    def max_tile(j, near, mask):
        for h in range(2):
            m = m_s[h]
            for chunk in scores(h, j, near, mask):
                m = jnp.maximum(m, chunk)
            m_s[h] = m

    def sum_tile(j, near, mask):
        v = v_ref[pl.ds(pl.multiple_of(j * tq, tq), tq), :]
        for h in range(2):
            m = m_s[h]
            ps = [jnp.exp2(chunk - m) for chunk in scores(h, j, near, mask)]
            l_s[h] += functools.reduce(lambda a, b: a + b, ps)
            acc_s[h] += _dot(jnp.concatenate(ps, axis=1).astype(BF16), v)

    def key_tiles(fn):
        def run(lo, hi, near):
            def body(j, carry):
                fn(j, near, None)
                return carry
            lax.fori_loop(lo, hi, body, 0)

        if bias_ref is None:
            run(0, i, None)
            fn(i, None, causal())
        else:
            n_far = jnp.maximum(i - 1, 0)
            run(0, n_far, None)
            run(n_far, i, 1)
            fn(i, 0, causal())

    m_s[...] = jnp.full(m_s.shape, -jnp.inf, F32)
    key_tiles(max_tile)
    for h in range(2):
        m_s[h] = jnp.broadcast_to(jnp.max(m_s[h], axis=1, keepdims=True), (tq, LANES))
    l_s[...] = jnp.zeros(l_s.shape, F32)
    acc_s[...] = jnp.zeros(acc_s.shape, F32)
    key_tiles(sum_tile)


def _normalised(l_s, acc_s, h):
    return acc_s[h] / jnp.sum(l_s[h], axis=1, keepdims=True)


def _fox_kernel(q_ref, k_ref, v_ref, c_ref, o_ref, ka_s, m_s, l_s, acc_s, *, tq):
    hp = pl.program_id(1)
    i = pl.program_id(2)
    lane = lax.broadcasted_iota(jnp.int32, (1, LANES), 1)
    halves = _halves(lane)

    def decay(c, hh):
        return jnp.sum(jnp.where(lane == 2 * hp + hh, c, 0.0), axis=1, keepdims=True) * LOG2E

    @pl.when(i == 0)
    def _():
        k = k_ref[...]
        c_all = c_ref[...]
        for hh in range(2):
            ka_s[hh] = _augment(k, halves[hh], lane, HEAD_DIM * (1 - hh), _split3(-decay(c_all, hh)), True)

    q = q_ref[...]
    c_q = c_ref[pl.ds(pl.multiple_of(i * tq, tq), tq), :]
    qa = [_augment(q, halves[hh], lane, HEAD_DIM * (1 - hh), _split3(decay(c_q, hh)), False) for hh in range(2)]
    _causal_attention(qa, ka_s, v_ref, None, i, tq, m_s, l_s, acc_s)
    out = jnp.where(lane < HEAD_DIM, _normalised(l_s, acc_s, 0), _normalised(l_s, acc_s, 1))
    o_ref[...] = out.astype(o_ref.dtype)


def _attn_scratch(seq, tq):
    return [pltpu.VMEM((2, seq, LANES), BF16)] + [pltpu.VMEM((2, tq, LANES), F32)] * 3


def fox_attention(qkv, c, *, batch, seq, tq, q_col, k_col, v_col, n_pairs):
    t = batch * seq
    nq = seq // tq
    return pl.pallas_call(
        functools.partial(_fox_kernel, tq=tq),
        grid=(batch, n_pairs, nq),
        in_specs=[
            pl.BlockSpec((tq, LANES), lambda b, h, i: (b * nq + i, q_col + h)),
            pl.BlockSpec((seq, LANES), lambda b, h, i: (b, k_col + h)),
            pl.BlockSpec((seq, LANES), lambda b, h, i: (b, v_col + h)),
            pl.BlockSpec((seq, LANES), lambda b, h, i: (b, 0)),
        ],
        out_specs=pl.BlockSpec((tq, LANES), lambda b, h, i: (b * nq + i, h)),
        out_shape=jax.ShapeDtypeStruct((t, n_pairs * LANES), BF16),
        scratch_shapes=_attn_scratch(seq, tq),
        compiler_params=_params("parallel", "parallel", "arbitrary"), name="fox_attention",
    )(qkv, qkv, qkv, c)


def _sgu_kernel(u_ref, v_ref, norm_ref, ws_ref, bs_ref, o_ref, *, n_groups, chunk):
    tb = u_ref.shape[0]
    row = lax.broadcasted_iota(jnp.int32, (chunk, chunk), 0)
    col = lax.broadcasted_iota(jnp.int32, (chunk, chunk), 1)
    tri = row >= col
    for g in range(n_groups):
        w = jnp.where(tri, ws_ref[g], 0.0).astype(BF16)
        bias = bs_ref[:, g:g + 1]
        gain = norm_ref[g:g + 1, :]
        for c in range(tb // chunk):
            rs = slice(c * chunk, (c + 1) * chunk)
            cs = slice(g * LANES, (g + 1) * LANES)
            vn = _rms(jax.nn.gelu(v_ref[rs, cs].astype(F32)), gain)
            mixed = _dot(w, vn.astype(BF16)) + bias
            o_ref[rs, cs] = (jax.nn.gelu(u_ref[rs, cs].astype(F32)) * mixed).astype(o_ref.dtype)


def spatial_gating(main, sgu_norm, w_s, b_s, *, tb, u_col, v_col):
    t = main.shape[0]
    n_groups, chunk, _ = w_s.shape
    width = n_groups * LANES
    return pl.pallas_call(
        functools.partial(_sgu_kernel, n_groups=n_groups, chunk=chunk),
        grid=(t // tb,),
        in_specs=[
            pl.BlockSpec((tb, width), lambda i: (i, u_col)),
            pl.BlockSpec((tb, width), lambda i: (i, v_col)),
            pl.BlockSpec(sgu_norm.shape, lambda i: (0, 0)),
            pl.BlockSpec(w_s.shape, lambda i: (0, 0, 0)),
            pl.BlockSpec((chunk, n_groups), lambda i: (0, 0)),
        ],
        out_specs=pl.BlockSpec((tb, width), lambda i: (i, 0)),
        out_shape=jax.ShapeDtypeStruct((t, width), BF16),
        compiler_params=_params("parallel"), name="spatial_gating",
    )(main, main, sgu_norm, w_s, b_s.T)


def _chunk_weights(w13, w2, tf):
    *lead, ff, d = w2.shape
    return w13.astype(BF16), w2.astype(BF16).reshape(*lead, ff // tf, tf, d)


def _swiglu_chunks(h_s, w13, w2, o_ref, gu_s):
    n, tf, _ = w2.shape

    def project(c, slot):
        for part in range(2):
            cols = pl.ds(pl.multiple_of((part * n + c) * tf, tf), tf)
            gu_s[slot, part] = _dot(h_s[...], w13[:, cols])

    def consume(c, slot):
        gate = gu_s[slot, 0]
        act = (gate * jax.nn.sigmoid(gate) * gu_s[slot, 1]).astype(BF16)
        o_ref[...] += _dot(act, w2[c])

    def pair(k, carry):
        c = 2 * k
        project(c + 1, 1)
        consume(c, 0)
        project(c + 2, 0)
        consume(c + 1, 1)
        return carry

    project(0, 0)
    lax.fori_loop(0, (n - 1) // 2, pair, 0)
    if n % 2 == 0:
        project(n - 1, 1)
        consume(n - 2, 0)
        consume(n - 1, 1)
    else:
        consume(n - 1, 0)


def _swiglu_scratch(tm, d, tf):
    return [pltpu.VMEM((tm, d), BF16), pltpu.VMEM((2, 2, tm, tf), F32)]


def _tail_kernel(x_ref, a_ref, b_ref, wa_ref, wb_ref, gx_ref, wq_ref, kv_ref, wo_ref, *rest, n_heads, dh):
    x = x_ref[...] + _dot(a_ref[...], wa_ref[...]) + _dot(b_ref[...], wb_ref[...])
    q = _dot(_rms(x, gx_ref[...]).astype(BF16), wq_ref[...]).astype(BF16)
    width = n_heads * dh
    outs = []
    for hd in range(n_heads):
        cs = slice(hd * dh, (hd + 1) * dh)
        s = _dot_nt(q[:, cs], kv_ref[:, cs]) * (dh ** -0.5)
        p = jnp.exp(s - jnp.max(s, axis=1, keepdims=True))
        p = p / jnp.sum(p, axis=1, keepdims=True)
        outs.append(_dot(p.astype(BF16), kv_ref[:, width + hd * dh:width + (hd + 1) * dh]).astype(BF16))
    x = x + _dot(jnp.concatenate(outs, axis=1), wo_ref[...])
    if len(rest) == 1:
        (o_ref,) = rest
        o_ref[...] = x
    else:
        gf_ref, w13_ref, w2_ref, o_ref, h_s, gu_s = rest
        h_s[...] = _rms(x, gf_ref[...]).astype(BF16)
        o_ref[...] = x
        _swiglu_chunks(h_s, w13_ref, w2_ref, o_ref, gu_s)


def layer_tail(x, a, b, w_out, gx, wq, kv, wo, ffn=None, *, tm, seq, mem_len, kv_col, n_heads, dh, name):
    t, d = x.shape
    per_b = seq // tm
    resident = pl.Buffered(1)
    const = lambda arr: pl.BlockSpec(arr.shape, lambda i: (0,) * arr.ndim, pipeline_mode=resident)
    rows = lambda arr: pl.BlockSpec((tm, arr.shape[1]), lambda i: (i, 0))
    vec = lambda g: g.reshape(1, d)
    wa, wb = w_out[:a.shape[1]], w_out[a.shape[1]:]
    args = [x, a, b, wa, wb, vec(gx), wq, kv, wo]
    in_specs = [rows(x), rows(a), rows(b), const(wa), const(wb), const(vec(gx)), const(wq),
                pl.BlockSpec((mem_len, 2 * n_heads * dh), lambda i: (i // per_b, kv_col)), const(wo)]
    scratch = []
    if ffn is not None:
        gf, w13, w2 = ffn
        args += [vec(gf), w13, w2]
        in_specs += [const(vec(gf)), const(w13), const(w2)]
        scratch = _swiglu_scratch(tm, d, w2.shape[1])
    return pl.pallas_call(
        functools.partial(_tail_kernel, n_heads=n_heads, dh=dh), grid=(t // tm,),
        in_specs=in_specs, out_specs=rows(x), out_shape=jax.ShapeDtypeStruct((t, d), F32),
        scratch_shapes=scratch, compiler_params=_params("parallel"), name=name,
    )(*args)


def _conv_kernel(bg_ref, cg_ref, xi_ref, w_ref, o_ref):
    s, width = o_ref.shape
    n_taps = w_ref.shape[0]
    xc = cg_ref[...].astype(F32) * xi_ref[...].astype(F32)
    row = lax.broadcasted_iota(jnp.int32, (s, width), 0)
    y = w_ref[n_taps - 1:n_taps, :] * xc
    for back in range(1, n_taps):
        shifted = jnp.where(row >= back, pltpu.roll(xc, back, axis=0), 0.0)
        y = y + w_ref[n_taps - 1 - back:n_taps - back, :] * shifted
    o_ref[...] = (bg_ref[...].astype(F32) * y).astype(o_ref.dtype)


def short_conv(main, conv_w, *, batch, seq):
    width = conv_w.shape[1]
    return pl.pallas_call(
        _conv_kernel, grid=(batch,),
        in_specs=[
            pl.BlockSpec((seq, width), lambda b: (b, 0)),
            pl.BlockSpec((seq, width), lambda b: (b, 1)),
            pl.BlockSpec((seq, width), lambda b: (b, 2)),
            pl.BlockSpec(conv_w.shape, lambda b: (0, 0)),
        ],
        out_specs=pl.BlockSpec((seq, width), lambda b: (b, 0)),
        out_shape=jax.ShapeDtypeStruct((batch * seq, width), BF16),
        compiler_params=_params("parallel"), name="short_conv",
    )(main, main, main, conv_w)


def _diff_prep_kernel(rb_ref, lq1_ref, lk1_ref, lq2_ref, lk2_ref, bias_ref, far_ref, lam_ref, *, tq, lam_init):
    n_heads = bias_ref.shape[0]
    strip = 32
    row = lax.broadcasted_iota(jnp.int32, (strip, tq), 0)
    col = lax.broadcasted_iota(jnp.int32, (strip, tq), 1)
    max_exact = N_BUCKETS // 2

    def fill(r, carry):
        r0 = pl.multiple_of(r * strip, strip)
        for which in range(2):
            n = jnp.maximum(row + r0 - col + which * tq, 0)
            nf = jnp.maximum(n, 1).astype(F32)
            large = max_exact + (jnp.log(nf / max_exact) / math.log(MAX_DIST / max_exact)
                                 * (N_BUCKETS - max_exact)).astype(jnp.int32)
            large = jnp.minimum(large, N_BUCKETS - 1)
            bucket = jnp.where(n < max_exact, n, large)
            for h in range(n_heads):
                b = jnp.zeros((strip, tq), F32)
                for kk in range(N_BUCKETS):
                    b = jnp.where(bucket == kk, rb_ref[kk, h], b)
                bias_ref[h, which, pl.ds(r0, strip), :] = (b - rb_ref[N_BUCKETS - 1, h]) * LOG2E
        return carry

    lax.fori_loop(0, tq // strip, fill, 0)
    for h in range(n_heads):
        far_ref[h] = jnp.full((1, LANES), rb_ref[N_BUCKETS - 1, h], F32) * LOG2E
    lam = (jnp.exp(jnp.sum(lq1_ref[...] * lk1_ref[...], axis=1, keepdims=True))
           - jnp.exp(jnp.sum(lq2_ref[...] * lk2_ref[...], axis=1, keepdims=True)) + lam_init)
    lam_ref[...] = jnp.broadcast_to(lam, (1, LANES))


def diff_prep(rel_bias, lq1, lk1, lq2, lk2, *, tq, lam_init):
    n_heads = rel_bias.shape[1]
    vec = lambda a: a.reshape(1, -1)
    vspec = pl.BlockSpec(memory_space=pltpu.VMEM)
    return pl.pallas_call(
        functools.partial(_diff_prep_kernel, tq=tq, lam_init=lam_init),
        in_specs=[pl.BlockSpec(memory_space=pltpu.SMEM), vspec, vspec, vspec, vspec],
        out_specs=[vspec, vspec, vspec],
        out_shape=[
            jax.ShapeDtypeStruct((n_heads, 2, tq, tq), F32),
            jax.ShapeDtypeStruct((n_heads, 1, LANES), F32),
            jax.ShapeDtypeStruct((1, LANES), F32),
        ],
        compiler_params=pltpu.CompilerParams(vmem_limit_bytes=VMEM_LIMIT), name="diff_prep",
    )(rel_bias, vec(lq1), vec(lk1), vec(lq2), vec(lk2))


def _diff_kernel(q_ref, k_ref, v_ref, bias_ref, far_ref, lam_ref, subln_ref, o_ref, ka_s, m_s, l_s, acc_s, *,
                 tq, lam_init):
    i = pl.program_id(2)
    lane = lax.broadcasted_iota(jnp.int32, (1, LANES), 1)
    halves = _halves(lane)

    @pl.when(i == 0)
    def _():
        k = k_ref[...]
        far = _split3(far_ref[0])
        for sub in range(2):
            ka_s[sub] = _augment(k, halves[sub], lane, HEAD_DIM * (1 - sub), far, True)

    q = q_ref[...]
    zero = jnp.zeros((1, LANES), F32)
    qa = [_augment(q, halves[sub], lane, HEAD_DIM * (1 - sub), (zero,) * 3, False) for sub in range(2)]
    _causal_attention(qa, ka_s, v_ref, bias_ref, i, tq, m_s, l_s, acc_s)
    o = _normalised(l_s, acc_s, 0) - lam_ref[...] * _normalised(l_s, acc_s, 1)
    o_ref[...] = (_rms(o, subln_ref[...]) * (1.0 - lam_init)).astype(o_ref.dtype)


def diff_attention(main, bias, far, lam, subln, *, batch, seq, tq, q_col, k_col, v_col, lam_init):
    t = batch * seq
    nq = seq // tq
    n_heads = bias.shape[0]
    return pl.pallas_call(
        functools.partial(_diff_kernel, tq=tq, lam_init=lam_init),
        grid=(batch, n_heads, nq),
        in_specs=[
            pl.BlockSpec((tq, LANES), lambda b, h, i: (b * nq + i, q_col + h)),
            pl.BlockSpec((seq, LANES), lambda b, h, i: (b, k_col + h)),
            pl.BlockSpec((seq, LANES), lambda b, h, i: (b, v_col + h)),
            pl.BlockSpec((1, 2, tq, tq), lambda b, h, i: (h, 0, 0, 0)),
            pl.BlockSpec((1, 1, LANES), lambda b, h, i: (h, 0, 0)),
            pl.BlockSpec((1, LANES), lambda b, h, i: (0, 0)),
            pl.BlockSpec((1, LANES), lambda b, h, i: (0, 0)),
        ],
        out_specs=pl.BlockSpec((tq, LANES), lambda b, h, i: (b * nq + i, h)),
        out_shape=jax.ShapeDtypeStruct((t, n_heads * LANES), BF16),
        scratch_shapes=_attn_scratch(seq, tq),
        compiler_params=_params("parallel", "parallel", "arbitrary"), name="diff_attention",
    )(main, main, main, bias, far, lam, subln.reshape(1, LANES))


def _router_kernel(x_ref, g_ref, wr_ref, meta_ref, wts_ref, before_ref, cnt_ref, carry_s, *, n_exp):
    tm = x_ref.shape[0]

    @pl.when(pl.program_id(0) == 0)
    def _():
        carry_s[...] = jnp.zeros_like(carry_s)

    before_ref[...] = jnp.broadcast_to(carry_s[...], before_ref.shape)

    h = _rms(x_ref[...], g_ref[...])
    logits = jnp.dot(h, wr_ref[...], precision=lax.Precision.HIGHEST, preferred_element_type=F32)
    lane = lax.broadcasted_iota(jnp.int32, (tm, LANES), 1)
    lane_f = lane.astype(F32)
    logits = jnp.where(lane < n_exp, logits, -jnp.inf)
    m1 = jnp.max(logits, axis=1, keepdims=True)
    i1 = jnp.min(jnp.where(logits == m1, lane_f, float(LANES)), axis=1, keepdims=True)
    rest = jnp.where(lane_f == i1, -jnp.inf, logits)
    m2 = jnp.max(rest, axis=1, keepdims=True)
    i2 = jnp.min(jnp.where(rest == m2, lane_f, float(LANES)), axis=1, keepdims=True)
    e = jnp.exp(m2 - m1)
    w1 = 1.0 / (1.0 + e)
    w2 = e / (1.0 + e)
    sel1 = lane_f == i1
    sel2 = lane_f == i2
    onehot = jnp.where(sel1 | sel2, 1.0, 0.0)
    row = lax.broadcasted_iota(jnp.int32, (tm, tm), 0)
    col = lax.broadcasted_iota(jnp.int32, (tm, tm), 1)
    before = jnp.where(row > col, 1.0, 0.0).astype(BF16)
    local = _dot(before, onehot.astype(BF16))
    rank = local + carry_s[...]
    pick = lambda sel, val: jnp.sum(jnp.where(sel, val, 0.0), axis=1, keepdims=True)
    fields = [i1, i2, pick(sel1, rank), pick(sel2, rank), pick(sel1, local), pick(sel2, local)]
    meta = jnp.zeros((tm, LANES), F32)
    for idx, field in enumerate(fields):
        meta = jnp.where(lane == idx, field, meta)
    meta_ref[...] = meta.astype(jnp.int32)
    wts_ref[...] = jnp.where(lane == 0, w1, jnp.where(lane == 1, w2, 0.0))
    carry_s[...] += jnp.sum(onehot, axis=0, keepdims=True)
    cnt_ref[...] = carry_s[...]


def route_tokens(x, g, wr, *, tm, n_exp):
    t, d = x.shape
    return pl.pallas_call(
        functools.partial(_router_kernel, n_exp=n_exp),
        grid=(t // tm,),
        in_specs=[
            pl.BlockSpec((tm, d), lambda i: (i, 0)),
            pl.BlockSpec((1, d), lambda i: (0, 0)),
            pl.BlockSpec((d, LANES), lambda i: (0, 0)),
        ],
        out_specs=[
            pl.BlockSpec((tm, LANES), lambda i: (i, 0)),
            pl.BlockSpec((tm, LANES), lambda i: (i, 0)),
            pl.BlockSpec((SUBLANES, LANES), lambda i: (i, 0)),
            pl.BlockSpec((1, LANES), lambda i: (0, 0)),
        ],
        out_shape=[
            jax.ShapeDtypeStruct((t, LANES), jnp.int32),
            jax.ShapeDtypeStruct((t, LANES), F32),
            jax.ShapeDtypeStruct((t // tm * SUBLANES, LANES), F32),
            jax.ShapeDtypeStruct((1, LANES), F32),
        ],
        scratch_shapes=[pltpu.VMEM((1, LANES), F32)],
        compiler_params=_params("arbitrary"), name="moe_router",
    )(x, g.reshape(1, d), wr)


def _row_copy(src, dst, s_row, d_row, sem):
    return pltpu.make_async_copy(src.at[pl.ds(s_row, 1)], dst.at[pl.ds(d_row, 1)], sem)


def _dispatch_kernel(pos_ref, x_ref, zeros_hbm, xs_hbm, sem, *, tm):
    del zeros_hbm
    base = pl.program_id(0) * tm
    n_tok = pl.num_programs(0) * tm

    def start(t, carry):
        for k in range(TOP_K):
            _row_copy(x_ref, xs_hbm, t, pos_ref[k * n_tok + base + t], sem).start()
        return carry

    def wait(t, carry):
        for k in range(TOP_K):
            _row_copy(x_ref, xs_hbm, t, pos_ref[k * n_tok + base + t], sem).wait()
        return carry

    lax.fori_loop(0, tm, start, 0)
    lax.fori_loop(0, tm, wait, 0)


def moe_dispatch(pos, x, n_rows, *, tm):
    t, d = x.shape
    zeros = jnp.zeros((n_rows, d), x.dtype)
    return pl.pallas_call(
        functools.partial(_dispatch_kernel, tm=tm),
        grid_spec=pltpu.PrefetchScalarGridSpec(
            num_scalar_prefetch=1, grid=(t // tm,),
            in_specs=[pl.BlockSpec((tm, d), lambda i, pos: (i, 0)), pl.BlockSpec(memory_space=pl.ANY)],
            out_specs=pl.BlockSpec(memory_space=pl.ANY),
            scratch_shapes=[pltpu.SemaphoreType.DMA(())],
        ),
        out_shape=jax.ShapeDtypeStruct((n_rows, d), x.dtype),
        input_output_aliases={2: 0},
        compiler_params=_params("arbitrary"),
        name="moe_dispatch",
    )(pos, x, zeros)


def _expert_kernel(te_ref, nu_ref, xs_ref, g_ref, w13_ref, w2_ref, o_ref, h_s, gu_s, acc_s):
    del te_ref
    used = pl.program_id(0) < nu_ref[0]

    @pl.when(used)
    def _():
        h_s[...] = _rms(xs_ref[...], g_ref[...]).astype(BF16)
        acc_s[...] = jnp.zeros_like(acc_s)
        _swiglu_chunks(h_s, w13_ref.at[0], w2_ref.at[0], acc_s, gu_s)
        o_ref[...] = acc_s[...].astype(o_ref.dtype)

    @pl.when(jnp.logical_not(used))
    def _():
        o_ref[...] = jnp.zeros_like(o_ref)


def moe_experts(tile_expert, n_used, xs, g, w13, w2, *, tm):
    n_rows, d = xs.shape
    tf = w2.shape[2]
    resident = pl.Buffered(1)
    return pl.pallas_call(
        _expert_kernel,
        grid_spec=pltpu.PrefetchScalarGridSpec(
            num_scalar_prefetch=2, grid=(n_rows // tm,),
            in_specs=[
                pl.BlockSpec((tm, d), lambda i, te, nu: (i, 0)),
                pl.BlockSpec((1, d), lambda i, te, nu: (0, 0)),
                pl.BlockSpec((1,) + w13.shape[1:], lambda i, te, nu: (te[i], 0, 0), pipeline_mode=resident),
                pl.BlockSpec((1,) + w2.shape[1:], lambda i, te, nu: (te[i], 0, 0, 0), pipeline_mode=resident),
            ],
            out_specs=pl.BlockSpec((tm, d), lambda i, te, nu: (i, 0)),
            scratch_shapes=_swiglu_scratch(tm, d, tf) + [pltpu.VMEM((tm, d), F32)],
        ),
        out_shape=jax.ShapeDtypeStruct((n_rows, d), BF16),
        compiler_params=_params("arbitrary"), name="moe_experts",
    )(tile_expert, n_used, xs, g.reshape(1, d), w13, w2)


def _combine_kernel(start_ref, shift_ref, tail_ref, x_ref, wts_ref, meta_ref, g_ref, y_hbm, o_ref, blk_s, sems, *,
                    tm, n_exp):
    j = pl.program_id(0)
    slot = j % 2
    rows = tm + BF16_ROWS

    def block_copy(step, e, buf):
        first = pl.multiple_of(start_ref[step * n_exp + e], BF16_ROWS)
        return pltpu.make_async_copy(y_hbm.at[pl.ds(first, rows)], blk_s.at[buf, e], sems.at[buf])

    def fetch(step, buf):
        for e in range(n_exp):
            block_copy(step, e, buf).start()

    @pl.when(j == 0)
    def _():
        fetch(0, 0)

    @pl.when(j + 1 < pl.num_programs(0))
    def _():
        fetch(j + 1, 1 - slot)

    for e in range(n_exp):
        block_copy(j, e, slot).wait()

    meta = meta_ref[...]
    wts = wts_ref[...]
    col = lax.broadcasted_iota(jnp.int32, (tm, tm), 1)
    tail_col = lax.broadcasted_iota(jnp.int32, (tm, BF16_ROWS), 1) + tm

    def pick(e):
        sel = [meta[:, k:k + 1] == e for k in range(TOP_K)]
        idx = jnp.where(sel[0], meta[:, 4:5], jnp.where(sel[1], meta[:, 5:6], -2 * rows)) + shift_ref[j * n_exp + e]
        w = jnp.where(sel[0], wts[:, 0:1], jnp.where(sel[1], wts[:, 1:2], 0.0))
        return idx, w

    acc = x_ref[...]
    for e in range(n_exp):
        idx, w = pick(e)
        onehot = jnp.where(col == idx, 1.0, 0.0).astype(BF16)
        acc = acc + w * _dot(onehot, blk_s[slot, e, 0:tm, :])
    o_ref[...] = acc

    for e in range(n_exp):
        @pl.when(tail_ref[j * n_exp + e] != 0)
        def _(e=e):
            idx, w = pick(e)
            onehot_tail = jnp.where(tail_col == idx, 1.0, 0.0).astype(BF16)
            o_ref[...] += w * _dot(onehot_tail, blk_s[slot, e, tm:rows, :])

    o_ref[...] = _rms(o_ref[...], g_ref[...])


def moe_combine(starts, shifts, tails, x, wts, meta, g, y, *, tm, n_exp):
    t, d = x.shape
    n_pre = 3
    return pl.pallas_call(
        functools.partial(_combine_kernel, tm=tm, n_exp=n_exp),
        grid_spec=pltpu.PrefetchScalarGridSpec(
            num_scalar_prefetch=n_pre, grid=(t // tm,),
            in_specs=[
                pl.BlockSpec((tm, d), lambda i, *_: (i, 0)),
                pl.BlockSpec((tm, LANES), lambda i, *_: (i, 0)),
                pl.BlockSpec((tm, LANES), lambda i, *_: (i, 0)),
                pl.BlockSpec((1, d), lambda i, *_: (0, 0)),
                pl.BlockSpec(memory_space=pl.ANY),
            ],
            out_specs=pl.BlockSpec((tm, d), lambda i, *_: (i, 0)),
            scratch_shapes=[pltpu.VMEM((2, n_exp, tm + BF16_ROWS, d), BF16), pltpu.SemaphoreType.DMA((2,))],
        ),
        out_shape=jax.ShapeDtypeStruct((t, d), F32),
        compiler_params=_params("arbitrary"), name="moe_combine",
    )(starts, shifts, tails, x, wts, meta, g.reshape(1, d), y)


def _pad_cols(w, n):
    return jnp.pad(w, ((0, 0), (0, n - w.shape[1])))


def _even_mixer(x, norm, w_in, b_f, sgu_norm, w_s, b_s, *, batch, seq, tq):
    n_heads = b_f.shape[0]
    a_width = n_heads * HEAD_DIM
    b_width = w_s.shape[0] * LANES
    f0 = 3 * a_width
    w_main = jnp.concatenate([w_in[:, :f0], w_in[:, f0 + n_heads:]], axis=1)
    w_gate = _pad_cols(w_in[:, f0:f0 + n_heads], LANES)
    main, gate = norm_matmul(x, norm, [w_main.astype(BF16), w_gate.astype(BF16)], [BF16, F32],
                             tm=512, name="even_in_proj", scaled=(0, a_width, LOG2E * HEAD_DIM ** -0.5))
    c = gate_cumsum(gate, _pad_cols(b_f.reshape(1, -1), LANES), seq=seq)
    n_pairs = a_width // LANES
    a = fox_attention(main, c, batch=batch, seq=seq, tq=tq, q_col=0, k_col=n_pairs, v_col=2 * n_pairs,
                      n_pairs=n_pairs)
    u_col = f0 // b_width
    g = spatial_gating(main, sgu_norm, w_s, b_s, tb=512, u_col=u_col, v_col=u_col + 1)
    return a, g


def _odd_mixer(x, norm, w_in, conv_w, lq1, lk1, lq2, lk2, subln, rel_bias, lam_init, *, batch, seq, tq):
    c_width = conv_w.shape[1]
    d_width = rel_bias.shape[1] * 2 * HEAD_DIM
    q0 = 3 * c_width
    (main,) = norm_matmul(x, norm, [w_in.astype(BF16)], [BF16], tm=512, name="odd_in_proj",
                          scaled=(q0, q0 + d_width, LOG2E * HEAD_DIM ** -0.5))
    c_out = short_conv(main, conv_w, batch=batch, seq=seq)
    bias, far, lam = diff_prep(rel_bias, lq1, lk1, lq2, lk2, tq=tq, lam_init=lam_init)
    q_col = q0 // LANES
    n_heads = rel_bias.shape[1]
    d_out = diff_attention(main, bias, far, lam, subln, batch=batch, seq=seq, tq=tq, q_col=q_col,
                           k_col=q_col + n_heads, v_col=q_col + 2 * n_heads, lam_init=lam_init)
    return c_out, d_out


def _moe_layer(x, norm, w_router, w13, w2, final_norm, *, tm_expert):
    t, d = x.shape
    n_exp = w_router.shape[1]
    tm_route = 256
    meta, wts, before, counts = route_tokens(x, norm, _pad_cols(w_router, LANES), tm=tm_route, n_exp=n_exp)
    counts = counts[0, :n_exp].astype(jnp.int32)
    padded = (counts + tm_expert - 1) // tm_expert * tm_expert
    ends = jnp.cumsum(padded)
    offsets = ends - padded
    pos = jnp.concatenate([offsets[meta[:, k]] + meta[:, TOP_K + k] for k in range(TOP_K)]).astype(jnp.int32)
    n_rows = TOP_K * t + (n_exp + 1) * tm_expert
    n_tiles = n_rows // tm_expert
    tile_start = jnp.arange(n_tiles, dtype=jnp.int32) * tm_expert
    tile_expert = jnp.minimum(jnp.sum(tile_start[:, None] >= ends[None, :], axis=1), n_exp - 1).astype(jnp.int32)
    n_used = (ends[-1:] // tm_expert).astype(jnp.int32)
    xs = moe_dispatch(pos, x, n_rows, tm=512)
    y = moe_experts(tile_expert, n_used, xs, norm, *_chunk_weights(w13, w2, 512), tm=tm_expert)
    before = before[::SUBLANES, :n_exp].astype(jnp.int32)
    in_tile = jnp.concatenate([before[1:], counts[None]]) - before
    first = offsets[None, :] + before
    starts = first // BF16_ROWS * BF16_ROWS
    shifts = first - starts
    tails = (shifts + in_tile > tm_route).astype(jnp.int32)
    flat = lambda a: a.reshape(-1).astype(jnp.int32)
    return moe_combine(flat(starts), flat(shifts), flat(tails), x, wts, meta, final_norm, y, tm=tm_route,
                       n_exp=n_exp)


def kernel(x, mem, rel_bias, mem_norm, final_norm, ev_norm, ev_w_in, ev_b_f, ev_sgu_norm, ev_w_s, ev_b_s, ev_w_out, ffn_w13, ffn_w2, od_norm, od_w_in, od_conv_w, od_lam_q1, od_lam_k1, od_lam_q2, od_lam_k2, od_subln, od_w_out, moe_router, moe_w13, moe_w2, x_norm, x_wq, x_wkv, x_wo, ffn_norm):
    batch, seq, d = x.shape
    mem_len = mem.shape[1]
    depth = x_norm.shape[0]
    assert depth == 2 and ev_norm.shape[0] == 1 and od_norm.shape[0] == 1
    x_heads, x_dh = 4, 128
    xf = x.reshape(batch * seq, d)
    wkv = jnp.concatenate([x_wkv[layer] for layer in range(depth)], axis=1).astype(BF16)
    (kv,) = norm_matmul(mem.reshape(batch * mem_len, d), mem_norm, [wkv], [BF16], tm=512, name="mem_kv")

    def tail(xf, mixed, w_out, layer, ffn, name):
        return layer_tail(xf, *mixed, w_out.astype(BF16), x_norm[layer], x_wq[layer].astype(BF16), kv,
                          x_wo[layer].astype(BF16), ffn, tm=512, seq=seq, mem_len=mem_len, kv_col=layer,
                          n_heads=x_heads, dh=x_dh, name=name)

    mixed = _even_mixer(xf, ev_norm[0], ev_w_in[0], ev_b_f[0], ev_sgu_norm[0], ev_w_s[0], ev_b_s[0],
                        batch=batch, seq=seq, tq=ATTN_TILE)
    xf = tail(xf, mixed, ev_w_out[0], 0, (ffn_norm[0],) + _chunk_weights(ffn_w13[0], ffn_w2[0], 256),
              "even_tail")
    lam_init = 0.8 - 0.6 * math.exp(-0.3 * 1)
    mixed = _odd_mixer(xf, od_norm[0], od_w_in[0], od_conv_w[0], od_lam_q1[0], od_lam_k1[0], od_lam_q2[0],
                       od_lam_k2[0], od_subln[0], rel_bias, lam_init, batch=batch, seq=seq, tq=ATTN_TILE)
    xf = tail(xf, mixed, od_w_out[0], 1, None, "odd_tail")
    out = _moe_layer(xf, ffn_norm[1], moe_router[0], moe_w13[0], moe_w2[0], final_norm, tm_expert=512)
    return out.reshape(batch, seq, d)
```

```python
import functools
import math

import jax
import jax.numpy as jnp
from jax import lax
from jax.experimental import pallas as pl
from jax.experimental.pallas import tpu as pltpu

F32 = jnp.float32
BF16 = jnp.bfloat16
EPS = 1e-6
HEAD_DIM = 64
LANES = 128
SUBLANES = 8
BF16_ROWS = 16
N_BUCKETS = 32
MAX_DIST = 128
TOP_K = 2
LOG2E = 1.4426950408889634
ATTN_TILE = 512
VMEM_LIMIT = 56 * 1024 * 1024


def _params(*sem):
    return pltpu.CompilerParams(dimension_semantics=sem, vmem_limit_bytes=VMEM_LIMIT)


def _rms(x, g):
    ms = jnp.mean(x * x, axis=-1, keepdims=True)
    return x * lax.rsqrt(ms + EPS) * g


def _dot(a, b):
    return jnp.dot(a, b, preferred_element_type=F32)


def _dot_nt(a, b):
    return lax.dot_general(a, b, (((1,), (1,)), ((), ())), preferred_element_type=F32)


def _norm_matmul_kernel(x_ref, g_ref, *refs, n_w, chunk, scaled):
    w_refs, o_refs = refs[:n_w], refs[n_w:]
    h = _rms(x_ref[...], g_ref[...]).astype(BF16)
    s0, s1, scale = scaled
    for k, (w_ref, o_ref) in enumerate(zip(w_refs, o_refs)):
        n = w_ref.shape[1]
        for c0 in range(0, n, chunk):
            c1 = min(c0 + chunk, n)
            y = _dot(h, w_ref[:, c0:c1])
            if k == 0 and s0 <= c0 and c1 <= s1:
                y = y * scale
            o_ref[:, c0:c1] = y.astype(o_ref.dtype)


def norm_matmul(x, g, ws, out_dtypes, *, tm, name, scaled=(0, 0, 1.0)):
    t, d = x.shape
    chunk = 512
    assert scaled[0] % chunk == 0 and scaled[1] % chunk == 0
    in_specs = [pl.BlockSpec((tm, d), lambda i: (i, 0)), pl.BlockSpec((1, d), lambda i: (0, 0))]
    in_specs += [pl.BlockSpec(w.shape, lambda i: (0, 0)) for w in ws]
    out_specs = [pl.BlockSpec((tm, w.shape[1]), lambda i: (i, 0)) for w in ws]
    out_shape = [jax.ShapeDtypeStruct((t, w.shape[1]), dt) for w, dt in zip(ws, out_dtypes)]
    return pl.pallas_call(
        functools.partial(_norm_matmul_kernel, n_w=len(ws), chunk=chunk, scaled=scaled),
        grid=(t // tm,), in_specs=in_specs, out_specs=out_specs, out_shape=out_shape,
        compiler_params=_params("parallel"), name=name,
    )(x, g.reshape(1, d), *ws)


def _gate_kernel(g_ref, b_ref, c_ref):
    s = g_ref.shape[0]
    row = lax.broadcasted_iota(jnp.int32, (LANES, LANES), 0)
    col = lax.broadcasted_iota(jnp.int32, (LANES, LANES), 1)
    tri = (row >= col).astype(F32)
    carry = jnp.zeros((1, LANES), F32)
    for blk in range(s // LANES):
        z = g_ref[blk * LANES:(blk + 1) * LANES, :] + b_ref[...]
        log_f = jnp.minimum(z, 0.0) - jnp.log1p(jnp.exp(-jnp.abs(z)))
        cs = jnp.dot(tri, log_f, precision=lax.Precision.HIGHEST, preferred_element_type=F32) + carry
        c_ref[blk * LANES:(blk + 1) * LANES, :] = cs
        carry = cs[LANES - 1:LANES, :]


def gate_cumsum(g, b, *, seq):
    t = g.shape[0]
    return pl.pallas_call(
        _gate_kernel, grid=(t // seq,),
        in_specs=[pl.BlockSpec((seq, LANES), lambda i: (i, 0)), pl.BlockSpec((1, LANES), lambda i: (0, 0))],
        out_specs=pl.BlockSpec((seq, LANES), lambda i: (i, 0)),
        out_shape=jax.ShapeDtypeStruct((t, LANES), F32),
        compiler_params=_params("parallel"), name="gate_cumsum",
    )(g, b)


def _split3(x):
    hi = x.astype(BF16).astype(F32)
    rest = x - hi
    mid = rest.astype(BF16).astype(F32)
    lo = (rest - mid).astype(BF16).astype(F32)
    return hi, mid, lo


def _augment(x, in_half, lane, base, pieces, pieces_first):
    n = len(pieces)
    p0, o0 = (base, base + n) if pieces_first else (base + n, base)
    aug = jnp.where((lane >= o0) & (lane < o0 + n), 1.0, 0.0)
    for idx, piece in enumerate(pieces):
        aug = jnp.where(lane == p0 + idx, piece, aug)
    return jnp.where(in_half, x, aug.astype(x.dtype))


def _halves(lane):
    return [(lane >= HEAD_DIM * hh) & (lane < HEAD_DIM * (hh + 1)) for hh in range(2)]


def _causal_attention(qa, ka_s, v_ref, bias_ref, i, tq, m_s, l_s, acc_s):
    n_chunks = tq // LANES

    def causal():
        row = lax.broadcasted_iota(jnp.int32, (tq, tq), 0)
        col = lax.broadcasted_iota(jnp.int32, (tq, tq), 1)
        return row >= col

    def scores(h, j, near, mask):
        start = pl.multiple_of(j * tq, tq)
        s = _dot_nt(qa[h], ka_s[h, pl.ds(start, tq), :])
        if near is not None:
            s = s + bias_ref[0, near]
        if mask is not None:
            s = jnp.where(mask, s, -jnp.inf)
        return [s[:, c * LANES:(c + 1) * LANES] for c in range(n_chunks)]

    def max_tile(j, near, mask):
        for h in range(2):
            m = m_s[h]
            for chunk in scores(h, j, near, mask):
                m = jnp.maximum(m, chunk)
            m_s[h] = m

    def sum_tile(j, near, mask):
        v = v_ref[pl.ds(pl.multiple_of(j * tq, tq), tq), :]
        for h in range(2):
            m = m_s[h]
            ps = [jnp.exp2(chunk - m) for chunk in scores(h, j, near, mask)]
            l_s[h] += functools.reduce(lambda a, b: a + b, ps)
            acc_s[h] += _dot(jnp.concatenate(ps, axis=1).astype(BF16), v)

    def key_tiles(fn):
        def run(lo, hi, near):
            def body(j, carry):
                fn(j, near, None)
                return carry
            lax.fori_loop(lo, hi, body, 0)

        if bias_ref is None:
            run(0, i, None)
            fn(i, None, causal())
        else:
            n_far = jnp.maximum(i - 1, 0)
            run(0, n_far, None)
            run(n_far, i, 1)
            fn(i, 0, causal())

    m_s[...] = jnp.full(m_s.shape, -jnp.inf, F32)
    key_tiles(max_tile)
    for h in range(2):
        m_s[h] = jnp.broadcast_to(jnp.max(m_s[h], axis=1, keepdims=True), (tq, LANES))
    l_s[...] = jnp.zeros(l_s.shape, F32)
    acc_s[...] = jnp.zeros(acc_s.shape, F32)
    key_tiles(sum_tile)


def _normalised(l_s, acc_s, h):
    return acc_s[h] / jnp.sum(l_s[h], axis=1, keepdims=True)


def _fox_kernel(q_ref, k_ref, v_ref, c_ref, o_ref, ka_s, m_s, l_s, acc_s, *, tq):
    hp = pl.program_id(1)
    i = pl.program_id(2)
    lane = lax.broadcasted_iota(jnp.int32, (1, LANES), 1)
    halves = _halves(lane)

    def decay(c, hh):
        return jnp.sum(jnp.where(lane == 2 * hp + hh, c, 0.0), axis=1, keepdims=True) * LOG2E

    @pl.when(i == 0)
    def _():
        k = k_ref[...]
        c_all = c_ref[...]
        for hh in range(2):
            ka_s[hh] = _augment(k, halves[hh], lane, HEAD_DIM * (1 - hh), _split3(-decay(c_all, hh)), True)

    q = q_ref[...]
    c_q = c_ref[pl.ds(pl.multiple_of(i * tq, tq), tq), :]
    qa = [_augment(q, halves[hh], lane, HEAD_DIM * (1 - hh), _split3(decay(c_q, hh)), False) for hh in range(2)]
    _causal_attention(qa, ka_s, v_ref, None, i, tq, m_s, l_s, acc_s)
    out = jnp.where(lane < HEAD_DIM, _normalised(l_s, acc_s, 0), _normalised(l_s, acc_s, 1))
    o_ref[...] = out.astype(o_ref.dtype)


def _attn_scratch(seq, tq):
    return [pltpu.VMEM((2, seq, LANES), BF16)] + [pltpu.VMEM((2, tq, LANES), F32)] * 3


def fox_attention(qkv, c, *, batch, seq, tq, q_col, k_col, v_col, n_pairs):
    t = batch * seq
    nq = seq // tq
    return pl.pallas_call(
        functools.partial(_fox_kernel, tq=tq),
        grid=(batch, n_pairs, nq),
        in_specs=[
            pl.BlockSpec((tq, LANES), lambda b, h, i: (b * nq + i, q_col + h)),
            pl.BlockSpec((seq, LANES), lambda b, h, i: (b, k_col + h)),
            pl.BlockSpec((seq, LANES), lambda b, h, i: (b, v_col + h)),
            pl.BlockSpec((seq, LANES), lambda b, h, i: (b, 0)),
        ],
        out_specs=pl.BlockSpec((tq, LANES), lambda b, h, i: (b * nq + i, h)),
        out_shape=jax.ShapeDtypeStruct((t, n_pairs * LANES), BF16),
        scratch_shapes=_attn_scratch(seq, tq),
        compiler_params=_params("parallel", "parallel", "arbitrary"), name="fox_attention",
    )(qkv, qkv, qkv, c)


def _sgu_kernel(u_ref, v_ref, norm_ref, ws_ref, bs_ref, o_ref, *, n_groups, chunk):
    tb = u_ref.shape[0]
    row = lax.broadcasted_iota(jnp.int32, (chunk, chunk), 0)
    col = lax.broadcasted_iota(jnp.int32, (chunk, chunk), 1)
    tri = row >= col
    for g in range(n_groups):
        w = jnp.where(tri, ws_ref[g], 0.0).astype(BF16)
        bias = bs_ref[:, g:g + 1]
        gain = norm_ref[g:g + 1, :]
        for c in range(tb // chunk):
            rs = slice(c * chunk, (c + 1) * chunk)
            cs = slice(g * LANES, (g + 1) * LANES)
            vn = _rms(jax.nn.gelu(v_ref[rs, cs].astype(F32)), gain)
            mixed = _dot(w, vn.astype(BF16)) + bias
            o_ref[rs, cs] = (jax.nn.gelu(u_ref[rs, cs].astype(F32)) * mixed).astype(o_ref.dtype)


def spatial_gating(main, sgu_norm, w_s, b_s, *, tb, u_col, v_col):
    t = main.shape[0]
    n_groups, chunk, _ = w_s.shape
    width = n_groups * LANES
    return pl.pallas_call(
        functools.partial(_sgu_kernel, n_groups=n_groups, chunk=chunk),
        grid=(t // tb,),
        in_specs=[
            pl.BlockSpec((tb, width), lambda i: (i, u_col)),
            pl.BlockSpec((tb, width), lambda i: (i, v_col)),
            pl.BlockSpec(sgu_norm.shape, lambda i: (0, 0)),
            pl.BlockSpec(w_s.shape, lambda i: (0, 0, 0)),
            pl.BlockSpec((chunk, n_groups), lambda i: (0, 0)),
        ],
        out_specs=pl.BlockSpec((tb, width), lambda i: (i, 0)),
        out_shape=jax.ShapeDtypeStruct((t, width), BF16),
        compiler_params=_params("parallel"), name="spatial_gating",
    )(main, main, sgu_norm, w_s, b_s.T)


def _chunk_weights(w13, w2, tf):
    *lead, ff, d = w2.shape
    return w13.astype(BF16), w2.astype(BF16).reshape(*lead, ff // tf, tf, d)


def _swiglu_chunks(h_s, w13, w2, o_ref, gu_s):
    n, tf, _ = w2.shape

    def project(c, slot):
        for part in range(2):
            cols = pl.ds(pl.multiple_of((part * n + c) * tf, tf), tf)
            gu_s[slot, part] = _dot(h_s[...], w13[:, cols])

    def consume(c, slot):
        gate = gu_s[slot, 0]
        act = (gate * jax.nn.sigmoid(gate) * gu_s[slot, 1]).astype(BF16)
        o_ref[...] += _dot(act, w2[c])

    def pair(k, carry):
        c = 2 * k
        project(c + 1, 1)
        consume(c, 0)
        project(c + 2, 0)
        consume(c + 1, 1)
        return carry

    project(0, 0)
    lax.fori_loop(0, (n - 1) // 2, pair, 0)
    if n % 2 == 0:
        project(n - 1, 1)
        consume(n - 2, 0)
        consume(n - 1, 1)
    else:
        consume(n - 1, 0)


def _swiglu_scratch(tm, d, tf):
    return [pltpu.VMEM((tm, d), BF16), pltpu.VMEM((2, 2, tm, tf), F32)]


def _tail_kernel(x_ref, a_ref, b_ref, wa_ref, wb_ref, gx_ref, wq_ref, kv_ref, wo_ref, *rest, n_heads, dh):
    x = x_ref[...] + _dot(a_ref[...], wa_ref[...]) + _dot(b_ref[...], wb_ref[...])
    q = _dot(_rms(x, gx_ref[...]).astype(BF16), wq_ref[...]).astype(BF16)
    width = n_heads * dh
    outs = []
    for hd in range(n_heads):
        cs = slice(hd * dh, (hd + 1) * dh)
        s = _dot_nt(q[:, cs], kv_ref[:, cs]) * (dh ** -0.5)
        p = jnp.exp(s - jnp.max(s, axis=1, keepdims=True))
        p = p / jnp.sum(p, axis=1, keepdims=True)
        outs.append(_dot(p.astype(BF16), kv_ref[:, width + hd * dh:width + (hd + 1) * dh]).astype(BF16))
    x = x + _dot(jnp.concatenate(outs, axis=1), wo_ref[...])
    if len(rest) == 1:
        (o_ref,) = rest
        o_ref[...] = x
    else:
        gf_ref, w13_ref, w2_ref, o_ref, h_s, gu_s = rest
        h_s[...] = _rms(x, gf_ref[...]).astype(BF16)
        o_ref[...] = x
        _swiglu_chunks(h_s, w13_ref, w2_ref, o_ref, gu_s)


def layer_tail(x, a, b, w_out, gx, wq, kv, wo, ffn=None, *, tm, seq, mem_len, kv_col, n_heads, dh, name):
    t, d = x.shape
    per_b = seq // tm
    resident = pl.Buffered(1)
    const = lambda arr: pl.BlockSpec(arr.shape, lambda i: (0,) * arr.ndim, pipeline_mode=resident)
    rows = lambda arr: pl.BlockSpec((tm, arr.shape[1]), lambda i: (i, 0))
    vec = lambda g: g.reshape(1, d)
    wa, wb = w_out[:a.shape[1]], w_out[a.shape[1]:]
    args = [x, a, b, wa, wb, vec(gx), wq, kv, wo]
    in_specs = [rows(x), rows(a), rows(b), const(wa), const(wb), const(vec(gx)), const(wq),
                pl.BlockSpec((mem_len, 2 * n_heads * dh), lambda i: (i // per_b, kv_col)), const(wo)]
    scratch = []
    if ffn is not None:
        gf, w13, w2 = ffn
        args += [vec(gf), w13, w2]
        in_specs += [const(vec(gf)), const(w13), const(w2)]
        scratch = _swiglu_scratch(tm, d, w2.shape[1])
    return pl.pallas_call(
        functools.partial(_tail_kernel, n_heads=n_heads, dh=dh), grid=(t // tm,),
        in_specs=in_specs, out_specs=rows(x), out_shape=jax.ShapeDtypeStruct((t, d), F32),
        scratch_shapes=scratch, compiler_params=_params("parallel"), name=name,
    )(*args)


def _conv_kernel(bg_ref, cg_ref, xi_ref, w_ref, o_ref):
    s, width = o_ref.shape
    n_taps = w_ref.shape[0]
    xc = cg_ref[...].astype(F32) * xi_ref[...].astype(F32)
    row = lax.broadcasted_iota(jnp.int32, (s, width), 0)
    y = w_ref[n_taps - 1:n_taps, :] * xc
    for back in range(1, n_taps):
        shifted = jnp.where(row >= back, pltpu.roll(xc, back, axis=0), 0.0)
        y = y + w_ref[n_taps - 1 - back:n_taps - back, :] * shifted
    o_ref[...] = (bg_ref[...].astype(F32) * y).astype(o_ref.dtype)


def short_conv(main, conv_w, *, batch, seq):
    width = conv_w.shape[1]
    return pl.pallas_call(
        _conv_kernel, grid=(batch,),
        in_specs=[
            pl.BlockSpec((seq, width), lambda b: (b, 0)),
            pl.BlockSpec((seq, width), lambda b: (b, 1)),
            pl.BlockSpec((seq, width), lambda b: (b, 2)),
            pl.BlockSpec(conv_w.shape, lambda b: (0, 0)),
        ],
        out_specs=pl.BlockSpec((seq, width), lambda b: (b, 0)),
        out_shape=jax.ShapeDtypeStruct((batch * seq, width), BF16),
        compiler_params=_params("parallel"), name="short_conv",
    )(main, main, main, conv_w)


def _diff_prep_kernel(rb_ref, lq1_ref, lk1_ref, lq2_ref, lk2_ref, bias_ref, far_ref, lam_ref, *, tq, lam_init):
    n_heads = bias_ref.shape[0]
    strip = 32
    row = lax.broadcasted_iota(jnp.int32, (strip, tq), 0)
    col = lax.broadcasted_iota(jnp.int32, (strip, tq), 1)
    max_exact = N_BUCKETS // 2

    def fill(r, carry):
        r0 = pl.multiple_of(r * strip, strip)
        for which in range(2):
            n = jnp.maximum(row + r0 - col + which * tq, 0)
            nf = jnp.maximum(n, 1).astype(F32)
            large = max_exact + (jnp.log(nf / max_exact) / math.log(MAX_DIST / max_exact)
                                 * (N_BUCKETS - max_exact)).astype(jnp.int32)
            large = jnp.minimum(large, N_BUCKETS - 1)
            bucket = jnp.where(n < max_exact, n, large)
            for h in range(n_heads):
                b = jnp.zeros((strip, tq), F32)
                for kk in range(N_BUCKETS):
                    b = jnp.where(bucket == kk, rb_ref[kk, h], b)
                bias_ref[h, which, pl.ds(r0, strip), :] = (b - rb_ref[N_BUCKETS - 1, h]) * LOG2E
        return carry

    lax.fori_loop(0, tq // strip, fill, 0)
    for h in range(n_heads):
        far_ref[h] = jnp.full((1, LANES), rb_ref[N_BUCKETS - 1, h], F32) * LOG2E
    lam = (jnp.exp(jnp.sum(lq1_ref[...] * lk1_ref[...], axis=1, keepdims=True))
           - jnp.exp(jnp.sum(lq2_ref[...] * lk2_ref[...], axis=1, keepdims=True)) + lam_init)
    lam_ref[...] = jnp.broadcast_to(lam, (1, LANES))


def diff_prep(rel_bias, lq1, lk1, lq2, lk2, *, tq, lam_init):
    n_heads = rel_bias.shape[1]
    vec = lambda a: a.reshape(1, -1)
    vspec = pl.BlockSpec(memory_space=pltpu.VMEM)
    return pl.pallas_call(
        functools.partial(_diff_prep_kernel, tq=tq, lam_init=lam_init),
        in_specs=[pl.BlockSpec(memory_space=pltpu.SMEM), vspec, vspec, vspec, vspec],
        out_specs=[vspec, vspec, vspec],
        out_shape=[
            jax.ShapeDtypeStruct((n_heads, 2, tq, tq), F32),
            jax.ShapeDtypeStruct((n_heads, 1, LANES), F32),
            jax.ShapeDtypeStruct((1, LANES), F32),
        ],
        compiler_params=pltpu.CompilerParams(vmem_limit_bytes=VMEM_LIMIT), name="diff_prep",
    )(rel_bias, vec(lq1), vec(lk1), vec(lq2), vec(lk2))


def _diff_kernel(q_ref, k_ref, v_ref, bias_ref, far_ref, lam_ref, subln_ref, o_ref, ka_s, m_s, l_s, acc_s, *,
                 tq, lam_init):
    i = pl.program_id(2)
    lane = lax.broadcasted_iota(jnp.int32, (1, LANES), 1)
    halves = _halves(lane)

    @pl.when(i == 0)
    def _():
        k = k_ref[...]
        far = _split3(far_ref[0])
        for sub in range(2):
            ka_s[sub] = _augment(k, halves[sub], lane, HEAD_DIM * (1 - sub), far, True)

    q = q_ref[...]
    zero = jnp.zeros((1, LANES), F32)
    qa = [_augment(q, halves[sub], lane, HEAD_DIM * (1 - sub), (zero,) * 3, False) for sub in range(2)]
    _causal_attention(qa, ka_s, v_ref, bias_ref, i, tq, m_s, l_s, acc_s)
    o = _normalised(l_s, acc_s, 0) - lam_ref[...] * _normalised(l_s, acc_s, 1)
    o_ref[...] = (_rms(o, subln_ref[...]) * (1.0 - lam_init)).astype(o_ref.dtype)


def diff_attention(main, bias, far, lam, subln, *, batch, seq, tq, q_col, k_col, v_col, lam_init):
    t = batch * seq
    nq = seq // tq
    n_heads = bias.shape[0]
    return pl.pallas_call(
        functools.partial(_diff_kernel, tq=tq, lam_init=lam_init),
        grid=(batch, n_heads, nq),
        in_specs=[
            pl.BlockSpec((tq, LANES), lambda b, h, i: (b * nq + i, q_col + h)),
            pl.BlockSpec((seq, LANES), lambda b, h, i: (b, k_col + h)),
            pl.BlockSpec((seq, LANES), lambda b, h, i: (b, v_col + h)),
            pl.BlockSpec((1, 2, tq, tq), lambda b, h, i: (h, 0, 0, 0)),
            pl.BlockSpec((1, 1, LANES), lambda b, h, i: (h, 0, 0)),
            pl.BlockSpec((1, LANES), lambda b, h, i: (0, 0)),
            pl.BlockSpec((1, LANES), lambda b, h, i: (0, 0)),
        ],
        out_specs=pl.BlockSpec((tq, LANES), lambda b, h, i: (b * nq + i, h)),
        out_shape=jax.ShapeDtypeStruct((t, n_heads * LANES), BF16),
        scratch_shapes=_attn_scratch(seq, tq),
        compiler_params=_params("parallel", "parallel", "arbitrary"), name="diff_attention",
    )(main, main, main, bias, far, lam, subln.reshape(1, LANES))


def _router_kernel(x_ref, g_ref, wr_ref, meta_ref, wts_ref, before_ref, cnt_ref, carry_s, *, n_exp):
    tm = x_ref.shape[0]

    @pl.when(pl.program_id(0) == 0)
    def _():
        carry_s[...] = jnp.zeros_like(carry_s)

    before_ref[...] = jnp.broadcast_to(carry_s[...], before_ref.shape)

    h = _rms(x_ref[...], g_ref[...])
    logits = jnp.dot(h, wr_ref[...], precision=lax.Precision.HIGHEST, preferred_element_type=F32)
    lane = lax.broadcasted_iota(jnp.int32, (tm, LANES), 1)
    lane_f = lane.astype(F32)
    logits = jnp.where(lane < n_exp, logits, -jnp.inf)
    m1 = jnp.max(logits, axis=1, keepdims=True)
    i1 = jnp.min(jnp.where(logits == m1, lane_f, float(LANES)), axis=1, keepdims=True)
    rest = jnp.where(lane_f == i1, -jnp.inf, logits)
    m2 = jnp.max(rest, axis=1, keepdims=True)
    i2 = jnp.min(jnp.where(rest == m2, lane_f, float(LANES)), axis=1, keepdims=True)
    e = jnp.exp(m2 - m1)
    w1 = 1.0 / (1.0 + e)
    w2 = e / (1.0 + e)
    sel1 = lane_f == i1
    sel2 = lane_f == i2
    onehot = jnp.where(sel1 | sel2, 1.0, 0.0)
    row = lax.broadcasted_iota(jnp.int32, (tm, tm), 0)
    col = lax.broadcasted_iota(jnp.int32, (tm, tm), 1)
    before = jnp.where(row > col, 1.0, 0.0).astype(BF16)
    local = _dot(before, onehot.astype(BF16))
    rank = local + carry_s[...]
    pick = lambda sel, val: jnp.sum(jnp.where(sel, val, 0.0), axis=1, keepdims=True)
    fields = [i1, i2, pick(sel1, rank), pick(sel2, rank), pick(sel1, local), pick(sel2, local)]
    meta = jnp.zeros((tm, LANES), F32)
    for idx, field in enumerate(fields):
        meta = jnp.where(lane == idx, field, meta)
    meta_ref[...] = meta.astype(jnp.int32)
    wts_ref[...] = jnp.where(lane == 0, w1, jnp.where(lane == 1, w2, 0.0))
    carry_s[...] += jnp.sum(onehot, axis=0, keepdims=True)
    cnt_ref[...] = carry_s[...]


def route_tokens(x, g, wr, *, tm, n_exp):
    t, d = x.shape
    return pl.pallas_call(
        functools.partial(_router_kernel, n_exp=n_exp),
        grid=(t // tm,),
        in_specs=[
            pl.BlockSpec((tm, d), lambda i: (i, 0)),
            pl.BlockSpec((1, d), lambda i: (0, 0)),
            pl.BlockSpec((d, LANES), lambda i: (0, 0)),
        ],
        out_specs=[
            pl.BlockSpec((tm, LANES), lambda i: (i, 0)),
            pl.BlockSpec((tm, LANES), lambda i: (i, 0)),
            pl.BlockSpec((SUBLANES, LANES), lambda i: (i, 0)),
            pl.BlockSpec((1, LANES), lambda i: (0, 0)),
        ],
        out_shape=[
            jax.ShapeDtypeStruct((t, LANES), jnp.int32),
            jax.ShapeDtypeStruct((t, LANES), F32),
            jax.ShapeDtypeStruct((t // tm * SUBLANES, LANES), F32),
            jax.ShapeDtypeStruct((1, LANES), F32),
        ],
        scratch_shapes=[pltpu.VMEM((1, LANES), F32)],
        compiler_params=_params("arbitrary"), name="moe_router",
    )(x, g.reshape(1, d), wr)


def _dispatch_kernel(start_ref, shift_ref, keep_ref, x_ref, g_ref, meta_ref, zeros_hbm, xs_hbm,
                     stage_s, carry_s, sems, *, tm, n_exp):
    del zeros_hbm
    j = pl.program_id(0)
    slot = j % 2
    rows = tm + BF16_ROWS

    def block_copy(step, e, buf):
        first = pl.multiple_of(start_ref[step * n_exp + e], BF16_ROWS)
        return pltpu.make_async_copy(stage_s.at[buf, e], xs_hbm.at[pl.ds(first, rows)], sems.at[buf])

    @pl.when(j == 0)
    def _():
        carry_s[...] = jnp.zeros_like(carry_s)

    h = _rms(x_ref[...], g_ref[...]).astype(BF16)
    fields = meta_ref[...].astype(F32).T
    slot_row = lax.broadcasted_iota(jnp.int32, (rows, tm), 0).astype(F32)
    for e in range(n_exp):
        key = j * n_exp + e
        idx = jnp.where(fields[0:1] == e, fields[4:5], jnp.where(fields[1:2] == e, fields[5:6], -2.0 * rows))
        idx = idx + shift_ref[key].astype(F32)
        onehot = jnp.where(slot_row == idx, 1.0, 0.0).astype(BF16)
        stage_s[slot, e] = _dot(onehot, h).astype(BF16)
        stage_s[slot, e, 0:BF16_ROWS, :] += carry_s[e]
        keep = pl.multiple_of(keep_ref[key], BF16_ROWS)
        carry_s[e] = stage_s[slot, e, pl.ds(keep, BF16_ROWS), :]

    @pl.when(j > 0)
    def _():
        for e in range(n_exp):
            block_copy(j - 1, e, 1 - slot).wait()

    for e in range(n_exp):
        block_copy(j, e, slot).start()

    @pl.when(j == pl.num_programs(0) - 1)
    def _():
        for e in range(n_exp):
            block_copy(j, e, slot).wait()


def moe_dispatch(starts, shifts, keeps, x, g, meta, n_rows, *, tm, n_exp):
    t, d = x.shape
    zeros = jnp.zeros((n_rows, d), BF16)
    rows = tm + BF16_ROWS
    return pl.pallas_call(
        functools.partial(_dispatch_kernel, tm=tm, n_exp=n_exp),
        grid_spec=pltpu.PrefetchScalarGridSpec(
            num_scalar_prefetch=3, grid=(t // tm,),
            in_specs=[
                pl.BlockSpec((tm, d), lambda i, *_: (i, 0)),
                pl.BlockSpec((1, d), lambda i, *_: (0, 0)),
                pl.BlockSpec((tm, LANES), lambda i, *_: (i, 0)),
                pl.BlockSpec(memory_space=pl.ANY),
            ],
            out_specs=pl.BlockSpec(memory_space=pl.ANY),
            scratch_shapes=[pltpu.VMEM((2, n_exp, rows, d), BF16), pltpu.VMEM((n_exp, BF16_ROWS, d), BF16),
                            pltpu.SemaphoreType.DMA((2,))],
        ),
        out_shape=jax.ShapeDtypeStruct((n_rows, d), BF16),
        input_output_aliases={6: 0},
        compiler_params=_params("arbitrary"),
        name="moe_dispatch",
    )(starts, shifts, keeps, x, g.reshape(1, d), meta, zeros)


def _expert_kernel(te_ref, used_ref, h_ref, w13_ref, w2_ref, o_ref, gu_s, acc_s):
    del te_ref
    used = used_ref[pl.program_id(0)] != 0

    @pl.when(used)
    def _():
        acc_s[...] = jnp.zeros_like(acc_s)
        _swiglu_chunks(h_ref, w13_ref.at[0], w2_ref.at[0], acc_s, gu_s)
        o_ref[...] = acc_s[...].astype(o_ref.dtype)

    @pl.when(jnp.logical_not(used))
    def _():
        o_ref[...] = jnp.zeros_like(o_ref)


def moe_experts(tile_expert, tile_used, hs, w13, w2, *, tm):
    n_rows, d = hs.shape
    tf = w2.shape[2]
    resident = pl.Buffered(1)
    return pl.pallas_call(
        _expert_kernel,
        grid_spec=pltpu.PrefetchScalarGridSpec(
            num_scalar_prefetch=2, grid=(n_rows // tm,),
            in_specs=[
                pl.BlockSpec((tm, d), lambda i, te, tu: (i, 0)),
                pl.BlockSpec((1,) + w13.shape[1:], lambda i, te, tu: (te[i], 0, 0), pipeline_mode=resident),
                pl.BlockSpec((1,) + w2.shape[1:], lambda i, te, tu: (te[i], 0, 0, 0), pipeline_mode=resident),
            ],
            out_specs=pl.BlockSpec((tm, d), lambda i, te, tu: (i, 0)),
            scratch_shapes=[pltpu.VMEM((2, 2, tm, tf), F32), pltpu.VMEM((tm, d), F32)],
        ),
        out_shape=jax.ShapeDtypeStruct((n_rows, d), BF16),
        compiler_params=_params("arbitrary"), name="moe_experts",
    )(tile_expert, tile_used, hs, w13, w2)


def _combine_kernel(start_ref, shift_ref, tail_ref, x_ref, wts_ref, meta_ref, g_ref, y_hbm, o_ref, blk_s, sems, *,
                    tm, n_exp):
    j = pl.program_id(0)
    slot = j % 2
    rows = tm + BF16_ROWS

    def block_copy(step, e, buf):
        first = pl.multiple_of(start_ref[step * n_exp + e], BF16_ROWS)
        return pltpu.make_async_copy(y_hbm.at[pl.ds(first, rows)], blk_s.at[buf, e], sems.at[buf])

    def fetch(step, buf):
        for e in range(n_exp):
            block_copy(step, e, buf).start()

    @pl.when(j == 0)
    def _():
        fetch(0, 0)

    @pl.when(j + 1 < pl.num_programs(0))
    def _():
        fetch(j + 1, 1 - slot)

    for e in range(n_exp):
        block_copy(j, e, slot).wait()

    meta = meta_ref[...]
    wts = wts_ref[...]
    col = lax.broadcasted_iota(jnp.int32, (tm, tm), 1)
    tail_col = lax.broadcasted_iota(jnp.int32, (tm, BF16_ROWS), 1) + tm

    def pick(e):
        sel = [meta[:, k:k + 1] == e for k in range(TOP_K)]
        idx = jnp.where(sel[0], meta[:, 4:5], jnp.where(sel[1], meta[:, 5:6], -2 * rows)) + shift_ref[j * n_exp + e]
        w = jnp.where(sel[0], wts[:, 0:1], jnp.where(sel[1], wts[:, 1:2], 0.0))
        return idx, w

    acc = x_ref[...]
    for e in range(n_exp):
        idx, w = pick(e)
        onehot = jnp.where(col == idx, 1.0, 0.0).astype(BF16)
        acc = acc + w * _dot(onehot, blk_s[slot, e, 0:tm, :])
    o_ref[...] = acc

    for e in range(n_exp):
        @pl.when(tail_ref[j * n_exp + e] != 0)
        def _(e=e):
            idx, w = pick(e)
            onehot_tail = jnp.where(tail_col == idx, 1.0, 0.0).astype(BF16)
            o_ref[...] += w * _dot(onehot_tail, blk_s[slot, e, tm:rows, :])

    o_ref[...] = _rms(o_ref[...], g_ref[...])


def moe_combine(starts, shifts, tails, x, wts, meta, g, y, *, tm, n_exp):
    t, d = x.shape
    n_pre = 3
    return pl.pallas_call(
        functools.partial(_combine_kernel, tm=tm, n_exp=n_exp),
        grid_spec=pltpu.PrefetchScalarGridSpec(
            num_scalar_prefetch=n_pre, grid=(t // tm,),
            in_specs=[
                pl.BlockSpec((tm, d), lambda i, *_: (i, 0)),
                pl.BlockSpec((tm, LANES), lambda i, *_: (i, 0)),
                pl.BlockSpec((tm, LANES), lambda i, *_: (i, 0)),
                pl.BlockSpec((1, d), lambda i, *_: (0, 0)),
                pl.BlockSpec(memory_space=pl.ANY),
            ],
            out_specs=pl.BlockSpec((tm, d), lambda i, *_: (i, 0)),
            scratch_shapes=[pltpu.VMEM((2, n_exp, tm + BF16_ROWS, d), BF16), pltpu.SemaphoreType.DMA((2,))],
        ),
        out_shape=jax.ShapeDtypeStruct((t, d), F32),
        compiler_params=_params("arbitrary"), name="moe_combine",
    )(starts, shifts, tails, x, wts, meta, g.reshape(1, d), y)


def _pad_cols(w, n):
    return jnp.pad(w, ((0, 0), (0, n - w.shape[1])))


def _even_mixer(x, norm, w_in, b_f, sgu_norm, w_s, b_s, *, batch, seq, tq):
    n_heads = b_f.shape[0]
    a_width = n_heads * HEAD_DIM
    b_width = w_s.shape[0] * LANES
    f0 = 3 * a_width
    w_main = jnp.concatenate([w_in[:, :f0], w_in[:, f0 + n_heads:]], axis=1)
    w_gate = _pad_cols(w_in[:, f0:f0 + n_heads], LANES)
    main, gate = norm_matmul(x, norm, [w_main.astype(BF16), w_gate.astype(BF16)], [BF16, F32],
                             tm=512, name="even_in_proj", scaled=(0, a_width, LOG2E * HEAD_DIM ** -0.5))
    c = gate_cumsum(gate, _pad_cols(b_f.reshape(1, -1), LANES), seq=seq)
    n_pairs = a_width // LANES
    a = fox_attention(main, c, batch=batch, seq=seq, tq=tq, q_col=0, k_col=n_pairs, v_col=2 * n_pairs,
                      n_pairs=n_pairs)
    u_col = f0 // b_width
    g = spatial_gating(main, sgu_norm, w_s, b_s, tb=512, u_col=u_col, v_col=u_col + 1)
    return a, g


def _odd_mixer(x, norm, w_in, conv_w, lq1, lk1, lq2, lk2, subln, rel_bias, lam_init, *, batch, seq, tq):
    c_width = conv_w.shape[1]
    d_width = rel_bias.shape[1] * 2 * HEAD_DIM
    q0 = 3 * c_width
    (main,) = norm_matmul(x, norm, [w_in.astype(BF16)], [BF16], tm=512, name="odd_in_proj",
                          scaled=(q0, q0 + d_width, LOG2E * HEAD_DIM ** -0.5))
    c_out = short_conv(main, conv_w, batch=batch, seq=seq)
    bias, far, lam = diff_prep(rel_bias, lq1, lk1, lq2, lk2, tq=tq, lam_init=lam_init)
    q_col = q0 // LANES
    n_heads = rel_bias.shape[1]
    d_out = diff_attention(main, bias, far, lam, subln, batch=batch, seq=seq, tq=tq, q_col=q_col,
                           k_col=q_col + n_heads, v_col=q_col + 2 * n_heads, lam_init=lam_init)
    return c_out, d_out


def _moe_layer(x, norm, w_router, w13, w2, final_norm, *, tm_expert):
    t, d = x.shape
    n_exp = w_router.shape[1]
    tm_route = 256
    meta, wts, before, counts = route_tokens(x, norm, _pad_cols(w_router, LANES), tm=tm_route, n_exp=n_exp)
    counts = counts[0, :n_exp].astype(jnp.int32)
    block_rows = tm_route + BF16_ROWS
    padded = (counts + block_rows + tm_expert - 1) // tm_expert * tm_expert
    ends = jnp.cumsum(padded)
    offsets = ends - padded
    n_rows = TOP_K * t + n_exp * (tm_expert + pl.cdiv(block_rows, tm_expert) * tm_expert)
    tile_start = jnp.arange(n_rows // tm_expert, dtype=jnp.int32) * tm_expert
    tile_expert = jnp.minimum(jnp.sum(tile_start[:, None] >= ends[None, :], axis=1), n_exp - 1).astype(jnp.int32)
    tile_used = (tile_start < (offsets + counts)[tile_expert]).astype(jnp.int32)
    before = before[::SUBLANES, :n_exp].astype(jnp.int32)
    in_tile = jnp.concatenate([before[1:], counts[None]]) - before
    first = offsets[None, :] + before
    starts = first // BF16_ROWS * BF16_ROWS
    shifts = first - starts
    keeps = (shifts + in_tile) // BF16_ROWS * BF16_ROWS
    tails = (shifts + in_tile > tm_route).astype(jnp.int32)
    flat = lambda a: a.reshape(-1).astype(jnp.int32)
    hs = moe_dispatch(flat(starts), flat(shifts), flat(keeps), x, norm, meta, n_rows, tm=tm_route, n_exp=n_exp)
    y = moe_experts(tile_expert, tile_used, hs, *_chunk_weights(w13, w2, 512), tm=tm_expert)
    return moe_combine(flat(starts), flat(shifts), flat(tails), x, wts, meta, final_norm, y, tm=tm_route,
                       n_exp=n_exp)


def kernel(x, mem, rel_bias, mem_norm, final_norm, ev_norm, ev_w_in, ev_b_f, ev_sgu_norm, ev_w_s, ev_b_s, ev_w_out, ffn_w13, ffn_w2, od_norm, od_w_in, od_conv_w, od_lam_q1, od_lam_k1, od_lam_q2, od_lam_k2, od_subln, od_w_out, moe_router, moe_w13, moe_w2, x_norm, x_wq, x_wkv, x_wo, ffn_norm):
    batch, seq, d = x.shape
    mem_len = mem.shape[1]
    depth = x_norm.shape[0]
    assert depth == 2 and ev_norm.shape[0] == 1 and od_norm.shape[0] == 1
    x_heads, x_dh = 4, 128
    xf = x.reshape(batch * seq, d)
    wkv = jnp.concatenate([x_wkv[layer] for layer in range(depth)], axis=1).astype(BF16)
    (kv,) = norm_matmul(mem.reshape(batch * mem_len, d), mem_norm, [wkv], [BF16], tm=512, name="mem_kv")

    def tail(xf, mixed, w_out, layer, ffn, name):
        return layer_tail(xf, *mixed, w_out.astype(BF16), x_norm[layer], x_wq[layer].astype(BF16), kv,
                          x_wo[layer].astype(BF16), ffn, tm=512, seq=seq, mem_len=mem_len, kv_col=layer,
                          n_heads=x_heads, dh=x_dh, name=name)

    mixed = _even_mixer(xf, ev_norm[0], ev_w_in[0], ev_b_f[0], ev_sgu_norm[0], ev_w_s[0], ev_b_s[0],
                        batch=batch, seq=seq, tq=ATTN_TILE)
    xf = tail(xf, mixed, ev_w_out[0], 0, (ffn_norm[0],) + _chunk_weights(ffn_w13[0], ffn_w2[0], 256),
              "even_tail")
    lam_init = 0.8 - 0.6 * math.exp(-0.3 * 1)
    mixed = _odd_mixer(xf, od_norm[0], od_w_in[0], od_conv_w[0], od_lam_q1[0], od_lam_k1[0], od_lam_q2[0],
                       od_lam_k2[0], od_subln[0], rel_bias, lam_init, batch=batch, seq=seq, tq=ATTN_TILE)
    xf = tail(xf, mixed, od_w_out[0], 1, None, "odd_tail")
    out = _moe_layer(xf, ffn_norm[1], moe_router[0], moe_w13[0], moe_w2[0], final_norm, tm_expert=512)
    return out.reshape(batch, seq, d)
```

```python
import functools
import math

import jax
import jax.numpy as jnp
from jax import lax
from jax.experimental import pallas as pl
from jax.experimental.pallas import tpu as pltpu

F32 = jnp.float32
BF16 = jnp.bfloat16
EPS = 1e-6
HEAD_DIM = 64
LANES = 128
SUBLANES = 8
BF16_ROWS = 16
N_BUCKETS = 32
MAX_DIST = 128
TOP_K = 2
LOG2E = 1.4426950408889634
ATTN_TILE = 512
VMEM_LIMIT = 56 * 1024 * 1024


def _params(*sem):
    return pltpu.CompilerParams(dimension_semantics=sem, vmem_limit_bytes=VMEM_LIMIT)


def _rms(x, g):
    ms = jnp.mean(x * x, axis=-1, keepdims=True)
    return x * lax.rsqrt(ms + EPS) * g


def _dot(a, b):
    return jnp.dot(a, b, preferred_element_type=F32)


def _dot_nt(a, b):
    return lax.dot_general(a, b, (((1,), (1,)), ((), ())), preferred_element_type=F32)


def _norm_matmul_kernel(x_ref, g_ref, *refs, n_w, chunk, scaled):
    w_refs, o_refs = refs[:n_w], refs[n_w:]
    h = _rms(x_ref[...], g_ref[...]).astype(BF16)
    s0, s1, scale = scaled
    for k, (w_ref, o_ref) in enumerate(zip(w_refs, o_refs)):
        n = w_ref.shape[1]
        for c0 in range(0, n, chunk):
            c1 = min(c0 + chunk, n)
            y = _dot(h, w_ref[:, c0:c1])
            if k == 0 and s0 <= c0 and c1 <= s1:
                y = y * scale
            o_ref[:, c0:c1] = y.astype(o_ref.dtype)


def norm_matmul(x, g, ws, out_dtypes, *, tm, name, scaled=(0, 0, 1.0)):
    t, d = x.shape
    chunk = 512
    assert scaled[0] % chunk == 0 and scaled[1] % chunk == 0
    in_specs = [pl.BlockSpec((tm, d), lambda i: (i, 0)), pl.BlockSpec((1, d), lambda i: (0, 0))]
    in_specs += [pl.BlockSpec(w.shape, lambda i: (0, 0)) for w in ws]
    out_specs = [pl.BlockSpec((tm, w.shape[1]), lambda i: (i, 0)) for w in ws]
    out_shape = [jax.ShapeDtypeStruct((t, w.shape[1]), dt) for w, dt in zip(ws, out_dtypes)]
    return pl.pallas_call(
        functools.partial(_norm_matmul_kernel, n_w=len(ws), chunk=chunk, scaled=scaled),
        grid=(t // tm,), in_specs=in_specs, out_specs=out_specs, out_shape=out_shape,
        compiler_params=_params("parallel"), name=name,
    )(x, g.reshape(1, d), *ws)


def _gate_kernel(g_ref, b_ref, c_ref):
    s = g_ref.shape[0]
    row = lax.broadcasted_iota(jnp.int32, (LANES, LANES), 0)
    col = lax.broadcasted_iota(jnp.int32, (LANES, LANES), 1)
    tri = (row >= col).astype(F32)
    carry = jnp.zeros((1, LANES), F32)
    for blk in range(s // LANES):
        z = g_ref[blk * LANES:(blk + 1) * LANES, :] + b_ref[...]
        log_f = jnp.minimum(z, 0.0) - jnp.log1p(jnp.exp(-jnp.abs(z)))
        cs = jnp.dot(tri, log_f, precision=lax.Precision.HIGHEST, preferred_element_type=F32) + carry
        c_ref[blk * LANES:(blk + 1) * LANES, :] = cs
        carry = cs[LANES - 1:LANES, :]


def gate_cumsum(g, b, *, seq):
    t = g.shape[0]
    return pl.pallas_call(
        _gate_kernel, grid=(t // seq,),
        in_specs=[pl.BlockSpec((seq, LANES), lambda i: (i, 0)), pl.BlockSpec((1, LANES), lambda i: (0, 0))],
        out_specs=pl.BlockSpec((seq, LANES), lambda i: (i, 0)),
        out_shape=jax.ShapeDtypeStruct((t, LANES), F32),
        compiler_params=_params("parallel"), name="gate_cumsum",
    )(g, b)


def _split3(x):
    hi = x.astype(BF16).astype(F32)
    rest = x - hi
    mid = rest.astype(BF16).astype(F32)
    lo = (rest - mid).astype(BF16).astype(F32)
    return hi, mid, lo


def _augment(x, in_half, lane, base, pieces, pieces_first):
    n = len(pieces)
    p0, o0 = (base, base + n) if pieces_first else (base + n, base)
    aug = jnp.where((lane >= o0) & (lane < o0 + n), 1.0, 0.0)
    for idx, piece in enumerate(pieces):
        aug = jnp.where(lane == p0 + idx, piece, aug)
    return jnp.where(in_half, x, aug.astype(x.dtype))


def _halves(lane):
    return [(lane >= HEAD_DIM * hh) & (lane < HEAD_DIM * (hh + 1)) for hh in range(2)]


def _causal_attention(qa, ka_s, v_ref, bias_ref, i, tq, m_s, l_s, acc_s):
    n_chunks = tq // LANES

    def causal():
        row = lax.broadcasted_iota(jnp.int32, (tq, tq), 0)
        col = lax.broadcasted_iota(jnp.int32, (tq, tq), 1)
        return row >= col

    def scores(h, j, near, mask):
        start = pl.multiple_of(j * tq, tq)
        s = _dot_nt(qa[h], ka_s[h, pl.ds(start, tq), :])
        if near is not None:
            s = s + bias_ref[0, near]
        if mask is not None:
            s = jnp.where(mask, s, -jnp.inf)
        return [s[:, c * LANES:(c + 1) * LANES] for c in range(n_chunks)]

    def max_tile(j, near, mask):
        for h in range(2):
            m = m_s[h]
            for chunk in scores(h, j, near, mask):
                m = jnp.maximum(m, chunk)
            m_s[h] = m

    def sum_tile(j, near, mask):
        v = v_ref[pl.ds(pl.multiple_of(j * tq, tq), tq), :]
        for h in range(2):
            m = m_s[h]
            ps = [jnp.exp2(chunk - m) for chunk in scores(h, j, near, mask)]
            l_s[h] += functools.reduce(lambda a, b: a + b, ps)
            acc_s[h] += _dot(jnp.concatenate(ps, axis=1).astype(BF16), v)

    def key_tiles(fn):
        def run(lo, hi, near):
            def body(j, carry):
                fn(j, near, None)
                return carry
            lax.fori_loop(lo, hi, body, 0)

        if bias_ref is None:
            run(0, i, None)
            fn(i, None, causal())
        else:
            n_far = jnp.maximum(i - 1, 0)
            run(0, n_far, None)
            run(n_far, i, 1)
            fn(i, 0, causal())

    m_s[...] = jnp.full(m_s.shape, -jnp.inf, F32)
    key_tiles(max_tile)
    for h in range(2):
        m_s[h] = jnp.broadcast_to(jnp.max(m_s[h], axis=1, keepdims=True), (tq, LANES))
    l_s[...] = jnp.zeros(l_s.shape, F32)
    acc_s[...] = jnp.zeros(acc_s.shape, F32)
    key_tiles(sum_tile)


def _normalised(l_s, acc_s, h):
    return acc_s[h] / jnp.sum(l_s[h], axis=1, keepdims=True)


def _fox_kernel(q_ref, k_ref, v_ref, c_ref, o_ref, ka_s, m_s, l_s, acc_s, *, tq):
    hp = pl.program_id(1)
    i = pl.program_id(2)
    lane = lax.broadcasted_iota(jnp.int32, (1, LANES), 1)
    halves = _halves(lane)

    def decay(c, hh):
        return jnp.sum(jnp.where(lane == 2 * hp + hh, c, 0.0), axis=1, keepdims=True) * LOG2E

    @pl.when(i == 0)
    def _():
        k = k_ref[...]
        c_all = c_ref[...]
        for hh in range(2):
            ka_s[hh] = _augment(k, halves[hh], lane, HEAD_DIM * (1 - hh), _split3(-decay(c_all, hh)), True)

    q = q_ref[...]
    c_q = c_ref[pl.ds(pl.multiple_of(i * tq, tq), tq), :]
    qa = [_augment(q, halves[hh], lane, HEAD_DIM * (1 - hh), _split3(decay(c_q, hh)), False) for hh in range(2)]
    _causal_attention(qa, ka_s, v_ref, None, i, tq, m_s, l_s, acc_s)
    out = jnp.where(lane < HEAD_DIM, _normalised(l_s, acc_s, 0), _normalised(l_s, acc_s, 1))
    o_ref[...] = out.astype(o_ref.dtype)


def _attn_scratch(seq, tq):
    return [pltpu.VMEM((2, seq, LANES), BF16)] + [pltpu.VMEM((2, tq, LANES), F32)] * 3


def fox_attention(qkv, c, *, batch, seq, tq, q_col, k_col, v_col, n_pairs):
    t = batch * seq
    nq = seq // tq
    return pl.pallas_call(
        functools.partial(_fox_kernel, tq=tq),
        grid=(batch, n_pairs, nq),
        in_specs=[
            pl.BlockSpec((tq, LANES), lambda b, h, i: (b * nq + i, q_col + h)),
            pl.BlockSpec((seq, LANES), lambda b, h, i: (b, k_col + h)),
            pl.BlockSpec((seq, LANES), lambda b, h, i: (b, v_col + h)),
            pl.BlockSpec((seq, LANES), lambda b, h, i: (b, 0)),
        ],
        out_specs=pl.BlockSpec((tq, LANES), lambda b, h, i: (b * nq + i, h)),
        out_shape=jax.ShapeDtypeStruct((t, n_pairs * LANES), BF16),
        scratch_shapes=_attn_scratch(seq, tq),
        compiler_params=_params("parallel", "parallel", "arbitrary"), name="fox_attention",
    )(qkv, qkv, qkv, c)


def _spatial_gate(u_ref, v_ref, norm_ref, ws_ref, bs_ref, o_ref):
    tb = u_ref.shape[0]
    n_groups, chunk, _ = ws_ref.shape
    row = lax.broadcasted_iota(jnp.int32, (chunk, chunk), 0)
    col = lax.broadcasted_iota(jnp.int32, (chunk, chunk), 1)
    tri = row >= col
    for g in range(n_groups):
        w = jnp.where(tri, ws_ref[g], 0.0).astype(BF16)
        bias = bs_ref[:, g:g + 1]
        gain = norm_ref[g:g + 1, :]
        for c in range(tb // chunk):
            rs = slice(c * chunk, (c + 1) * chunk)
            cs = slice(g * LANES, (g + 1) * LANES)
            vn = _rms(jax.nn.gelu(v_ref[rs, cs].astype(F32)), gain)
            mixed = _dot(w, vn.astype(BF16)) + bias
            o_ref[rs, cs] = (jax.nn.gelu(u_ref[rs, cs].astype(F32)) * mixed).astype(o_ref.dtype)


def _chunk_weights(w13, w2, tf):
    *lead, ff, d = w2.shape
    return w13.astype(BF16), w2.astype(BF16).reshape(*lead, ff // tf, tf, d)


def _swiglu_chunks(h_s, w13, w2, o_ref, gu_s):
    n, tf, _ = w2.shape

    def project(c, slot):
        for part in range(2):
            cols = pl.ds(pl.multiple_of((part * n + c) * tf, tf), tf)
            gu_s[slot, part] = _dot(h_s[...], w13[:, cols])

    def consume(c, slot):
        gate = gu_s[slot, 0]
        act = (gate * jax.nn.sigmoid(gate) * gu_s[slot, 1]).astype(BF16)
        o_ref[...] += _dot(act, w2[c])

    def pair(k, carry):
        c = 2 * k
        project(c + 1, 1)
        consume(c, 0)
        project(c + 2, 0)
        consume(c + 1, 1)
        return carry

    project(0, 0)
    lax.fori_loop(0, (n - 1) // 2, pair, 0)
    if n % 2 == 0:
        project(n - 1, 1)
        consume(n - 2, 0)
        consume(n - 1, 1)
    else:
        consume(n - 1, 0)


def _swiglu_scratch(tm, d, tf):
    return [pltpu.VMEM((tm, d), BF16), pltpu.VMEM((2, 2, tm, tf), F32)]


def _tail_kernel(*refs, n_heads, dh, gated, ffn):
    refs = list(refs)
    take = lambda n: [refs.pop(0) for _ in range(n)]
    x_ref, a_ref = take(2)
    b_in = take(5 if gated else 1)
    wa_ref, wb_ref, gx_ref, wq_ref, kv_ref, wo_ref = take(6)
    ffn_in = take(3 if ffn else 0)
    (o_ref,) = take(1)
    if gated:
        (b_ref,) = take(1)
        _spatial_gate(*b_in, b_ref)
    else:
        (b_ref,) = b_in
    rest = ffn_in + [o_ref] + refs
    x = x_ref[...] + _dot(a_ref[...], wa_ref[...]) + _dot(b_ref[...], wb_ref[...])
    q = _dot(_rms(x, gx_ref[...]).astype(BF16), wq_ref[...]).astype(BF16)
    width = n_heads * dh
    outs = []
    for hd in range(n_heads):
        cs = slice(hd * dh, (hd + 1) * dh)
        s = _dot_nt(q[:, cs], kv_ref[:, cs]) * (dh ** -0.5)
        p = jnp.exp(s - jnp.max(s, axis=1, keepdims=True))
        p = p / jnp.sum(p, axis=1, keepdims=True)
        outs.append(_dot(p.astype(BF16), kv_ref[:, width + hd * dh:width + (hd + 1) * dh]).astype(BF16))
    x = x + _dot(jnp.concatenate(outs, axis=1), wo_ref[...])
    if len(rest) == 1:
        (o_ref,) = rest
        o_ref[...] = x
    else:
        gf_ref, w13_ref, w2_ref, o_ref, h_s, gu_s = rest
        h_s[...] = _rms(x, gf_ref[...]).astype(BF16)
        o_ref[...] = x
        _swiglu_chunks(h_s, w13_ref, w2_ref, o_ref, gu_s)


def layer_tail(x, a, b, w_out, gx, wq, kv, wo, ffn=None, *, tm, seq, mem_len, kv_col, n_heads, dh, name):
    t, d = x.shape
    per_b = seq // tm
    resident = pl.Buffered(1)
    const = lambda arr: pl.BlockSpec(arr.shape, lambda i: (0,) * arr.ndim, pipeline_mode=resident)
    rows = lambda arr: pl.BlockSpec((tm, arr.shape[1]), lambda i: (i, 0))
    vec = lambda g: g.reshape(1, d)
    gated = isinstance(b, tuple)
    scratch = []
    if gated:
        proj, u_col, v_col, sgu_norm, w_s, b_s = b
        b_width = w_s.shape[0] * LANES
        b_args = [proj, proj, sgu_norm, w_s, b_s.T]
        b_specs = [pl.BlockSpec((tm, b_width), lambda i: (i, u_col)), pl.BlockSpec((tm, b_width), lambda i: (i, v_col)),
                   const(sgu_norm), const(w_s), const(b_s.T)]
        scratch.append(pltpu.VMEM((tm, b_width), BF16))
    else:
        b_width = b.shape[1]
        b_args, b_specs = [b], [rows(b)]
    wa, wb = w_out[:a.shape[1]], w_out[a.shape[1]:]
    assert wb.shape[0] == b_width
    args = [x, a] + b_args + [wa, wb, vec(gx), wq, kv, wo]
    in_specs = [rows(x), rows(a)] + b_specs + [
        const(wa), const(wb), const(vec(gx)), const(wq),
        pl.BlockSpec((mem_len, 2 * n_heads * dh), lambda i: (i // per_b, kv_col)), const(wo)]
    if ffn is not None:
        gf, w13, w2 = ffn
        args += [vec(gf), w13, w2]
        in_specs += [const(vec(gf)), const(w13), const(w2)]
        scratch += _swiglu_scratch(tm, d, w2.shape[1])
    return pl.pallas_call(
        functools.partial(_tail_kernel, n_heads=n_heads, dh=dh, gated=gated, ffn=ffn is not None), grid=(t // tm,),
        in_specs=in_specs, out_specs=rows(x), out_shape=jax.ShapeDtypeStruct((t, d), F32),
        scratch_shapes=scratch, compiler_params=_params("parallel"), name=name,
    )(*args)


def _conv_kernel(bg_ref, cg_ref, xi_ref, w_ref, o_ref):
    s, width = o_ref.shape
    n_taps = w_ref.shape[0]
    xc = cg_ref[...].astype(F32) * xi_ref[...].astype(F32)
    row = lax.broadcasted_iota(jnp.int32, (s, width), 0)
    y = w_ref[n_taps - 1:n_taps, :] * xc
    for back in range(1, n_taps):
        shifted = jnp.where(row >= back, pltpu.roll(xc, back, axis=0), 0.0)
        y = y + w_ref[n_taps - 1 - back:n_taps - back, :] * shifted
    o_ref[...] = (bg_ref[...].astype(F32) * y).astype(o_ref.dtype)


def short_conv(main, conv_w, *, batch, seq):
    width = conv_w.shape[1]
    return pl.pallas_call(
        _conv_kernel, grid=(batch,),
        in_specs=[
            pl.BlockSpec((seq, width), lambda b: (b, 0)),
            pl.BlockSpec((seq, width), lambda b: (b, 1)),
            pl.BlockSpec((seq, width), lambda b: (b, 2)),
            pl.BlockSpec(conv_w.shape, lambda b: (0, 0)),
        ],
        out_specs=pl.BlockSpec((seq, width), lambda b: (b, 0)),
        out_shape=jax.ShapeDtypeStruct((batch * seq, width), BF16),
        compiler_params=_params("parallel"), name="short_conv",
    )(main, main, main, conv_w)


def _diff_prep_kernel(rb_ref, lq1_ref, lk1_ref, lq2_ref, lk2_ref, bias_ref, far_ref, lam_ref, *, tq, lam_init):
    n_heads = bias_ref.shape[0]
    strip = 32
    row = lax.broadcasted_iota(jnp.int32, (strip, tq), 0)
    col = lax.broadcasted_iota(jnp.int32, (strip, tq), 1)
    max_exact = N_BUCKETS // 2

    def fill(r, carry):
        r0 = pl.multiple_of(r * strip, strip)
        for which in range(2):
            n = jnp.maximum(row + r0 - col + which * tq, 0)
            nf = jnp.maximum(n, 1).astype(F32)
            large = max_exact + (jnp.log(nf / max_exact) / math.log(MAX_DIST / max_exact)
                                 * (N_BUCKETS - max_exact)).astype(jnp.int32)
            large = jnp.minimum(large, N_BUCKETS - 1)
            bucket = jnp.where(n < max_exact, n, large)
            for h in range(n_heads):
                b = jnp.zeros((strip, tq), F32)
                for kk in range(N_BUCKETS):
                    b = jnp.where(bucket == kk, rb_ref[kk, h], b)
                bias_ref[h, which, pl.ds(r0, strip), :] = (b - rb_ref[N_BUCKETS - 1, h]) * LOG2E
        return carry

    lax.fori_loop(0, tq // strip, fill, 0)
    for h in range(n_heads):
        far_ref[h] = jnp.full((1, LANES), rb_ref[N_BUCKETS - 1, h], F32) * LOG2E
    lam = (jnp.exp(jnp.sum(lq1_ref[...] * lk1_ref[...], axis=1, keepdims=True))
           - jnp.exp(jnp.sum(lq2_ref[...] * lk2_ref[...], axis=1, keepdims=True)) + lam_init)
    lam_ref[...] = jnp.broadcast_to(lam, (1, LANES))


def diff_prep(rel_bias, lq1, lk1, lq2, lk2, *, tq, lam_init):
    n_heads = rel_bias.shape[1]
    vec = lambda a: a.reshape(1, -1)
    vspec = pl.BlockSpec(memory_space=pltpu.VMEM)
    return pl.pallas_call(
        functools.partial(_diff_prep_kernel, tq=tq, lam_init=lam_init),
        in_specs=[pl.BlockSpec(memory_space=pltpu.SMEM), vspec, vspec, vspec, vspec],
        out_specs=[vspec, vspec, vspec],
        out_shape=[
            jax.ShapeDtypeStruct((n_heads, 2, tq, tq), F32),
            jax.ShapeDtypeStruct((n_heads, 1, LANES), F32),
            jax.ShapeDtypeStruct((1, LANES), F32),
        ],
        compiler_params=pltpu.CompilerParams(vmem_limit_bytes=VMEM_LIMIT), name="diff_prep",
    )(rel_bias, vec(lq1), vec(lk1), vec(lq2), vec(lk2))


def _diff_kernel(q_ref, k_ref, v_ref, bias_ref, far_ref, lam_ref, subln_ref, o_ref, ka_s, m_s, l_s, acc_s, *,
                 tq, lam_init):
    i = pl.program_id(2)
    lane = lax.broadcasted_iota(jnp.int32, (1, LANES), 1)
    halves = _halves(lane)

    @pl.when(i == 0)
    def _():
        k = k_ref[...]
        far = _split3(far_ref[0])
        for sub in range(2):
            ka_s[sub] = _augment(k, halves[sub], lane, HEAD_DIM * (1 - sub), far, True)

    q = q_ref[...]
    zero = jnp.zeros((1, LANES), F32)
    qa = [_augment(q, halves[sub], lane, HEAD_DIM * (1 - sub), (zero,) * 3, False) for sub in range(2)]
    _causal_attention(qa, ka_s, v_ref, bias_ref, i, tq, m_s, l_s, acc_s)
    o = _normalised(l_s, acc_s, 0) - lam_ref[...] * _normalised(l_s, acc_s, 1)
    o_ref[...] = (_rms(o, subln_ref[...]) * (1.0 - lam_init)).astype(o_ref.dtype)


def diff_attention(main, bias, far, lam, subln, *, batch, seq, tq, q_col, k_col, v_col, lam_init):
    t = batch * seq
    nq = seq // tq
    n_heads = bias.shape[0]
    return pl.pallas_call(
        functools.partial(_diff_kernel, tq=tq, lam_init=lam_init),
        grid=(batch, n_heads, nq),
        in_specs=[
            pl.BlockSpec((tq, LANES), lambda b, h, i: (b * nq + i, q_col + h)),
            pl.BlockSpec((seq, LANES), lambda b, h, i: (b, k_col + h)),
            pl.BlockSpec((seq, LANES), lambda b, h, i: (b, v_col + h)),
            pl.BlockSpec((1, 2, tq, tq), lambda b, h, i: (h, 0, 0, 0)),
            pl.BlockSpec((1, 1, LANES), lambda b, h, i: (h, 0, 0)),
            pl.BlockSpec((1, LANES), lambda b, h, i: (0, 0)),
            pl.BlockSpec((1, LANES), lambda b, h, i: (0, 0)),
        ],
        out_specs=pl.BlockSpec((tq, LANES), lambda b, h, i: (b * nq + i, h)),
        out_shape=jax.ShapeDtypeStruct((t, n_heads * LANES), BF16),
        scratch_shapes=_attn_scratch(seq, tq),
        compiler_params=_params("parallel", "parallel", "arbitrary"), name="diff_attention",
    )(main, main, main, bias, far, lam, subln.reshape(1, LANES))


def _router_kernel(x_ref, g_ref, wr_ref, meta_ref, wts_ref, before_ref, cnt_ref, carry_s, *, n_exp):
    tm = x_ref.shape[0]

    @pl.when(pl.program_id(0) == 0)
    def _():
        carry_s[...] = jnp.zeros_like(carry_s)

    before_ref[...] = jnp.broadcast_to(carry_s[...], before_ref.shape)

    h = _rms(x_ref[...], g_ref[...])
    h_hi = h.astype(BF16)
    h_lo = (h - h_hi.astype(F32)).astype(BF16)
    logits = _dot(h_hi, wr_ref[0]) + (_dot(h_hi, wr_ref[1]) + _dot(h_lo, wr_ref[0]))
    lane = lax.broadcasted_iota(jnp.int32, (tm, LANES), 1)
    lane_f = lane.astype(F32)
    logits = jnp.where(lane < n_exp, logits, -jnp.inf)
    m1 = jnp.max(logits, axis=1, keepdims=True)
    i1 = jnp.min(jnp.where(logits == m1, lane_f, float(LANES)), axis=1, keepdims=True)
    rest = jnp.where(lane_f == i1, -jnp.inf, logits)
    m2 = jnp.max(rest, axis=1, keepdims=True)
    i2 = jnp.min(jnp.where(rest == m2, lane_f, float(LANES)), axis=1, keepdims=True)
    e = jnp.exp(m2 - m1)
    w1 = 1.0 / (1.0 + e)
    w2 = e / (1.0 + e)
    sel1 = lane_f == i1
    sel2 = lane_f == i2
    onehot = jnp.where(sel1 | sel2, 1.0, 0.0)
    row = lax.broadcasted_iota(jnp.int32, (tm, tm), 0)
    col = lax.broadcasted_iota(jnp.int32, (tm, tm), 1)
    before = jnp.where(row > col, 1.0, 0.0).astype(BF16)
    local = _dot(before, onehot.astype(BF16))
    rank = local + carry_s[...]
    pick = lambda sel, val: jnp.sum(jnp.where(sel, val, 0.0), axis=1, keepdims=True)
    fields = [i1, i2, pick(sel1, rank), pick(sel2, rank), pick(sel1, local), pick(sel2, local)]
    meta = jnp.zeros((tm, LANES), F32)
    for idx, field in enumerate(fields):
        meta = jnp.where(lane == idx, field, meta)
    meta_ref[...] = meta.astype(jnp.int32)
    wts_ref[...] = jnp.where(lane == 0, w1, jnp.where(lane == 1, w2, 0.0))
    carry_s[...] += jnp.sum(onehot, axis=0, keepdims=True)
    cnt_ref[...] = carry_s[...]


def route_tokens(x, g, wr, *, tm, n_exp):
    t, d = x.shape
    return pl.pallas_call(
        functools.partial(_router_kernel, n_exp=n_exp),
        grid=(t // tm,),
        in_specs=[
            pl.BlockSpec((tm, d), lambda i: (i, 0)),
            pl.BlockSpec((1, d), lambda i: (0, 0)),
            pl.BlockSpec((2, d, LANES), lambda i: (0, 0, 0)),
        ],
        out_specs=[
            pl.BlockSpec((tm, LANES), lambda i: (i, 0)),
            pl.BlockSpec((tm, LANES), lambda i: (i, 0)),
            pl.BlockSpec((SUBLANES, LANES), lambda i: (i, 0)),
            pl.BlockSpec((1, LANES), lambda i: (0, 0)),
        ],
        out_shape=[
            jax.ShapeDtypeStruct((t, LANES), jnp.int32),
            jax.ShapeDtypeStruct((t, LANES), F32),
            jax.ShapeDtypeStruct((t // tm * SUBLANES, LANES), F32),
            jax.ShapeDtypeStruct((1, LANES), F32),
        ],
        scratch_shapes=[pltpu.VMEM((1, LANES), F32)],
        compiler_params=_params("arbitrary"), name="moe_router",
    )(x, g.reshape(1, d), wr)


def _dispatch_kernel(start_ref, shift_ref, keep_ref, x_ref, g_ref, meta_ref, zeros_hbm, xs_hbm,
                     stage_s, carry_s, sems, *, tm, n_exp):
    del zeros_hbm
    j = pl.program_id(0)
    slot = j % 2
    rows = tm + BF16_ROWS

    def block_copy(step, e, buf):
        first = pl.multiple_of(start_ref[step * n_exp + e], BF16_ROWS)
        return pltpu.make_async_copy(stage_s.at[buf, e], xs_hbm.at[pl.ds(first, rows)], sems.at[buf])

    @pl.when(j == 0)
    def _():
        carry_s[...] = jnp.zeros_like(carry_s)

    h = _rms(x_ref[...], g_ref[...]).astype(BF16)
    fields = meta_ref[...].astype(F32).T
    slot_row = lax.broadcasted_iota(jnp.int32, (rows, tm), 0).astype(F32)
    for e in range(n_exp):
        key = j * n_exp + e
        idx = jnp.where(fields[0:1] == e, fields[4:5], jnp.where(fields[1:2] == e, fields[5:6], -2.0 * rows))
        idx = idx + shift_ref[key].astype(F32)
        onehot = jnp.where(slot_row == idx, 1.0, 0.0).astype(BF16)
        stage_s[slot, e] = _dot(onehot, h).astype(BF16)
        stage_s[slot, e, 0:BF16_ROWS, :] += carry_s[e]
        keep = pl.multiple_of(keep_ref[key], BF16_ROWS)
        carry_s[e] = stage_s[slot, e, pl.ds(keep, BF16_ROWS), :]

    @pl.when(j > 0)
    def _():
        for e in range(n_exp):
            block_copy(j - 1, e, 1 - slot).wait()

    for e in range(n_exp):
        block_copy(j, e, slot).start()

    @pl.when(j == pl.num_programs(0) - 1)
    def _():
        for e in range(n_exp):
            block_copy(j, e, slot).wait()


def moe_dispatch(starts, shifts, keeps, x, g, meta, n_rows, *, tm, n_exp):
    t, d = x.shape
    zeros = jnp.zeros((n_rows, d), BF16)
    rows = tm + BF16_ROWS
    return pl.pallas_call(
        functools.partial(_dispatch_kernel, tm=tm, n_exp=n_exp),
        grid_spec=pltpu.PrefetchScalarGridSpec(
            num_scalar_prefetch=3, grid=(t // tm,),
            in_specs=[
                pl.BlockSpec((tm, d), lambda i, *_: (i, 0)),
                pl.BlockSpec((1, d), lambda i, *_: (0, 0)),
                pl.BlockSpec((tm, LANES), lambda i, *_: (i, 0)),
                pl.BlockSpec(memory_space=pl.ANY),
            ],
            out_specs=pl.BlockSpec(memory_space=pl.ANY),
            scratch_shapes=[pltpu.VMEM((2, n_exp, rows, d), BF16), pltpu.VMEM((n_exp, BF16_ROWS, d), BF16),
                            pltpu.SemaphoreType.DMA((2,))],
        ),
        out_shape=jax.ShapeDtypeStruct((n_rows, d), BF16),
        input_output_aliases={6: 0},
        compiler_params=_params("arbitrary"),
        name="moe_dispatch",
    )(starts, shifts, keeps, x, g.reshape(1, d), meta, zeros)


def _expert_kernel(te_ref, used_ref, h_ref, w13_ref, w2_ref, o_ref, gu_s, acc_s):
    del te_ref
    used = used_ref[pl.program_id(0)] != 0

    @pl.when(used)
    def _():
        acc_s[...] = jnp.zeros_like(acc_s)
        _swiglu_chunks(h_ref, w13_ref.at[0], w2_ref.at[0], acc_s, gu_s)
        o_ref[...] = acc_s[...].astype(o_ref.dtype)

    @pl.when(jnp.logical_not(used))
    def _():
        o_ref[...] = jnp.zeros_like(o_ref)


def moe_experts(tile_expert, tile_used, hs, w13, w2, *, tm):
    n_rows, d = hs.shape
    tf = w2.shape[2]
    resident = pl.Buffered(1)
    return pl.pallas_call(
        _expert_kernel,
        grid_spec=pltpu.PrefetchScalarGridSpec(
            num_scalar_prefetch=2, grid=(n_rows // tm,),
            in_specs=[
                pl.BlockSpec((tm, d), lambda i, te, tu: (i, 0)),
                pl.BlockSpec((1,) + w13.shape[1:], lambda i, te, tu: (te[i], 0, 0)),
                pl.BlockSpec((1,) + w2.shape[1:], lambda i, te, tu: (te[i], 0, 0, 0)),
            ],
            out_specs=pl.BlockSpec((tm, d), lambda i, te, tu: (i, 0)),
            scratch_shapes=[pltpu.VMEM((2, 2, tm, tf), F32), pltpu.VMEM((tm, d), F32)],
        ),
        out_shape=jax.ShapeDtypeStruct((n_rows, d), BF16),
        compiler_params=pltpu.CompilerParams(dimension_semantics=("arbitrary",), vmem_limit_bytes=62 * 1024 * 1024),
        name="moe_experts",
    )(tile_expert, tile_used, hs, w13, w2)


def _combine_kernel(start_ref, shift_ref, tail_ref, x_ref, wts_ref, meta_ref, g_ref, y_hbm, o_ref, blk_s, sems, *,
                    tm, n_exp):
    j = pl.program_id(0)
    slot = j % 2
    rows = tm + BF16_ROWS

    def block_copy(step, e, buf):
        first = pl.multiple_of(start_ref[step * n_exp + e], BF16_ROWS)
        return pltpu.make_async_copy(y_hbm.at[pl.ds(first, rows)], blk_s.at[buf, e], sems.at[buf])

    def fetch(step, buf):
        for e in range(n_exp):
            block_copy(step, e, buf).start()

    @pl.when(j == 0)
    def _():
        fetch(0, 0)

    @pl.when(j + 1 < pl.num_programs(0))
    def _():
        fetch(j + 1, 1 - slot)

    for e in range(n_exp):
        block_copy(j, e, slot).wait()

    meta = meta_ref[...]
    wts = wts_ref[...]
    col = lax.broadcasted_iota(jnp.int32, (tm, tm), 1)
    tail_col = lax.broadcasted_iota(jnp.int32, (tm, BF16_ROWS), 1) + tm

    def pick(e):
        sel = [meta[:, k:k + 1] == e for k in range(TOP_K)]
        idx = jnp.where(sel[0], meta[:, 4:5], jnp.where(sel[1], meta[:, 5:6], -2 * rows)) + shift_ref[j * n_exp + e]
        w = jnp.where(sel[0], wts[:, 0:1], jnp.where(sel[1], wts[:, 1:2], 0.0))
        return idx, w

    acc = x_ref[...]
    for e in range(n_exp):
        idx, w = pick(e)
        onehot = jnp.where(col == idx, 1.0, 0.0).astype(BF16)
        acc = acc + w * _dot(onehot, blk_s[slot, e, 0:tm, :])
    o_ref[...] = acc

    for e in range(n_exp):
        @pl.when(tail_ref[j * n_exp + e] != 0)
        def _(e=e):
            idx, w = pick(e)
            onehot_tail = jnp.where(tail_col == idx, 1.0, 0.0).astype(BF16)
            o_ref[...] += w * _dot(onehot_tail, blk_s[slot, e, tm:rows, :])

    o_ref[...] = _rms(o_ref[...], g_ref[...])


def moe_combine(starts, shifts, tails, x, wts, meta, g, y, *, tm, n_exp):
    t, d = x.shape
    n_pre = 3
    return pl.pallas_call(
        functools.partial(_combine_kernel, tm=tm, n_exp=n_exp),
        grid_spec=pltpu.PrefetchScalarGridSpec(
            num_scalar_prefetch=n_pre, grid=(t // tm,),
            in_specs=[
                pl.BlockSpec((tm, d), lambda i, *_: (i, 0)),
                pl.BlockSpec((tm, LANES), lambda i, *_: (i, 0)),
                pl.BlockSpec((tm, LANES), lambda i, *_: (i, 0)),
                pl.BlockSpec((1, d), lambda i, *_: (0, 0)),
                pl.BlockSpec(memory_space=pl.ANY),
            ],
            out_specs=pl.BlockSpec((tm, d), lambda i, *_: (i, 0)),
            scratch_shapes=[pltpu.VMEM((2, n_exp, tm + BF16_ROWS, d), BF16), pltpu.SemaphoreType.DMA((2,))],
        ),
        out_shape=jax.ShapeDtypeStruct((t, d), F32),
        compiler_params=_params("arbitrary"), name="moe_combine",
    )(starts, shifts, tails, x, wts, meta, g.reshape(1, d), y)


def _pad_cols(w, n):
    return jnp.pad(w, ((0, 0), (0, n - w.shape[1])))


def _even_mixer(x, norm, w_in, b_f, sgu_norm, w_s, b_s, *, batch, seq, tq):
    n_heads = b_f.shape[0]
    a_width = n_heads * HEAD_DIM
    b_width = w_s.shape[0] * LANES
    f0 = 3 * a_width
    w_main = jnp.concatenate([w_in[:, :f0], w_in[:, f0 + n_heads:]], axis=1)
    w_gate = _pad_cols(w_in[:, f0:f0 + n_heads], LANES)
    main, gate = norm_matmul(x, norm, [w_main.astype(BF16), w_gate.astype(BF16)], [BF16, F32],
                             tm=512, name="even_in_proj", scaled=(0, a_width, LOG2E * HEAD_DIM ** -0.5))
    c = gate_cumsum(gate, _pad_cols(b_f.reshape(1, -1), LANES), seq=seq)
    n_pairs = a_width // LANES
    a = fox_attention(main, c, batch=batch, seq=seq, tq=tq, q_col=0, k_col=n_pairs, v_col=2 * n_pairs,
                      n_pairs=n_pairs)
    u_col = f0 // b_width
    return a, (main, u_col, u_col + 1, sgu_norm, w_s, b_s)


def _odd_mixer(x, norm, w_in, conv_w, lq1, lk1, lq2, lk2, subln, rel_bias, lam_init, *, batch, seq, tq):
    c_width = conv_w.shape[1]
    d_width = rel_bias.shape[1] * 2 * HEAD_DIM
    q0 = 3 * c_width
    (main,) = norm_matmul(x, norm, [w_in.astype(BF16)], [BF16], tm=512, name="odd_in_proj",
                          scaled=(q0, q0 + d_width, LOG2E * HEAD_DIM ** -0.5))
    c_out = short_conv(main, conv_w, batch=batch, seq=seq)
    bias, far, lam = diff_prep(rel_bias, lq1, lk1, lq2, lk2, tq=tq, lam_init=lam_init)
    q_col = q0 // LANES
    n_heads = rel_bias.shape[1]
    d_out = diff_attention(main, bias, far, lam, subln, batch=batch, seq=seq, tq=tq, q_col=q_col,
                           k_col=q_col + n_heads, v_col=q_col + 2 * n_heads, lam_init=lam_init)
    return c_out, d_out


def _moe_layer(x, norm, w_router, w13, w2, final_norm, *, tm_expert):
    t, d = x.shape
    n_exp = w_router.shape[1]
    tm_route = 256
    wr = _pad_cols(w_router, LANES)
    wr_hi = wr.astype(BF16)
    wr_split = jnp.stack([wr_hi, (wr - wr_hi.astype(F32)).astype(BF16)])
    meta, wts, before, counts = route_tokens(x, norm, wr_split, tm=tm_route, n_exp=n_exp)
    counts = counts[0, :n_exp].astype(jnp.int32)
    block_rows = tm_route + BF16_ROWS
    padded = (counts + block_rows + tm_expert - 1) // tm_expert * tm_expert
    ends = jnp.cumsum(padded)
    offsets = ends - padded
    n_rows = TOP_K * t + n_exp * (tm_expert + pl.cdiv(block_rows, tm_expert) * tm_expert)
    tile_start = jnp.arange(n_rows // tm_expert, dtype=jnp.int32) * tm_expert
    tile_expert = jnp.minimum(jnp.sum(tile_start[:, None] >= ends[None, :], axis=1), n_exp - 1).astype(jnp.int32)
    tile_used = (tile_start < (offsets + counts)[tile_expert]).astype(jnp.int32)
    before = before[::SUBLANES, :n_exp].astype(jnp.int32)
    in_tile = jnp.concatenate([before[1:], counts[None]]) - before
    first = offsets[None, :] + before
    starts = first // BF16_ROWS * BF16_ROWS
    shifts = first - starts
    keeps = (shifts + in_tile) // BF16_ROWS * BF16_ROWS
    tails = (shifts + in_tile > tm_route).astype(jnp.int32)
    flat = lambda a: a.reshape(-1).astype(jnp.int32)
    hs = moe_dispatch(flat(starts), flat(shifts), flat(keeps), x, norm, meta, n_rows, tm=tm_route, n_exp=n_exp)
    y = moe_experts(tile_expert, tile_used, hs, *_chunk_weights(w13, w2, 512), tm=tm_expert)
    return moe_combine(flat(starts), flat(shifts), flat(tails), x, wts, meta, final_norm, y, tm=tm_route,
                       n_exp=n_exp)


def kernel(x, mem, rel_bias, mem_norm, final_norm, ev_norm, ev_w_in, ev_b_f, ev_sgu_norm, ev_w_s, ev_b_s, ev_w_out, ffn_w13, ffn_w2, od_norm, od_w_in, od_conv_w, od_lam_q1, od_lam_k1, od_lam_q2, od_lam_k2, od_subln, od_w_out, moe_router, moe_w13, moe_w2, x_norm, x_wq, x_wkv, x_wo, ffn_norm):
    batch, seq, d = x.shape
    mem_len = mem.shape[1]
    depth = x_norm.shape[0]
    assert depth == 2 and ev_norm.shape[0] == 1 and od_norm.shape[0] == 1
    x_heads, x_dh = 4, 128
    xf = x.reshape(batch * seq, d)
    wkv = jnp.concatenate([x_wkv[layer] for layer in range(depth)], axis=1).astype(BF16)
    (kv,) = norm_matmul(mem.reshape(batch * mem_len, d), mem_norm, [wkv], [BF16], tm=512, name="mem_kv")

    def tail(xf, mixed, w_out, layer, ffn, name):
        return layer_tail(xf, *mixed, w_out.astype(BF16), x_norm[layer], x_wq[layer].astype(BF16), kv,
                          x_wo[layer].astype(BF16), ffn, tm=512, seq=seq, mem_len=mem_len, kv_col=layer,
                          n_heads=x_heads, dh=x_dh, name=name)

    mixed = _even_mixer(xf, ev_norm[0], ev_w_in[0], ev_b_f[0], ev_sgu_norm[0], ev_w_s[0], ev_b_s[0],
                        batch=batch, seq=seq, tq=ATTN_TILE)
    xf = tail(xf, mixed, ev_w_out[0], 0, (ffn_norm[0],) + _chunk_weights(ffn_w13[0], ffn_w2[0], 256),
              "even_tail")
    lam_init = 0.8 - 0.6 * math.exp(-0.3 * 1)
    mixed = _odd_mixer(xf, od_norm[0], od_w_in[0], od_conv_w[0], od_lam_q1[0], od_lam_k1[0], od_lam_q2[0],
                       od_lam_k2[0], od_subln[0], rel_bias, lam_init, batch=batch, seq=seq, tq=ATTN_TILE)
    xf = tail(xf, mixed, od_w_out[0], 1, None, "odd_tail")
    out = _moe_layer(xf, ffn_norm[1], moe_router[0], moe_w13[0], moe_w2[0], final_norm, tm_expert=512)
    return out.reshape(batch, seq, d)
```

```python
import functools
import math

import jax
import jax.numpy as jnp
from jax import lax
from jax.experimental import pallas as pl
from jax.experimental.pallas import tpu as pltpu

F32 = jnp.float32
BF16 = jnp.bfloat16
EPS = 1e-6
HEAD_DIM = 64
LANES = 128
SUBLANES = 8
BF16_ROWS = 16
N_BUCKETS = 32
MAX_DIST = 128
TOP_K = 2
LOG2E = 1.4426950408889634
ATTN_TILE = 512
VMEM_LIMIT = 56 * 1024 * 1024


def _params(*sem):
    return pltpu.CompilerParams(dimension_semantics=sem, vmem_limit_bytes=VMEM_LIMIT)


def _rms(x, g):
    ms = jnp.mean(x * x, axis=-1, keepdims=True)
    return x * lax.rsqrt(ms + EPS) * g


def _dot(a, b):
    return jnp.dot(a, b, preferred_element_type=F32)


def _dot_nt(a, b):
    return lax.dot_general(a, b, (((1,), (1,)), ((), ())), preferred_element_type=F32)


def _norm_matmul_kernel(x_ref, g_ref, *refs, n_w, chunk, scaled):
    w_refs, o_refs = refs[:n_w], refs[n_w:]
    h = _rms(x_ref[...], g_ref[...]).astype(BF16)
    s0, s1, scale = scaled
    for k, (w_ref, o_ref) in enumerate(zip(w_refs, o_refs)):
        n = w_ref.shape[1]
        for c0 in range(0, n, chunk):
            c1 = min(c0 + chunk, n)
            y = _dot(h, w_ref[:, c0:c1])
            if k == 0 and s0 <= c0 and c1 <= s1:
                y = y * scale
            o_ref[:, c0:c1] = y.astype(o_ref.dtype)


def norm_matmul(x, g, ws, out_dtypes, *, tm, name, scaled=(0, 0, 1.0)):
    t, d = x.shape
    chunk = 512
    assert scaled[0] % chunk == 0 and scaled[1] % chunk == 0
    in_specs = [pl.BlockSpec((tm, d), lambda i: (i, 0)), pl.BlockSpec((1, d), lambda i: (0, 0))]
    in_specs += [pl.BlockSpec(w.shape, lambda i: (0, 0)) for w in ws]
    out_specs = [pl.BlockSpec((tm, w.shape[1]), lambda i: (i, 0)) for w in ws]
    out_shape = [jax.ShapeDtypeStruct((t, w.shape[1]), dt) for w, dt in zip(ws, out_dtypes)]
    return pl.pallas_call(
        functools.partial(_norm_matmul_kernel, n_w=len(ws), chunk=chunk, scaled=scaled),
        grid=(t // tm,), in_specs=in_specs, out_specs=out_specs, out_shape=out_shape,
        compiler_params=_params("parallel"), name=name,
    )(x, g.reshape(1, d), *ws)


def _gate_kernel(g_ref, b_ref, c_ref):
    s = g_ref.shape[0]
    row = lax.broadcasted_iota(jnp.int32, (LANES, LANES), 0)
    col = lax.broadcasted_iota(jnp.int32, (LANES, LANES), 1)
    tri = (row >= col).astype(F32)
    carry = jnp.zeros((1, LANES), F32)
    for blk in range(s // LANES):
        z = g_ref[blk * LANES:(blk + 1) * LANES, :] + b_ref[...]
        log_f = jnp.minimum(z, 0.0) - jnp.log1p(jnp.exp(-jnp.abs(z)))
        cs = jnp.dot(tri, log_f, precision=lax.Precision.HIGHEST, preferred_element_type=F32) + carry
        c_ref[blk * LANES:(blk + 1) * LANES, :] = cs
        carry = cs[LANES - 1:LANES, :]


def gate_cumsum(g, b, *, seq):
    t = g.shape[0]
    return pl.pallas_call(
        _gate_kernel, grid=(t // seq,),
        in_specs=[pl.BlockSpec((seq, LANES), lambda i: (i, 0)), pl.BlockSpec((1, LANES), lambda i: (0, 0))],
        out_specs=pl.BlockSpec((seq, LANES), lambda i: (i, 0)),
        out_shape=jax.ShapeDtypeStruct((t, LANES), F32),
        compiler_params=_params("parallel"), name="gate_cumsum",
    )(g, b)


def _split3(x):
    hi = x.astype(BF16).astype(F32)
    rest = x - hi
    mid = rest.astype(BF16).astype(F32)
    lo = (rest - mid).astype(BF16).astype(F32)
    return hi, mid, lo


def _augment(x, in_half, lane, base, pieces, pieces_first):
    n = len(pieces)
    p0, o0 = (base, base + n) if pieces_first else (base + n, base)
    aug = jnp.where((lane >= o0) & (lane < o0 + n), 1.0, 0.0)
    for idx, piece in enumerate(pieces):
        aug = jnp.where(lane == p0 + idx, piece, aug)
    return jnp.where(in_half, x, aug.astype(x.dtype))


def _halves(lane):
    return [(lane >= HEAD_DIM * hh) & (lane < HEAD_DIM * (hh + 1)) for hh in range(2)]


def _causal_attention(qa, ka_s, v_ref, bias_ref, i, tq, m_s, l_s, acc_s):
    n_chunks = tq // LANES

    def causal():
        row = lax.broadcasted_iota(jnp.int32, (tq, tq), 0)
        col = lax.broadcasted_iota(jnp.int32, (tq, tq), 1)
        return row >= col

    def scores(h, j, near, mask):
        start = pl.multiple_of(j * tq, tq)
        s = _dot_nt(qa[h], ka_s[h, pl.ds(start, tq), :])
        if near is not None:
            s = s + bias_ref[0, near]
        if mask is not None:
            s = jnp.where(mask, s, -jnp.inf)
        return [s[:, c * LANES:(c + 1) * LANES] for c in range(n_chunks)]

    def max_tile(j, near, mask):
        for h in range(2):
            m = m_s[h]
            for chunk in scores(h, j, near, mask):
                m = jnp.maximum(m, chunk)
            m_s[h] = m

    def sum_tile(j, near, mask):
        v = v_ref[pl.ds(pl.multiple_of(j * tq, tq), tq), :]
        for h in range(2):
            m = m_s[h]
            ps = [jnp.exp2(chunk - m) for chunk in scores(h, j, near, mask)]
            l_s[h] += functools.reduce(lambda a, b: a + b, ps)
            acc_s[h] += _dot(jnp.concatenate(ps, axis=1).astype(BF16), v)

    def key_tiles(fn):
        def run(lo, hi, near):
            def body(j, carry):
                fn(j, near, None)
                return carry
            lax.fori_loop(lo, hi, body, 0)

        if bias_ref is None:
            run(0, i, None)
            fn(i, None, causal())
        else:
            n_far = jnp.maximum(i - 1, 0)
            run(0, n_far, None)
            run(n_far, i, 1)
            fn(i, 0, causal())

    m_s[...] = jnp.full(m_s.shape, -jnp.inf, F32)
    key_tiles(max_tile)
    for h in range(2):
        m_s[h] = jnp.broadcast_to(jnp.max(m_s[h], axis=1, keepdims=True), (tq, LANES))
    l_s[...] = jnp.zeros(l_s.shape, F32)
    acc_s[...] = jnp.zeros(acc_s.shape, F32)
    key_tiles(sum_tile)


def _normalised(l_s, acc_s, h):
    return acc_s[h] / jnp.sum(l_s[h], axis=1, keepdims=True)


def _rider(w, grid):
    steps = math.prod(grid)
    rows = w.shape[0] // steps
    assert rows * steps == w.shape[0] and rows % BF16_ROWS == 0

    def index(*ids):
        step = 0
        for n, idx in zip(grid, ids):
            step = step * n + idx
        return step, 0

    return pl.BlockSpec((rows, w.shape[1]), index), jax.ShapeDtypeStruct(w.shape, BF16)


def _fox_kernel(q_ref, k_ref, v_ref, c_ref, w_ref, o_ref, wcast_ref, ka_s, m_s, l_s, acc_s, *, tq):
    wcast_ref[...] = w_ref[...].astype(BF16)
    hp = pl.program_id(1)
    i = pl.program_id(2)
    lane = lax.broadcasted_iota(jnp.int32, (1, LANES), 1)
    halves = _halves(lane)

    def decay(c, hh):
        return jnp.sum(jnp.where(lane == 2 * hp + hh, c, 0.0), axis=1, keepdims=True) * LOG2E

    @pl.when(i == 0)
    def _():
        k = k_ref[...]
        c_all = c_ref[...]
        for hh in range(2):
            ka_s[hh] = _augment(k, halves[hh], lane, HEAD_DIM * (1 - hh), _split3(-decay(c_all, hh)), True)

    q = q_ref[...]
    c_q = c_ref[pl.ds(pl.multiple_of(i * tq, tq), tq), :]
    qa = [_augment(q, halves[hh], lane, HEAD_DIM * (1 - hh), _split3(decay(c_q, hh)), False) for hh in range(2)]
    _causal_attention(qa, ka_s, v_ref, None, i, tq, m_s, l_s, acc_s)
    out = jnp.where(lane < HEAD_DIM, _normalised(l_s, acc_s, 0), _normalised(l_s, acc_s, 1))
    o_ref[...] = out.astype(o_ref.dtype)


def _attn_scratch(seq, tq):
    return [pltpu.VMEM((2, seq, LANES), BF16)] + [pltpu.VMEM((2, tq, LANES), F32)] * 3


def fox_attention(qkv, c, w_ride, *, batch, seq, tq, q_col, k_col, v_col, n_pairs):
    t = batch * seq
    nq = seq // tq
    grid = (batch, n_pairs, nq)
    ride_spec, ride_shape = _rider(w_ride, grid)
    return pl.pallas_call(
        functools.partial(_fox_kernel, tq=tq),
        grid=grid,
        in_specs=[
            pl.BlockSpec((tq, LANES), lambda b, h, i: (b * nq + i, q_col + h)),
            pl.BlockSpec((seq, LANES), lambda b, h, i: (b, k_col + h)),
            pl.BlockSpec((seq, LANES), lambda b, h, i: (b, v_col + h)),
            pl.BlockSpec((seq, LANES), lambda b, h, i: (b, 0)),
            ride_spec,
        ],
        out_specs=[pl.BlockSpec((tq, LANES), lambda b, h, i: (b * nq + i, h)), ride_spec],
        out_shape=[jax.ShapeDtypeStruct((t, n_pairs * LANES), BF16), ride_shape],
        scratch_shapes=_attn_scratch(seq, tq),
        compiler_params=_params("parallel", "parallel", "arbitrary"), name="fox_attention",
    )(qkv, qkv, qkv, c, w_ride)


def _spatial_gate(u_ref, v_ref, norm_ref, ws_ref, bs_ref, o_ref):
    tb = u_ref.shape[0]
    n_groups, chunk, _ = ws_ref.shape
    row = lax.broadcasted_iota(jnp.int32, (chunk, chunk), 0)
    col = lax.broadcasted_iota(jnp.int32, (chunk, chunk), 1)
    tri = row >= col
    for g in range(n_groups):
        w = jnp.where(tri, ws_ref[g], 0.0).astype(BF16)
        bias = bs_ref[:, g:g + 1]
        gain = norm_ref[g:g + 1, :]
        for c in range(tb // chunk):
            rs = slice(c * chunk, (c + 1) * chunk)
            cs = slice(g * LANES, (g + 1) * LANES)
            vn = _rms(jax.nn.gelu(v_ref[rs, cs].astype(F32)), gain)
            mixed = _dot(w, vn.astype(BF16)) + bias
            o_ref[rs, cs] = (jax.nn.gelu(u_ref[rs, cs].astype(F32)) * mixed).astype(o_ref.dtype)


def _chunk_weights(w13, w2, tf):
    *lead, ff, d = w2.shape
    return w13.astype(BF16), w2.astype(BF16).reshape(*lead, ff // tf, tf, d)


def _swiglu_chunks(h_s, w13, w2, o_ref, gu_s):
    n, tf, _ = w2.shape

    def project(c, slot):
        for part in range(2):
            cols = pl.ds(pl.multiple_of((part * n + c) * tf, tf), tf)
            gu_s[slot, part] = _dot(h_s[...], w13[:, cols])

    def consume(c, slot):
        gate = gu_s[slot, 0]
        act = (gate * jax.nn.sigmoid(gate) * gu_s[slot, 1]).astype(BF16)
        o_ref[...] += _dot(act, w2[c])

    def pair(k, carry):
        c = 2 * k
        project(c + 1, 1)
        consume(c, 0)
        project(c + 2, 0)
        consume(c + 1, 1)
        return carry

    project(0, 0)
    lax.fori_loop(0, (n - 1) // 2, pair, 0)
    if n % 2 == 0:
        project(n - 1, 1)
        consume(n - 2, 0)
        consume(n - 1, 1)
    else:
        consume(n - 1, 0)


def _swiglu_scratch(tm, d, tf):
    return [pltpu.VMEM((tm, d), BF16), pltpu.VMEM((2, 2, tm, tf), F32)]


def _tail_kernel(*refs, n_heads, dh, gated, ffn):
    refs = list(refs)
    take = lambda n: [refs.pop(0) for _ in range(n)]
    x_ref, a_ref = take(2)
    b_in = take(5 if gated else 1)
    wa_ref, wb_ref, gx_ref, wq_ref, kv_ref, wo_ref = take(6)
    ffn_in = take(3 if ffn else 0)
    (o_ref,) = take(1)
    if gated:
        (b_ref,) = take(1)
        _spatial_gate(*b_in, b_ref)
    else:
        (b_ref,) = b_in
    rest = ffn_in + [o_ref] + refs
    x = x_ref[...] + _dot(a_ref[...], wa_ref[...]) + _dot(b_ref[...], wb_ref[...])
    q = _dot(_rms(x, gx_ref[...]).astype(BF16), wq_ref[...]).astype(BF16)
    width = n_heads * dh
    outs = []
    for hd in range(n_heads):
        cs = slice(hd * dh, (hd + 1) * dh)
        s = _dot_nt(q[:, cs], kv_ref[:, cs]) * (dh ** -0.5)
        p = jnp.exp(s - jnp.max(s, axis=1, keepdims=True))
        p = p / jnp.sum(p, axis=1, keepdims=True)
        outs.append(_dot(p.astype(BF16), kv_ref[:, width + hd * dh:width + (hd + 1) * dh]).astype(BF16))
    x = x + _dot(jnp.concatenate(outs, axis=1), wo_ref[...])
    if len(rest) == 1:
        (o_ref,) = rest
        o_ref[...] = x
    else:
        gf_ref, w13_ref, w2_ref, o_ref, h_s, gu_s = rest
        h_s[...] = _rms(x, gf_ref[...]).astype(BF16)
        o_ref[...] = x
        _swiglu_chunks(h_s, w13_ref, w2_ref, o_ref, gu_s)


def layer_tail(x, a, b, w_out, gx, wq, kv, wo, ffn=None, *, tm, seq, mem_len, kv_col, n_heads, dh, name):
    t, d = x.shape
    per_b = seq // tm
    resident = pl.Buffered(1)
    const = lambda arr: pl.BlockSpec(arr.shape, lambda i: (0,) * arr.ndim, pipeline_mode=resident)
    rows = lambda arr: pl.BlockSpec((tm, arr.shape[1]), lambda i: (i, 0))
    vec = lambda g: g.reshape(1, d)
    gated = isinstance(b, tuple)
    scratch = []
    if gated:
        proj, u_col, v_col, sgu_norm, w_s, b_s = b
        b_width = w_s.shape[0] * LANES
        b_args = [proj, proj, sgu_norm, w_s, b_s.T]
        b_specs = [pl.BlockSpec((tm, b_width), lambda i: (i, u_col)), pl.BlockSpec((tm, b_width), lambda i: (i, v_col)),
                   const(sgu_norm), const(w_s), const(b_s.T)]
        scratch.append(pltpu.VMEM((tm, b_width), BF16))
    else:
        b_width = b.shape[1]
        b_args, b_specs = [b], [rows(b)]
    wa, wb = w_out[:a.shape[1]], w_out[a.shape[1]:]
    assert wb.shape[0] == b_width
    args = [x, a] + b_args + [wa, wb, vec(gx), wq, kv, wo]
    in_specs = [rows(x), rows(a)] + b_specs + [
        const(wa), const(wb), const(vec(gx)), const(wq),
        pl.BlockSpec((mem_len, 2 * n_heads * dh), lambda i: (i // per_b, kv_col)), const(wo)]
    if ffn is not None:
        gf, w13, w2 = ffn
        args += [vec(gf), w13, w2]
        in_specs += [const(vec(gf)), const(w13), const(w2)]
        scratch += _swiglu_scratch(tm, d, w2.shape[1])
    return pl.pallas_call(
        functools.partial(_tail_kernel, n_heads=n_heads, dh=dh, gated=gated, ffn=ffn is not None), grid=(t // tm,),
        in_specs=in_specs, out_specs=rows(x), out_shape=jax.ShapeDtypeStruct((t, d), F32),
        scratch_shapes=scratch, compiler_params=_params("parallel"), name=name,
    )(*args)


def _conv_kernel(bg_ref, cg_ref, xi_ref, w_ref, o_ref):
    s, width = o_ref.shape
    n_taps = w_ref.shape[0]
    xc = cg_ref[...].astype(F32) * xi_ref[...].astype(F32)
    row = lax.broadcasted_iota(jnp.int32, (s, width), 0)
    y = w_ref[n_taps - 1:n_taps, :] * xc
    for back in range(1, n_taps):
        shifted = jnp.where(row >= back, pltpu.roll(xc, back, axis=0), 0.0)
        y = y + w_ref[n_taps - 1 - back:n_taps - back, :] * shifted
    o_ref[...] = (bg_ref[...].astype(F32) * y).astype(o_ref.dtype)


def short_conv(main, conv_w, *, batch, seq):
    width = conv_w.shape[1]
    return pl.pallas_call(
        _conv_kernel, grid=(batch,),
        in_specs=[
            pl.BlockSpec((seq, width), lambda b: (b, 0)),
            pl.BlockSpec((seq, width), lambda b: (b, 1)),
            pl.BlockSpec((seq, width), lambda b: (b, 2)),
            pl.BlockSpec(conv_w.shape, lambda b: (0, 0)),
        ],
        out_specs=pl.BlockSpec((seq, width), lambda b: (b, 0)),
        out_shape=jax.ShapeDtypeStruct((batch * seq, width), BF16),
        compiler_params=_params("parallel"), name="short_conv",
    )(main, main, main, conv_w)


def _diff_prep_kernel(rb_ref, lq1_ref, lk1_ref, lq2_ref, lk2_ref, bias_ref, far_ref, lam_ref, *, tq, lam_init):
    n_heads = bias_ref.shape[0]
    strip = 32
    row = lax.broadcasted_iota(jnp.int32, (strip, tq), 0)
    col = lax.broadcasted_iota(jnp.int32, (strip, tq), 1)
    max_exact = N_BUCKETS // 2

    def fill(r, carry):
        r0 = pl.multiple_of(r * strip, strip)
        for which in range(2):
            n = jnp.maximum(row + r0 - col + which * tq, 0)
            nf = jnp.maximum(n, 1).astype(F32)
            large = max_exact + (jnp.log(nf / max_exact) / math.log(MAX_DIST / max_exact)
                                 * (N_BUCKETS - max_exact)).astype(jnp.int32)
            large = jnp.minimum(large, N_BUCKETS - 1)
            bucket = jnp.where(n < max_exact, n, large)
            for h in range(n_heads):
                b = jnp.zeros((strip, tq), F32)
                for kk in range(N_BUCKETS):
                    b = jnp.where(bucket == kk, rb_ref[kk, h], b)
                bias_ref[h, which, pl.ds(r0, strip), :] = (b - rb_ref[N_BUCKETS - 1, h]) * LOG2E
        return carry

    lax.fori_loop(0, tq // strip, fill, 0)
    for h in range(n_heads):
        far_ref[h] = jnp.full((1, LANES), rb_ref[N_BUCKETS - 1, h], F32) * LOG2E
    lam = (jnp.exp(jnp.sum(lq1_ref[...] * lk1_ref[...], axis=1, keepdims=True))
           - jnp.exp(jnp.sum(lq2_ref[...] * lk2_ref[...], axis=1, keepdims=True)) + lam_init)
    lam_ref[...] = jnp.broadcast_to(lam, (1, LANES))


def diff_prep(rel_bias, lq1, lk1, lq2, lk2, *, tq, lam_init):
    n_heads = rel_bias.shape[1]
    vec = lambda a: a.reshape(1, -1)
    vspec = pl.BlockSpec(memory_space=pltpu.VMEM)
    return pl.pallas_call(
        functools.partial(_diff_prep_kernel, tq=tq, lam_init=lam_init),
        in_specs=[pl.BlockSpec(memory_space=pltpu.SMEM), vspec, vspec, vspec, vspec],
        out_specs=[vspec, vspec, vspec],
        out_shape=[
            jax.ShapeDtypeStruct((n_heads, 2, tq, tq), F32),
            jax.ShapeDtypeStruct((n_heads, 1, LANES), F32),
            jax.ShapeDtypeStruct((1, LANES), F32),
        ],
        compiler_params=pltpu.CompilerParams(vmem_limit_bytes=VMEM_LIMIT), name="diff_prep",
    )(rel_bias, vec(lq1), vec(lk1), vec(lq2), vec(lk2))


def _diff_kernel(q_ref, k_ref, v_ref, bias_ref, far_ref, lam_ref, subln_ref, w_ref, o_ref, wcast_ref,
                 ka_s, m_s, l_s, acc_s, *, tq, lam_init):
    wcast_ref[...] = w_ref[...].astype(BF16)
    i = pl.program_id(2)
    lane = lax.broadcasted_iota(jnp.int32, (1, LANES), 1)
    halves = _halves(lane)

    @pl.when(i == 0)
    def _():
        k = k_ref[...]
        far = _split3(far_ref[0])
        for sub in range(2):
            ka_s[sub] = _augment(k, halves[sub], lane, HEAD_DIM * (1 - sub), far, True)

    q = q_ref[...]
    zero = jnp.zeros((1, LANES), F32)
    qa = [_augment(q, halves[sub], lane, HEAD_DIM * (1 - sub), (zero,) * 3, False) for sub in range(2)]
    _causal_attention(qa, ka_s, v_ref, bias_ref, i, tq, m_s, l_s, acc_s)
    o = _normalised(l_s, acc_s, 0) - lam_ref[...] * _normalised(l_s, acc_s, 1)
    o_ref[...] = (_rms(o, subln_ref[...]) * (1.0 - lam_init)).astype(o_ref.dtype)


def diff_attention(main, bias, far, lam, subln, w_ride, *, batch, seq, tq, q_col, k_col, v_col, lam_init):
    t = batch * seq
    nq = seq // tq
    n_heads = bias.shape[0]
    grid = (batch, n_heads, nq)
    ride_spec, ride_shape = _rider(w_ride, grid)
    return pl.pallas_call(
        functools.partial(_diff_kernel, tq=tq, lam_init=lam_init),
        grid=grid,
        in_specs=[
            pl.BlockSpec((tq, LANES), lambda b, h, i: (b * nq + i, q_col + h)),
            pl.BlockSpec((seq, LANES), lambda b, h, i: (b, k_col + h)),
            pl.BlockSpec((seq, LANES), lambda b, h, i: (b, v_col + h)),
            pl.BlockSpec((1, 2, tq, tq), lambda b, h, i: (h, 0, 0, 0)),
            pl.BlockSpec((1, 1, LANES), lambda b, h, i: (h, 0, 0)),
            pl.BlockSpec((1, LANES), lambda b, h, i: (0, 0)),
            pl.BlockSpec((1, LANES), lambda b, h, i: (0, 0)),
            ride_spec,
        ],
        out_specs=[pl.BlockSpec((tq, LANES), lambda b, h, i: (b * nq + i, h)), ride_spec],
        out_shape=[jax.ShapeDtypeStruct((t, n_heads * LANES), BF16), ride_shape],
        scratch_shapes=_attn_scratch(seq, tq),
        compiler_params=_params("parallel", "parallel", "arbitrary"), name="diff_attention",
    )(main, main, main, bias, far, lam, subln.reshape(1, LANES), w_ride)


def _router_kernel(x_ref, g_ref, wr_ref, meta_ref, wts_ref, before_ref, cnt_ref, carry_s, *, n_exp):
    tm = x_ref.shape[0]

    @pl.when(pl.program_id(0) == 0)
    def _():
        carry_s[...] = jnp.zeros_like(carry_s)

    before_ref[...] = jnp.broadcast_to(carry_s[...], before_ref.shape)

    h = _rms(x_ref[...], g_ref[...])
    h_hi = h.astype(BF16)
    h_lo = (h - h_hi.astype(F32)).astype(BF16)
    logits = _dot(h_hi, wr_ref[0]) + (_dot(h_hi, wr_ref[1]) + _dot(h_lo, wr_ref[0]))
    lane = lax.broadcasted_iota(jnp.int32, (tm, LANES), 1)
    lane_f = lane.astype(F32)
    logits = jnp.where(lane < n_exp, logits, -jnp.inf)
    m1 = jnp.max(logits, axis=1, keepdims=True)
    i1 = jnp.min(jnp.where(logits == m1, lane_f, float(LANES)), axis=1, keepdims=True)
    rest = jnp.where(lane_f == i1, -jnp.inf, logits)
    m2 = jnp.max(rest, axis=1, keepdims=True)
    i2 = jnp.min(jnp.where(rest == m2, lane_f, float(LANES)), axis=1, keepdims=True)
    e = jnp.exp(m2 - m1)
    w1 = 1.0 / (1.0 + e)
    w2 = e / (1.0 + e)
    sel1 = lane_f == i1
    sel2 = lane_f == i2
    onehot = jnp.where(sel1 | sel2, 1.0, 0.0)
    row = lax.broadcasted_iota(jnp.int32, (tm, tm), 0)
    col = lax.broadcasted_iota(jnp.int32, (tm, tm), 1)
    before = jnp.where(row > col, 1.0, 0.0).astype(BF16)
    local = _dot(before, onehot.astype(BF16))
    rank = local + carry_s[...]
    pick = lambda sel, val: jnp.sum(jnp.where(sel, val, 0.0), axis=1, keepdims=True)
    fields = [i1, i2, pick(sel1, rank), pick(sel2, rank), pick(sel1, local), pick(sel2, local)]
    meta = jnp.zeros((tm, LANES), F32)
    for idx, field in enumerate(fields):
        meta = jnp.where(lane == idx, field, meta)
    meta_ref[...] = meta.astype(jnp.int32)
    wts_ref[...] = jnp.where(lane == 0, w1, jnp.where(lane == 1, w2, 0.0))
    carry_s[...] += jnp.sum(onehot, axis=0, keepdims=True)
    cnt_ref[...] = carry_s[...]


def route_tokens(x, g, wr, *, tm, n_exp):
    t, d = x.shape
    return pl.pallas_call(
        functools.partial(_router_kernel, n_exp=n_exp),
        grid=(t // tm,),
        in_specs=[
            pl.BlockSpec((tm, d), lambda i: (i, 0)),
            pl.BlockSpec((1, d), lambda i: (0, 0)),
            pl.BlockSpec((2, d, LANES), lambda i: (0, 0, 0)),
        ],
        out_specs=[
            pl.BlockSpec((tm, LANES), lambda i: (i, 0)),
            pl.BlockSpec((tm, LANES), lambda i: (i, 0)),
            pl.BlockSpec((SUBLANES, LANES), lambda i: (i, 0)),
            pl.BlockSpec((1, LANES), lambda i: (0, 0)),
        ],
        out_shape=[
            jax.ShapeDtypeStruct((t, LANES), jnp.int32),
            jax.ShapeDtypeStruct((t, LANES), F32),
            jax.ShapeDtypeStruct((t // tm * SUBLANES, LANES), F32),
            jax.ShapeDtypeStruct((1, LANES), F32),
        ],
        scratch_shapes=[pltpu.VMEM((1, LANES), F32)],
        compiler_params=_params("arbitrary"), name="moe_router",
    )(x, g.reshape(1, d), wr)


def _dispatch_kernel(start_ref, shift_ref, keep_ref, x_ref, g_ref, meta_ref, zeros_hbm, xs_hbm,
                     stage_s, carry_s, sems, *, tm, n_exp):
    del zeros_hbm
    j = pl.program_id(0)
    slot = j % 2
    rows = tm + BF16_ROWS

    def block_copy(step, e, buf):
        first = pl.multiple_of(start_ref[step * n_exp + e], BF16_ROWS)
        return pltpu.make_async_copy(stage_s.at[buf, e], xs_hbm.at[pl.ds(first, rows)], sems.at[buf])

    @pl.when(j == 0)
    def _():
        carry_s[...] = jnp.zeros_like(carry_s)

    h = _rms(x_ref[...], g_ref[...]).astype(BF16)
    fields = meta_ref[...].astype(F32).T
    slot_row = lax.broadcasted_iota(jnp.int32, (rows, tm), 0).astype(F32)
    for e in range(n_exp):
        key = j * n_exp + e
        idx = jnp.where(fields[0:1] == e, fields[4:5], jnp.where(fields[1:2] == e, fields[5:6], -2.0 * rows))
        idx = idx + shift_ref[key].astype(F32)
        onehot = jnp.where(slot_row == idx, 1.0, 0.0).astype(BF16)
        stage_s[slot, e] = _dot(onehot, h).astype(BF16)
        stage_s[slot, e, 0:BF16_ROWS, :] += carry_s[e]
        keep = pl.multiple_of(keep_ref[key], BF16_ROWS)
        carry_s[e] = stage_s[slot, e, pl.ds(keep, BF16_ROWS), :]

    @pl.when(j > 0)
    def _():
        for e in range(n_exp):
            block_copy(j - 1, e, 1 - slot).wait()

    for e in range(n_exp):
        block_copy(j, e, slot).start()

    @pl.when(j == pl.num_programs(0) - 1)
    def _():
        for e in range(n_exp):
            block_copy(j, e, slot).wait()


def moe_dispatch(starts, shifts, keeps, x, g, meta, n_rows, *, tm, n_exp):
    t, d = x.shape
    zeros = jnp.zeros((n_rows, d), BF16)
    rows = tm + BF16_ROWS
    return pl.pallas_call(
        functools.partial(_dispatch_kernel, tm=tm, n_exp=n_exp),
        grid_spec=pltpu.PrefetchScalarGridSpec(
            num_scalar_prefetch=3, grid=(t // tm,),
            in_specs=[
                pl.BlockSpec((tm, d), lambda i, *_: (i, 0)),
                pl.BlockSpec((1, d), lambda i, *_: (0, 0)),
                pl.BlockSpec((tm, LANES), lambda i, *_: (i, 0)),
                pl.BlockSpec(memory_space=pl.ANY),
            ],
            out_specs=pl.BlockSpec(memory_space=pl.ANY),
            scratch_shapes=[pltpu.VMEM((2, n_exp, rows, d), BF16), pltpu.VMEM((n_exp, BF16_ROWS, d), BF16),
                            pltpu.SemaphoreType.DMA((2,))],
        ),
        out_shape=jax.ShapeDtypeStruct((n_rows, d), BF16),
        input_output_aliases={6: 0},
        compiler_params=_params("arbitrary"),
        name="moe_dispatch",
    )(starts, shifts, keeps, x, g.reshape(1, d), meta, zeros)


def _expert_kernel(te_ref, used_ref, h_ref, w13_ref, w2_ref, o_ref, gu_s, acc_s):
    del te_ref
    used = used_ref[pl.program_id(0)] != 0

    @pl.when(used)
    def _():
        acc_s[...] = jnp.zeros_like(acc_s)
        _swiglu_chunks(h_ref, w13_ref.at[0], w2_ref.at[0], acc_s, gu_s)
        o_ref[...] = acc_s[...].astype(o_ref.dtype)

    @pl.when(jnp.logical_not(used))
    def _():
        o_ref[...] = jnp.zeros_like(o_ref)


def moe_experts(tile_expert, tile_used, hs, w13, w2, *, tm):
    n_rows, d = hs.shape
    tf = w2.shape[2]
    resident = pl.Buffered(1)
    return pl.pallas_call(
        _expert_kernel,
        grid_spec=pltpu.PrefetchScalarGridSpec(
            num_scalar_prefetch=2, grid=(n_rows // tm,),
            in_specs=[
                pl.BlockSpec((tm, d), lambda i, te, tu: (i, 0)),
                pl.BlockSpec((1,) + w13.shape[1:], lambda i, te, tu: (te[i], 0, 0)),
                pl.BlockSpec((1,) + w2.shape[1:], lambda i, te, tu: (te[i], 0, 0, 0)),
            ],
            out_specs=pl.BlockSpec((tm, d), lambda i, te, tu: (i, 0)),
            scratch_shapes=[pltpu.VMEM((2, 2, tm, tf), F32), pltpu.VMEM((tm, d), F32)],
        ),
        out_shape=jax.ShapeDtypeStruct((n_rows, d), BF16),
        compiler_params=pltpu.CompilerParams(dimension_semantics=("arbitrary",), vmem_limit_bytes=62 * 1024 * 1024),
        name="moe_experts",
    )(tile_expert, tile_used, hs, w13, w2)


def _combine_kernel(start_ref, shift_ref, tail_ref, x_ref, wts_ref, meta_ref, g_ref, y_hbm, o_ref, blk_s, sems, *,
                    tm, n_exp):
    j = pl.program_id(0)
    slot = j % 2
    rows = tm + BF16_ROWS

    def block_copy(step, e, buf):
        first = pl.multiple_of(start_ref[step * n_exp + e], BF16_ROWS)
        return pltpu.make_async_copy(y_hbm.at[pl.ds(first, rows)], blk_s.at[buf, e], sems.at[buf])

    def fetch(step, buf):
        for e in range(n_exp):
            block_copy(step, e, buf).start()

    @pl.when(j == 0)
    def _():
        fetch(0, 0)

    @pl.when(j + 1 < pl.num_programs(0))
    def _():
        fetch(j + 1, 1 - slot)

    for e in range(n_exp):
        block_copy(j, e, slot).wait()

    meta = meta_ref[...]
    wts = wts_ref[...]
    col = lax.broadcasted_iota(jnp.int32, (tm, tm), 1)
    tail_col = lax.broadcasted_iota(jnp.int32, (tm, BF16_ROWS), 1) + tm

    def pick(e):
        sel = [meta[:, k:k + 1] == e for k in range(TOP_K)]
        idx = jnp.where(sel[0], meta[:, 4:5], jnp.where(sel[1], meta[:, 5:6], -2 * rows)) + shift_ref[j * n_exp + e]
        w = jnp.where(sel[0], wts[:, 0:1], jnp.where(sel[1], wts[:, 1:2], 0.0))
        return idx, w

    acc = x_ref[...]
    for e in range(n_exp):
        idx, w = pick(e)
        onehot = jnp.where(col == idx, 1.0, 0.0).astype(BF16)
        acc = acc + w * _dot(onehot, blk_s[slot, e, 0:tm, :])
    o_ref[...] = acc

    for e in range(n_exp):
        @pl.when(tail_ref[j * n_exp + e] != 0)
        def _(e=e):
            idx, w = pick(e)
            onehot_tail = jnp.where(tail_col == idx, 1.0, 0.0).astype(BF16)
            o_ref[...] += w * _dot(onehot_tail, blk_s[slot, e, tm:rows, :])

    o_ref[...] = _rms(o_ref[...], g_ref[...])


def moe_combine(starts, shifts, tails, x, wts, meta, g, y, *, tm, n_exp):
    t, d = x.shape
    n_pre = 3
    return pl.pallas_call(
        functools.partial(_combine_kernel, tm=tm, n_exp=n_exp),
        grid_spec=pltpu.PrefetchScalarGridSpec(
            num_scalar_prefetch=n_pre, grid=(t // tm,),
            in_specs=[
                pl.BlockSpec((tm, d), lambda i, *_: (i, 0)),
                pl.BlockSpec((tm, LANES), lambda i, *_: (i, 0)),
                pl.BlockSpec((tm, LANES), lambda i, *_: (i, 0)),
                pl.BlockSpec((1, d), lambda i, *_: (0, 0)),
                pl.BlockSpec(memory_space=pl.ANY),
            ],
            out_specs=pl.BlockSpec((tm, d), lambda i, *_: (i, 0)),
            scratch_shapes=[pltpu.VMEM((2, n_exp, tm + BF16_ROWS, d), BF16), pltpu.SemaphoreType.DMA((2,))],
        ),
        out_shape=jax.ShapeDtypeStruct((t, d), F32),
        compiler_params=_params("arbitrary"), name="moe_combine",
    )(starts, shifts, tails, x, wts, meta, g.reshape(1, d), y)


def _pad_cols(w, n):
    return jnp.pad(w, ((0, 0), (0, n - w.shape[1])))


def _even_mixer(x, norm, w_in, b_f, sgu_norm, w_s, b_s, w_ride, *, batch, seq, tq):
    n_heads = b_f.shape[0]
    a_width = n_heads * HEAD_DIM
    b_width = w_s.shape[0] * LANES
    f0 = 3 * a_width
    w_main = jnp.concatenate([w_in[:, :f0], w_in[:, f0 + n_heads:]], axis=1)
    w_gate = _pad_cols(w_in[:, f0:f0 + n_heads], LANES)
    main, gate = norm_matmul(x, norm, [w_main.astype(BF16), w_gate.astype(BF16)], [BF16, F32],
                             tm=512, name="even_in_proj", scaled=(0, a_width, LOG2E * HEAD_DIM ** -0.5))
    c = gate_cumsum(gate, _pad_cols(b_f.reshape(1, -1), LANES), seq=seq)
    n_pairs = a_width // LANES
    a, w_cast = fox_attention(main, c, w_ride, batch=batch, seq=seq, tq=tq, q_col=0, k_col=n_pairs,
                              v_col=2 * n_pairs, n_pairs=n_pairs)
    u_col = f0 // b_width
    return (a, (main, u_col, u_col + 1, sgu_norm, w_s, b_s)), w_cast


def _odd_mixer(x, norm, w_in, conv_w, lq1, lk1, lq2, lk2, subln, rel_bias, lam_init, w_ride, *, batch, seq, tq):
    c_width = conv_w.shape[1]
    d_width = rel_bias.shape[1] * 2 * HEAD_DIM
    q0 = 3 * c_width
    (main,) = norm_matmul(x, norm, [w_in.astype(BF16)], [BF16], tm=512, name="odd_in_proj",
                          scaled=(q0, q0 + d_width, LOG2E * HEAD_DIM ** -0.5))
    c_out = short_conv(main, conv_w, batch=batch, seq=seq)
    bias, far, lam = diff_prep(rel_bias, lq1, lk1, lq2, lk2, tq=tq, lam_init=lam_init)
    q_col = q0 // LANES
    n_heads = rel_bias.shape[1]
    d_out, w_cast = diff_attention(main, bias, far, lam, subln, w_ride, batch=batch, seq=seq, tq=tq, q_col=q_col,
                                   k_col=q_col + n_heads, v_col=q_col + 2 * n_heads, lam_init=lam_init)
    return (c_out, d_out), w_cast


def _moe_layer(x, norm, w_router, w13, w2, final_norm, *, tm_expert):
    t, d = x.shape
    n_exp = w_router.shape[1]
    tm_route = 256
    wr = _pad_cols(w_router, LANES)
    wr_hi = wr.astype(BF16)
    wr_split = jnp.stack([wr_hi, (wr - wr_hi.astype(F32)).astype(BF16)])
    meta, wts, before, counts = route_tokens(x, norm, wr_split, tm=tm_route, n_exp=n_exp)
    counts = counts[0, :n_exp].astype(jnp.int32)
    block_rows = tm_route + BF16_ROWS
    padded = (counts + block_rows + tm_expert - 1) // tm_expert * tm_expert
    ends = jnp.cumsum(padded)
    offsets = ends - padded
    n_rows = TOP_K * t + n_exp * (tm_expert + pl.cdiv(block_rows, tm_expert) * tm_expert)
    tile_start = jnp.arange(n_rows // tm_expert, dtype=jnp.int32) * tm_expert
    tile_expert = jnp.minimum(jnp.sum(tile_start[:, None] >= ends[None, :], axis=1), n_exp - 1).astype(jnp.int32)
    tile_used = (tile_start < (offsets + counts)[tile_expert]).astype(jnp.int32)
    before = before[::SUBLANES, :n_exp].astype(jnp.int32)
    in_tile = jnp.concatenate([before[1:], counts[None]]) - before
    first = offsets[None, :] + before
    starts = first // BF16_ROWS * BF16_ROWS
    shifts = first - starts
    keeps = (shifts + in_tile) // BF16_ROWS * BF16_ROWS
    tails = (shifts + in_tile > tm_route).astype(jnp.int32)
    flat = lambda a: a.reshape(-1).astype(jnp.int32)
    hs = moe_dispatch(flat(starts), flat(shifts), flat(keeps), x, norm, meta, n_rows, tm=tm_route, n_exp=n_exp)
    y = moe_experts(tile_expert, tile_used, hs, *_chunk_weights(w13, w2, 512), tm=tm_expert)
    return moe_combine(flat(starts), flat(shifts), flat(tails), x, wts, meta, final_norm, y, tm=tm_route,
                       n_exp=n_exp)


def kernel(x, mem, rel_bias, mem_norm, final_norm, ev_norm, ev_w_in, ev_b_f, ev_sgu_norm, ev_w_s, ev_b_s, ev_w_out, ffn_w13, ffn_w2, od_norm, od_w_in, od_conv_w, od_lam_q1, od_lam_k1, od_lam_q2, od_lam_k2, od_subln, od_w_out, moe_router, moe_w13, moe_w2, x_norm, x_wq, x_wkv, x_wo, ffn_norm):
    batch, seq, d = x.shape
    mem_len = mem.shape[1]
    depth = x_norm.shape[0]
    assert depth == 2 and ev_norm.shape[0] == 1 and od_norm.shape[0] == 1
    x_heads, x_dh = 4, 128
    xf = x.reshape(batch * seq, d)
    wkv = jnp.concatenate([x_wkv[layer] for layer in range(depth)], axis=1).astype(BF16)
    (kv,) = norm_matmul(mem.reshape(batch * mem_len, d), mem_norm, [wkv], [BF16], tm=512, name="mem_kv")

    def tail(xf, mixed, w_out, layer, ffn, name):
        return layer_tail(xf, *mixed, w_out.astype(BF16), x_norm[layer], x_wq[layer].astype(BF16), kv,
                          x_wo[layer].astype(BF16), ffn, tm=512, seq=seq, mem_len=mem_len, kv_col=layer,
                          n_heads=x_heads, dh=x_dh, name=name)

    e13, e2 = moe_w13[0], moe_w2[0]
    mixed, e13_bf16 = _even_mixer(xf, ev_norm[0], ev_w_in[0], ev_b_f[0], ev_sgu_norm[0], ev_w_s[0], ev_b_s[0],
                                  e13.reshape(-1, e13.shape[-1]), batch=batch, seq=seq, tq=ATTN_TILE)
    xf = tail(xf, mixed, ev_w_out[0], 0, (ffn_norm[0],) + _chunk_weights(ffn_w13[0], ffn_w2[0], 256),
              "even_tail")
    lam_init = 0.8 - 0.6 * math.exp(-0.3 * 1)
    mixed, e2_bf16 = _odd_mixer(xf, od_norm[0], od_w_in[0], od_conv_w[0], od_lam_q1[0], od_lam_k1[0],
                                od_lam_q2[0], od_lam_k2[0], od_subln[0], rel_bias, lam_init,
                                e2.reshape(-1, e2.shape[-1]), batch=batch, seq=seq, tq=ATTN_TILE)
    xf = tail(xf, mixed, od_w_out[0], 1, None, "odd_tail")
    out = _moe_layer(xf, ffn_norm[1], moe_router[0], e13_bf16.reshape(e13.shape), e2_bf16.reshape(e2.shape),
                     final_norm, tm_expert=512)
    return out.reshape(batch, seq, d)
```

```python
import functools
import math

import jax
import jax.numpy as jnp
from jax import lax
from jax.experimental import pallas as pl
from jax.experimental.pallas import tpu as pltpu

F32 = jnp.float32
BF16 = jnp.bfloat16
EPS = 1e-6
HEAD_DIM = 64
LANES = 128
SUBLANES = 8
BF16_ROWS = 16
N_BUCKETS = 32
MAX_DIST = 128
TOP_K = 2
LOG2E = 1.4426950408889634
ATTN_TILE = 512
IN_PROJ_ROWS = 1024
MEM_ROWS = 512
TAIL_ROWS = 512
ROUTE_ROWS = 256
EXPERT_ROWS = 512
FFN_CHUNK = 256
EXPERT_CHUNK = 512
VMEM_BYTES = 64 * 1024 * 1024
VMEM_LIMIT = VMEM_BYTES - 8 * 1024 * 1024
EXPERT_VMEM_LIMIT = VMEM_BYTES - 2 * 1024 * 1024


def _params(*sem):
    return pltpu.CompilerParams(dimension_semantics=sem, vmem_limit_bytes=VMEM_LIMIT)


def _rms(x, g):
    ms = jnp.mean(x * x, axis=-1, keepdims=True)
    return x * lax.rsqrt(ms + EPS) * g


def _dot(a, b):
    return jnp.dot(a, b, preferred_element_type=F32)


def _dot_nt(a, b):
    return lax.dot_general(a, b, (((1,), (1,)), ((), ())), preferred_element_type=F32)


def _norm_matmul_kernel(x_ref, g_ref, *refs, n_w, chunk, scaled):
    w_refs, o_refs = refs[:n_w], refs[n_w:]
    h = _rms(x_ref[...], g_ref[...]).astype(BF16)
    s0, s1, scale = scaled
    for k, (w_ref, o_ref) in enumerate(zip(w_refs, o_refs)):
        n = w_ref.shape[1]
        for c0 in range(0, n, chunk):
            c1 = min(c0 + chunk, n)
            y = _dot(h, w_ref[:, c0:c1])
            if k == 0 and s0 <= c0 and c1 <= s1:
                y = y * scale
            o_ref[:, c0:c1] = y.astype(o_ref.dtype)


def norm_matmul(x, g, ws, out_dtypes, *, tm, name, scaled=(0, 0, 1.0)):
    t, d = x.shape
    chunk = 512
    assert scaled[0] % chunk == 0 and scaled[1] % chunk == 0
    in_specs = [pl.BlockSpec((tm, d), lambda i: (i, 0)), pl.BlockSpec((1, d), lambda i: (0, 0))]
    in_specs += [pl.BlockSpec(w.shape, lambda i: (0, 0)) for w in ws]
    out_specs = [pl.BlockSpec((tm, w.shape[1]), lambda i: (i, 0)) for w in ws]
    out_shape = [jax.ShapeDtypeStruct((t, w.shape[1]), dt) for w, dt in zip(ws, out_dtypes)]
    return pl.pallas_call(
        functools.partial(_norm_matmul_kernel, n_w=len(ws), chunk=chunk, scaled=scaled),
        grid=(t // tm,), in_specs=in_specs, out_specs=out_specs, out_shape=out_shape,
        compiler_params=_params("parallel"), name=name,
    )(x, g.reshape(1, d), *ws)


def _gate_kernel(g_ref, b_ref, c_ref):
    s = g_ref.shape[0]
    row = lax.broadcasted_iota(jnp.int32, (LANES, LANES), 0)
    col = lax.broadcasted_iota(jnp.int32, (LANES, LANES), 1)
    tri = (row >= col).astype(F32)
    carry = jnp.zeros((1, LANES), F32)
    for blk in range(s // LANES):
        z = g_ref[blk * LANES:(blk + 1) * LANES, :] + b_ref[...]
        log_f = jnp.minimum(z, 0.0) - jnp.log1p(jnp.exp(-jnp.abs(z)))
        cs = jnp.dot(tri, log_f, precision=lax.Precision.HIGHEST, preferred_element_type=F32) + carry
        c_ref[blk * LANES:(blk + 1) * LANES, :] = cs
        carry = cs[LANES - 1:LANES, :]


def gate_cumsum(g, b, *, seq):
    t = g.shape[0]
    return pl.pallas_call(
        _gate_kernel, grid=(t // seq,),
        in_specs=[pl.BlockSpec((seq, LANES), lambda i: (i, 0)), pl.BlockSpec((1, LANES), lambda i: (0, 0))],
        out_specs=pl.BlockSpec((seq, LANES), lambda i: (i, 0)),
        out_shape=jax.ShapeDtypeStruct((t, LANES), F32),
        compiler_params=_params("parallel"), name="gate_cumsum",
    )(g, b)


def _split3(x):
    hi = x.astype(BF16).astype(F32)
    rest = x - hi
    mid = rest.astype(BF16).astype(F32)
    lo = (rest - mid).astype(BF16).astype(F32)
    return hi, mid, lo


def _augment(x, in_half, lane, base, pieces, pieces_first):
    n = len(pieces)
    p0, o0 = (base, base + n) if pieces_first else (base + n, base)
    aug = jnp.where((lane >= o0) & (lane < o0 + n), 1.0, 0.0)
    for idx, piece in enumerate(pieces):
        aug = jnp.where(lane == p0 + idx, piece, aug)
    return jnp.where(in_half, x, aug.astype(x.dtype))


def _halves(lane):
    return [(lane >= HEAD_DIM * hh) & (lane < HEAD_DIM * (hh + 1)) for hh in range(2)]


def _causal_attention(qa, ka_s, v_ref, bias_ref, i, tq, m_s, l_s, acc_s):
    n_chunks = tq // LANES

    def causal():
        row = lax.broadcasted_iota(jnp.int32, (tq, tq), 0)
        col = lax.broadcasted_iota(jnp.int32, (tq, tq), 1)
        return row >= col

    def scores(h, j, near, mask):
        start = pl.multiple_of(j * tq, tq)
        s = _dot_nt(qa[h], ka_s[h, pl.ds(start, tq), :])
        if near is not None:
            s = s + bias_ref[0, near]
        if mask is not None:
            s = jnp.where(mask, s, -jnp.inf)
        return [s[:, c * LANES:(c + 1) * LANES] for c in range(n_chunks)]

    def max_tile(j, near, mask):
        for h in range(2):
            m = m_s[h]
            for chunk in scores(h, j, near, mask):
                m = jnp.maximum(m, chunk)
            m_s[h] = m

    def sum_tile(j, near, mask):
        v = v_ref[pl.ds(pl.multiple_of(j * tq, tq), tq), :]
        for h in range(2):
            m = m_s[h]
            ps = [jnp.exp2(chunk - m) for chunk in scores(h, j, near, mask)]
            l_s[h] += functools.reduce(lambda a, b: a + b, ps)
            acc_s[h] += _dot(jnp.concatenate(ps, axis=1).astype(BF16), v)

    def key_tiles(fn):
        def run(lo, hi, near):
            def body(j, carry):
                fn(j, near, None)
                return carry
            lax.fori_loop(lo, hi, body, 0)

        if bias_ref is None:
            run(0, i, None)
            fn(i, None, causal())
        else:
            n_far = jnp.maximum(i - 1, 0)
            run(0, n_far, None)
            run(n_far, i, 1)
            fn(i, 0, causal())

    m_s[...] = jnp.full(m_s.shape, -jnp.inf, F32)
    key_tiles(max_tile)
    for h in range(2):
        m_s[h] = jnp.broadcast_to(jnp.max(m_s[h], axis=1, keepdims=True), (tq, LANES))
    l_s[...] = jnp.zeros(l_s.shape, F32)
    acc_s[...] = jnp.zeros(acc_s.shape, F32)
    key_tiles(sum_tile)


def _normalised(l_s, acc_s, h):
    return acc_s[h] / jnp.sum(l_s[h], axis=1, keepdims=True)


def _rider(w, grid):
    steps = math.prod(grid)
    rows = w.shape[0] // steps
    assert rows * steps == w.shape[0] and rows % BF16_ROWS == 0

    def index(*ids):
        step = 0
        for n, idx in zip(grid, ids):
            step = step * n + idx
        return step, 0

    return pl.BlockSpec((rows, w.shape[1]), index), jax.ShapeDtypeStruct(w.shape, BF16)


def _fox_kernel(q_ref, k_ref, v_ref, c_ref, w_ref, o_ref, wcast_ref, ka_s, m_s, l_s, acc_s, *, tq):
    wcast_ref[...] = w_ref[...].astype(BF16)
    hp = pl.program_id(1)
    i = pl.program_id(2)
    lane = lax.broadcasted_iota(jnp.int32, (1, LANES), 1)
    halves = _halves(lane)

    def decay(c, hh):
        return jnp.sum(jnp.where(lane == 2 * hp + hh, c, 0.0), axis=1, keepdims=True) * LOG2E

    @pl.when(i == 0)
    def _():
        k = k_ref[...]
        c_all = c_ref[...]
        for hh in range(2):
            ka_s[hh] = _augment(k, halves[hh], lane, HEAD_DIM * (1 - hh), _split3(-decay(c_all, hh)), True)

    q = q_ref[...]
    c_q = c_ref[pl.ds(pl.multiple_of(i * tq, tq), tq), :]
    qa = [_augment(q, halves[hh], lane, HEAD_DIM * (1 - hh), _split3(decay(c_q, hh)), False) for hh in range(2)]
    _causal_attention(qa, ka_s, v_ref, None, i, tq, m_s, l_s, acc_s)
    out = jnp.where(lane < HEAD_DIM, _normalised(l_s, acc_s, 0), _normalised(l_s, acc_s, 1))
    o_ref[...] = out.astype(o_ref.dtype)


def _attn_scratch(seq, tq):
    return [pltpu.VMEM((2, seq, LANES), BF16)] + [pltpu.VMEM((2, tq, LANES), F32)] * 3


def fox_attention(qkv, c, w_ride, *, batch, seq, tq, q_col, k_col, v_col, n_pairs):
    t = batch * seq
    nq = seq // tq
    grid = (batch, n_pairs, nq)
    ride_spec, ride_shape = _rider(w_ride, grid)
    return pl.pallas_call(
        functools.partial(_fox_kernel, tq=tq),
        grid=grid,
        in_specs=[
            pl.BlockSpec((tq, LANES), lambda b, h, i: (b * nq + i, q_col + h)),
            pl.BlockSpec((seq, LANES), lambda b, h, i: (b, k_col + h)),
            pl.BlockSpec((seq, LANES), lambda b, h, i: (b, v_col + h)),
            pl.BlockSpec((seq, LANES), lambda b, h, i: (b, 0)),
            ride_spec,
        ],
        out_specs=[pl.BlockSpec((tq, LANES), lambda b, h, i: (b * nq + i, h)), ride_spec],
        out_shape=[jax.ShapeDtypeStruct((t, n_pairs * LANES), BF16), ride_shape],
        scratch_shapes=_attn_scratch(seq, tq),
        compiler_params=_params("parallel", "parallel", "arbitrary"), name="fox_attention",
    )(qkv, qkv, qkv, c, w_ride)


def _spatial_gate(u_ref, v_ref, norm_ref, ws_ref, bs_ref, o_ref):
    tb = u_ref.shape[0]
    n_groups, chunk, _ = ws_ref.shape
    row = lax.broadcasted_iota(jnp.int32, (chunk, chunk), 0)
    col = lax.broadcasted_iota(jnp.int32, (chunk, chunk), 1)
    tri = row >= col
    for g in range(n_groups):
        w = jnp.where(tri, ws_ref[g], 0.0).astype(BF16)
        bias = bs_ref[:, g:g + 1]
        gain = norm_ref[g:g + 1, :]
        for c in range(tb // chunk):
            rs = slice(c * chunk, (c + 1) * chunk)
            cs = slice(g * LANES, (g + 1) * LANES)
            vn = _rms(jax.nn.gelu(v_ref[rs, cs].astype(F32)), gain)
            mixed = _dot(w, vn.astype(BF16)) + bias
            o_ref[rs, cs] = (jax.nn.gelu(u_ref[rs, cs].astype(F32)) * mixed).astype(o_ref.dtype)


def _chunk_weights(w13, w2, tf):
    *lead, ff, d = w2.shape
    return w13.astype(BF16), w2.astype(BF16).reshape(*lead, ff // tf, tf, d)


def _swiglu_chunks(h_s, w13, w2, o_ref, gu_s):
    n, tf, _ = w2.shape

    def project(c, slot):
        for part in range(2):
            cols = pl.ds(pl.multiple_of((part * n + c) * tf, tf), tf)
            gu_s[slot, part] = _dot(h_s[...], w13[:, cols])

    def consume(c, slot):
        gate = gu_s[slot, 0]
        act = (gate * jax.nn.sigmoid(gate) * gu_s[slot, 1]).astype(BF16)
        o_ref[...] += _dot(act, w2[c])

    def pair(k, carry):
        c = 2 * k
        project(c + 1, 1)
        consume(c, 0)
        project(c + 2, 0)
        consume(c + 1, 1)
        return carry

    project(0, 0)
    lax.fori_loop(0, (n - 1) // 2, pair, 0)
    if n % 2 == 0:
        project(n - 1, 1)
        consume(n - 2, 0)
        consume(n - 1, 1)
    else:
        consume(n - 1, 0)


def _swiglu_scratch(tm, d, tf):
    return [pltpu.VMEM((tm, d), BF16), pltpu.VMEM((2, 2, tm, tf), F32)]


def _tail_kernel(*refs, n_heads, dh, gated, ffn):
    refs = list(refs)
    take = lambda n: [refs.pop(0) for _ in range(n)]
    x_ref, a_ref = take(2)
    b_in = take(5 if gated else 1)
    wa_ref, wb_ref, gx_ref, wq_ref, kv_ref, wo_ref = take(6)
    ffn_in = take(3 if ffn else 0)
    (o_ref,) = take(1)
    if gated:
        (b_ref,) = take(1)
        _spatial_gate(*b_in, b_ref)
    else:
        (b_ref,) = b_in
    rest = ffn_in + [o_ref] + refs
    x = x_ref[...] + _dot(a_ref[...], wa_ref[...]) + _dot(b_ref[...], wb_ref[...])
    q = _dot(_rms(x, gx_ref[...]).astype(BF16), wq_ref[...]).astype(BF16)
    width = n_heads * dh
    outs = []
    for hd in range(n_heads):
        cs = slice(hd * dh, (hd + 1) * dh)
        s = _dot_nt(q[:, cs], kv_ref[:, cs]) * (dh ** -0.5)
        p = jnp.exp(s - jnp.max(s, axis=1, keepdims=True))
        p = p / jnp.sum(p, axis=1, keepdims=True)
        outs.append(_dot(p.astype(BF16), kv_ref[:, width + hd * dh:width + (hd + 1) * dh]).astype(BF16))
    x = x + _dot(jnp.concatenate(outs, axis=1), wo_ref[...])
    if len(rest) == 1:
        (o_ref,) = rest
        o_ref[...] = x
    else:
        gf_ref, w13_ref, w2_ref, o_ref, h_s, gu_s = rest
        h_s[...] = _rms(x, gf_ref[...]).astype(BF16)
        o_ref[...] = x
        _swiglu_chunks(h_s, w13_ref, w2_ref, o_ref, gu_s)


def layer_tail(x, a, b, w_out, gx, wq, kv, wo, ffn=None, *, tm, seq, mem_len, kv_col, n_heads, dh, name):
    t, d = x.shape
    per_b = seq // tm
    resident = pl.Buffered(1)
    const = lambda arr: pl.BlockSpec(arr.shape, lambda i: (0,) * arr.ndim, pipeline_mode=resident)
    rows = lambda arr: pl.BlockSpec((tm, arr.shape[1]), lambda i: (i, 0))
    vec = lambda g: g.reshape(1, d)
    gated = isinstance(b, tuple)
    scratch = []
    if gated:
        proj, u_col, v_col, sgu_norm, w_s, b_s = b
        b_width = w_s.shape[0] * LANES
        b_args = [proj, proj, sgu_norm, w_s, b_s.T]
        b_specs = [pl.BlockSpec((tm, b_width), lambda i: (i, u_col)), pl.BlockSpec((tm, b_width), lambda i: (i, v_col)),
                   const(sgu_norm), const(w_s), const(b_s.T)]
        scratch.append(pltpu.VMEM((tm, b_width), BF16))
    else:
        b_width = b.shape[1]
        b_args, b_specs = [b], [rows(b)]
    wa, wb = w_out[:a.shape[1]], w_out[a.shape[1]:]
    assert wb.shape[0] == b_width
    args = [x, a] + b_args + [wa, wb, vec(gx), wq, kv, wo]
    in_specs = [rows(x), rows(a)] + b_specs + [
        const(wa), const(wb), const(vec(gx)), const(wq),
        pl.BlockSpec((mem_len, 2 * n_heads * dh), lambda i: (i // per_b, kv_col)), const(wo)]
    if ffn is not None:
        gf, w13, w2 = ffn
        args += [vec(gf), w13, w2]
        in_specs += [const(vec(gf)), const(w13), const(w2)]
        scratch += _swiglu_scratch(tm, d, w2.shape[1])
    return pl.pallas_call(
        functools.partial(_tail_kernel, n_heads=n_heads, dh=dh, gated=gated, ffn=ffn is not None), grid=(t // tm,),
        in_specs=in_specs, out_specs=rows(x), out_shape=jax.ShapeDtypeStruct((t, d), F32),
        scratch_shapes=scratch, compiler_params=_params("parallel"), name=name,
    )(*args)


def _conv_kernel(bg_ref, cg_ref, xi_ref, w_ref, o_ref):
    s, width = o_ref.shape
    n_taps = w_ref.shape[0]
    xc = cg_ref[...].astype(F32) * xi_ref[...].astype(F32)
    row = lax.broadcasted_iota(jnp.int32, (s, width), 0)
    y = w_ref[n_taps - 1:n_taps, :] * xc
    for back in range(1, n_taps):
        shifted = jnp.where(row >= back, pltpu.roll(xc, back, axis=0), 0.0)
        y = y + w_ref[n_taps - 1 - back:n_taps - back, :] * shifted
    o_ref[...] = (bg_ref[...].astype(F32) * y).astype(o_ref.dtype)


def short_conv(main, conv_w, *, batch, seq):
    width = conv_w.shape[1]
    return pl.pallas_call(
        _conv_kernel, grid=(batch,),
        in_specs=[
            pl.BlockSpec((seq, width), lambda b: (b, 0)),
            pl.BlockSpec((seq, width), lambda b: (b, 1)),
            pl.BlockSpec((seq, width), lambda b: (b, 2)),
            pl.BlockSpec(conv_w.shape, lambda b: (0, 0)),
        ],
        out_specs=pl.BlockSpec((seq, width), lambda b: (b, 0)),
        out_shape=jax.ShapeDtypeStruct((batch * seq, width), BF16),
        compiler_params=_params("parallel"), name="short_conv",
    )(main, main, main, conv_w)


def _diff_prep_kernel(rb_ref, lq1_ref, lk1_ref, lq2_ref, lk2_ref, bias_ref, far_ref, lam_ref, *, tq, lam_init):
    n_heads = bias_ref.shape[0]
    strip = 32
    row = lax.broadcasted_iota(jnp.int32, (strip, tq), 0)
    col = lax.broadcasted_iota(jnp.int32, (strip, tq), 1)
    max_exact = N_BUCKETS // 2

    def fill(r, carry):
        r0 = pl.multiple_of(r * strip, strip)
        for which in range(2):
            n = jnp.maximum(row + r0 - col + which * tq, 0)
            nf = jnp.maximum(n, 1).astype(F32)
            large = max_exact + (jnp.log(nf / max_exact) / math.log(MAX_DIST / max_exact)
                                 * (N_BUCKETS - max_exact)).astype(jnp.int32)
            large = jnp.minimum(large, N_BUCKETS - 1)
            bucket = jnp.where(n < max_exact, n, large)
            for h in range(n_heads):
                b = jnp.zeros((strip, tq), F32)
                for kk in range(N_BUCKETS):
                    b = jnp.where(bucket == kk, rb_ref[kk, h], b)
                bias_ref[h, which, pl.ds(r0, strip), :] = (b - rb_ref[N_BUCKETS - 1, h]) * LOG2E
        return carry

    lax.fori_loop(0, tq // strip, fill, 0)
    for h in range(n_heads):
        far_ref[h] = jnp.full((1, LANES), rb_ref[N_BUCKETS - 1, h], F32) * LOG2E
    lam = (jnp.exp(jnp.sum(lq1_ref[...] * lk1_ref[...], axis=1, keepdims=True))
           - jnp.exp(jnp.sum(lq2_ref[...] * lk2_ref[...], axis=1, keepdims=True)) + lam_init)
    lam_ref[...] = jnp.broadcast_to(lam, (1, LANES))


def diff_prep(rel_bias, lq1, lk1, lq2, lk2, *, tq, lam_init):
    n_heads = rel_bias.shape[1]
    vec = lambda a: a.reshape(1, -1)
    vspec = pl.BlockSpec(memory_space=pltpu.VMEM)
    return pl.pallas_call(
        functools.partial(_diff_prep_kernel, tq=tq, lam_init=lam_init),
        in_specs=[pl.BlockSpec(memory_space=pltpu.SMEM), vspec, vspec, vspec, vspec],
        out_specs=[vspec, vspec, vspec],
        out_shape=[
            jax.ShapeDtypeStruct((n_heads, 2, tq, tq), F32),
            jax.ShapeDtypeStruct((n_heads, 1, LANES), F32),
            jax.ShapeDtypeStruct((1, LANES), F32),
        ],
        compiler_params=pltpu.CompilerParams(vmem_limit_bytes=VMEM_LIMIT), name="diff_prep",
    )(rel_bias, vec(lq1), vec(lk1), vec(lq2), vec(lk2))


def _diff_kernel(q_ref, k_ref, v_ref, bias_ref, far_ref, lam_ref, subln_ref, w_ref, o_ref, wcast_ref,
                 ka_s, m_s, l_s, acc_s, *, tq, lam_init):
    wcast_ref[...] = w_ref[...].astype(BF16)
    i = pl.program_id(2)
    lane = lax.broadcasted_iota(jnp.int32, (1, LANES), 1)
    halves = _halves(lane)

    @pl.when(i == 0)
    def _():
        k = k_ref[...]
        far = _split3(far_ref[0])
        for sub in range(2):
            ka_s[sub] = _augment(k, halves[sub], lane, HEAD_DIM * (1 - sub), far, True)

    q = q_ref[...]
    zero = jnp.zeros((1, LANES), F32)
    qa = [_augment(q, halves[sub], lane, HEAD_DIM * (1 - sub), (zero,) * 3, False) for sub in range(2)]
    _causal_attention(qa, ka_s, v_ref, bias_ref, i, tq, m_s, l_s, acc_s)
    o = _normalised(l_s, acc_s, 0) - lam_ref[...] * _normalised(l_s, acc_s, 1)
    o_ref[...] = (_rms(o, subln_ref[...]) * (1.0 - lam_init)).astype(o_ref.dtype)


def diff_attention(main, bias, far, lam, subln, w_ride, *, batch, seq, tq, q_col, k_col, v_col, lam_init):
    t = batch * seq
    nq = seq // tq
    n_heads = bias.shape[0]
    grid = (batch, n_heads, nq)
    ride_spec, ride_shape = _rider(w_ride, grid)
    return pl.pallas_call(
        functools.partial(_diff_kernel, tq=tq, lam_init=lam_init),
        grid=grid,
        in_specs=[
            pl.BlockSpec((tq, LANES), lambda b, h, i: (b * nq + i, q_col + h)),
            pl.BlockSpec((seq, LANES), lambda b, h, i: (b, k_col + h)),
            pl.BlockSpec((seq, LANES), lambda b, h, i: (b, v_col + h)),
            pl.BlockSpec((1, 2, tq, tq), lambda b, h, i: (h, 0, 0, 0)),
            pl.BlockSpec((1, 1, LANES), lambda b, h, i: (h, 0, 0)),
            pl.BlockSpec((1, LANES), lambda b, h, i: (0, 0)),
            pl.BlockSpec((1, LANES), lambda b, h, i: (0, 0)),
            ride_spec,
        ],
        out_specs=[pl.BlockSpec((tq, LANES), lambda b, h, i: (b * nq + i, h)), ride_spec],
        out_shape=[jax.ShapeDtypeStruct((t, n_heads * LANES), BF16), ride_shape],
        scratch_shapes=_attn_scratch(seq, tq),
        compiler_params=_params("parallel", "parallel", "arbitrary"), name="diff_attention",
    )(main, main, main, bias, far, lam, subln.reshape(1, LANES), w_ride)


def _router_kernel(x_ref, g_ref, wr_ref, meta_ref, wts_ref, before_ref, cnt_ref, carry_s, *, n_exp):
    tm = x_ref.shape[0]

    @pl.when(pl.program_id(0) == 0)
    def _():
        carry_s[...] = jnp.zeros_like(carry_s)

    before_ref[...] = jnp.broadcast_to(carry_s[...], before_ref.shape)

    h = _rms(x_ref[...], g_ref[...])
    h_hi = h.astype(BF16)
    h_lo = (h - h_hi.astype(F32)).astype(BF16)
    logits = _dot(h_hi, wr_ref[0]) + (_dot(h_hi, wr_ref[1]) + _dot(h_lo, wr_ref[0]))
    lane = lax.broadcasted_iota(jnp.int32, (tm, LANES), 1)
    lane_f = lane.astype(F32)
    logits = jnp.where(lane < n_exp, logits, -jnp.inf)
    m1 = jnp.max(logits, axis=1, keepdims=True)
    i1 = jnp.min(jnp.where(logits == m1, lane_f, float(LANES)), axis=1, keepdims=True)
    rest = jnp.where(lane_f == i1, -jnp.inf, logits)
    m2 = jnp.max(rest, axis=1, keepdims=True)
    i2 = jnp.min(jnp.where(rest == m2, lane_f, float(LANES)), axis=1, keepdims=True)
    e = jnp.exp(m2 - m1)
    w1 = 1.0 / (1.0 + e)
    w2 = e / (1.0 + e)
    sel1 = lane_f == i1
    sel2 = lane_f == i2
    onehot = jnp.where(sel1 | sel2, 1.0, 0.0)
    row = lax.broadcasted_iota(jnp.int32, (tm, tm), 0)
    col = lax.broadcasted_iota(jnp.int32, (tm, tm), 1)
    before = jnp.where(row > col, 1.0, 0.0).astype(BF16)
    local = _dot(before, onehot.astype(BF16))
    rank = local + carry_s[...]
    pick = lambda sel, val: jnp.sum(jnp.where(sel, val, 0.0), axis=1, keepdims=True)
    fields = [i1, i2, pick(sel1, rank), pick(sel2, rank), pick(sel1, local), pick(sel2, local)]
    meta = jnp.zeros((tm, LANES), F32)
    for idx, field in enumerate(fields):
        meta = jnp.where(lane == idx, field, meta)
    meta_ref[...] = meta.astype(jnp.int32)
    wts_ref[...] = jnp.where(lane == 0, w1, jnp.where(lane == 1, w2, 0.0))
    carry_s[...] += jnp.sum(onehot, axis=0, keepdims=True)
    cnt_ref[...] = carry_s[...]


def route_tokens(x, g, wr, *, tm, n_exp):
    t, d = x.shape
    return pl.pallas_call(
        functools.partial(_router_kernel, n_exp=n_exp),
        grid=(t // tm,),
        in_specs=[
            pl.BlockSpec((tm, d), lambda i: (i, 0)),
            pl.BlockSpec((1, d), lambda i: (0, 0)),
            pl.BlockSpec((2, d, LANES), lambda i: (0, 0, 0)),
        ],
        out_specs=[
            pl.BlockSpec((tm, LANES), lambda i: (i, 0)),
            pl.BlockSpec((tm, LANES), lambda i: (i, 0)),
            pl.BlockSpec((SUBLANES, LANES), lambda i: (i, 0)),
            pl.BlockSpec((1, LANES), lambda i: (0, 0)),
        ],
        out_shape=[
            jax.ShapeDtypeStruct((t, LANES), jnp.int32),
            jax.ShapeDtypeStruct((t, LANES), F32),
            jax.ShapeDtypeStruct((t // tm * SUBLANES, LANES), F32),
            jax.ShapeDtypeStruct((1, LANES), F32),
        ],
        scratch_shapes=[pltpu.VMEM((1, LANES), F32)],
        compiler_params=_params("arbitrary"), name="moe_router",
    )(x, g.reshape(1, d), wr)


def _dispatch_kernel(start_ref, shift_ref, keep_ref, x_ref, g_ref, meta_ref, zeros_hbm, xs_hbm,
                     stage_s, carry_s, sems, *, tm, n_exp):
    del zeros_hbm
    j = pl.program_id(0)
    slot = j % 2
    rows = tm + BF16_ROWS

    def block_copy(step, e, buf):
        first = pl.multiple_of(start_ref[step * n_exp + e], BF16_ROWS)
        return pltpu.make_async_copy(stage_s.at[buf, e], xs_hbm.at[pl.ds(first, rows)], sems.at[buf])

    @pl.when(j == 0)
    def _():
        carry_s[...] = jnp.zeros_like(carry_s)

    h = _rms(x_ref[...], g_ref[...]).astype(BF16)
    fields = meta_ref[...].astype(F32).T
    slot_row = lax.broadcasted_iota(jnp.int32, (rows, tm), 0).astype(F32)
    for e in range(n_exp):
        key = j * n_exp + e
        idx = jnp.where(fields[0:1] == e, fields[4:5], jnp.where(fields[1:2] == e, fields[5:6], -2.0 * rows))
        idx = idx + shift_ref[key].astype(F32)
        onehot = jnp.where(slot_row == idx, 1.0, 0.0).astype(BF16)
        stage_s[slot, e] = _dot(onehot, h).astype(BF16)
        stage_s[slot, e, 0:BF16_ROWS, :] += carry_s[e]
        keep = pl.multiple_of(keep_ref[key], BF16_ROWS)
        carry_s[e] = stage_s[slot, e, pl.ds(keep, BF16_ROWS), :]

    @pl.when(j > 0)
    def _():
        for e in range(n_exp):
            block_copy(j - 1, e, 1 - slot).wait()

    for e in range(n_exp):
        block_copy(j, e, slot).start()

    @pl.when(j == pl.num_programs(0) - 1)
    def _():
        for e in range(n_exp):
            block_copy(j, e, slot).wait()


def moe_dispatch(starts, shifts, keeps, x, g, meta, n_rows, *, tm, n_exp):
    t, d = x.shape
    zeros = jnp.zeros((n_rows, d), BF16)
    rows = tm + BF16_ROWS
    return pl.pallas_call(
        functools.partial(_dispatch_kernel, tm=tm, n_exp=n_exp),
        grid_spec=pltpu.PrefetchScalarGridSpec(
            num_scalar_prefetch=3, grid=(t // tm,),
            in_specs=[
                pl.BlockSpec((tm, d), lambda i, *_: (i, 0)),
                pl.BlockSpec((1, d), lambda i, *_: (0, 0)),
                pl.BlockSpec((tm, LANES), lambda i, *_: (i, 0)),
                pl.BlockSpec(memory_space=pl.ANY),
            ],
            out_specs=pl.BlockSpec(memory_space=pl.ANY),
            scratch_shapes=[pltpu.VMEM((2, n_exp, rows, d), BF16), pltpu.VMEM((n_exp, BF16_ROWS, d), BF16),
                            pltpu.SemaphoreType.DMA((2,))],
        ),
        out_shape=jax.ShapeDtypeStruct((n_rows, d), BF16),
        input_output_aliases={6: 0},
        compiler_params=_params("arbitrary"),
        name="moe_dispatch",
    )(starts, shifts, keeps, x, g.reshape(1, d), meta, zeros)


def _expert_kernel(te_ref, used_ref, h_ref, w13_ref, w2_ref, o_ref, gu_s, acc_s):
    del te_ref
    used = used_ref[pl.program_id(0)] != 0

    @pl.when(used)
    def _():
        acc_s[...] = jnp.zeros_like(acc_s)
        _swiglu_chunks(h_ref, w13_ref.at[0], w2_ref.at[0], acc_s, gu_s)
        o_ref[...] = acc_s[...].astype(o_ref.dtype)

    @pl.when(jnp.logical_not(used))
    def _():
        o_ref[...] = jnp.zeros_like(o_ref)


def moe_experts(tile_expert, tile_used, hs, w13, w2, *, tm):
    n_rows, d = hs.shape
    tf = w2.shape[2]
    resident = pl.Buffered(1)
    return pl.pallas_call(
        _expert_kernel,
        grid_spec=pltpu.PrefetchScalarGridSpec(
            num_scalar_prefetch=2, grid=(n_rows // tm,),
            in_specs=[
                pl.BlockSpec((tm, d), lambda i, te, tu: (i, 0)),
                pl.BlockSpec((1,) + w13.shape[1:], lambda i, te, tu: (te[i], 0, 0)),
                pl.BlockSpec((1,) + w2.shape[1:], lambda i, te, tu: (te[i], 0, 0, 0)),
            ],
            out_specs=pl.BlockSpec((tm, d), lambda i, te, tu: (i, 0)),
            scratch_shapes=[pltpu.VMEM((2, 2, tm, tf), F32), pltpu.VMEM((tm, d), F32)],
        ),
        out_shape=jax.ShapeDtypeStruct((n_rows, d), BF16),
        compiler_params=pltpu.CompilerParams(dimension_semantics=("arbitrary",), vmem_limit_bytes=EXPERT_VMEM_LIMIT),
        name="moe_experts",
    )(tile_expert, tile_used, hs, w13, w2)


def _combine_kernel(start_ref, shift_ref, tail_ref, x_ref, wts_ref, meta_ref, g_ref, y_hbm, o_ref, blk_s, sems, *,
                    tm, n_exp):
    j = pl.program_id(0)
    slot = j % 2
    rows = tm + BF16_ROWS

    def block_copy(step, e, buf):
        first = pl.multiple_of(start_ref[step * n_exp + e], BF16_ROWS)
        return pltpu.make_async_copy(y_hbm.at[pl.ds(first, rows)], blk_s.at[buf, e], sems.at[buf])

    def fetch(step, buf):
        for e in range(n_exp):
            block_copy(step, e, buf).start()

    @pl.when(j == 0)
    def _():
        fetch(0, 0)

    @pl.when(j + 1 < pl.num_programs(0))
    def _():
        fetch(j + 1, 1 - slot)

    for e in range(n_exp):
        block_copy(j, e, slot).wait()

    meta = meta_ref[...]
    wts = wts_ref[...]
    col = lax.broadcasted_iota(jnp.int32, (tm, tm), 1)
    tail_col = lax.broadcasted_iota(jnp.int32, (tm, BF16_ROWS), 1) + tm

    def pick(e):
        sel = [meta[:, k:k + 1] == e for k in range(TOP_K)]
        idx = jnp.where(sel[0], meta[:, 4:5], jnp.where(sel[1], meta[:, 5:6], -2 * rows)) + shift_ref[j * n_exp + e]
        w = jnp.where(sel[0], wts[:, 0:1], jnp.where(sel[1], wts[:, 1:2], 0.0))
        return idx, w

    acc = x_ref[...]
    for e in range(n_exp):
        idx, w = pick(e)
        onehot = jnp.where(col == idx, 1.0, 0.0).astype(BF16)
        acc = acc + w * _dot(onehot, blk_s[slot, e, 0:tm, :])
    o_ref[...] = acc

    for e in range(n_exp):
        @pl.when(tail_ref[j * n_exp + e] != 0)
        def _(e=e):
            idx, w = pick(e)
            onehot_tail = jnp.where(tail_col == idx, 1.0, 0.0).astype(BF16)
            o_ref[...] += w * _dot(onehot_tail, blk_s[slot, e, tm:rows, :])

    o_ref[...] = _rms(o_ref[...], g_ref[...])


def moe_combine(starts, shifts, tails, x, wts, meta, g, y, *, tm, n_exp):
    t, d = x.shape
    n_pre = 3
    return pl.pallas_call(
        functools.partial(_combine_kernel, tm=tm, n_exp=n_exp),
        grid_spec=pltpu.PrefetchScalarGridSpec(
            num_scalar_prefetch=n_pre, grid=(t // tm,),
            in_specs=[
                pl.BlockSpec((tm, d), lambda i, *_: (i, 0)),
                pl.BlockSpec((tm, LANES), lambda i, *_: (i, 0)),
                pl.BlockSpec((tm, LANES), lambda i, *_: (i, 0)),
                pl.BlockSpec((1, d), lambda i, *_: (0, 0)),
                pl.BlockSpec(memory_space=pl.ANY),
            ],
            out_specs=pl.BlockSpec((tm, d), lambda i, *_: (i, 0)),
            scratch_shapes=[pltpu.VMEM((2, n_exp, tm + BF16_ROWS, d), BF16), pltpu.SemaphoreType.DMA((2,))],
        ),
        out_shape=jax.ShapeDtypeStruct((t, d), F32),
        compiler_params=_params("arbitrary"), name="moe_combine",
    )(starts, shifts, tails, x, wts, meta, g.reshape(1, d), y)


def _pad_cols(w, n):
    return jnp.pad(w, ((0, 0), (0, n - w.shape[1])))


def _even_mixer(x, norm, w_in, b_f, sgu_norm, w_s, b_s, w_ride, *, batch, seq, tq):
    n_heads = b_f.shape[0]
    a_width = n_heads * HEAD_DIM
    b_width = w_s.shape[0] * LANES
    f0 = 3 * a_width
    w_main = jnp.concatenate([w_in[:, :f0], w_in[:, f0 + n_heads:]], axis=1)
    w_gate = _pad_cols(w_in[:, f0:f0 + n_heads], LANES)
    main, gate = norm_matmul(x, norm, [w_main.astype(BF16), w_gate.astype(BF16)], [BF16, F32],
                             tm=min(IN_PROJ_ROWS, x.shape[0]), name="even_in_proj",
                             scaled=(0, a_width, LOG2E * HEAD_DIM ** -0.5))
    c = gate_cumsum(gate, _pad_cols(b_f.reshape(1, -1), LANES), seq=seq)
    n_pairs = a_width // LANES
    a, w_cast = fox_attention(main, c, w_ride, batch=batch, seq=seq, tq=tq, q_col=0, k_col=n_pairs,
                              v_col=2 * n_pairs, n_pairs=n_pairs)
    u_col = f0 // b_width
    return (a, (main, u_col, u_col + 1, sgu_norm, w_s, b_s)), w_cast


def _odd_mixer(x, norm, w_in, conv_w, lq1, lk1, lq2, lk2, subln, rel_bias, lam_init, w_ride, *, batch, seq, tq):
    c_width = conv_w.shape[1]
    d_width = rel_bias.shape[1] * 2 * HEAD_DIM
    q0 = 3 * c_width
    (main,) = norm_matmul(x, norm, [w_in.astype(BF16)], [BF16], tm=min(IN_PROJ_ROWS, x.shape[0]), name="odd_in_proj",
                          scaled=(q0, q0 + d_width, LOG2E * HEAD_DIM ** -0.5))
    c_out = short_conv(main, conv_w, batch=batch, seq=seq)
    bias, far, lam = diff_prep(rel_bias, lq1, lk1, lq2, lk2, tq=tq, lam_init=lam_init)
    q_col = q0 // LANES
    n_heads = rel_bias.shape[1]
    d_out, w_cast = diff_attention(main, bias, far, lam, subln, w_ride, batch=batch, seq=seq, tq=tq, q_col=q_col,
                                   k_col=q_col + n_heads, v_col=q_col + 2 * n_heads, lam_init=lam_init)
    return (c_out, d_out), w_cast


def _moe_layer(x, norm, w_router, w13, w2, final_norm, *, tm_expert):
    t, d = x.shape
    n_exp = w_router.shape[1]
    tm_route = ROUTE_ROWS
    wr = _pad_cols(w_router, LANES)
    wr_hi = wr.astype(BF16)
    wr_split = jnp.stack([wr_hi, (wr - wr_hi.astype(F32)).astype(BF16)])
    meta, wts, before, counts = route_tokens(x, norm, wr_split, tm=tm_route, n_exp=n_exp)
    counts = counts[0, :n_exp].astype(jnp.int32)
    block_rows = tm_route + BF16_ROWS
    padded = (counts + block_rows + tm_expert - 1) // tm_expert * tm_expert
    ends = jnp.cumsum(padded)
    offsets = ends - padded
    n_rows = TOP_K * t + n_exp * (tm_expert + pl.cdiv(block_rows, tm_expert) * tm_expert)
    tile_start = jnp.arange(n_rows // tm_expert, dtype=jnp.int32) * tm_expert
    tile_expert = jnp.minimum(jnp.sum(tile_start[:, None] >= ends[None, :], axis=1), n_exp - 1).astype(jnp.int32)
    tile_used = (tile_start < (offsets + counts)[tile_expert]).astype(jnp.int32)
    before = before[::SUBLANES, :n_exp].astype(jnp.int32)
    in_tile = jnp.concatenate([before[1:], counts[None]]) - before
    first = offsets[None, :] + before
    starts = first // BF16_ROWS * BF16_ROWS
    shifts = first - starts
    keeps = (shifts + in_tile) // BF16_ROWS * BF16_ROWS
    tails = (shifts + in_tile > tm_route).astype(jnp.int32)
    flat = lambda a: a.reshape(-1).astype(jnp.int32)
    hs = moe_dispatch(flat(starts), flat(shifts), flat(keeps), x, norm, meta, n_rows, tm=tm_route, n_exp=n_exp)
    y = moe_experts(tile_expert, tile_used, hs, *_chunk_weights(w13, w2, EXPERT_CHUNK), tm=tm_expert)
    return moe_combine(flat(starts), flat(shifts), flat(tails), x, wts, meta, final_norm, y, tm=tm_route,
                       n_exp=n_exp)


def kernel(x, mem, rel_bias, mem_norm, final_norm, ev_norm, ev_w_in, ev_b_f, ev_sgu_norm, ev_w_s, ev_b_s, ev_w_out, ffn_w13, ffn_w2, od_norm, od_w_in, od_conv_w, od_lam_q1, od_lam_k1, od_lam_q2, od_lam_k2, od_subln, od_w_out, moe_router, moe_w13, moe_w2, x_norm, x_wq, x_wkv, x_wo, ffn_norm):
    batch, seq, d = x.shape
    mem_len = mem.shape[1]
    depth = x_norm.shape[0]
    assert depth == 2 and ev_norm.shape[0] == 1 and od_norm.shape[0] == 1
    x_heads, x_dh = 4, 128
    xf = x.reshape(batch * seq, d)
    wkv = jnp.concatenate([x_wkv[layer] for layer in range(depth)], axis=1).astype(BF16)
    (kv,) = norm_matmul(mem.reshape(batch * mem_len, d), mem_norm, [wkv], [BF16], tm=MEM_ROWS, name="mem_kv")

    def tail(xf, mixed, w_out, layer, ffn, name):
        return layer_tail(xf, *mixed, w_out.astype(BF16), x_norm[layer], x_wq[layer].astype(BF16), kv,
                          x_wo[layer].astype(BF16), ffn, tm=TAIL_ROWS, seq=seq, mem_len=mem_len, kv_col=layer,
                          n_heads=x_heads, dh=x_dh, name=name)

    e13, e2 = moe_w13[0], moe_w2[0]
    mixed, e13_bf16 = _even_mixer(xf, ev_norm[0], ev_w_in[0], ev_b_f[0], ev_sgu_norm[0], ev_w_s[0], ev_b_s[0],
                                  e13.reshape(-1, e13.shape[-1]), batch=batch, seq=seq, tq=ATTN_TILE)
    xf = tail(xf, mixed, ev_w_out[0], 0, (ffn_norm[0],) + _chunk_weights(ffn_w13[0], ffn_w2[0], FFN_CHUNK),
              "even_tail")
    lam_init = 0.8 - 0.6 * math.exp(-0.3 * 1)
    mixed, e2_bf16 = _odd_mixer(xf, od_norm[0], od_w_in[0], od_conv_w[0], od_lam_q1[0], od_lam_k1[0],
                                od_lam_q2[0], od_lam_k2[0], od_subln[0], rel_bias, lam_init,
                                e2.reshape(-1, e2.shape[-1]), batch=batch, seq=seq, tq=ATTN_TILE)
    xf = tail(xf, mixed, od_w_out[0], 1, None, "odd_tail")
    out = _moe_layer(xf, ffn_norm[1], moe_router[0], e13_bf16.reshape(e13.shape), e2_bf16.reshape(e2.shape),
                     final_norm, tm_expert=EXPERT_ROWS)
    return out.reshape(batch, seq, d)
```

```python
import functools
import math

import jax
import jax.numpy as jnp
from jax import lax
from jax.experimental import pallas as pl
from jax.experimental.pallas import tpu as pltpu

F32 = jnp.float32
BF16 = jnp.bfloat16
EPS = 1e-6
HEAD_DIM = 64
LANES = 128
SUBLANES = 8
BF16_ROWS = 16
N_BUCKETS = 32
MAX_DIST = 128
TOP_K = 2
LOG2E = 1.4426950408889634
ATTN_TILE = 512
IN_PROJ_ROWS = 1024
MEM_ROWS = 512
TAIL_ROWS = 512
ROUTE_ROWS = 256
EXPERT_ROWS = 512
FFN_CHUNK = 256
EXPERT_CHUNK = 512
VMEM_BYTES = 64 * 1024 * 1024
VMEM_LIMIT = VMEM_BYTES - 8 * 1024 * 1024
EXPERT_VMEM_LIMIT = VMEM_BYTES - 2 * 1024 * 1024


def _params(*sem):
    return pltpu.CompilerParams(dimension_semantics=sem, vmem_limit_bytes=VMEM_LIMIT)


def _rms(x, g):
    ms = jnp.mean(x * x, axis=-1, keepdims=True)
    return x * lax.rsqrt(ms + EPS) * g


def _dot(a, b):
    return jnp.dot(a, b, preferred_element_type=F32)


def _dot_nt(a, b):
    return lax.dot_general(a, b, (((1,), (1,)), ((), ())), preferred_element_type=F32)


def _norm_matmul_kernel(x_ref, g_ref, *refs, n_w, chunk, scaled):
    w_refs, o_refs = refs[:n_w], refs[n_w:]
    h = _rms(x_ref[...], g_ref[...]).astype(BF16)
    s0, s1, scale = scaled
    for k, (w_ref, o_ref) in enumerate(zip(w_refs, o_refs)):
        n = w_ref.shape[1]
        for c0 in range(0, n, chunk):
            c1 = min(c0 + chunk, n)
            y = _dot(h, w_ref[:, c0:c1])
            if k == 0 and s0 <= c0 and c1 <= s1:
                y = y * scale
            o_ref[:, c0:c1] = y.astype(o_ref.dtype)


def norm_matmul(x, g, ws, out_dtypes, *, tm, name, scaled=(0, 0, 1.0)):
    t, d = x.shape
    chunk = 512
    assert scaled[0] % chunk == 0 and scaled[1] % chunk == 0
    in_specs = [pl.BlockSpec((tm, d), lambda i: (i, 0)), pl.BlockSpec((1, d), lambda i: (0, 0))]
    in_specs += [pl.BlockSpec(w.shape, lambda i: (0, 0)) for w in ws]
    out_specs = [pl.BlockSpec((tm, w.shape[1]), lambda i: (i, 0)) for w in ws]
    out_shape = [jax.ShapeDtypeStruct((t, w.shape[1]), dt) for w, dt in zip(ws, out_dtypes)]
    return pl.pallas_call(
        functools.partial(_norm_matmul_kernel, n_w=len(ws), chunk=chunk, scaled=scaled),
        grid=(t // tm,), in_specs=in_specs, out_specs=out_specs, out_shape=out_shape,
        compiler_params=_params("parallel"), name=name,
    )(x, g.reshape(1, d), *ws)


def _gate_kernel(g_ref, b_ref, c_ref):
    s = g_ref.shape[0]
    row = lax.broadcasted_iota(jnp.int32, (LANES, LANES), 0)
    col = lax.broadcasted_iota(jnp.int32, (LANES, LANES), 1)
    tri = (row >= col).astype(F32)
    carry = jnp.zeros((1, LANES), F32)
    for blk in range(s // LANES):
        z = g_ref[blk * LANES:(blk + 1) * LANES, :] + b_ref[...]
        log_f = jnp.minimum(z, 0.0) - jnp.log1p(jnp.exp(-jnp.abs(z)))
        cs = jnp.dot(tri, log_f, precision=lax.Precision.HIGHEST, preferred_element_type=F32) + carry
        c_ref[blk * LANES:(blk + 1) * LANES, :] = cs
        carry = cs[LANES - 1:LANES, :]


def gate_cumsum(g, b, *, seq):
    t = g.shape[0]
    return pl.pallas_call(
        _gate_kernel, grid=(t // seq,),
        in_specs=[pl.BlockSpec((seq, LANES), lambda i: (i, 0)), pl.BlockSpec((1, LANES), lambda i: (0, 0))],
        out_specs=pl.BlockSpec((seq, LANES), lambda i: (i, 0)),
        out_shape=jax.ShapeDtypeStruct((t, LANES), F32),
        compiler_params=_params("parallel"), name="gate_cumsum",
    )(g, b)


def _split3(x):
    hi = x.astype(BF16).astype(F32)
    rest = x - hi
    mid = rest.astype(BF16).astype(F32)
    lo = (rest - mid).astype(BF16).astype(F32)
    return hi, mid, lo


def _augment(x, in_half, lane, base, pieces, pieces_first):
    n = len(pieces)
    p0, o0 = (base, base + n) if pieces_first else (base + n, base)
    aug = jnp.where((lane >= o0) & (lane < o0 + n), 1.0, 0.0)
    for idx, piece in enumerate(pieces):
        aug = jnp.where(lane == p0 + idx, piece, aug)
    return jnp.where(in_half, x, aug.astype(x.dtype))


def _halves(lane):
    return [(lane >= HEAD_DIM * hh) & (lane < HEAD_DIM * (hh + 1)) for hh in range(2)]


def _causal_attention(qa, ka_s, v_ref, bias_ref, i, tq):
    n_chunks = tq // LANES
    row = lax.broadcasted_iota(jnp.int32, (tq, tq), 0)
    col = lax.broadcasted_iota(jnp.int32, (tq, tq), 1)

    def scores(j):
        s = _dot_nt(qa, ka_s[j * tq:(j + 1) * tq, :])
        if bias_ref is not None and j >= i - 1:
            s = s + bias_ref[0, i - j]
        if j == i:
            s = jnp.where(row >= col, s, -jnp.inf)
        return [s[:, c * LANES:(c + 1) * LANES] for c in range(n_chunks)]

    m = jnp.full((tq, LANES), -jnp.inf, F32)
    for j in range(i + 1):
        for chunk in scores(j):
            m = jnp.maximum(m, chunk)
    m = jnp.broadcast_to(jnp.max(m, axis=1, keepdims=True), (tq, LANES))
    l = jnp.zeros((tq, LANES), F32)
    acc = jnp.zeros((tq, LANES), F32)
    for j in range(i + 1):
        ps = [jnp.exp2(chunk - m) for chunk in scores(j)]
        l = l + functools.reduce(lambda a, b: a + b, ps)
        acc = acc + _dot(jnp.concatenate(ps, axis=1).astype(BF16), v_ref[j * tq:(j + 1) * tq, :])
    return acc / jnp.sum(l, axis=1, keepdims=True)


def _rider(w, grid):
    steps = math.prod(grid)
    rows = w.shape[0] // steps
    assert rows * steps == w.shape[0] and rows % BF16_ROWS == 0

    def index(*ids):
        step = 0
        for n, idx in zip(grid, ids):
            step = step * n + idx
        return step, 0

    return pl.BlockSpec((rows, w.shape[1]), index), jax.ShapeDtypeStruct(w.shape, BF16)


def _fox_kernel(q_ref, k_ref, v_ref, c_ref, w_ref, o_ref, wcast_ref, ka_s, *, tq):
    wcast_ref[...] = w_ref[...].astype(BF16)
    hp = pl.program_id(1)
    lane = lax.broadcasted_iota(jnp.int32, (1, LANES), 1)
    halves = _halves(lane)

    def decay(c, hh):
        return jnp.sum(jnp.where(lane == 2 * hp + hh, c, 0.0), axis=1, keepdims=True) * LOG2E

    k = k_ref[...]
    c_all = c_ref[...]
    for hh in range(2):
        ka_s[hh] = _augment(k, halves[hh], lane, HEAD_DIM * (1 - hh), _split3(-decay(c_all, hh)), True)

    for i in range(q_ref.shape[0] // tq):
        rows = slice(i * tq, (i + 1) * tq)
        outs = []
        for hh in range(2):
            qa = _augment(q_ref[rows, :], halves[hh], lane, HEAD_DIM * (1 - hh), _split3(decay(c_ref[rows, :], hh)),
                          False)
            outs.append(_causal_attention(qa, ka_s.at[hh], v_ref, None, i, tq))
        o_ref[rows, :] = jnp.where(lane < HEAD_DIM, outs[0], outs[1]).astype(o_ref.dtype)


def fox_attention(qkv, c, w_ride, *, batch, seq, tq, q_col, k_col, v_col, n_pairs):
    t = batch * seq
    grid = (batch, n_pairs)
    ride_spec, ride_shape = _rider(w_ride, grid)
    block = lambda col: pl.BlockSpec((seq, LANES), lambda b, h: (b, col + h))
    return pl.pallas_call(
        functools.partial(_fox_kernel, tq=tq),
        grid=grid,
        in_specs=[block(q_col), block(k_col), block(v_col), pl.BlockSpec((seq, LANES), lambda b, h: (b, 0)),
                  ride_spec],
        out_specs=[block(0), ride_spec],
        out_shape=[jax.ShapeDtypeStruct((t, n_pairs * LANES), BF16), ride_shape],
        scratch_shapes=[pltpu.VMEM((2, seq, LANES), BF16)],
        compiler_params=_params("parallel", "parallel"), name="fox_attention",
    )(qkv, qkv, qkv, c, w_ride)


def _spatial_gate(u_ref, v_ref, norm_ref, ws_ref, bs_ref, o_ref):
    tb = u_ref.shape[0]
    n_groups, chunk, _ = ws_ref.shape
    row = lax.broadcasted_iota(jnp.int32, (chunk, chunk), 0)
    col = lax.broadcasted_iota(jnp.int32, (chunk, chunk), 1)
    tri = row >= col
    for g in range(n_groups):
        w = jnp.where(tri, ws_ref[g], 0.0).astype(BF16)
        bias = bs_ref[:, g:g + 1]
        gain = norm_ref[g:g + 1, :]
        for c in range(tb // chunk):
            rs = slice(c * chunk, (c + 1) * chunk)
            cs = slice(g * LANES, (g + 1) * LANES)
            vn = _rms(jax.nn.gelu(v_ref[rs, cs].astype(F32)), gain)
            mixed = _dot(w, vn.astype(BF16)) + bias
            o_ref[rs, cs] = (jax.nn.gelu(u_ref[rs, cs].astype(F32)) * mixed).astype(o_ref.dtype)


def _chunk_weights(w13, w2, tf):
    *lead, ff, d = w2.shape
    return w13.astype(BF16), w2.astype(BF16).reshape(*lead, ff // tf, tf, d)


def _swiglu_chunks(h_s, w13, w2, o_ref, gu_s):
    n, tf, _ = w2.shape

    def project(c, slot):
        for part in range(2):
            cols = pl.ds(pl.multiple_of((part * n + c) * tf, tf), tf)
            gu_s[slot, part] = _dot(h_s[...], w13[:, cols])

    def consume(c, slot):
        gate = gu_s[slot, 0]
        act = (gate * jax.nn.sigmoid(gate) * gu_s[slot, 1]).astype(BF16)
        o_ref[...] += _dot(act, w2[c])

    def pair(k, carry):
        c = 2 * k
        project(c + 1, 1)
        consume(c, 0)
        project(c + 2, 0)
        consume(c + 1, 1)
        return carry

    project(0, 0)
    lax.fori_loop(0, (n - 1) // 2, pair, 0)
    if n % 2 == 0:
        project(n - 1, 1)
        consume(n - 2, 0)
        consume(n - 1, 1)
    else:
        consume(n - 1, 0)


def _swiglu_scratch(tm, d, tf):
    return [pltpu.VMEM((tm, d), BF16), pltpu.VMEM((2, 2, tm, tf), F32)]


def _tail_kernel(*refs, n_heads, dh, gated, ffn):
    refs = list(refs)
    take = lambda n: [refs.pop(0) for _ in range(n)]
    x_ref, a_ref = take(2)
    b_in = take(5 if gated else 1)
    wa_ref, wb_ref, gx_ref, wq_ref, kv_ref, wo_ref = take(6)
    ffn_in = take(3 if ffn else 0)
    (o_ref,) = take(1)
    if gated:
        (b_ref,) = take(1)
        _spatial_gate(*b_in, b_ref)
    else:
        (b_ref,) = b_in
    rest = ffn_in + [o_ref] + refs
    x = x_ref[...] + _dot(a_ref[...], wa_ref[...]) + _dot(b_ref[...], wb_ref[...])
    q = _dot(_rms(x, gx_ref[...]).astype(BF16), wq_ref[...]).astype(BF16)
    width = n_heads * dh
    outs = []
    for hd in range(n_heads):
        cs = slice(hd * dh, (hd + 1) * dh)
        s = _dot_nt(q[:, cs], kv_ref[:, cs]) * (dh ** -0.5)
        p = jnp.exp(s - jnp.max(s, axis=1, keepdims=True))
        p = p / jnp.sum(p, axis=1, keepdims=True)
        outs.append(_dot(p.astype(BF16), kv_ref[:, width + hd * dh:width + (hd + 1) * dh]).astype(BF16))
    x = x + _dot(jnp.concatenate(outs, axis=1), wo_ref[...])
    if len(rest) == 1:
        (o_ref,) = rest
        o_ref[...] = x
    else:
        gf_ref, w13_ref, w2_ref, o_ref, h_s, gu_s = rest
        h_s[...] = _rms(x, gf_ref[...]).astype(BF16)
        o_ref[...] = x
        _swiglu_chunks(h_s, w13_ref, w2_ref, o_ref, gu_s)


def layer_tail(x, a, b, w_out, gx, wq, kv, wo, ffn=None, *, tm, seq, mem_len, kv_col, n_heads, dh, name):
    t, d = x.shape
    per_b = seq // tm
    resident = pl.Buffered(1)
    const = lambda arr: pl.BlockSpec(arr.shape, lambda i: (0,) * arr.ndim, pipeline_mode=resident)
    rows = lambda arr: pl.BlockSpec((tm, arr.shape[1]), lambda i: (i, 0))
    vec = lambda g: g.reshape(1, d)
    gated = isinstance(b, tuple)
    scratch = []
    if gated:
        proj, u_col, v_col, sgu_norm, w_s, b_s = b
        b_width = w_s.shape[0] * LANES
        b_args = [proj, proj, sgu_norm, w_s, b_s.T]
        b_specs = [pl.BlockSpec((tm, b_width), lambda i: (i, u_col)), pl.BlockSpec((tm, b_width), lambda i: (i, v_col)),
                   const(sgu_norm), const(w_s), const(b_s.T)]
        scratch.append(pltpu.VMEM((tm, b_width), BF16))
    else:
        b_width = b.shape[1]
        b_args, b_specs = [b], [rows(b)]
    wa, wb = w_out[:a.shape[1]], w_out[a.shape[1]:]
    assert wb.shape[0] == b_width
    args = [x, a] + b_args + [wa, wb, vec(gx), wq, kv, wo]
    in_specs = [rows(x), rows(a)] + b_specs + [
        const(wa), const(wb), const(vec(gx)), const(wq),
        pl.BlockSpec((mem_len, 2 * n_heads * dh), lambda i: (i // per_b, kv_col)), const(wo)]
    if ffn is not None:
        gf, w13, w2 = ffn
        args += [vec(gf), w13, w2]
        in_specs += [const(vec(gf)), const(w13), const(w2)]
        scratch += _swiglu_scratch(tm, d, w2.shape[1])
    return pl.pallas_call(
        functools.partial(_tail_kernel, n_heads=n_heads, dh=dh, gated=gated, ffn=ffn is not None), grid=(t // tm,),
        in_specs=in_specs, out_specs=rows(x), out_shape=jax.ShapeDtypeStruct((t, d), F32),
        scratch_shapes=scratch, compiler_params=_params("parallel"), name=name,
    )(*args)


def _conv_kernel(bg_ref, cg_ref, xi_ref, w_ref, o_ref):
    s, width = o_ref.shape
    n_taps = w_ref.shape[0]
    xc = cg_ref[...].astype(F32) * xi_ref[...].astype(F32)
    row = lax.broadcasted_iota(jnp.int32, (s, width), 0)
    y = w_ref[n_taps - 1:n_taps, :] * xc
    for back in range(1, n_taps):
        shifted = jnp.where(row >= back, pltpu.roll(xc, back, axis=0), 0.0)
        y = y + w_ref[n_taps - 1 - back:n_taps - back, :] * shifted
    o_ref[...] = (bg_ref[...].astype(F32) * y).astype(o_ref.dtype)


def short_conv(main, conv_w, *, batch, seq):
    width = conv_w.shape[1]
    return pl.pallas_call(
        _conv_kernel, grid=(batch,),
        in_specs=[
            pl.BlockSpec((seq, width), lambda b: (b, 0)),
            pl.BlockSpec((seq, width), lambda b: (b, 1)),
            pl.BlockSpec((seq, width), lambda b: (b, 2)),
            pl.BlockSpec(conv_w.shape, lambda b: (0, 0)),
        ],
        out_specs=pl.BlockSpec((seq, width), lambda b: (b, 0)),
        out_shape=jax.ShapeDtypeStruct((batch * seq, width), BF16),
        compiler_params=_params("parallel"), name="short_conv",
    )(main, main, main, conv_w)


def _diff_prep_kernel(rb_ref, lq1_ref, lk1_ref, lq2_ref, lk2_ref, bias_ref, far_ref, lam_ref, *, tq, lam_init):
    n_heads = bias_ref.shape[0]
    strip = 32
    row = lax.broadcasted_iota(jnp.int32, (strip, tq), 0)
    col = lax.broadcasted_iota(jnp.int32, (strip, tq), 1)
    max_exact = N_BUCKETS // 2

    def fill(r, carry):
        r0 = pl.multiple_of(r * strip, strip)
        for which in range(2):
            n = jnp.maximum(row + r0 - col + which * tq, 0)
            nf = jnp.maximum(n, 1).astype(F32)
            large = max_exact + (jnp.log(nf / max_exact) / math.log(MAX_DIST / max_exact)
                                 * (N_BUCKETS - max_exact)).astype(jnp.int32)
            large = jnp.minimum(large, N_BUCKETS - 1)
            bucket = jnp.where(n < max_exact, n, large)
            for h in range(n_heads):
                b = jnp.zeros((strip, tq), F32)
                for kk in range(N_BUCKETS):
                    b = jnp.where(bucket == kk, rb_ref[kk, h], b)
                bias_ref[h, which, pl.ds(r0, strip), :] = (b - rb_ref[N_BUCKETS - 1, h]) * LOG2E
        return carry

    lax.fori_loop(0, tq // strip, fill, 0)
    for h in range(n_heads):
        far_ref[h] = jnp.full((1, LANES), rb_ref[N_BUCKETS - 1, h], F32) * LOG2E
    lam = (jnp.exp(jnp.sum(lq1_ref[...] * lk1_ref[...], axis=1, keepdims=True))
           - jnp.exp(jnp.sum(lq2_ref[...] * lk2_ref[...], axis=1, keepdims=True)) + lam_init)
    lam_ref[...] = jnp.broadcast_to(lam, (1, LANES))


def diff_prep(rel_bias, lq1, lk1, lq2, lk2, *, tq, lam_init):
    n_heads = rel_bias.shape[1]
    vec = lambda a: a.reshape(1, -1)
    vspec = pl.BlockSpec(memory_space=pltpu.VMEM)
    return pl.pallas_call(
        functools.partial(_diff_prep_kernel, tq=tq, lam_init=lam_init),
        in_specs=[pl.BlockSpec(memory_space=pltpu.SMEM), vspec, vspec, vspec, vspec],
        out_specs=[vspec, vspec, vspec],
        out_shape=[
            jax.ShapeDtypeStruct((n_heads, 2, tq, tq), F32),
            jax.ShapeDtypeStruct((n_heads, 1, LANES), F32),
            jax.ShapeDtypeStruct((1, LANES), F32),
        ],
        compiler_params=pltpu.CompilerParams(vmem_limit_bytes=VMEM_LIMIT), name="diff_prep",
    )(rel_bias, vec(lq1), vec(lk1), vec(lq2), vec(lk2))


def _diff_kernel(q_ref, k_ref, v_ref, bias_ref, far_ref, lam_ref, subln_ref, w_ref, o_ref, wcast_ref,
                 ka_s, *, tq, lam_init):
    wcast_ref[...] = w_ref[...].astype(BF16)
    lane = lax.broadcasted_iota(jnp.int32, (1, LANES), 1)
    halves = _halves(lane)
    k = k_ref[...]
    far = _split3(far_ref[0])
    for sub in range(2):
        ka_s[sub] = _augment(k, halves[sub], lane, HEAD_DIM * (1 - sub), far, True)

    zero = jnp.zeros((1, LANES), F32)
    for i in range(q_ref.shape[0] // tq):
        rows = slice(i * tq, (i + 1) * tq)
        outs = []
        for sub in range(2):
            qa = _augment(q_ref[rows, :], halves[sub], lane, HEAD_DIM * (1 - sub), (zero,) * 3, False)
            outs.append(_causal_attention(qa, ka_s.at[sub], v_ref, bias_ref, i, tq))
        o = outs[0] - lam_ref[...] * outs[1]
        o_ref[rows, :] = (_rms(o, subln_ref[...]) * (1.0 - lam_init)).astype(o_ref.dtype)


def diff_attention(main, bias, far, lam, subln, w_ride, *, batch, seq, tq, q_col, k_col, v_col, lam_init):
    t = batch * seq
    n_heads = bias.shape[0]
    grid = (batch, n_heads)
    ride_spec, ride_shape = _rider(w_ride, grid)
    block = lambda col: pl.BlockSpec((seq, LANES), lambda b, h: (b, col + h))
    return pl.pallas_call(
        functools.partial(_diff_kernel, tq=tq, lam_init=lam_init),
        grid=grid,
        in_specs=[
            block(q_col), block(k_col), block(v_col),
            pl.BlockSpec((1, 2, tq, tq), lambda b, h: (h, 0, 0, 0)),
            pl.BlockSpec((1, 1, LANES), lambda b, h: (h, 0, 0)),
            pl.BlockSpec((1, LANES), lambda b, h: (0, 0)),
            pl.BlockSpec((1, LANES), lambda b, h: (0, 0)),
            ride_spec,
        ],
        out_specs=[block(0), ride_spec],
        out_shape=[jax.ShapeDtypeStruct((t, n_heads * LANES), BF16), ride_shape],
        scratch_shapes=[pltpu.VMEM((2, seq, LANES), BF16)],
        compiler_params=_params("parallel", "parallel"), name="diff_attention",
    )(main, main, main, bias, far, lam, subln.reshape(1, LANES), w_ride)


def _router_kernel(x_ref, g_ref, wr_ref, meta_ref, wts_ref, before_ref, cnt_ref, carry_s, *, n_exp):
    tm = x_ref.shape[0]

    @pl.when(pl.program_id(0) == 0)
    def _():
        carry_s[...] = jnp.zeros_like(carry_s)

    before_ref[...] = jnp.broadcast_to(carry_s[...], before_ref.shape)

    h = _rms(x_ref[...], g_ref[...])
    h_hi = h.astype(BF16)
    h_lo = (h - h_hi.astype(F32)).astype(BF16)
    logits = _dot(h_hi, wr_ref[0]) + (_dot(h_hi, wr_ref[1]) + _dot(h_lo, wr_ref[0]))
    lane = lax.broadcasted_iota(jnp.int32, (tm, LANES), 1)
    lane_f = lane.astype(F32)
    logits = jnp.where(lane < n_exp, logits, -jnp.inf)
    m1 = jnp.max(logits, axis=1, keepdims=True)
    i1 = jnp.min(jnp.where(logits == m1, lane_f, float(LANES)), axis=1, keepdims=True)
    rest = jnp.where(lane_f == i1, -jnp.inf, logits)
    m2 = jnp.max(rest, axis=1, keepdims=True)
    i2 = jnp.min(jnp.where(rest == m2, lane_f, float(LANES)), axis=1, keepdims=True)
    e = jnp.exp(m2 - m1)
    w1 = 1.0 / (1.0 + e)
    w2 = e / (1.0 + e)
    sel1 = lane_f == i1
    sel2 = lane_f == i2
    onehot = jnp.where(sel1 | sel2, 1.0, 0.0)
    row = lax.broadcasted_iota(jnp.int32, (tm, tm), 0)
    col = lax.broadcasted_iota(jnp.int32, (tm, tm), 1)
    before = jnp.where(row > col, 1.0, 0.0).astype(BF16)
    local = _dot(before, onehot.astype(BF16))
    rank = local + carry_s[...]
    pick = lambda sel, val: jnp.sum(jnp.where(sel, val, 0.0), axis=1, keepdims=True)
    fields = [i1, i2, pick(sel1, rank), pick(sel2, rank), pick(sel1, local), pick(sel2, local)]
    meta = jnp.zeros((tm, LANES), F32)
    for idx, field in enumerate(fields):
        meta = jnp.where(lane == idx, field, meta)
    meta_ref[...] = meta.astype(jnp.int32)
    wts_ref[...] = jnp.where(lane == 0, w1, jnp.where(lane == 1, w2, 0.0))
    carry_s[...] += jnp.sum(onehot, axis=0, keepdims=True)
    cnt_ref[...] = carry_s[...]


def route_tokens(x, g, wr, *, tm, n_exp):
    t, d = x.shape
    return pl.pallas_call(
        functools.partial(_router_kernel, n_exp=n_exp),
        grid=(t // tm,),
        in_specs=[
            pl.BlockSpec((tm, d), lambda i: (i, 0)),
            pl.BlockSpec((1, d), lambda i: (0, 0)),
            pl.BlockSpec((2, d, LANES), lambda i: (0, 0, 0)),
        ],
        out_specs=[
            pl.BlockSpec((tm, LANES), lambda i: (i, 0)),
            pl.BlockSpec((tm, LANES), lambda i: (i, 0)),
            pl.BlockSpec((SUBLANES, LANES), lambda i: (i, 0)),
            pl.BlockSpec((1, LANES), lambda i: (0, 0)),
        ],
        out_shape=[
            jax.ShapeDtypeStruct((t, LANES), jnp.int32),
            jax.ShapeDtypeStruct((t, LANES), F32),
            jax.ShapeDtypeStruct((t // tm * SUBLANES, LANES), F32),
            jax.ShapeDtypeStruct((1, LANES), F32),
        ],
        scratch_shapes=[pltpu.VMEM((1, LANES), F32)],
        compiler_params=_params("arbitrary"), name="moe_router",
    )(x, g.reshape(1, d), wr)


def _dispatch_kernel(start_ref, shift_ref, keep_ref, x_ref, g_ref, meta_ref, zeros_hbm, xs_hbm,
                     stage_s, carry_s, sems, *, tm, n_exp):
    del zeros_hbm
    j = pl.program_id(0)
    slot = j % 2
    rows = tm + BF16_ROWS

    def block_copy(step, e, buf):
        first = pl.multiple_of(start_ref[step * n_exp + e], BF16_ROWS)
        return pltpu.make_async_copy(stage_s.at[buf, e], xs_hbm.at[pl.ds(first, rows)], sems.at[buf])

    @pl.when(j == 0)
    def _():
        carry_s[...] = jnp.zeros_like(carry_s)

    h = _rms(x_ref[...], g_ref[...]).astype(BF16)
    fields = meta_ref[...].astype(F32).T
    slot_row = lax.broadcasted_iota(jnp.int32, (rows, tm), 0).astype(F32)
    for e in range(n_exp):
        key = j * n_exp + e
        idx = jnp.where(fields[0:1] == e, fields[4:5], jnp.where(fields[1:2] == e, fields[5:6], -2.0 * rows))
        idx = idx + shift_ref[key].astype(F32)
        onehot = jnp.where(slot_row == idx, 1.0, 0.0).astype(BF16)
        stage_s[slot, e] = _dot(onehot, h).astype(BF16)
        stage_s[slot, e, 0:BF16_ROWS, :] += carry_s[e]
        keep = pl.multiple_of(keep_ref[key], BF16_ROWS)
        carry_s[e] = stage_s[slot, e, pl.ds(keep, BF16_ROWS), :]

    @pl.when(j > 0)
    def _():
        for e in range(n_exp):
            block_copy(j - 1, e, 1 - slot).wait()

    for e in range(n_exp):
        block_copy(j, e, slot).start()

    @pl.when(j == pl.num_programs(0) - 1)
    def _():
        for e in range(n_exp):
            block_copy(j, e, slot).wait()


def moe_dispatch(starts, shifts, keeps, x, g, meta, n_rows, *, tm, n_exp):
    t, d = x.shape
    zeros = jnp.zeros((n_rows, d), BF16)
    rows = tm + BF16_ROWS
    return pl.pallas_call(
        functools.partial(_dispatch_kernel, tm=tm, n_exp=n_exp),
        grid_spec=pltpu.PrefetchScalarGridSpec(
            num_scalar_prefetch=3, grid=(t // tm,),
            in_specs=[
                pl.BlockSpec((tm, d), lambda i, *_: (i, 0)),
                pl.BlockSpec((1, d), lambda i, *_: (0, 0)),
                pl.BlockSpec((tm, LANES), lambda i, *_: (i, 0)),
                pl.BlockSpec(memory_space=pl.ANY),
            ],
            out_specs=pl.BlockSpec(memory_space=pl.ANY),
            scratch_shapes=[pltpu.VMEM((2, n_exp, rows, d), BF16), pltpu.VMEM((n_exp, BF16_ROWS, d), BF16),
                            pltpu.SemaphoreType.DMA((2,))],
        ),
        out_shape=jax.ShapeDtypeStruct((n_rows, d), BF16),
        input_output_aliases={6: 0},
        compiler_params=_params("arbitrary"),
        name="moe_dispatch",
    )(starts, shifts, keeps, x, g.reshape(1, d), meta, zeros)


def _expert_kernel(te_ref, used_ref, h_ref, w13_ref, w2_ref, o_ref, gu_s, acc_s):
    del te_ref
    used = used_ref[pl.program_id(0)] != 0

    @pl.when(used)
    def _():
        acc_s[...] = jnp.zeros_like(acc_s)
        _swiglu_chunks(h_ref, w13_ref.at[0], w2_ref.at[0], acc_s, gu_s)
        o_ref[...] = acc_s[...].astype(o_ref.dtype)

    @pl.when(jnp.logical_not(used))
    def _():
        o_ref[...] = jnp.zeros_like(o_ref)


def moe_experts(tile_expert, tile_used, hs, w13, w2, *, tm):
    n_rows, d = hs.shape
    tf = w2.shape[2]
    resident = pl.Buffered(1)
    return pl.pallas_call(
        _expert_kernel,
        grid_spec=pltpu.PrefetchScalarGridSpec(
            num_scalar_prefetch=2, grid=(n_rows // tm,),
            in_specs=[
                pl.BlockSpec((tm, d), lambda i, te, tu: (i, 0)),
                pl.BlockSpec((1,) + w13.shape[1:], lambda i, te, tu: (te[i], 0, 0)),
                pl.BlockSpec((1,) + w2.shape[1:], lambda i, te, tu: (te[i], 0, 0, 0)),
            ],
            out_specs=pl.BlockSpec((tm, d), lambda i, te, tu: (i, 0)),
            scratch_shapes=[pltpu.VMEM((2, 2, tm, tf), F32), pltpu.VMEM((tm, d), F32)],
        ),
        out_shape=jax.ShapeDtypeStruct((n_rows, d), BF16),
        compiler_params=pltpu.CompilerParams(dimension_semantics=("arbitrary",), vmem_limit_bytes=EXPERT_VMEM_LIMIT),
        name="moe_experts",
    )(tile_expert, tile_used, hs, w13, w2)


def _combine_kernel(start_ref, shift_ref, tail_ref, x_ref, wts_ref, meta_ref, g_ref, y_hbm, o_ref, blk_s, sems, *,
                    tm, n_exp):
    j = pl.program_id(0)
    slot = j % 2
    rows = tm + BF16_ROWS

    def block_copy(step, e, buf):
        first = pl.multiple_of(start_ref[step * n_exp + e], BF16_ROWS)
        return pltpu.make_async_copy(y_hbm.at[pl.ds(first, rows)], blk_s.at[buf, e], sems.at[buf])

    def fetch(step, buf):
        for e in range(n_exp):
            block_copy(step, e, buf).start()

    @pl.when(j == 0)
    def _():
        fetch(0, 0)

    @pl.when(j + 1 < pl.num_programs(0))
    def _():
        fetch(j + 1, 1 - slot)

    for e in range(n_exp):
        block_copy(j, e, slot).wait()

    meta = meta_ref[...]
    wts = wts_ref[...]
    col = lax.broadcasted_iota(jnp.int32, (tm, tm), 1)
    tail_col = lax.broadcasted_iota(jnp.int32, (tm, BF16_ROWS), 1) + tm

    def pick(e):
        sel = [meta[:, k:k + 1] == e for k in range(TOP_K)]
        idx = jnp.where(sel[0], meta[:, 4:5], jnp.where(sel[1], meta[:, 5:6], -2 * rows)) + shift_ref[j * n_exp + e]
        w = jnp.where(sel[0], wts[:, 0:1], jnp.where(sel[1], wts[:, 1:2], 0.0))
        return idx, w

    acc = x_ref[...]
    for e in range(n_exp):
        idx, w = pick(e)
        onehot = jnp.where(col == idx, 1.0, 0.0).astype(BF16)
        acc = acc + w * _dot(onehot, blk_s[slot, e, 0:tm, :])
    o_ref[...] = acc

    for e in range(n_exp):
        @pl.when(tail_ref[j * n_exp + e] != 0)
        def _(e=e):
            idx, w = pick(e)
            onehot_tail = jnp.where(tail_col == idx, 1.0, 0.0).astype(BF16)
            o_ref[...] += w * _dot(onehot_tail, blk_s[slot, e, tm:rows, :])

    o_ref[...] = _rms(o_ref[...], g_ref[...])


def moe_combine(starts, shifts, tails, x, wts, meta, g, y, *, tm, n_exp):
    t, d = x.shape
    n_pre = 3
    return pl.pallas_call(
        functools.partial(_combine_kernel, tm=tm, n_exp=n_exp),
        grid_spec=pltpu.PrefetchScalarGridSpec(
            num_scalar_prefetch=n_pre, grid=(t // tm,),
            in_specs=[
                pl.BlockSpec((tm, d), lambda i, *_: (i, 0)),
                pl.BlockSpec((tm, LANES), lambda i, *_: (i, 0)),
                pl.BlockSpec((tm, LANES), lambda i, *_: (i, 0)),
                pl.BlockSpec((1, d), lambda i, *_: (0, 0)),
                pl.BlockSpec(memory_space=pl.ANY),
            ],
            out_specs=pl.BlockSpec((tm, d), lambda i, *_: (i, 0)),
            scratch_shapes=[pltpu.VMEM((2, n_exp, tm + BF16_ROWS, d), BF16), pltpu.SemaphoreType.DMA((2,))],
        ),
        out_shape=jax.ShapeDtypeStruct((t, d), F32),
        compiler_params=_params("arbitrary"), name="moe_combine",
    )(starts, shifts, tails, x, wts, meta, g.reshape(1, d), y)


def _pad_cols(w, n):
    return jnp.pad(w, ((0, 0), (0, n - w.shape[1])))


def _even_mixer(x, norm, w_in, b_f, sgu_norm, w_s, b_s, w_ride, *, batch, seq, tq):
    n_heads = b_f.shape[0]
    a_width = n_heads * HEAD_DIM
    b_width = w_s.shape[0] * LANES
    f0 = 3 * a_width
    w_main = jnp.concatenate([w_in[:, :f0], w_in[:, f0 + n_heads:]], axis=1)
    w_gate = _pad_cols(w_in[:, f0:f0 + n_heads], LANES)
    main, gate = norm_matmul(x, norm, [w_main.astype(BF16), w_gate.astype(BF16)], [BF16, F32],
                             tm=min(IN_PROJ_ROWS, x.shape[0]), name="even_in_proj",
                             scaled=(0, a_width, LOG2E * HEAD_DIM ** -0.5))
    c = gate_cumsum(gate, _pad_cols(b_f.reshape(1, -1), LANES), seq=seq)
    n_pairs = a_width // LANES
    a, w_cast = fox_attention(main, c, w_ride, batch=batch, seq=seq, tq=tq, q_col=0, k_col=n_pairs,
                              v_col=2 * n_pairs, n_pairs=n_pairs)
    u_col = f0 // b_width
    return (a, (main, u_col, u_col + 1, sgu_norm, w_s, b_s)), w_cast


def _odd_mixer(x, norm, w_in, conv_w, lq1, lk1, lq2, lk2, subln, rel_bias, lam_init, w_ride, *, batch, seq, tq):
    c_width = conv_w.shape[1]
    d_width = rel_bias.shape[1] * 2 * HEAD_DIM
    q0 = 3 * c_width
    (main,) = norm_matmul(x, norm, [w_in.astype(BF16)], [BF16], tm=min(IN_PROJ_ROWS, x.shape[0]), name="odd_in_proj",
                          scaled=(q0, q0 + d_width, LOG2E * HEAD_DIM ** -0.5))
    c_out = short_conv(main, conv_w, batch=batch, seq=seq)
    bias, far, lam = diff_prep(rel_bias, lq1, lk1, lq2, lk2, tq=tq, lam_init=lam_init)
    q_col = q0 // LANES
    n_heads = rel_bias.shape[1]
    d_out, w_cast = diff_attention(main, bias, far, lam, subln, w_ride, batch=batch, seq=seq, tq=tq, q_col=q_col,
                                   k_col=q_col + n_heads, v_col=q_col + 2 * n_heads, lam_init=lam_init)
    return (c_out, d_out), w_cast


def _moe_layer(x, norm, w_router, w13, w2, final_norm, *, tm_expert):
    t, d = x.shape
    n_exp = w_router.shape[1]
    tm_route = ROUTE_ROWS
    wr = _pad_cols(w_router, LANES)
    wr_hi = wr.astype(BF16)
    wr_split = jnp.stack([wr_hi, (wr - wr_hi.astype(F32)).astype(BF16)])
    meta, wts, before, counts = route_tokens(x, norm, wr_split, tm=tm_route, n_exp=n_exp)
    counts = counts[0, :n_exp].astype(jnp.int32)
    block_rows = tm_route + BF16_ROWS
    padded = (counts + block_rows + tm_expert - 1) // tm_expert * tm_expert
    ends = jnp.cumsum(padded)
    offsets = ends - padded
    n_rows = TOP_K * t + n_exp * (tm_expert + pl.cdiv(block_rows, tm_expert) * tm_expert)
    tile_start = jnp.arange(n_rows // tm_expert, dtype=jnp.int32) * tm_expert
    tile_expert = jnp.minimum(jnp.sum(tile_start[:, None] >= ends[None, :], axis=1), n_exp - 1).astype(jnp.int32)
    tile_used = (tile_start < (offsets + counts)[tile_expert]).astype(jnp.int32)
    before = before[::SUBLANES, :n_exp].astype(jnp.int32)
    in_tile = jnp.concatenate([before[1:], counts[None]]) - before
    first = offsets[None, :] + before
    starts = first // BF16_ROWS * BF16_ROWS
    shifts = first - starts
    keeps = (shifts + in_tile) // BF16_ROWS * BF16_ROWS
    tails = (shifts + in_tile > tm_route).astype(jnp.int32)
    flat = lambda a: a.reshape(-1).astype(jnp.int32)
    hs = moe_dispatch(flat(starts), flat(shifts), flat(keeps), x, norm, meta, n_rows, tm=tm_route, n_exp=n_exp)
    y = moe_experts(tile_expert, tile_used, hs, *_chunk_weights(w13, w2, EXPERT_CHUNK), tm=tm_expert)
    return moe_combine(flat(starts), flat(shifts), flat(tails), x, wts, meta, final_norm, y, tm=tm_route,
                       n_exp=n_exp)


def kernel(x, mem, rel_bias, mem_norm, final_norm, ev_norm, ev_w_in, ev_b_f, ev_sgu_norm, ev_w_s, ev_b_s, ev_w_out, ffn_w13, ffn_w2, od_norm, od_w_in, od_conv_w, od_lam_q1, od_lam_k1, od_lam_q2, od_lam_k2, od_subln, od_w_out, moe_router, moe_w13, moe_w2, x_norm, x_wq, x_wkv, x_wo, ffn_norm):
    batch, seq, d = x.shape
    mem_len = mem.shape[1]
    depth = x_norm.shape[0]
    assert depth == 2 and ev_norm.shape[0] == 1 and od_norm.shape[0] == 1
    x_heads, x_dh = 4, 128
    xf = x.reshape(batch * seq, d)
    wkv = jnp.concatenate([x_wkv[layer] for layer in range(depth)], axis=1).astype(BF16)
    (kv,) = norm_matmul(mem.reshape(batch * mem_len, d), mem_norm, [wkv], [BF16], tm=MEM_ROWS, name="mem_kv")

    def tail(xf, mixed, w_out, layer, ffn, name):
        return layer_tail(xf, *mixed, w_out.astype(BF16), x_norm[layer], x_wq[layer].astype(BF16), kv,
                          x_wo[layer].astype(BF16), ffn, tm=TAIL_ROWS, seq=seq, mem_len=mem_len, kv_col=layer,
                          n_heads=x_heads, dh=x_dh, name=name)

    e13, e2 = moe_w13[0], moe_w2[0]
    mixed, e13_bf16 = _even_mixer(xf, ev_norm[0], ev_w_in[0], ev_b_f[0], ev_sgu_norm[0], ev_w_s[0], ev_b_s[0],
                                  e13.reshape(-1, e13.shape[-1]), batch=batch, seq=seq, tq=ATTN_TILE)
    xf = tail(xf, mixed, ev_w_out[0], 0, (ffn_norm[0],) + _chunk_weights(ffn_w13[0], ffn_w2[0], FFN_CHUNK),
              "even_tail")
    lam_init = 0.8 - 0.6 * math.exp(-0.3 * 1)
    mixed, e2_bf16 = _odd_mixer(xf, od_norm[0], od_w_in[0], od_conv_w[0], od_lam_q1[0], od_lam_k1[0],
                                od_lam_q2[0], od_lam_k2[0], od_subln[0], rel_bias, lam_init,
                                e2.reshape(-1, e2.shape[-1]), batch=batch, seq=seq, tq=ATTN_TILE)
    xf = tail(xf, mixed, od_w_out[0], 1, None, "odd_tail")
    out = _moe_layer(xf, ffn_norm[1], moe_router[0], e13_bf16.reshape(e13.shape), e2_bf16.reshape(e2.shape),
                     final_norm, tm_expert=EXPERT_ROWS)
    return out.reshape(batch, seq, d)
```

```python
import functools
import math

import jax
import jax.numpy as jnp
from jax import lax
from jax.experimental import pallas as pl
from jax.experimental.pallas import tpu as pltpu

F32 = jnp.float32
BF16 = jnp.bfloat16
EPS = 1e-6
HEAD_DIM = 64
LANES = 128
SUBLANES = 8
BF16_ROWS = 16
N_BUCKETS = 32
MAX_DIST = 128
TOP_K = 2
LOG2E = 1.4426950408889634
ATTN_TILE = 512
IN_PROJ_ROWS = 1024
MEM_ROWS = 512
TAIL_ROWS = 512
ROUTE_ROWS = 256
ROUTE_STEP_ROWS = 512
EXPERT_ROWS = 512
FFN_CHUNK = 256
EXPERT_CHUNK = 512
VMEM_BYTES = 64 * 1024 * 1024
VMEM_LIMIT = VMEM_BYTES - 8 * 1024 * 1024
EXPERT_VMEM_LIMIT = VMEM_BYTES - 2 * 1024 * 1024


def _params(*sem):
    return pltpu.CompilerParams(dimension_semantics=sem, vmem_limit_bytes=VMEM_LIMIT)


def _rms(x, g):
    ms = jnp.mean(x * x, axis=-1, keepdims=True)
    return x * lax.rsqrt(ms + EPS) * g


def _dot(a, b):
    return jnp.dot(a, b, preferred_element_type=F32)


def _dot_nt(a, b):
    return lax.dot_general(a, b, (((1,), (1,)), ((), ())), preferred_element_type=F32)


def _norm_matmul_kernel(x_ref, g_ref, *refs, n_w, chunk, scaled):
    w_refs, o_refs = refs[:n_w], refs[n_w:]
    h = _rms(x_ref[...], g_ref[...]).astype(BF16)
    s0, s1, scale = scaled
    for k, (w_ref, o_ref) in enumerate(zip(w_refs, o_refs)):
        n = w_ref.shape[1]
        for c0 in range(0, n, chunk):
            c1 = min(c0 + chunk, n)
            y = _dot(h, w_ref[:, c0:c1])
            if k == 0 and s0 <= c0 and c1 <= s1:
                y = y * scale
            o_ref[:, c0:c1] = y.astype(o_ref.dtype)


def norm_matmul(x, g, ws, out_dtypes, *, tm, name, scaled=(0, 0, 1.0)):
    t, d = x.shape
    chunk = 512
    assert scaled[0] % chunk == 0 and scaled[1] % chunk == 0
    in_specs = [pl.BlockSpec((tm, d), lambda i: (i, 0)), pl.BlockSpec((1, d), lambda i: (0, 0))]
    in_specs += [pl.BlockSpec(w.shape, lambda i: (0, 0)) for w in ws]
    out_specs = [pl.BlockSpec((tm, w.shape[1]), lambda i: (i, 0)) for w in ws]
    out_shape = [jax.ShapeDtypeStruct((t, w.shape[1]), dt) for w, dt in zip(ws, out_dtypes)]
    return pl.pallas_call(
        functools.partial(_norm_matmul_kernel, n_w=len(ws), chunk=chunk, scaled=scaled),
        grid=(t // tm,), in_specs=in_specs, out_specs=out_specs, out_shape=out_shape,
        compiler_params=_params("parallel"), name=name,
    )(x, g.reshape(1, d), *ws)


def _gate_kernel(g_ref, b_ref, c_ref):
    s = g_ref.shape[0]
    row = lax.broadcasted_iota(jnp.int32, (LANES, LANES), 0)
    col = lax.broadcasted_iota(jnp.int32, (LANES, LANES), 1)
    tri = (row >= col).astype(F32)
    carry = jnp.zeros((1, LANES), F32)
    for blk in range(s // LANES):
        z = g_ref[blk * LANES:(blk + 1) * LANES, :] + b_ref[...]
        log_f = jnp.minimum(z, 0.0) - jnp.log1p(jnp.exp(-jnp.abs(z)))
        cs = jnp.dot(tri, log_f, precision=lax.Precision.HIGHEST, preferred_element_type=F32) + carry
        c_ref[blk * LANES:(blk + 1) * LANES, :] = cs
        carry = cs[LANES - 1:LANES, :]


def gate_cumsum(g, b, *, seq):
    t = g.shape[0]
    return pl.pallas_call(
        _gate_kernel, grid=(t // seq,),
        in_specs=[pl.BlockSpec((seq, LANES), lambda i: (i, 0)), pl.BlockSpec((1, LANES), lambda i: (0, 0))],
        out_specs=pl.BlockSpec((seq, LANES), lambda i: (i, 0)),
        out_shape=jax.ShapeDtypeStruct((t, LANES), F32),
        compiler_params=_params("parallel"), name="gate_cumsum",
    )(g, b)


def _split3(x):
    hi = x.astype(BF16).astype(F32)
    rest = x - hi
    mid = rest.astype(BF16).astype(F32)
    lo = (rest - mid).astype(BF16).astype(F32)
    return hi, mid, lo


def _augment(x, in_half, lane, base, pieces, pieces_first):
    n = len(pieces)
    p0, o0 = (base, base + n) if pieces_first else (base + n, base)
    aug = jnp.where((lane >= o0) & (lane < o0 + n), 1.0, 0.0)
    for idx, piece in enumerate(pieces):
        aug = jnp.where(lane == p0 + idx, piece, aug)
    return jnp.where(in_half, x, aug.astype(x.dtype))


def _halves(lane):
    return [(lane >= HEAD_DIM * hh) & (lane < HEAD_DIM * (hh + 1)) for hh in range(2)]


def _causal_attention(qa, ka_s, v_ref, bias_ref, i, tq):
    n_chunks = tq // LANES
    row = lax.broadcasted_iota(jnp.int32, (tq, tq), 0)
    col = lax.broadcasted_iota(jnp.int32, (tq, tq), 1)

    def scores(j):
        s = _dot_nt(qa, ka_s[j * tq:(j + 1) * tq, :])
        if bias_ref is not None and j >= i - 1:
            s = s + bias_ref[0, i - j]
        if j == i:
            s = jnp.where(row >= col, s, -jnp.inf)
        return [s[:, c * LANES:(c + 1) * LANES] for c in range(n_chunks)]

    m = jnp.full((tq, LANES), -jnp.inf, F32)
    for j in range(i + 1):
        for chunk in scores(j):
            m = jnp.maximum(m, chunk)
    m = jnp.broadcast_to(jnp.max(m, axis=1, keepdims=True), (tq, LANES))
    l = jnp.zeros((tq, LANES), F32)
    acc = jnp.zeros((tq, LANES), F32)
    for j in range(i + 1):
        ps = [jnp.exp2(chunk - m) for chunk in scores(j)]
        l = l + functools.reduce(lambda a, b: a + b, ps)
        acc = acc + _dot(jnp.concatenate(ps, axis=1).astype(BF16), v_ref[j * tq:(j + 1) * tq, :])
    return acc / jnp.sum(l, axis=1, keepdims=True)


def _rider(w, grid):
    steps = math.prod(grid)
    rows = w.shape[0] // steps
    assert rows * steps == w.shape[0] and rows % BF16_ROWS == 0

    def index(*ids):
        step = 0
        for n, idx in zip(grid, ids):
            step = step * n + idx
        return step, 0

    return pl.BlockSpec((rows, w.shape[1]), index), jax.ShapeDtypeStruct(w.shape, BF16)


def _fox_kernel(q_ref, k_ref, v_ref, c_ref, w_ref, o_ref, wcast_ref, ka_s, *, tq):
    wcast_ref[...] = w_ref[...].astype(BF16)
    hp = pl.program_id(1)
    lane = lax.broadcasted_iota(jnp.int32, (1, LANES), 1)
    halves = _halves(lane)

    def decay(c, hh):
        return jnp.sum(jnp.where(lane == 2 * hp + hh, c, 0.0), axis=1, keepdims=True) * LOG2E

    k = k_ref[...]
    c_all = c_ref[...]
    for hh in range(2):
        ka_s[hh] = _augment(k, halves[hh], lane, HEAD_DIM * (1 - hh), _split3(-decay(c_all, hh)), True)

    for i in range(q_ref.shape[0] // tq):
        rows = slice(i * tq, (i + 1) * tq)
        outs = []
        for hh in range(2):
            qa = _augment(q_ref[rows, :], halves[hh], lane, HEAD_DIM * (1 - hh), _split3(decay(c_ref[rows, :], hh)),
                          False)
            outs.append(_causal_attention(qa, ka_s.at[hh], v_ref, None, i, tq))
        o_ref[rows, :] = jnp.where(lane < HEAD_DIM, outs[0], outs[1]).astype(o_ref.dtype)


def fox_attention(qkv, c, w_ride, *, batch, seq, tq, q_col, k_col, v_col, n_pairs):
    t = batch * seq
    grid = (batch, n_pairs)
    ride_spec, ride_shape = _rider(w_ride, grid)
    block = lambda col: pl.BlockSpec((seq, LANES), lambda b, h: (b, col + h))
    return pl.pallas_call(
        functools.partial(_fox_kernel, tq=tq),
        grid=grid,
        in_specs=[block(q_col), block(k_col), block(v_col), pl.BlockSpec((seq, LANES), lambda b, h: (b, 0)),
                  ride_spec],
        out_specs=[block(0), ride_spec],
        out_shape=[jax.ShapeDtypeStruct((t, n_pairs * LANES), BF16), ride_shape],
        scratch_shapes=[pltpu.VMEM((2, seq, LANES), BF16)],
        compiler_params=_params("parallel", "parallel"), name="fox_attention",
    )(qkv, qkv, qkv, c, w_ride)


def _spatial_gate(u_ref, v_ref, norm_ref, ws_ref, bs_ref, o_ref):
    tb = u_ref.shape[0]
    n_groups, chunk, _ = ws_ref.shape
    row = lax.broadcasted_iota(jnp.int32, (chunk, chunk), 0)
    col = lax.broadcasted_iota(jnp.int32, (chunk, chunk), 1)
    tri = row >= col
    for g in range(n_groups):
        w = jnp.where(tri, ws_ref[g], 0.0).astype(BF16)
        bias = bs_ref[:, g:g + 1]
        gain = norm_ref[g:g + 1, :]
        for c in range(tb // chunk):
            rs = slice(c * chunk, (c + 1) * chunk)
            cs = slice(g * LANES, (g + 1) * LANES)
            vn = _rms(jax.nn.gelu(v_ref[rs, cs].astype(F32)), gain)
            mixed = _dot(w, vn.astype(BF16)) + bias
            o_ref[rs, cs] = (jax.nn.gelu(u_ref[rs, cs].astype(F32)) * mixed).astype(o_ref.dtype)


def _chunk_weights(w13, w2, tf):
    *lead, ff, d = w2.shape
    return w13.astype(BF16), w2.astype(BF16).reshape(*lead, ff // tf, tf, d)


def _swiglu_chunks(h_s, w13, w2, o_ref, gu_s):
    n, tf, _ = w2.shape

    def project(c, slot):
        for part in range(2):
            cols = pl.ds(pl.multiple_of((part * n + c) * tf, tf), tf)
            gu_s[slot, part] = _dot(h_s[...], w13[:, cols])

    def consume(c, slot):
        gate = gu_s[slot, 0]
        act = (gate * jax.nn.sigmoid(gate) * gu_s[slot, 1]).astype(BF16)
        o_ref[...] += _dot(act, w2[c])

    def pair(k, carry):
        c = 2 * k
        project(c + 1, 1)
        consume(c, 0)
        project(c + 2, 0)
        consume(c + 1, 1)
        return carry

    project(0, 0)
    lax.fori_loop(0, (n - 1) // 2, pair, 0)
    if n % 2 == 0:
        project(n - 1, 1)
        consume(n - 2, 0)
        consume(n - 1, 1)
    else:
        consume(n - 1, 0)


def _swiglu_scratch(tm, d, tf):
    return [pltpu.VMEM((tm, d), BF16), pltpu.VMEM((2, 2, tm, tf), F32)]


def _tail_kernel(*refs, n_heads, dh, gated, ffn, zero_fill):
    refs = list(refs)
    take = lambda n: [refs.pop(0) for _ in range(n)]
    x_ref, a_ref = take(2)
    b_in = take(5 if gated else 1)
    wa_ref, wb_ref, gx_ref, wq_ref, kv_ref, wo_ref = take(6)
    ffn_in = take(3 if ffn else 0)
    (o_ref,) = take(1)
    for z_ref in take(1 if zero_fill else 0):
        z_ref[...] = jnp.zeros_like(z_ref)
    if gated:
        (b_ref,) = take(1)
        _spatial_gate(*b_in, b_ref)
    else:
        (b_ref,) = b_in
    rest = ffn_in + [o_ref] + refs
    x = x_ref[...] + _dot(a_ref[...], wa_ref[...]) + _dot(b_ref[...], wb_ref[...])
    q = _dot(_rms(x, gx_ref[...]).astype(BF16), wq_ref[...]).astype(BF16)
    width = n_heads * dh
    outs = []
    for hd in range(n_heads):
        cs = slice(hd * dh, (hd + 1) * dh)
        s = _dot_nt(q[:, cs], kv_ref[:, cs]) * (dh ** -0.5)
        p = jnp.exp(s - jnp.max(s, axis=1, keepdims=True))
        p = p / jnp.sum(p, axis=1, keepdims=True)
        outs.append(_dot(p.astype(BF16), kv_ref[:, width + hd * dh:width + (hd + 1) * dh]).astype(BF16))
    x = x + _dot(jnp.concatenate(outs, axis=1), wo_ref[...])
    if len(rest) == 1:
        (o_ref,) = rest
        o_ref[...] = x
    else:
        gf_ref, w13_ref, w2_ref, o_ref, h_s, gu_s = rest
        h_s[...] = _rms(x, gf_ref[...]).astype(BF16)
        o_ref[...] = x
        _swiglu_chunks(h_s, w13_ref, w2_ref, o_ref, gu_s)


def layer_tail(x, a, b, w_out, gx, wq, kv, wo, ffn=None, zero_rows=0, *, tm, seq, mem_len, kv_col, n_heads, dh,
               name):
    t, d = x.shape
    per_b = seq // tm
    resident = pl.Buffered(1)
    const = lambda arr: pl.BlockSpec(arr.shape, lambda i: (0,) * arr.ndim, pipeline_mode=resident)
    rows = lambda arr: pl.BlockSpec((tm, arr.shape[1]), lambda i: (i, 0))
    vec = lambda g: g.reshape(1, d)
    gated = isinstance(b, tuple)
    scratch = []
    if gated:
        proj, u_col, v_col, sgu_norm, w_s, b_s = b
        b_width = w_s.shape[0] * LANES
        b_args = [proj, proj, sgu_norm, w_s, b_s.T]
        b_specs = [pl.BlockSpec((tm, b_width), lambda i: (i, u_col)), pl.BlockSpec((tm, b_width), lambda i: (i, v_col)),
                   const(sgu_norm), const(w_s), const(b_s.T)]
        scratch.append(pltpu.VMEM((tm, b_width), BF16))
    else:
        b_width = b.shape[1]
        b_args, b_specs = [b], [rows(b)]
    wa, wb = w_out[:a.shape[1]], w_out[a.shape[1]:]
    assert wb.shape[0] == b_width
    args = [x, a] + b_args + [wa, wb, vec(gx), wq, kv, wo]
    in_specs = [rows(x), rows(a)] + b_specs + [
        const(wa), const(wb), const(vec(gx)), const(wq),
        pl.BlockSpec((mem_len, 2 * n_heads * dh), lambda i: (i // per_b, kv_col)), const(wo)]
    if ffn is not None:
        gf, w13, w2 = ffn
        args += [vec(gf), w13, w2]
        in_specs += [const(vec(gf)), const(w13), const(w2)]
        scratch += _swiglu_scratch(tm, d, w2.shape[1])
    out_specs, out_shape = [rows(x)], [jax.ShapeDtypeStruct((t, d), F32)]
    if zero_rows:
        per_step = zero_rows // (t // tm)
        assert per_step * (t // tm) == zero_rows and per_step % BF16_ROWS == 0
        out_specs.append(pl.BlockSpec((per_step, d), lambda i: (i, 0)))
        out_shape.append(jax.ShapeDtypeStruct((zero_rows, d), BF16))
    outs = pl.pallas_call(
        functools.partial(_tail_kernel, n_heads=n_heads, dh=dh, gated=gated, ffn=ffn is not None,
                          zero_fill=bool(zero_rows)),
        grid=(t // tm,), in_specs=in_specs, out_specs=out_specs, out_shape=out_shape,
        scratch_shapes=scratch, compiler_params=_params("parallel"), name=name,
    )(*args)
    return outs if zero_rows else outs[0]


def _conv_kernel(bg_ref, cg_ref, xi_ref, w_ref, o_ref):
    s, width = o_ref.shape
    n_taps = w_ref.shape[0]
    xc = cg_ref[...].astype(F32) * xi_ref[...].astype(F32)
    row = lax.broadcasted_iota(jnp.int32, (s, width), 0)
    y = w_ref[n_taps - 1:n_taps, :] * xc
    for back in range(1, n_taps):
        shifted = jnp.where(row >= back, pltpu.roll(xc, back, axis=0), 0.0)
        y = y + w_ref[n_taps - 1 - back:n_taps - back, :] * shifted
    o_ref[...] = (bg_ref[...].astype(F32) * y).astype(o_ref.dtype)


def short_conv(main, conv_w, *, batch, seq):
    width = conv_w.shape[1]
    return pl.pallas_call(
        _conv_kernel, grid=(batch,),
        in_specs=[
            pl.BlockSpec((seq, width), lambda b: (b, 0)),
            pl.BlockSpec((seq, width), lambda b: (b, 1)),
            pl.BlockSpec((seq, width), lambda b: (b, 2)),
            pl.BlockSpec(conv_w.shape, lambda b: (0, 0)),
        ],
        out_specs=pl.BlockSpec((seq, width), lambda b: (b, 0)),
        out_shape=jax.ShapeDtypeStruct((batch * seq, width), BF16),
        compiler_params=_params("parallel"), name="short_conv",
    )(main, main, main, conv_w)


def _diff_prep_kernel(rb_ref, lq1_ref, lk1_ref, lq2_ref, lk2_ref, bias_ref, far_ref, lam_ref, *, tq, lam_init):
    n_heads = bias_ref.shape[0]
    strip = 32
    row = lax.broadcasted_iota(jnp.int32, (strip, tq), 0)
    col = lax.broadcasted_iota(jnp.int32, (strip, tq), 1)
    max_exact = N_BUCKETS // 2

    def fill(r, carry):
        r0 = pl.multiple_of(r * strip, strip)
        for which in range(2):
            n = jnp.maximum(row + r0 - col + which * tq, 0)
            nf = jnp.maximum(n, 1).astype(F32)
            large = max_exact + (jnp.log(nf / max_exact) / math.log(MAX_DIST / max_exact)
                                 * (N_BUCKETS - max_exact)).astype(jnp.int32)
            large = jnp.minimum(large, N_BUCKETS - 1)
            bucket = jnp.where(n < max_exact, n, large)
            for h in range(n_heads):
                b = jnp.zeros((strip, tq), F32)
                for kk in range(N_BUCKETS):
                    b = jnp.where(bucket == kk, rb_ref[kk, h], b)
                bias_ref[h, which, pl.ds(r0, strip), :] = (b - rb_ref[N_BUCKETS - 1, h]) * LOG2E
        return carry

    lax.fori_loop(0, tq // strip, fill, 0)
    for h in range(n_heads):
        far_ref[h] = jnp.full((1, LANES), rb_ref[N_BUCKETS - 1, h], F32) * LOG2E
    lam = (jnp.exp(jnp.sum(lq1_ref[...] * lk1_ref[...], axis=1, keepdims=True))
           - jnp.exp(jnp.sum(lq2_ref[...] * lk2_ref[...], axis=1, keepdims=True)) + lam_init)
    lam_ref[...] = jnp.broadcast_to(lam, (1, LANES))


def diff_prep(rel_bias, lq1, lk1, lq2, lk2, *, tq, lam_init):
    n_heads = rel_bias.shape[1]
    vec = lambda a: a.reshape(1, -1)
    vspec = pl.BlockSpec(memory_space=pltpu.VMEM)
    return pl.pallas_call(
        functools.partial(_diff_prep_kernel, tq=tq, lam_init=lam_init),
        in_specs=[pl.BlockSpec(memory_space=pltpu.SMEM), vspec, vspec, vspec, vspec],
        out_specs=[vspec, vspec, vspec],
        out_shape=[
            jax.ShapeDtypeStruct((n_heads, 2, tq, tq), F32),
            jax.ShapeDtypeStruct((n_heads, 1, LANES), F32),
            jax.ShapeDtypeStruct((1, LANES), F32),
        ],
        compiler_params=pltpu.CompilerParams(vmem_limit_bytes=VMEM_LIMIT), name="diff_prep",
    )(rel_bias, vec(lq1), vec(lk1), vec(lq2), vec(lk2))


def _diff_kernel(q_ref, k_ref, v_ref, bias_ref, far_ref, lam_ref, subln_ref, w_ref, o_ref, wcast_ref,
                 ka_s, *, tq, lam_init):
    wcast_ref[...] = w_ref[...].astype(BF16)
    lane = lax.broadcasted_iota(jnp.int32, (1, LANES), 1)
    halves = _halves(lane)
    k = k_ref[...]
    far = _split3(far_ref[0])
    for sub in range(2):
        ka_s[sub] = _augment(k, halves[sub], lane, HEAD_DIM * (1 - sub), far, True)

    zero = jnp.zeros((1, LANES), F32)
    for i in range(q_ref.shape[0] // tq):
        rows = slice(i * tq, (i + 1) * tq)
        outs = []
        for sub in range(2):
            qa = _augment(q_ref[rows, :], halves[sub], lane, HEAD_DIM * (1 - sub), (zero,) * 3, False)
            outs.append(_causal_attention(qa, ka_s.at[sub], v_ref, bias_ref, i, tq))
        o = outs[0] - lam_ref[...] * outs[1]
        o_ref[rows, :] = (_rms(o, subln_ref[...]) * (1.0 - lam_init)).astype(o_ref.dtype)


def diff_attention(main, bias, far, lam, subln, w_ride, *, batch, seq, tq, q_col, k_col, v_col, lam_init):
    t = batch * seq
    n_heads = bias.shape[0]
    grid = (batch, n_heads)
    ride_spec, ride_shape = _rider(w_ride, grid)
    block = lambda col: pl.BlockSpec((seq, LANES), lambda b, h: (b, col + h))
    return pl.pallas_call(
        functools.partial(_diff_kernel, tq=tq, lam_init=lam_init),
        grid=grid,
        in_specs=[
            block(q_col), block(k_col), block(v_col),
            pl.BlockSpec((1, 2, tq, tq), lambda b, h: (h, 0, 0, 0)),
            pl.BlockSpec((1, 1, LANES), lambda b, h: (h, 0, 0)),
            pl.BlockSpec((1, LANES), lambda b, h: (0, 0)),
            pl.BlockSpec((1, LANES), lambda b, h: (0, 0)),
            ride_spec,
        ],
        out_specs=[block(0), ride_spec],
        out_shape=[jax.ShapeDtypeStruct((t, n_heads * LANES), BF16), ride_shape],
        scratch_shapes=[pltpu.VMEM((2, seq, LANES), BF16)],
        compiler_params=_params("parallel", "parallel"), name="diff_attention",
    )(main, main, main, bias, far, lam, subln.reshape(1, LANES), w_ride)


def _router_kernel(x_ref, g_ref, wr_ref, meta_ref, wts_ref, before_ref, cnt_ref, carry_s, *, n_exp, sub):
    @pl.when(pl.program_id(0) == 0)
    def _():
        carry_s[...] = jnp.zeros_like(carry_s)

    lane = lax.broadcasted_iota(jnp.int32, (sub, LANES), 1)
    lane_f = lane.astype(F32)
    row = lax.broadcasted_iota(jnp.int32, (sub, sub), 0)
    col = lax.broadcasted_iota(jnp.int32, (sub, sub), 1)
    earlier = jnp.where(row > col, 1.0, 0.0).astype(BF16)
    pick = lambda sel, val: jnp.sum(jnp.where(sel, val, 0.0), axis=1, keepdims=True)
    carry = carry_s[...]
    for r in range(x_ref.shape[0] // sub):
        rows = slice(r * sub, (r + 1) * sub)
        before_ref[r * SUBLANES:(r + 1) * SUBLANES, :] = jnp.broadcast_to(carry, (SUBLANES, LANES))
        h = _rms(x_ref[rows, :], g_ref[...])
        h_hi = h.astype(BF16)
        h_lo = (h - h_hi.astype(F32)).astype(BF16)
        logits = _dot(h_hi, wr_ref[0]) + (_dot(h_hi, wr_ref[1]) + _dot(h_lo, wr_ref[0]))
        logits = jnp.where(lane < n_exp, logits, -jnp.inf)
        m1 = jnp.max(logits, axis=1, keepdims=True)
        i1 = jnp.min(jnp.where(logits == m1, lane_f, float(LANES)), axis=1, keepdims=True)
        rest = jnp.where(lane_f == i1, -jnp.inf, logits)
        m2 = jnp.max(rest, axis=1, keepdims=True)
        i2 = jnp.min(jnp.where(rest == m2, lane_f, float(LANES)), axis=1, keepdims=True)
        e = jnp.exp(m2 - m1)
        sel1 = lane_f == i1
        sel2 = lane_f == i2
        onehot = jnp.where(sel1 | sel2, 1.0, 0.0)
        local = _dot(earlier, onehot.astype(BF16))
        meta = jnp.zeros((sub, LANES), F32)
        for idx, field in enumerate([i1, i2, pick(sel1, local), pick(sel2, local)]):
            meta = jnp.where(lane == idx, field, meta)
        meta_ref[rows, :] = meta.astype(jnp.int32)
        wts_ref[rows, :] = jnp.where(lane == 0, 1.0 / (1.0 + e), jnp.where(lane == 1, e / (1.0 + e), 0.0))
        carry = carry + jnp.sum(onehot, axis=0, keepdims=True)
    carry_s[...] = carry
    cnt_ref[...] = carry


def route_tokens(x, g, wr, *, tm, sub, n_exp):
    t, d = x.shape
    per_step = tm // sub * SUBLANES
    return pl.pallas_call(
        functools.partial(_router_kernel, n_exp=n_exp, sub=sub),
        grid=(t // tm,),
        in_specs=[
            pl.BlockSpec((tm, d), lambda i: (i, 0)),
            pl.BlockSpec((1, d), lambda i: (0, 0)),
            pl.BlockSpec((2, d, LANES), lambda i: (0, 0, 0)),
        ],
        out_specs=[
            pl.BlockSpec((tm, LANES), lambda i: (i, 0)),
            pl.BlockSpec((tm, LANES), lambda i: (i, 0)),
            pl.BlockSpec((per_step, LANES), lambda i: (i, 0)),
            pl.BlockSpec((1, LANES), lambda i: (0, 0)),
        ],
        out_shape=[
            jax.ShapeDtypeStruct((t, LANES), jnp.int32),
            jax.ShapeDtypeStruct((t, LANES), F32),
            jax.ShapeDtypeStruct((t // tm * per_step, LANES), F32),
            jax.ShapeDtypeStruct((1, LANES), F32),
        ],
        scratch_shapes=[pltpu.VMEM((1, LANES), F32)],
        compiler_params=_params("arbitrary"), name="moe_router",
    )(x, g.reshape(1, d), wr)


def _dispatch_kernel(start_ref, shift_ref, keep_ref, x_ref, g_ref, meta_ref, zeros_hbm, xs_hbm,
                     stage_s, carry_s, sems, *, tm, n_exp):
    del zeros_hbm
    j = pl.program_id(0)
    slot = j % 2
    rows = tm + BF16_ROWS

    def block_copy(step, e, buf):
        first = pl.multiple_of(start_ref[step * n_exp + e], BF16_ROWS)
        return pltpu.make_async_copy(stage_s.at[buf, e], xs_hbm.at[pl.ds(first, rows)], sems.at[buf])

    @pl.when(j == 0)
    def _():
        carry_s[...] = jnp.zeros_like(carry_s)

    h = _rms(x_ref[...], g_ref[...]).astype(BF16)
    fields = meta_ref[...].astype(F32).T
    slot_row = lax.broadcasted_iota(jnp.int32, (rows, tm), 0).astype(F32)
    for e in range(n_exp):
        key = j * n_exp + e
        idx = jnp.where(fields[0:1] == e, fields[2:3], jnp.where(fields[1:2] == e, fields[3:4], -2.0 * rows))
        idx = idx + shift_ref[key].astype(F32)
        onehot = jnp.where(slot_row == idx, 1.0, 0.0).astype(BF16)
        stage_s[slot, e] = _dot(onehot, h).astype(BF16)
        stage_s[slot, e, 0:BF16_ROWS, :] += carry_s[e]
        keep = pl.multiple_of(keep_ref[key], BF16_ROWS)
        carry_s[e] = stage_s[slot, e, pl.ds(keep, BF16_ROWS), :]

    @pl.when(j > 0)
    def _():
        for e in range(n_exp):
            block_copy(j - 1, e, 1 - slot).wait()

    for e in range(n_exp):
        block_copy(j, e, slot).start()

    @pl.when(j == pl.num_programs(0) - 1)
    def _():
        for e in range(n_exp):
            block_copy(j, e, slot).wait()


def moe_dispatch(starts, shifts, keeps, x, g, meta, zeros, *, tm, n_exp):
    t, d = x.shape
    n_rows = zeros.shape[0]
    rows = tm + BF16_ROWS
    return pl.pallas_call(
        functools.partial(_dispatch_kernel, tm=tm, n_exp=n_exp),
        grid_spec=pltpu.PrefetchScalarGridSpec(
            num_scalar_prefetch=3, grid=(t // tm,),
            in_specs=[
                pl.BlockSpec((tm, d), lambda i, *_: (i, 0)),
                pl.BlockSpec((1, d), lambda i, *_: (0, 0)),
                pl.BlockSpec((tm, LANES), lambda i, *_: (i, 0)),
                pl.BlockSpec(memory_space=pl.ANY),
            ],
            out_specs=pl.BlockSpec(memory_space=pl.ANY),
            scratch_shapes=[pltpu.VMEM((2, n_exp, rows, d), BF16), pltpu.VMEM((n_exp, BF16_ROWS, d), BF16),
                            pltpu.SemaphoreType.DMA((2,))],
        ),
        out_shape=jax.ShapeDtypeStruct((n_rows, d), BF16),
        input_output_aliases={6: 0},
        compiler_params=_params("arbitrary"),
        name="moe_dispatch",
    )(starts, shifts, keeps, x, g.reshape(1, d), meta, zeros)


def _expert_kernel(te_ref, used_ref, h_ref, w13_ref, w2_ref, o_ref, gu_s, acc_s):
    del te_ref
    used = used_ref[pl.program_id(0)] != 0

    @pl.when(used)
    def _():
        acc_s[...] = jnp.zeros_like(acc_s)
        _swiglu_chunks(h_ref, w13_ref.at[0], w2_ref.at[0], acc_s, gu_s)
        o_ref[...] = acc_s[...].astype(o_ref.dtype)

    @pl.when(jnp.logical_not(used))
    def _():
        o_ref[...] = jnp.zeros_like(o_ref)


def moe_experts(tile_expert, tile_used, hs, w13, w2, *, tm):
    n_rows, d = hs.shape
    tf = w2.shape[2]
    resident = pl.Buffered(1)
    return pl.pallas_call(
        _expert_kernel,
        grid_spec=pltpu.PrefetchScalarGridSpec(
            num_scalar_prefetch=2, grid=(n_rows // tm,),
            in_specs=[
                pl.BlockSpec((tm, d), lambda i, te, tu: (i, 0)),
                pl.BlockSpec((1,) + w13.shape[1:], lambda i, te, tu: (te[i], 0, 0)),
                pl.BlockSpec((1,) + w2.shape[1:], lambda i, te, tu: (te[i], 0, 0, 0)),
            ],
            out_specs=pl.BlockSpec((tm, d), lambda i, te, tu: (i, 0)),
            scratch_shapes=[pltpu.VMEM((2, 2, tm, tf), F32), pltpu.VMEM((tm, d), F32)],
        ),
        out_shape=jax.ShapeDtypeStruct((n_rows, d), BF16),
        compiler_params=pltpu.CompilerParams(dimension_semantics=("arbitrary",), vmem_limit_bytes=EXPERT_VMEM_LIMIT),
        name="moe_experts",
    )(tile_expert, tile_used, hs, w13, w2)


def _combine_kernel(start_ref, shift_ref, tail_ref, x_ref, wts_ref, meta_ref, g_ref, y_hbm, o_ref, blk_s, sems, *,
                    tm, n_exp):
    j = pl.program_id(0)
    slot = j % 2
    rows = tm + BF16_ROWS

    def block_copy(step, e, buf):
        first = pl.multiple_of(start_ref[step * n_exp + e], BF16_ROWS)
        return pltpu.make_async_copy(y_hbm.at[pl.ds(first, rows)], blk_s.at[buf, e], sems.at[buf])

    def fetch(step, buf):
        for e in range(n_exp):
            block_copy(step, e, buf).start()

    @pl.when(j == 0)
    def _():
        fetch(0, 0)

    @pl.when(j + 1 < pl.num_programs(0))
    def _():
        fetch(j + 1, 1 - slot)

    for e in range(n_exp):
        block_copy(j, e, slot).wait()

    meta = meta_ref[...]
    wts = wts_ref[...]
    col = lax.broadcasted_iota(jnp.int32, (tm, tm), 1)
    tail_col = lax.broadcasted_iota(jnp.int32, (tm, BF16_ROWS), 1) + tm

    def pick(e):
        sel = [meta[:, k:k + 1] == e for k in range(TOP_K)]
        idx = jnp.where(sel[0], meta[:, 2:3], jnp.where(sel[1], meta[:, 3:4], -2 * rows)) + shift_ref[j * n_exp + e]
        w = jnp.where(sel[0], wts[:, 0:1], jnp.where(sel[1], wts[:, 1:2], 0.0))
        return idx, w

    acc = x_ref[...]
    for e in range(n_exp):
        idx, w = pick(e)
        onehot = jnp.where(col == idx, 1.0, 0.0).astype(BF16)
        acc = acc + w * _dot(onehot, blk_s[slot, e, 0:tm, :])
    o_ref[...] = acc

    for e in range(n_exp):
        @pl.when(tail_ref[j * n_exp + e] != 0)
        def _(e=e):
            idx, w = pick(e)
            onehot_tail = jnp.where(tail_col == idx, 1.0, 0.0).astype(BF16)
            o_ref[...] += w * _dot(onehot_tail, blk_s[slot, e, tm:rows, :])

    o_ref[...] = _rms(o_ref[...], g_ref[...])


def moe_combine(starts, shifts, tails, x, wts, meta, g, y, *, tm, n_exp):
    t, d = x.shape
    n_pre = 3
    return pl.pallas_call(
        functools.partial(_combine_kernel, tm=tm, n_exp=n_exp),
        grid_spec=pltpu.PrefetchScalarGridSpec(
            num_scalar_prefetch=n_pre, grid=(t // tm,),
            in_specs=[
                pl.BlockSpec((tm, d), lambda i, *_: (i, 0)),
                pl.BlockSpec((tm, LANES), lambda i, *_: (i, 0)),
                pl.BlockSpec((tm, LANES), lambda i, *_: (i, 0)),
                pl.BlockSpec((1, d), lambda i, *_: (0, 0)),
                pl.BlockSpec(memory_space=pl.ANY),
            ],
            out_specs=pl.BlockSpec((tm, d), lambda i, *_: (i, 0)),
            scratch_shapes=[pltpu.VMEM((2, n_exp, tm + BF16_ROWS, d), BF16), pltpu.SemaphoreType.DMA((2,))],
        ),
        out_shape=jax.ShapeDtypeStruct((t, d), F32),
        compiler_params=_params("arbitrary"), name="moe_combine",
    )(starts, shifts, tails, x, wts, meta, g.reshape(1, d), y)


def _pad_cols(w, n):
    return jnp.pad(w, ((0, 0), (0, n - w.shape[1])))


def _even_mixer(x, norm, w_in, b_f, sgu_norm, w_s, b_s, w_ride, *, batch, seq, tq):
    n_heads = b_f.shape[0]
    a_width = n_heads * HEAD_DIM
    b_width = w_s.shape[0] * LANES
    f0 = 3 * a_width
    w_main = jnp.concatenate([w_in[:, :f0], w_in[:, f0 + n_heads:]], axis=1)
    w_gate = _pad_cols(w_in[:, f0:f0 + n_heads], LANES)
    main, gate = norm_matmul(x, norm, [w_main.astype(BF16), w_gate.astype(BF16)], [BF16, F32],
                             tm=min(IN_PROJ_ROWS, x.shape[0]), name="even_in_proj",
                             scaled=(0, a_width, LOG2E * HEAD_DIM ** -0.5))
    c = gate_cumsum(gate, _pad_cols(b_f.reshape(1, -1), LANES), seq=seq)
    n_pairs = a_width // LANES
    a, w_cast = fox_attention(main, c, w_ride, batch=batch, seq=seq, tq=tq, q_col=0, k_col=n_pairs,
                              v_col=2 * n_pairs, n_pairs=n_pairs)
    u_col = f0 // b_width
    return (a, (main, u_col, u_col + 1, sgu_norm, w_s, b_s)), w_cast


def _odd_mixer(x, norm, w_in, conv_w, lq1, lk1, lq2, lk2, subln, rel_bias, lam_init, w_ride, *, batch, seq, tq):
    c_width = conv_w.shape[1]
    d_width = rel_bias.shape[1] * 2 * HEAD_DIM
    q0 = 3 * c_width
    (main,) = norm_matmul(x, norm, [w_in.astype(BF16)], [BF16], tm=min(IN_PROJ_ROWS, x.shape[0]), name="odd_in_proj",
                          scaled=(q0, q0 + d_width, LOG2E * HEAD_DIM ** -0.5))
    c_out = short_conv(main, conv_w, batch=batch, seq=seq)
    bias, far, lam = diff_prep(rel_bias, lq1, lk1, lq2, lk2, tq=tq, lam_init=lam_init)
    q_col = q0 // LANES
    n_heads = rel_bias.shape[1]
    d_out, w_cast = diff_attention(main, bias, far, lam, subln, w_ride, batch=batch, seq=seq, tq=tq, q_col=q_col,
                                   k_col=q_col + n_heads, v_col=q_col + 2 * n_heads, lam_init=lam_init)
    return (c_out, d_out), w_cast


def _sorted_rows(t, n_exp):
    block_rows = ROUTE_ROWS + BF16_ROWS
    return TOP_K * t + n_exp * (EXPERT_ROWS + pl.cdiv(block_rows, EXPERT_ROWS) * EXPERT_ROWS)


def _moe_layer(x, norm, w_router, w13, w2, final_norm, zeros):
    t, d = x.shape
    n_exp = w_router.shape[1]
    tm_route = ROUTE_ROWS
    tm_expert = EXPERT_ROWS
    n_rows = zeros.shape[0]
    assert n_rows == _sorted_rows(t, n_exp)
    wr = _pad_cols(w_router, LANES)
    wr_hi = wr.astype(BF16)
    wr_split = jnp.stack([wr_hi, (wr - wr_hi.astype(F32)).astype(BF16)])
    meta, wts, before, counts = route_tokens(x, norm, wr_split, tm=ROUTE_STEP_ROWS, sub=tm_route, n_exp=n_exp)
    counts = counts[0, :n_exp].astype(jnp.int32)
    block_rows = tm_route + BF16_ROWS
    padded = (counts + block_rows + tm_expert - 1) // tm_expert * tm_expert
    ends = jnp.cumsum(padded)
    offsets = ends - padded
    tile_start = jnp.arange(n_rows // tm_expert, dtype=jnp.int32) * tm_expert
    tile_expert = jnp.minimum(jnp.sum(tile_start[:, None] >= ends[None, :], axis=1), n_exp - 1).astype(jnp.int32)
    tile_used = (tile_start < (offsets + counts)[tile_expert]).astype(jnp.int32)
    before = before[::SUBLANES, :n_exp].astype(jnp.int32)
    in_tile = jnp.concatenate([before[1:], counts[None]]) - before
    first = offsets[None, :] + before
    starts = first // BF16_ROWS * BF16_ROWS
    shifts = first - starts
    keeps = (shifts + in_tile) // BF16_ROWS * BF16_ROWS
    tails = (shifts + in_tile > tm_route).astype(jnp.int32)
    flat = lambda a: a.reshape(-1).astype(jnp.int32)
    hs = moe_dispatch(flat(starts), flat(shifts), flat(keeps), x, norm, meta, zeros, tm=tm_route, n_exp=n_exp)
    y = moe_experts(tile_expert, tile_used, hs, *_chunk_weights(w13, w2, EXPERT_CHUNK), tm=tm_expert)
    return moe_combine(flat(starts), flat(shifts), flat(tails), x, wts, meta, final_norm, y, tm=tm_route,
                       n_exp=n_exp)


def kernel(x, mem, rel_bias, mem_norm, final_norm, ev_norm, ev_w_in, ev_b_f, ev_sgu_norm, ev_w_s, ev_b_s, ev_w_out, ffn_w13, ffn_w2, od_norm, od_w_in, od_conv_w, od_lam_q1, od_lam_k1, od_lam_q2, od_lam_k2, od_subln, od_w_out, moe_router, moe_w13, moe_w2, x_norm, x_wq, x_wkv, x_wo, ffn_norm):
    batch, seq, d = x.shape
    mem_len = mem.shape[1]
    depth = x_norm.shape[0]
    assert depth == 2 and ev_norm.shape[0] == 1 and od_norm.shape[0] == 1
    x_heads, x_dh = 4, 128
    xf = x.reshape(batch * seq, d)
    wkv = jnp.concatenate([x_wkv[layer] for layer in range(depth)], axis=1).astype(BF16)
    (kv,) = norm_matmul(mem.reshape(batch * mem_len, d), mem_norm, [wkv], [BF16], tm=MEM_ROWS, name="mem_kv")

    def tail(xf, mixed, w_out, layer, ffn, name, zero_rows=0):
        return layer_tail(xf, *mixed, w_out.astype(BF16), x_norm[layer], x_wq[layer].astype(BF16), kv,
                          x_wo[layer].astype(BF16), ffn, zero_rows, tm=TAIL_ROWS, seq=seq, mem_len=mem_len, kv_col=layer,
                          n_heads=x_heads, dh=x_dh, name=name)

    e13, e2 = moe_w13[0], moe_w2[0]
    mixed, e13_bf16 = _even_mixer(xf, ev_norm[0], ev_w_in[0], ev_b_f[0], ev_sgu_norm[0], ev_w_s[0], ev_b_s[0],
                                  e13.reshape(-1, e13.shape[-1]), batch=batch, seq=seq, tq=ATTN_TILE)
    xf = tail(xf, mixed, ev_w_out[0], 0, (ffn_norm[0],) + _chunk_weights(ffn_w13[0], ffn_w2[0], FFN_CHUNK),
              "even_tail")
    lam_init = 0.8 - 0.6 * math.exp(-0.3 * 1)
    mixed, e2_bf16 = _odd_mixer(xf, od_norm[0], od_w_in[0], od_conv_w[0], od_lam_q1[0], od_lam_k1[0],
                                od_lam_q2[0], od_lam_k2[0], od_subln[0], rel_bias, lam_init,
                                e2.reshape(-1, e2.shape[-1]), batch=batch, seq=seq, tq=ATTN_TILE)
    xf, zeros = tail(xf, mixed, od_w_out[0], 1, None, "odd_tail", _sorted_rows(xf.shape[0], moe_router.shape[-1]))
    out = _moe_layer(xf, ffn_norm[1], moe_router[0], e13_bf16.reshape(e13.shape), e2_bf16.reshape(e2.shape),
                     final_norm, zeros)
    return out.reshape(batch, seq, d)
```

```python
import functools
import math

import jax
import jax.numpy as jnp
from jax import lax
from jax.experimental import pallas as pl
from jax.experimental.pallas import tpu as pltpu

F32 = jnp.float32
BF16 = jnp.bfloat16
EPS = 1e-6
HEAD_DIM = 64
LANES = 128
SUBLANES = 8
BF16_ROWS = 16
N_BUCKETS = 32
MAX_DIST = 128
TOP_K = 2
LOG2E = 1.4426950408889634
ATTN_TILE = 512
IN_PROJ_ROWS = 1024
MEM_ROWS = 512
TAIL_ROWS = 512
ROUTE_ROWS = 256
ROUTE_STEP_ROWS = 512
EXPERT_ROWS = 512
FFN_CHUNK = 256
EXPERT_CHUNK = 512
VMEM_BYTES = 64 * 1024 * 1024
VMEM_LIMIT = VMEM_BYTES - 8 * 1024 * 1024
EXPERT_VMEM_LIMIT = VMEM_BYTES - 2 * 1024 * 1024


def _params(*sem):
    return pltpu.CompilerParams(dimension_semantics=sem, vmem_limit_bytes=VMEM_LIMIT)


def _rms(x, g):
    ms = jnp.mean(x * x, axis=-1, keepdims=True)
    return x * lax.rsqrt(ms + EPS) * g


def _dot(a, b):
    return jnp.dot(a, b, preferred_element_type=F32)


def _dot_nt(a, b):
    return lax.dot_general(a, b, (((1,), (1,)), ((), ())), preferred_element_type=F32)


def _norm_matmul_kernel(x_ref, g_ref, *refs, n_w, chunk, scaled):
    w_refs, o_refs = refs[:n_w], refs[n_w:]
    h = _rms(x_ref[...], g_ref[...]).astype(BF16)
    s0, s1, scale = scaled
    for k, (w_ref, o_ref) in enumerate(zip(w_refs, o_refs)):
        n = w_ref.shape[1]
        for c0 in range(0, n, chunk):
            c1 = min(c0 + chunk, n)
            y = _dot(h, w_ref[:, c0:c1])
            if k == 0 and s0 <= c0 and c1 <= s1:
                y = y * scale
            o_ref[:, c0:c1] = y.astype(o_ref.dtype)


def norm_matmul(x, g, ws, out_dtypes, *, tm, name, scaled=(0, 0, 1.0)):
    t, d = x.shape
    chunk = 512
    assert scaled[0] % chunk == 0 and scaled[1] % chunk == 0
    in_specs = [pl.BlockSpec((tm, d), lambda i: (i, 0)), pl.BlockSpec((1, d), lambda i: (0, 0))]
    in_specs += [pl.BlockSpec(w.shape, lambda i: (0, 0)) for w in ws]
    out_specs = [pl.BlockSpec((tm, w.shape[1]), lambda i: (i, 0)) for w in ws]
    out_shape = [jax.ShapeDtypeStruct((t, w.shape[1]), dt) for w, dt in zip(ws, out_dtypes)]
    return pl.pallas_call(
        functools.partial(_norm_matmul_kernel, n_w=len(ws), chunk=chunk, scaled=scaled),
        grid=(t // tm,), in_specs=in_specs, out_specs=out_specs, out_shape=out_shape,
        compiler_params=_params("parallel"), name=name,
    )(x, g.reshape(1, d), *ws)


def _gate_kernel(g_ref, b_ref, c_ref):
    s = g_ref.shape[0]
    row = lax.broadcasted_iota(jnp.int32, (LANES, LANES), 0)
    col = lax.broadcasted_iota(jnp.int32, (LANES, LANES), 1)
    tri = (row >= col).astype(F32)
    carry = jnp.zeros((1, LANES), F32)
    for blk in range(s // LANES):
        z = g_ref[blk * LANES:(blk + 1) * LANES, :] + b_ref[...]
        log_f = jnp.minimum(z, 0.0) - jnp.log1p(jnp.exp(-jnp.abs(z)))
        cs = jnp.dot(tri, log_f, precision=lax.Precision.HIGHEST, preferred_element_type=F32) + carry
        c_ref[blk * LANES:(blk + 1) * LANES, :] = cs
        carry = cs[LANES - 1:LANES, :]


def gate_cumsum(g, b, *, seq):
    t = g.shape[0]
    return pl.pallas_call(
        _gate_kernel, grid=(t // seq,),
        in_specs=[pl.BlockSpec((seq, LANES), lambda i: (i, 0)), pl.BlockSpec((1, LANES), lambda i: (0, 0))],
        out_specs=pl.BlockSpec((seq, LANES), lambda i: (i, 0)),
        out_shape=jax.ShapeDtypeStruct((t, LANES), F32),
        compiler_params=_params("parallel"), name="gate_cumsum",
    )(g, b)


def _split3(x):
    hi = x.astype(BF16).astype(F32)
    rest = x - hi
    mid = rest.astype(BF16).astype(F32)
    lo = (rest - mid).astype(BF16).astype(F32)
    return hi, mid, lo


def _augment(x, in_half, lane, base, pieces, pieces_first):
    n = len(pieces)
    p0, o0 = (base, base + n) if pieces_first else (base + n, base)
    aug = jnp.where((lane >= o0) & (lane < o0 + n), 1.0, 0.0)
    for idx, piece in enumerate(pieces):
        aug = jnp.where(lane == p0 + idx, piece, aug)
    return jnp.where(in_half, x, aug.astype(x.dtype))


def _halves(lane):
    return [(lane >= HEAD_DIM * hh) & (lane < HEAD_DIM * (hh + 1)) for hh in range(2)]


def _causal_attention(qa, ka_s, v_ref, bias_ref, i, tq):
    n_chunks = tq // LANES
    row = lax.broadcasted_iota(jnp.int32, (tq, tq), 0)
    col = lax.broadcasted_iota(jnp.int32, (tq, tq), 1)

    def scores(j):
        s = _dot_nt(qa, ka_s[j * tq:(j + 1) * tq, :])
        if bias_ref is not None and j >= i - 1:
            s = s + bias_ref[0, i - j]
        if j == i:
            s = jnp.where(row >= col, s, -jnp.inf)
        return [s[:, c * LANES:(c + 1) * LANES] for c in range(n_chunks)]

    m = jnp.full((tq, LANES), -jnp.inf, F32)
    for j in range(i + 1):
        for chunk in scores(j):
            m = jnp.maximum(m, chunk)
    m = jnp.broadcast_to(jnp.max(m, axis=1, keepdims=True), (tq, LANES))
    l = jnp.zeros((tq, LANES), F32)
    acc = jnp.zeros((tq, LANES), F32)
    for j in range(i + 1):
        ps = [jnp.exp2(chunk - m) for chunk in scores(j)]
        l = l + functools.reduce(lambda a, b: a + b, ps)
        acc = acc + _dot(jnp.concatenate(ps, axis=1).astype(BF16), v_ref[j * tq:(j + 1) * tq, :])
    return acc / jnp.sum(l, axis=1, keepdims=True)


def _rider(w, grid):
    steps = math.prod(grid)
    rows = w.shape[0] // steps
    assert rows * steps == w.shape[0] and rows % BF16_ROWS == 0

    def index(*ids):
        step = 0
        for n, idx in zip(grid, ids):
            step = step * n + idx
        return step, 0

    return pl.BlockSpec((rows, w.shape[1]), index), jax.ShapeDtypeStruct(w.shape, BF16)


def _fox_kernel(q_ref, k_ref, v_ref, c_ref, w_ref, o_ref, wcast_ref, ka_s, *, tq):
    wcast_ref[...] = w_ref[...].astype(BF16)
    hp = pl.program_id(1)
    lane = lax.broadcasted_iota(jnp.int32, (1, LANES), 1)
    halves = _halves(lane)

    def decay(c, hh):
        return jnp.sum(jnp.where(lane == 2 * hp + hh, c, 0.0), axis=1, keepdims=True) * LOG2E

    k = k_ref[...]
    c_all = c_ref[...]
    for hh in range(2):
        ka_s[hh] = _augment(k, halves[hh], lane, HEAD_DIM * (1 - hh), _split3(-decay(c_all, hh)), True)

    for i in range(q_ref.shape[0] // tq):
        rows = slice(i * tq, (i + 1) * tq)
        outs = []
        for hh in range(2):
            qa = _augment(q_ref[rows, :], halves[hh], lane, HEAD_DIM * (1 - hh), _split3(decay(c_ref[rows, :], hh)),
                          False)
            outs.append(_causal_attention(qa, ka_s.at[hh], v_ref, None, i, tq))
        o_ref[rows, :] = jnp.where(lane < HEAD_DIM, outs[0], outs[1]).astype(o_ref.dtype)


def fox_attention(qkv, c, w_ride, *, batch, seq, tq, q_col, k_col, v_col, n_pairs):
    t = batch * seq
    grid = (batch, n_pairs)
    ride_spec, ride_shape = _rider(w_ride, grid)
    block = lambda col: pl.BlockSpec((seq, LANES), lambda b, h: (b, col + h))
    return pl.pallas_call(
        functools.partial(_fox_kernel, tq=tq),
        grid=grid,
        in_specs=[block(q_col), block(k_col), block(v_col), pl.BlockSpec((seq, LANES), lambda b, h: (b, 0)),
                  ride_spec],
        out_specs=[block(0), ride_spec],
        out_shape=[jax.ShapeDtypeStruct((t, n_pairs * LANES), BF16), ride_shape],
        scratch_shapes=[pltpu.VMEM((2, seq, LANES), BF16)],
        compiler_params=_params("parallel", "parallel"), name="fox_attention",
    )(qkv, qkv, qkv, c, w_ride)


def _spatial_gate(u_ref, v_ref, norm_ref, ws_ref, bs_ref, o_ref):
    tb = u_ref.shape[0]
    n_groups, chunk, _ = ws_ref.shape
    row = lax.broadcasted_iota(jnp.int32, (chunk, chunk), 0)
    col = lax.broadcasted_iota(jnp.int32, (chunk, chunk), 1)
    tri = row >= col
    for g in range(n_groups):
        w = jnp.where(tri, ws_ref[g], 0.0).astype(BF16)
        bias = bs_ref[:, g:g + 1]
        gain = norm_ref[g:g + 1, :]
        for c in range(tb // chunk):
            rs = slice(c * chunk, (c + 1) * chunk)
            cs = slice(g * LANES, (g + 1) * LANES)
            vn = _rms(jax.nn.gelu(v_ref[rs, cs].astype(F32)), gain)
            mixed = _dot(w, vn.astype(BF16)) + bias
            o_ref[rs, cs] = (jax.nn.gelu(u_ref[rs, cs].astype(F32)) * mixed).astype(o_ref.dtype)


def _chunk_weights(w13, w2, tf):
    *lead, ff, d = w2.shape
    return w13.astype(BF16), w2.astype(BF16).reshape(*lead, ff // tf, tf, d)


def _swiglu_chunks(h_s, w13, w2, o_ref, gu_s):
    n, tf, _ = w2.shape

    def project(c, slot):
        for part in range(2):
            cols = pl.ds(pl.multiple_of((part * n + c) * tf, tf), tf)
            gu_s[slot, part] = _dot(h_s[...], w13[:, cols])

    def consume(c, slot):
        gate = gu_s[slot, 0]
        act = (gate * jax.nn.sigmoid(gate) * gu_s[slot, 1]).astype(BF16)
        o_ref[...] += _dot(act, w2[c])

    def pair(k, carry):
        c = 2 * k
        project(c + 1, 1)
        consume(c, 0)
        project(c + 2, 0)
        consume(c + 1, 1)
        return carry

    project(0, 0)
    lax.fori_loop(0, (n - 1) // 2, pair, 0)
    if n % 2 == 0:
        project(n - 1, 1)
        consume(n - 2, 0)
        consume(n - 1, 1)
    else:
        consume(n - 1, 0)


def _swiglu_scratch(tm, d, tf):
    return [pltpu.VMEM((tm, d), BF16), pltpu.VMEM((2, 2, tm, tf), F32)]


def _tail_kernel(*refs, n_heads, dh, gated, ffn, zero_fill):
    refs = list(refs)
    take = lambda n: [refs.pop(0) for _ in range(n)]
    x_ref, a_ref = take(2)
    b_in = take(5 if gated else 1)
    wa_ref, wb_ref, gx_ref, wq_ref, kv_ref, wo_ref = take(6)
    ffn_in = take(3 if ffn else 0)
    (o_ref,) = take(1)
    for z_ref in take(1 if zero_fill else 0):
        z_ref[...] = jnp.zeros_like(z_ref)
    if gated:
        (b_ref,) = take(1)
        _spatial_gate(*b_in, b_ref)
    else:
        (b_ref,) = b_in
    rest = ffn_in + [o_ref] + refs
    x = x_ref[...] + _dot(a_ref[...], wa_ref[...]) + _dot(b_ref[...], wb_ref[...])
    q = _dot(_rms(x, gx_ref[...]).astype(BF16), wq_ref[...]).astype(BF16)
    width = n_heads * dh
    outs = []
    for hd in range(n_heads):
        cs = slice(hd * dh, (hd + 1) * dh)
        s = _dot_nt(q[:, cs], kv_ref[:, cs]) * (dh ** -0.5)
        p = jnp.exp(s - jnp.max(s, axis=1, keepdims=True))
        p = p / jnp.sum(p, axis=1, keepdims=True)
        outs.append(_dot(p.astype(BF16), kv_ref[:, width + hd * dh:width + (hd + 1) * dh]).astype(BF16))
    x = x + _dot(jnp.concatenate(outs, axis=1), wo_ref[...])
    if len(rest) == 1:
        (o_ref,) = rest
        o_ref[...] = x
    else:
        gf_ref, w13_ref, w2_ref, o_ref, h_s, gu_s = rest
        h_s[...] = _rms(x, gf_ref[...]).astype(BF16)
        o_ref[...] = x
        _swiglu_chunks(h_s, w13_ref, w2_ref, o_ref, gu_s)


def layer_tail(x, a, b, w_out, gx, wq, kv, wo, ffn=None, zero_rows=0, *, tm, seq, mem_len, kv_col, n_heads, dh,
               name):
    t, d = x.shape
    per_b = seq // tm
    resident = pl.Buffered(1)
    const = lambda arr: pl.BlockSpec(arr.shape, lambda i: (0,) * arr.ndim, pipeline_mode=resident)
    rows = lambda arr: pl.BlockSpec((tm, arr.shape[1]), lambda i: (i, 0))
    vec = lambda g: g.reshape(1, d)
    gated = isinstance(b, tuple)
    scratch = []
    if gated:
        proj, u_col, v_col, sgu_norm, w_s, b_s = b
        b_width = w_s.shape[0] * LANES
        b_args = [proj, proj, sgu_norm, w_s, b_s.T]
        b_specs = [pl.BlockSpec((tm, b_width), lambda i: (i, u_col)), pl.BlockSpec((tm, b_width), lambda i: (i, v_col)),
                   const(sgu_norm), const(w_s), const(b_s.T)]
        scratch.append(pltpu.VMEM((tm, b_width), BF16))
    else:
        b_width = b.shape[1]
        b_args, b_specs = [b], [rows(b)]
    wa, wb = w_out[:a.shape[1]], w_out[a.shape[1]:]
    assert wb.shape[0] == b_width
    args = [x, a] + b_args + [wa, wb, vec(gx), wq, kv, wo]
    in_specs = [rows(x), rows(a)] + b_specs + [
        const(wa), const(wb), const(vec(gx)), const(wq),
        pl.BlockSpec((mem_len, 2 * n_heads * dh), lambda i: (i // per_b, kv_col)), const(wo)]
    if ffn is not None:
        gf, w13, w2 = ffn
        args += [vec(gf), w13, w2]
        in_specs += [const(vec(gf)), const(w13), const(w2)]
        scratch += _swiglu_scratch(tm, d, w2.shape[1])
    out_specs, out_shape = [rows(x)], [jax.ShapeDtypeStruct((t, d), F32)]
    if zero_rows:
        per_step = zero_rows // (t // tm)
        assert per_step * (t // tm) == zero_rows and per_step % BF16_ROWS == 0
        out_specs.append(pl.BlockSpec((per_step, d), lambda i: (i, 0)))
        out_shape.append(jax.ShapeDtypeStruct((zero_rows, d), BF16))
    outs = pl.pallas_call(
        functools.partial(_tail_kernel, n_heads=n_heads, dh=dh, gated=gated, ffn=ffn is not None,
                          zero_fill=bool(zero_rows)),
        grid=(t // tm,), in_specs=in_specs, out_specs=out_specs, out_shape=out_shape,
        scratch_shapes=scratch, compiler_params=_params("parallel"), name=name,
    )(*args)
    return outs if zero_rows else outs[0]


def _conv_proj_kernel(x_ref, g_ref, wc_ref, wa_ref, cw_ref, c_ref, o_ref, carry_s, *, steps_per_seq, scaled, chunk):
    h = _rms(x_ref[...], g_ref[...]).astype(BF16)
    tm, width = c_ref.shape
    n_taps = cw_ref.shape[0]
    bg, cg, xi = [_dot(h, wc_ref[:, k * width:(k + 1) * width]) for k in range(3)]
    xc = cg * xi
    first = pl.program_id(0) % steps_per_seq == 0
    prev = jnp.where(first, 0.0, carry_s[...])
    row = lax.broadcasted_iota(jnp.int32, (tm, width), 0)
    y = cw_ref[n_taps - 1:n_taps, :] * xc
    for back in range(1, n_taps):
        shifted = pltpu.roll(xc, back, axis=0)
        for r in range(back):
            src = SUBLANES - back + r
            shifted = jnp.where(row == r, prev[src:src + 1, :], shifted)
        y = y + cw_ref[n_taps - 1 - back:n_taps - back, :] * shifted
    c_ref[...] = (bg * y).astype(c_ref.dtype)
    carry_s[...] = xc[tm - SUBLANES:tm, :]
    s0, s1, scale = scaled
    for c0 in range(0, wa_ref.shape[1], chunk):
        y = _dot(h, wa_ref[:, c0:c0 + chunk])
        if s0 <= c0 and c0 + chunk <= s1:
            y = y * scale
        o_ref[:, c0:c0 + chunk] = y.astype(o_ref.dtype)


def conv_proj(x, g, w_conv, w_attn, conv_w, *, tm, seq, scaled):
    t, d = x.shape
    width = conv_w.shape[1]
    chunk = 512
    assert seq % tm == 0 and w_attn.shape[1] % chunk == 0 and conv_w.shape[0] - 1 <= SUBLANES
    const = lambda arr: pl.BlockSpec(arr.shape, lambda i: (0, 0))
    return pl.pallas_call(
        functools.partial(_conv_proj_kernel, steps_per_seq=seq // tm, scaled=scaled, chunk=chunk),
        grid=(t // tm,),
        in_specs=[pl.BlockSpec((tm, d), lambda i: (i, 0)), const(g.reshape(1, d)), const(w_conv), const(w_attn),
                  const(conv_w)],
        out_specs=[pl.BlockSpec((tm, width), lambda i: (i, 0)), pl.BlockSpec((tm, w_attn.shape[1]), lambda i: (i, 0))],
        out_shape=[jax.ShapeDtypeStruct((t, width), BF16), jax.ShapeDtypeStruct((t, w_attn.shape[1]), BF16)],
        scratch_shapes=[pltpu.VMEM((SUBLANES, width), F32)],
        compiler_params=_params("arbitrary"), name="odd_in_proj",
    )(x, g.reshape(1, d), w_conv, w_attn, conv_w)


def _diff_prep_kernel(rb_ref, lq1_ref, lk1_ref, lq2_ref, lk2_ref, bias_ref, far_ref, lam_ref, *, tq, lam_init):
    n_heads = bias_ref.shape[0]
    strip = 32
    row = lax.broadcasted_iota(jnp.int32, (strip, tq), 0)
    col = lax.broadcasted_iota(jnp.int32, (strip, tq), 1)
    max_exact = N_BUCKETS // 2

    def fill(r, carry):
        r0 = pl.multiple_of(r * strip, strip)
        for which in range(2):
            n = jnp.maximum(row + r0 - col + which * tq, 0)
            nf = jnp.maximum(n, 1).astype(F32)
            large = max_exact + (jnp.log(nf / max_exact) / math.log(MAX_DIST / max_exact)
                                 * (N_BUCKETS - max_exact)).astype(jnp.int32)
            large = jnp.minimum(large, N_BUCKETS - 1)
            bucket = jnp.where(n < max_exact, n, large)
            for h in range(n_heads):
                b = jnp.zeros((strip, tq), F32)
                for kk in range(N_BUCKETS):
                    b = jnp.where(bucket == kk, rb_ref[kk, h], b)
                bias_ref[h, which, pl.ds(r0, strip), :] = (b - rb_ref[N_BUCKETS - 1, h]) * LOG2E
        return carry

    lax.fori_loop(0, tq // strip, fill, 0)
    for h in range(n_heads):
        far_ref[h] = jnp.full((1, LANES), rb_ref[N_BUCKETS - 1, h], F32) * LOG2E
    lam = (jnp.exp(jnp.sum(lq1_ref[...] * lk1_ref[...], axis=1, keepdims=True))
           - jnp.exp(jnp.sum(lq2_ref[...] * lk2_ref[...], axis=1, keepdims=True)) + lam_init)
    lam_ref[...] = jnp.broadcast_to(lam, (1, LANES))


def diff_prep(rel_bias, lq1, lk1, lq2, lk2, *, tq, lam_init):
    n_heads = rel_bias.shape[1]
    vec = lambda a: a.reshape(1, -1)
    vspec = pl.BlockSpec(memory_space=pltpu.VMEM)
    return pl.pallas_call(
        functools.partial(_diff_prep_kernel, tq=tq, lam_init=lam_init),
        in_specs=[pl.BlockSpec(memory_space=pltpu.SMEM), vspec, vspec, vspec, vspec],
        out_specs=[vspec, vspec, vspec],
        out_shape=[
            jax.ShapeDtypeStruct((n_heads, 2, tq, tq), F32),
            jax.ShapeDtypeStruct((n_heads, 1, LANES), F32),
            jax.ShapeDtypeStruct((1, LANES), F32),
        ],
        compiler_params=pltpu.CompilerParams(vmem_limit_bytes=VMEM_LIMIT), name="diff_prep",
    )(rel_bias, vec(lq1), vec(lk1), vec(lq2), vec(lk2))


def _diff_kernel(q_ref, k_ref, v_ref, bias_ref, far_ref, lam_ref, subln_ref, w_ref, o_ref, wcast_ref,
                 ka_s, *, tq, lam_init):
    wcast_ref[...] = w_ref[...].astype(BF16)
    lane = lax.broadcasted_iota(jnp.int32, (1, LANES), 1)
    halves = _halves(lane)
    k = k_ref[...]
    far = _split3(far_ref[0])
    for sub in range(2):
        ka_s[sub] = _augment(k, halves[sub], lane, HEAD_DIM * (1 - sub), far, True)

    zero = jnp.zeros((1, LANES), F32)
    for i in range(q_ref.shape[0] // tq):
        rows = slice(i * tq, (i + 1) * tq)
        outs = []
        for sub in range(2):
            qa = _augment(q_ref[rows, :], halves[sub], lane, HEAD_DIM * (1 - sub), (zero,) * 3, False)
            outs.append(_causal_attention(qa, ka_s.at[sub], v_ref, bias_ref, i, tq))
        o = outs[0] - lam_ref[...] * outs[1]
        o_ref[rows, :] = (_rms(o, subln_ref[...]) * (1.0 - lam_init)).astype(o_ref.dtype)


def diff_attention(main, bias, far, lam, subln, w_ride, *, batch, seq, tq, q_col, k_col, v_col, lam_init):
    t = batch * seq
    n_heads = bias.shape[0]
    grid = (batch, n_heads)
    ride_spec, ride_shape = _rider(w_ride, grid)
    block = lambda col: pl.BlockSpec((seq, LANES), lambda b, h: (b, col + h))
    return pl.pallas_call(
        functools.partial(_diff_kernel, tq=tq, lam_init=lam_init),
        grid=grid,
        in_specs=[
            block(q_col), block(k_col), block(v_col),
            pl.BlockSpec((1, 2, tq, tq), lambda b, h: (h, 0, 0, 0)),
            pl.BlockSpec((1, 1, LANES), lambda b, h: (h, 0, 0)),
            pl.BlockSpec((1, LANES), lambda b, h: (0, 0)),
            pl.BlockSpec((1, LANES), lambda b, h: (0, 0)),
            ride_spec,
        ],
        out_specs=[block(0), ride_spec],
        out_shape=[jax.ShapeDtypeStruct((t, n_heads * LANES), BF16), ride_shape],
        scratch_shapes=[pltpu.VMEM((2, seq, LANES), BF16)],
        compiler_params=_params("parallel", "parallel"), name="diff_attention",
    )(main, main, main, bias, far, lam, subln.reshape(1, LANES), w_ride)


def _router_kernel(x_ref, g_ref, wr_ref, meta_ref, wts_ref, before_ref, cnt_ref, carry_s, *, n_exp, sub):
    @pl.when(pl.program_id(0) == 0)
    def _():
        carry_s[...] = jnp.zeros_like(carry_s)

    lane = lax.broadcasted_iota(jnp.int32, (sub, LANES), 1)
    lane_f = lane.astype(F32)
    row = lax.broadcasted_iota(jnp.int32, (sub, sub), 0)
    col = lax.broadcasted_iota(jnp.int32, (sub, sub), 1)
    earlier = jnp.where(row > col, 1.0, 0.0).astype(BF16)
    pick = lambda sel, val: jnp.sum(jnp.where(sel, val, 0.0), axis=1, keepdims=True)
    carry = carry_s[...]
    for r in range(x_ref.shape[0] // sub):
        rows = slice(r * sub, (r + 1) * sub)
        before_ref[r * SUBLANES:(r + 1) * SUBLANES, :] = jnp.broadcast_to(carry, (SUBLANES, LANES))
        h = _rms(x_ref[rows, :], g_ref[...])
        h_hi = h.astype(BF16)
        h_lo = (h - h_hi.astype(F32)).astype(BF16)
        logits = _dot(h_hi, wr_ref[0]) + (_dot(h_hi, wr_ref[1]) + _dot(h_lo, wr_ref[0]))
        logits = jnp.where(lane < n_exp, logits, -jnp.inf)
        m1 = jnp.max(logits, axis=1, keepdims=True)
        i1 = jnp.min(jnp.where(logits == m1, lane_f, float(LANES)), axis=1, keepdims=True)
        rest = jnp.where(lane_f == i1, -jnp.inf, logits)
        m2 = jnp.max(rest, axis=1, keepdims=True)
        i2 = jnp.min(jnp.where(rest == m2, lane_f, float(LANES)), axis=1, keepdims=True)
        e = jnp.exp(m2 - m1)
        sel1 = lane_f == i1
        sel2 = lane_f == i2
        onehot = jnp.where(sel1 | sel2, 1.0, 0.0)
        local = _dot(earlier, onehot.astype(BF16))
        meta = jnp.zeros((sub, LANES), F32)
        for idx, field in enumerate([i1, i2, pick(sel1, local), pick(sel2, local)]):
            meta = jnp.where(lane == idx, field, meta)
        meta_ref[rows, :] = meta.astype(jnp.int32)
        wts_ref[rows, :] = jnp.where(lane == 0, 1.0 / (1.0 + e), jnp.where(lane == 1, e / (1.0 + e), 0.0))
        carry = carry + jnp.sum(onehot, axis=0, keepdims=True)
    carry_s[...] = carry
    cnt_ref[...] = carry


def route_tokens(x, g, wr, *, tm, sub, n_exp):
    t, d = x.shape
    per_step = tm // sub * SUBLANES
    return pl.pallas_call(
        functools.partial(_router_kernel, n_exp=n_exp, sub=sub),
        grid=(t // tm,),
        in_specs=[
            pl.BlockSpec((tm, d), lambda i: (i, 0)),
            pl.BlockSpec((1, d), lambda i: (0, 0)),
            pl.BlockSpec((2, d, LANES), lambda i: (0, 0, 0)),
        ],
        out_specs=[
            pl.BlockSpec((tm, LANES), lambda i: (i, 0)),
            pl.BlockSpec((tm, LANES), lambda i: (i, 0)),
            pl.BlockSpec((per_step, LANES), lambda i: (i, 0)),
            pl.BlockSpec((1, LANES), lambda i: (0, 0)),
        ],
        out_shape=[
            jax.ShapeDtypeStruct((t, LANES), jnp.int32),
            jax.ShapeDtypeStruct((t, LANES), F32),
            jax.ShapeDtypeStruct((t // tm * per_step, LANES), F32),
            jax.ShapeDtypeStruct((1, LANES), F32),
        ],
        scratch_shapes=[pltpu.VMEM((1, LANES), F32)],
        compiler_params=_params("arbitrary"), name="moe_router",
    )(x, g.reshape(1, d), wr)


def _dispatch_kernel(start_ref, shift_ref, keep_ref, x_ref, g_ref, meta_ref, zeros_hbm, xs_hbm,
                     stage_s, carry_s, sems, *, tm, n_exp):
    del zeros_hbm
    j = pl.program_id(0)
    slot = j % 2
    rows = tm + BF16_ROWS

    def block_copy(step, e, buf):
        first = pl.multiple_of(start_ref[step * n_exp + e], BF16_ROWS)
        return pltpu.make_async_copy(stage_s.at[buf, e], xs_hbm.at[pl.ds(first, rows)], sems.at[buf])

    @pl.when(j == 0)
    def _():
        carry_s[...] = jnp.zeros_like(carry_s)

    h = _rms(x_ref[...], g_ref[...]).astype(BF16)
    fields = meta_ref[...].astype(F32).T
    slot_row = lax.broadcasted_iota(jnp.int32, (rows, tm), 0).astype(F32)
    for e in range(n_exp):
        key = j * n_exp + e
        idx = jnp.where(fields[0:1] == e, fields[2:3], jnp.where(fields[1:2] == e, fields[3:4], -2.0 * rows))
        idx = idx + shift_ref[key].astype(F32)
        onehot = jnp.where(slot_row == idx, 1.0, 0.0).astype(BF16)
        stage_s[slot, e] = _dot(onehot, h).astype(BF16)
        stage_s[slot, e, 0:BF16_ROWS, :] += carry_s[e]
        keep = pl.multiple_of(keep_ref[key], BF16_ROWS)
        carry_s[e] = stage_s[slot, e, pl.ds(keep, BF16_ROWS), :]

    @pl.when(j > 0)
    def _():
        for e in range(n_exp):
            block_copy(j - 1, e, 1 - slot).wait()

    for e in range(n_exp):
        block_copy(j, e, slot).start()

    @pl.when(j == pl.num_programs(0) - 1)
    def _():
        for e in range(n_exp):
            block_copy(j, e, slot).wait()


def moe_dispatch(starts, shifts, keeps, x, g, meta, zeros, *, tm, n_exp):
    t, d = x.shape
    n_rows = zeros.shape[0]
    rows = tm + BF16_ROWS
    return pl.pallas_call(
        functools.partial(_dispatch_kernel, tm=tm, n_exp=n_exp),
        grid_spec=pltpu.PrefetchScalarGridSpec(
            num_scalar_prefetch=3, grid=(t // tm,),
            in_specs=[
                pl.BlockSpec((tm, d), lambda i, *_: (i, 0)),
                pl.BlockSpec((1, d), lambda i, *_: (0, 0)),
                pl.BlockSpec((tm, LANES), lambda i, *_: (i, 0)),
                pl.BlockSpec(memory_space=pl.ANY),
            ],
            out_specs=pl.BlockSpec(memory_space=pl.ANY),
            scratch_shapes=[pltpu.VMEM((2, n_exp, rows, d), BF16), pltpu.VMEM((n_exp, BF16_ROWS, d), BF16),
                            pltpu.SemaphoreType.DMA((2,))],
        ),
        out_shape=jax.ShapeDtypeStruct((n_rows, d), BF16),
        input_output_aliases={6: 0},
        compiler_params=_params("arbitrary"),
        name="moe_dispatch",
    )(starts, shifts, keeps, x, g.reshape(1, d), meta, zeros)


def _expert_kernel(te_ref, used_ref, h_ref, w13_ref, w2_ref, o_ref, gu_s, acc_s):
    del te_ref
    used = used_ref[pl.program_id(0)] != 0

    @pl.when(used)
    def _():
        acc_s[...] = jnp.zeros_like(acc_s)
        _swiglu_chunks(h_ref, w13_ref.at[0], w2_ref.at[0], acc_s, gu_s)
        o_ref[...] = acc_s[...].astype(o_ref.dtype)

    @pl.when(jnp.logical_not(used))
    def _():
        o_ref[...] = jnp.zeros_like(o_ref)


def moe_experts(tile_expert, tile_used, hs, w13, w2, *, tm):
    n_rows, d = hs.shape
    tf = w2.shape[2]
    resident = pl.Buffered(1)
    return pl.pallas_call(
        _expert_kernel,
        grid_spec=pltpu.PrefetchScalarGridSpec(
            num_scalar_prefetch=2, grid=(n_rows // tm,),
            in_specs=[
                pl.BlockSpec((tm, d), lambda i, te, tu: (i, 0)),
                pl.BlockSpec((1,) + w13.shape[1:], lambda i, te, tu: (te[i], 0, 0)),
                pl.BlockSpec((1,) + w2.shape[1:], lambda i, te, tu: (te[i], 0, 0, 0)),
            ],
            out_specs=pl.BlockSpec((tm, d), lambda i, te, tu: (i, 0)),
            scratch_shapes=[pltpu.VMEM((2, 2, tm, tf), F32), pltpu.VMEM((tm, d), F32)],
        ),
        out_shape=jax.ShapeDtypeStruct((n_rows, d), BF16),
        compiler_params=pltpu.CompilerParams(dimension_semantics=("arbitrary",), vmem_limit_bytes=EXPERT_VMEM_LIMIT),
        name="moe_experts",
    )(tile_expert, tile_used, hs, w13, w2)


def _combine_kernel(start_ref, shift_ref, tail_ref, x_ref, wts_ref, meta_ref, g_ref, y_hbm, o_ref, blk_s, sems, *,
                    tm, n_exp):
    j = pl.program_id(0)
    slot = j % 2
    rows = tm + BF16_ROWS

    def block_copy(step, e, buf):
        first = pl.multiple_of(start_ref[step * n_exp + e], BF16_ROWS)
        return pltpu.make_async_copy(y_hbm.at[pl.ds(first, rows)], blk_s.at[buf, e], sems.at[buf])

    def fetch(step, buf):
        for e in range(n_exp):
            block_copy(step, e, buf).start()

    @pl.when(j == 0)
    def _():
        fetch(0, 0)

    @pl.when(j + 1 < pl.num_programs(0))
    def _():
        fetch(j + 1, 1 - slot)

    for e in range(n_exp):
        block_copy(j, e, slot).wait()

    meta = meta_ref[...]
    wts = wts_ref[...]
    col = lax.broadcasted_iota(jnp.int32, (tm, tm), 1)
    tail_col = lax.broadcasted_iota(jnp.int32, (tm, BF16_ROWS), 1) + tm

    def pick(e):
        sel = [meta[:, k:k + 1] == e for k in range(TOP_K)]
        idx = jnp.where(sel[0], meta[:, 2:3], jnp.where(sel[1], meta[:, 3:4], -2 * rows)) + shift_ref[j * n_exp + e]
        w = jnp.where(sel[0], wts[:, 0:1], jnp.where(sel[1], wts[:, 1:2], 0.0))
        return idx, w

    acc = x_ref[...]
    for e in range(n_exp):
        idx, w = pick(e)
        onehot = jnp.where(col == idx, 1.0, 0.0).astype(BF16)
        acc = acc + w * _dot(onehot, blk_s[slot, e, 0:tm, :])
    o_ref[...] = acc

    for e in range(n_exp):
        @pl.when(tail_ref[j * n_exp + e] != 0)
        def _(e=e):
            idx, w = pick(e)
            onehot_tail = jnp.where(tail_col == idx, 1.0, 0.0).astype(BF16)
            o_ref[...] += w * _dot(onehot_tail, blk_s[slot, e, tm:rows, :])

    o_ref[...] = _rms(o_ref[...], g_ref[...])


def moe_combine(starts, shifts, tails, x, wts, meta, g, y, *, tm, n_exp):
    t, d = x.shape
    n_pre = 3
    return pl.pallas_call(
        functools.partial(_combine_kernel, tm=tm, n_exp=n_exp),
        grid_spec=pltpu.PrefetchScalarGridSpec(
            num_scalar_prefetch=n_pre, grid=(t // tm,),
            in_specs=[
                pl.BlockSpec((tm, d), lambda i, *_: (i, 0)),
                pl.BlockSpec((tm, LANES), lambda i, *_: (i, 0)),
                pl.BlockSpec((tm, LANES), lambda i, *_: (i, 0)),
                pl.BlockSpec((1, d), lambda i, *_: (0, 0)),
                pl.BlockSpec(memory_space=pl.ANY),
            ],
            out_specs=pl.BlockSpec((tm, d), lambda i, *_: (i, 0)),
            scratch_shapes=[pltpu.VMEM((2, n_exp, tm + BF16_ROWS, d), BF16), pltpu.SemaphoreType.DMA((2,))],
        ),
        out_shape=jax.ShapeDtypeStruct((t, d), F32),
        compiler_params=_params("arbitrary"), name="moe_combine",
    )(starts, shifts, tails, x, wts, meta, g.reshape(1, d), y)


def _pad_cols(w, n):
    return jnp.pad(w, ((0, 0), (0, n - w.shape[1])))


def _even_mixer(x, norm, w_in, b_f, sgu_norm, w_s, b_s, w_ride, *, batch, seq, tq):
    n_heads = b_f.shape[0]
    a_width = n_heads * HEAD_DIM
    b_width = w_s.shape[0] * LANES
    f0 = 3 * a_width
    w_main = jnp.concatenate([w_in[:, :f0], w_in[:, f0 + n_heads:]], axis=1)
    w_gate = _pad_cols(w_in[:, f0:f0 + n_heads], LANES)
    main, gate = norm_matmul(x, norm, [w_main.astype(BF16), w_gate.astype(BF16)], [BF16, F32],
                             tm=min(IN_PROJ_ROWS, x.shape[0]), name="even_in_proj",
                             scaled=(0, a_width, LOG2E * HEAD_DIM ** -0.5))
    c = gate_cumsum(gate, _pad_cols(b_f.reshape(1, -1), LANES), seq=seq)
    n_pairs = a_width // LANES
    a, w_cast = fox_attention(main, c, w_ride, batch=batch, seq=seq, tq=tq, q_col=0, k_col=n_pairs,
                              v_col=2 * n_pairs, n_pairs=n_pairs)
    u_col = f0 // b_width
    return (a, (main, u_col, u_col + 1, sgu_norm, w_s, b_s)), w_cast


def _odd_mixer(x, norm, w_in, conv_w, lq1, lk1, lq2, lk2, subln, rel_bias, lam_init, w_ride, *, batch, seq, tq):
    c_width = conv_w.shape[1]
    d_width = rel_bias.shape[1] * 2 * HEAD_DIM
    q0 = 3 * c_width
    w_in = w_in.astype(BF16)
    c_out, main = conv_proj(x, norm, w_in[:, :q0], w_in[:, q0:], conv_w, tm=min(IN_PROJ_ROWS, seq), seq=seq,
                            scaled=(0, d_width, LOG2E * HEAD_DIM ** -0.5))
    bias, far, lam = diff_prep(rel_bias, lq1, lk1, lq2, lk2, tq=tq, lam_init=lam_init)
    n_heads = rel_bias.shape[1]
    d_out, w_cast = diff_attention(main, bias, far, lam, subln, w_ride, batch=batch, seq=seq, tq=tq, q_col=0,
                                   k_col=n_heads, v_col=2 * n_heads, lam_init=lam_init)
    return (c_out, d_out), w_cast


def _sorted_rows(t, n_exp):
    block_rows = ROUTE_ROWS + BF16_ROWS
    return TOP_K * t + n_exp * (EXPERT_ROWS + pl.cdiv(block_rows, EXPERT_ROWS) * EXPERT_ROWS)


def _moe_layer(x, norm, w_router, w13, w2, final_norm, zeros):
    t, d = x.shape
    n_exp = w_router.shape[1]
    tm_route = ROUTE_ROWS
    tm_expert = EXPERT_ROWS
    n_rows = zeros.shape[0]
    assert n_rows == _sorted_rows(t, n_exp)
    wr = _pad_cols(w_router, LANES)
    wr_hi = wr.astype(BF16)
    wr_split = jnp.stack([wr_hi, (wr - wr_hi.astype(F32)).astype(BF16)])
    meta, wts, before, counts = route_tokens(x, norm, wr_split, tm=ROUTE_STEP_ROWS, sub=tm_route, n_exp=n_exp)
    counts = counts[0, :n_exp].astype(jnp.int32)
    block_rows = tm_route + BF16_ROWS
    padded = (counts + block_rows + tm_expert - 1) // tm_expert * tm_expert
    ends = jnp.cumsum(padded)
    offsets = ends - padded
    tile_start = jnp.arange(n_rows // tm_expert, dtype=jnp.int32) * tm_expert
    tile_expert = jnp.minimum(jnp.sum(tile_start[:, None] >= ends[None, :], axis=1), n_exp - 1).astype(jnp.int32)
    tile_used = (tile_start < (offsets + counts)[tile_expert]).astype(jnp.int32)
    before = before[::SUBLANES, :n_exp].astype(jnp.int32)
    in_tile = jnp.concatenate([before[1:], counts[None]]) - before
    first = offsets[None, :] + before
    starts = first // BF16_ROWS * BF16_ROWS
    shifts = first - starts
    keeps = (shifts + in_tile) // BF16_ROWS * BF16_ROWS
    tails = (shifts + in_tile > tm_route).astype(jnp.int32)
    flat = lambda a: a.reshape(-1).astype(jnp.int32)
    hs = moe_dispatch(flat(starts), flat(shifts), flat(keeps), x, norm, meta, zeros, tm=tm_route, n_exp=n_exp)
    y = moe_experts(tile_expert, tile_used, hs, *_chunk_weights(w13, w2, EXPERT_CHUNK), tm=tm_expert)
    return moe_combine(flat(starts), flat(shifts), flat(tails), x, wts, meta, final_norm, y, tm=tm_route,
                       n_exp=n_exp)


def kernel(x, mem, rel_bias, mem_norm, final_norm, ev_norm, ev_w_in, ev_b_f, ev_sgu_norm, ev_w_s, ev_b_s, ev_w_out, ffn_w13, ffn_w2, od_norm, od_w_in, od_conv_w, od_lam_q1, od_lam_k1, od_lam_q2, od_lam_k2, od_subln, od_w_out, moe_router, moe_w13, moe_w2, x_norm, x_wq, x_wkv, x_wo, ffn_norm):
    batch, seq, d = x.shape
    mem_len = mem.shape[1]
    depth = x_norm.shape[0]
    assert depth == 2 and ev_norm.shape[0] == 1 and od_norm.shape[0] == 1
    x_heads, x_dh = 4, 128
    xf = x.reshape(batch * seq, d)
    wkv = jnp.concatenate([x_wkv[layer] for layer in range(depth)], axis=1).astype(BF16)
    (kv,) = norm_matmul(mem.reshape(batch * mem_len, d), mem_norm, [wkv], [BF16], tm=MEM_ROWS, name="mem_kv")

    def tail(xf, mixed, w_out, layer, ffn, name, zero_rows=0):
        return layer_tail(xf, *mixed, w_out.astype(BF16), x_norm[layer], x_wq[layer].astype(BF16), kv,
                          x_wo[layer].astype(BF16), ffn, zero_rows, tm=TAIL_ROWS, seq=seq, mem_len=mem_len, kv_col=layer,
                          n_heads=x_heads, dh=x_dh, name=name)

    e13, e2 = moe_w13[0], moe_w2[0]
    mixed, e13_bf16 = _even_mixer(xf, ev_norm[0], ev_w_in[0], ev_b_f[0], ev_sgu_norm[0], ev_w_s[0], ev_b_s[0],
                                  e13.reshape(-1, e13.shape[-1]), batch=batch, seq=seq, tq=ATTN_TILE)
    xf = tail(xf, mixed, ev_w_out[0], 0, (ffn_norm[0],) + _chunk_weights(ffn_w13[0], ffn_w2[0], FFN_CHUNK),
              "even_tail")
    lam_init = 0.8 - 0.6 * math.exp(-0.3 * 1)
    mixed, e2_bf16 = _odd_mixer(xf, od_norm[0], od_w_in[0], od_conv_w[0], od_lam_q1[0], od_lam_k1[0],
                                od_lam_q2[0], od_lam_k2[0], od_subln[0], rel_bias, lam_init,
                                e2.reshape(-1, e2.shape[-1]), batch=batch, seq=seq, tq=ATTN_TILE)
    xf, zeros = tail(xf, mixed, od_w_out[0], 1, None, "odd_tail", _sorted_rows(xf.shape[0], moe_router.shape[-1]))
    out = _moe_layer(xf, ffn_norm[1], moe_router[0], e13_bf16.reshape(e13.shape), e2_bf16.reshape(e2.shape),
                     final_norm, zeros)
    return out.reshape(batch, seq, d)
```

```python
import functools
import math

import jax
import jax.numpy as jnp
from jax import lax
from jax.experimental import pallas as pl
from jax.experimental.pallas import tpu as pltpu

F32 = jnp.float32
BF16 = jnp.bfloat16
EPS = 1e-6
HEAD_DIM = 64
LANES = 128
SUBLANES = 8
BF16_ROWS = 16
N_BUCKETS = 32
MAX_DIST = 128
TOP_K = 2
LOG2E = 1.4426950408889634
ATTN_TILE = 512
IN_PROJ_ROWS = 1024
MEM_ROWS = 512
TAIL_ROWS = 512
ROUTE_ROWS = 256
ROUTE_STEP_ROWS = 512
EXPERT_ROWS = 512
FFN_CHUNK = 256
EXPERT_CHUNK = 512
VMEM_BYTES = 64 * 1024 * 1024
VMEM_LIMIT = VMEM_BYTES - 8 * 1024 * 1024
EXPERT_VMEM_LIMIT = VMEM_BYTES - 2 * 1024 * 1024


def _params(*sem):
    return pltpu.CompilerParams(dimension_semantics=sem, vmem_limit_bytes=VMEM_LIMIT)


def _rms(x, g):
    ms = jnp.mean(x * x, axis=-1, keepdims=True)
    return x * lax.rsqrt(ms + EPS) * g


def _dot(a, b):
    return jnp.dot(a, b, preferred_element_type=F32)


def _dot_nt(a, b):
    return lax.dot_general(a, b, (((1,), (1,)), ((), ())), preferred_element_type=F32)


def _norm_matmul_kernel(x_ref, g_ref, *refs, n_w, n_ride, chunk, scaled):
    w_refs, ride_in, o_refs = refs[:n_w], refs[n_w:n_w + n_ride], refs[n_w + n_ride:2 * n_w + n_ride]
    for src, dst in zip(ride_in, refs[2 * n_w + n_ride:]):
        dst[...] = src[...].astype(BF16)
    h = _rms(x_ref[...], g_ref[...]).astype(BF16)
    s0, s1, scale = scaled
    for k, (w_ref, o_ref) in enumerate(zip(w_refs, o_refs)):
        n = w_ref.shape[1]
        for c0 in range(0, n, chunk):
            c1 = min(c0 + chunk, n)
            y = _dot(h, w_ref[:, c0:c1])
            if k == 0 and s0 <= c0 and c1 <= s1:
                y = y * scale
            o_ref[:, c0:c1] = y.astype(o_ref.dtype)


def norm_matmul(x, g, ws, out_dtypes, *, tm, name, scaled=(0, 0, 1.0), rides=()):
    t, d = x.shape
    chunk = 512
    assert scaled[0] % chunk == 0 and scaled[1] % chunk == 0
    grid = (t // tm,)
    ride_specs = [_rider(w, grid) for w in rides]
    in_specs = [pl.BlockSpec((tm, d), lambda i: (i, 0)), pl.BlockSpec((1, d), lambda i: (0, 0))]
    in_specs += [pl.BlockSpec(w.shape, lambda i: (0, 0)) for w in ws] + [spec for spec, _ in ride_specs]
    out_specs = [pl.BlockSpec((tm, w.shape[1]), lambda i: (i, 0)) for w in ws] + [spec for spec, _ in ride_specs]
    out_shape = [jax.ShapeDtypeStruct((t, w.shape[1]), dt) for w, dt in zip(ws, out_dtypes)]
    out_shape += [shape for _, shape in ride_specs]
    return pl.pallas_call(
        functools.partial(_norm_matmul_kernel, n_w=len(ws), n_ride=len(rides), chunk=chunk, scaled=scaled),
        grid=grid, in_specs=in_specs, out_specs=out_specs, out_shape=out_shape,
        compiler_params=_params("parallel"), name=name,
    )(x, g.reshape(1, d), *ws, *rides)


def _gate_kernel(g_ref, b_ref, c_ref):
    s = g_ref.shape[0]
    row = lax.broadcasted_iota(jnp.int32, (LANES, LANES), 0)
    col = lax.broadcasted_iota(jnp.int32, (LANES, LANES), 1)
    tri = (row >= col).astype(F32)
    carry = jnp.zeros((1, LANES), F32)
    for blk in range(s // LANES):
        z = g_ref[blk * LANES:(blk + 1) * LANES, :] + b_ref[...]
        log_f = jnp.minimum(z, 0.0) - jnp.log1p(jnp.exp(-jnp.abs(z)))
        cs = jnp.dot(tri, log_f, precision=lax.Precision.HIGHEST, preferred_element_type=F32) + carry
        c_ref[blk * LANES:(blk + 1) * LANES, :] = cs
        carry = cs[LANES - 1:LANES, :]


def gate_cumsum(g, b, *, seq):
    t = g.shape[0]
    return pl.pallas_call(
        _gate_kernel, grid=(t // seq,),
        in_specs=[pl.BlockSpec((seq, LANES), lambda i: (i, 0)), pl.BlockSpec((1, LANES), lambda i: (0, 0))],
        out_specs=pl.BlockSpec((seq, LANES), lambda i: (i, 0)),
        out_shape=jax.ShapeDtypeStruct((t, LANES), F32),
        compiler_params=_params("parallel"), name="gate_cumsum",
    )(g, b)


def _split3(x):
    hi = x.astype(BF16).astype(F32)
    rest = x - hi
    mid = rest.astype(BF16).astype(F32)
    lo = (rest - mid).astype(BF16).astype(F32)
    return hi, mid, lo


def _augment(x, in_half, lane, base, pieces, pieces_first):
    n = len(pieces)
    p0, o0 = (base, base + n) if pieces_first else (base + n, base)
    aug = jnp.where((lane >= o0) & (lane < o0 + n), 1.0, 0.0)
    for idx, piece in enumerate(pieces):
        aug = jnp.where(lane == p0 + idx, piece, aug)
    return jnp.where(in_half, x, aug.astype(x.dtype))


def _halves(lane):
    return [(lane >= HEAD_DIM * hh) & (lane < HEAD_DIM * (hh + 1)) for hh in range(2)]


def _causal_attention(qa, ka_s, v_ref, bias_ref, i, tq):
    n_chunks = tq // LANES
    row = lax.broadcasted_iota(jnp.int32, (tq, tq), 0)
    col = lax.broadcasted_iota(jnp.int32, (tq, tq), 1)

    def scores(j):
        s = _dot_nt(qa, ka_s[j * tq:(j + 1) * tq, :])
        if bias_ref is not None and j >= i - 1:
            s = s + bias_ref[0, i - j]
        if j == i:
            s = jnp.where(row >= col, s, -jnp.inf)
        return [s[:, c * LANES:(c + 1) * LANES] for c in range(n_chunks)]

    m = jnp.full((tq, LANES), -jnp.inf, F32)
    for j in range(i + 1):
        for chunk in scores(j):
            m = jnp.maximum(m, chunk)
    m = jnp.broadcast_to(jnp.max(m, axis=1, keepdims=True), (tq, LANES))
    l = jnp.zeros((tq, LANES), F32)
    acc = jnp.zeros((tq, LANES), F32)
    for j in range(i + 1):
        ps = [jnp.exp2(chunk - m) for chunk in scores(j)]
        l = l + functools.reduce(lambda a, b: a + b, ps)
        acc = acc + _dot(jnp.concatenate(ps, axis=1).astype(BF16), v_ref[j * tq:(j + 1) * tq, :])
    return acc / jnp.sum(l, axis=1, keepdims=True)


def _rider(w, grid):
    steps = math.prod(grid)
    rows = w.shape[0] // steps
    assert rows * steps == w.shape[0] and rows % BF16_ROWS == 0

    def index(*ids):
        step = 0
        for n, idx in zip(grid, ids):
            step = step * n + idx
        return step, 0

    return pl.BlockSpec((rows, w.shape[1]), index), jax.ShapeDtypeStruct(w.shape, BF16)


def _fox_kernel(q_ref, k_ref, v_ref, c_ref, w_ref, o_ref, wcast_ref, ka_s, *, tq):
    wcast_ref[...] = w_ref[...].astype(BF16)
    hp = pl.program_id(1)
    lane = lax.broadcasted_iota(jnp.int32, (1, LANES), 1)
    halves = _halves(lane)

    def decay(c, hh):
        return jnp.sum(jnp.where(lane == 2 * hp + hh, c, 0.0), axis=1, keepdims=True) * LOG2E

    k = k_ref[...]
    c_all = c_ref[...]
    for hh in range(2):
        ka_s[hh] = _augment(k, halves[hh], lane, HEAD_DIM * (1 - hh), _split3(-decay(c_all, hh)), True)

    for i in range(q_ref.shape[0] // tq):
        rows = slice(i * tq, (i + 1) * tq)
        outs = []
        for hh in range(2):
            qa = _augment(q_ref[rows, :], halves[hh], lane, HEAD_DIM * (1 - hh), _split3(decay(c_ref[rows, :], hh)),
                          False)
            outs.append(_causal_attention(qa, ka_s.at[hh], v_ref, None, i, tq))
        o_ref[rows, :] = jnp.where(lane < HEAD_DIM, outs[0], outs[1]).astype(o_ref.dtype)


def fox_attention(qkv, c, w_ride, *, batch, seq, tq, q_col, k_col, v_col, n_pairs):
    t = batch * seq
    grid = (batch, n_pairs)
    ride_spec, ride_shape = _rider(w_ride, grid)
    block = lambda col: pl.BlockSpec((seq, LANES), lambda b, h: (b, col + h))
    return pl.pallas_call(
        functools.partial(_fox_kernel, tq=tq),
        grid=grid,
        in_specs=[block(q_col), block(k_col), block(v_col), pl.BlockSpec((seq, LANES), lambda b, h: (b, 0)),
                  ride_spec],
        out_specs=[block(0), ride_spec],
        out_shape=[jax.ShapeDtypeStruct((t, n_pairs * LANES), BF16), ride_shape],
        scratch_shapes=[pltpu.VMEM((2, seq, LANES), BF16)],
        compiler_params=_params("parallel", "parallel"), name="fox_attention",
    )(qkv, qkv, qkv, c, w_ride)


def _spatial_gate(u_ref, v_ref, norm_ref, ws_ref, bs_ref, o_ref):
    tb = u_ref.shape[0]
    n_groups, chunk, _ = ws_ref.shape
    row = lax.broadcasted_iota(jnp.int32, (chunk, chunk), 0)
    col = lax.broadcasted_iota(jnp.int32, (chunk, chunk), 1)
    tri = row >= col
    for g in range(n_groups):
        w = jnp.where(tri, ws_ref[g], 0.0).astype(BF16)
        bias = bs_ref[:, g:g + 1]
        gain = norm_ref[g:g + 1, :]
        for c in range(tb // chunk):
            rs = slice(c * chunk, (c + 1) * chunk)
            cs = slice(g * LANES, (g + 1) * LANES)
            vn = _rms(jax.nn.gelu(v_ref[rs, cs].astype(F32)), gain)
            mixed = _dot(w, vn.astype(BF16)) + bias
            o_ref[rs, cs] = (jax.nn.gelu(u_ref[rs, cs].astype(F32)) * mixed).astype(o_ref.dtype)


def _chunk_weights(w13, w2, tf):
    *lead, ff, d = w2.shape
    return w13.astype(BF16), w2.astype(BF16).reshape(*lead, ff // tf, tf, d)


def _swiglu_chunks(h_s, w13, w2, o_ref, gu_s):
    n, tf, _ = w2.shape

    def project(c, slot):
        for part in range(2):
            cols = pl.ds(pl.multiple_of((part * n + c) * tf, tf), tf)
            gu_s[slot, part] = _dot(h_s[...], w13[:, cols])

    def consume(c, slot):
        gate = gu_s[slot, 0]
        act = (gate * jax.nn.sigmoid(gate) * gu_s[slot, 1]).astype(BF16)
        o_ref[...] += _dot(act, w2[c])

    def pair(k, carry):
        c = 2 * k
        project(c + 1, 1)
        consume(c, 0)
        project(c + 2, 0)
        consume(c + 1, 1)
        return carry

    project(0, 0)
    lax.fori_loop(0, (n - 1) // 2, pair, 0)
    if n % 2 == 0:
        project(n - 1, 1)
        consume(n - 2, 0)
        consume(n - 1, 1)
    else:
        consume(n - 1, 0)


def _swiglu_scratch(tm, d, tf):
    return [pltpu.VMEM((tm, d), BF16), pltpu.VMEM((2, 2, tm, tf), F32)]


def _tail_kernel(*refs, n_heads, dh, gated, ffn, zero_fill):
    refs = list(refs)
    take = lambda n: [refs.pop(0) for _ in range(n)]
    x_ref, a_ref = take(2)
    b_in = take(5 if gated else 1)
    wa_ref, wb_ref, gx_ref, wq_ref, kv_ref, wo_ref = take(6)
    ffn_in = take(3 if ffn else 0)
    (o_ref,) = take(1)
    for z_ref in take(1 if zero_fill else 0):
        z_ref[...] = jnp.zeros_like(z_ref)
    if gated:
        (b_ref,) = take(1)
        _spatial_gate(*b_in, b_ref)
    else:
        (b_ref,) = b_in
    rest = ffn_in + [o_ref] + refs
    x = x_ref[...] + _dot(a_ref[...], wa_ref[...]) + _dot(b_ref[...], wb_ref[...])
    q = _dot(_rms(x, gx_ref[...]).astype(BF16), wq_ref[...]).astype(BF16)
    width = n_heads * dh
    outs = []
    for hd in range(n_heads):
        cs = slice(hd * dh, (hd + 1) * dh)
        s = _dot_nt(q[:, cs], kv_ref[:, cs]) * (dh ** -0.5)
        p = jnp.exp(s - jnp.max(s, axis=1, keepdims=True))
        p = p / jnp.sum(p, axis=1, keepdims=True)
        outs.append(_dot(p.astype(BF16), kv_ref[:, width + hd * dh:width + (hd + 1) * dh]).astype(BF16))
    x = x + _dot(jnp.concatenate(outs, axis=1), wo_ref[...])
    if len(rest) == 1:
        (o_ref,) = rest
        o_ref[...] = x
    else:
        gf_ref, w13_ref, w2_ref, o_ref, h_s, gu_s = rest
        h_s[...] = _rms(x, gf_ref[...]).astype(BF16)
        o_ref[...] = x
        _swiglu_chunks(h_s, w13_ref, w2_ref, o_ref, gu_s)


def layer_tail(x, a, b, w_out, gx, wq, kv, wo, ffn=None, zero_rows=0, *, tm, seq, mem_len, kv_col, n_heads, dh,
               name):
    t, d = x.shape
    per_b = seq // tm
    resident = pl.Buffered(1)
    const = lambda arr: pl.BlockSpec(arr.shape, lambda i: (0,) * arr.ndim, pipeline_mode=resident)
    rows = lambda arr: pl.BlockSpec((tm, arr.shape[1]), lambda i: (i, 0))
    vec = lambda g: g.reshape(1, d)
    gated = isinstance(b, tuple)
    scratch = []
    if gated:
        proj, u_col, v_col, sgu_norm, w_s, b_s = b
        b_width = w_s.shape[0] * LANES
        b_args = [proj, proj, sgu_norm, w_s, b_s.T]
        b_specs = [pl.BlockSpec((tm, b_width), lambda i: (i, u_col)), pl.BlockSpec((tm, b_width), lambda i: (i, v_col)),
                   const(sgu_norm), const(w_s), const(b_s.T)]
        scratch.append(pltpu.VMEM((tm, b_width), BF16))
    else:
        b_width = b.shape[1]
        b_args, b_specs = [b], [rows(b)]
    wa, wb = w_out[:a.shape[1]], w_out[a.shape[1]:]
    assert wb.shape[0] == b_width
    args = [x, a] + b_args + [wa, wb, vec(gx), wq, kv, wo]
    in_specs = [rows(x), rows(a)] + b_specs + [
        const(wa), const(wb), const(vec(gx)), const(wq),
        pl.BlockSpec((mem_len, 2 * n_heads * dh), lambda i: (i // per_b, kv_col)), const(wo)]
    if ffn is not None:
        gf, w13, w2 = ffn
        args += [vec(gf), w13, w2]
        in_specs += [const(vec(gf)), const(w13), const(w2)]
        scratch += _swiglu_scratch(tm, d, w2.shape[1])
    out_specs, out_shape = [rows(x)], [jax.ShapeDtypeStruct((t, d), F32)]
    if zero_rows:
        per_step = zero_rows // (t // tm)
        assert per_step * (t // tm) == zero_rows and per_step % BF16_ROWS == 0
        out_specs.append(pl.BlockSpec((per_step, d), lambda i: (i, 0)))
        out_shape.append(jax.ShapeDtypeStruct((zero_rows, d), BF16))
    outs = pl.pallas_call(
        functools.partial(_tail_kernel, n_heads=n_heads, dh=dh, gated=gated, ffn=ffn is not None,
                          zero_fill=bool(zero_rows)),
        grid=(t // tm,), in_specs=in_specs, out_specs=out_specs, out_shape=out_shape,
        scratch_shapes=scratch, compiler_params=_params("parallel"), name=name,
    )(*args)
    return outs if zero_rows else outs[0]


def _conv_proj_kernel(x_ref, g_ref, wc_ref, wa_ref, cw_ref, c_ref, o_ref, carry_s, *, steps_per_seq, scaled, chunk):
    h = _rms(x_ref[...], g_ref[...]).astype(BF16)
    tm, width = c_ref.shape
    n_taps = cw_ref.shape[0]
    bg, cg, xi = [_dot(h, wc_ref[:, k * width:(k + 1) * width]) for k in range(3)]
    xc = cg * xi
    first = pl.program_id(0) % steps_per_seq == 0
    prev = jnp.where(first, 0.0, carry_s[...])
    row = lax.broadcasted_iota(jnp.int32, (tm, width), 0)
    y = cw_ref[n_taps - 1:n_taps, :] * xc
    for back in range(1, n_taps):
        shifted = pltpu.roll(xc, back, axis=0)
        for r in range(back):
            src = SUBLANES - back + r
            shifted = jnp.where(row == r, prev[src:src + 1, :], shifted)
        y = y + cw_ref[n_taps - 1 - back:n_taps - back, :] * shifted
    c_ref[...] = (bg * y).astype(c_ref.dtype)
    carry_s[...] = xc[tm - SUBLANES:tm, :]
    s0, s1, scale = scaled
    for c0 in range(0, wa_ref.shape[1], chunk):
        y = _dot(h, wa_ref[:, c0:c0 + chunk])
        if s0 <= c0 and c0 + chunk <= s1:
            y = y * scale
        o_ref[:, c0:c0 + chunk] = y.astype(o_ref.dtype)


def conv_proj(x, g, w_conv, w_attn, conv_w, *, tm, seq, scaled):
    t, d = x.shape
    width = conv_w.shape[1]
    chunk = 512
    assert seq % tm == 0 and w_attn.shape[1] % chunk == 0 and conv_w.shape[0] - 1 <= SUBLANES
    const = lambda arr: pl.BlockSpec(arr.shape, lambda i: (0, 0))
    return pl.pallas_call(
        functools.partial(_conv_proj_kernel, steps_per_seq=seq // tm, scaled=scaled, chunk=chunk),
        grid=(t // tm,),
        in_specs=[pl.BlockSpec((tm, d), lambda i: (i, 0)), const(g.reshape(1, d)), const(w_conv), const(w_attn),
                  const(conv_w)],
        out_specs=[pl.BlockSpec((tm, width), lambda i: (i, 0)), pl.BlockSpec((tm, w_attn.shape[1]), lambda i: (i, 0))],
        out_shape=[jax.ShapeDtypeStruct((t, width), BF16), jax.ShapeDtypeStruct((t, w_attn.shape[1]), BF16)],
        scratch_shapes=[pltpu.VMEM((SUBLANES, width), F32)],
        compiler_params=_params("arbitrary"), name="odd_in_proj",
    )(x, g.reshape(1, d), w_conv, w_attn, conv_w)


def _diff_prep_kernel(rb_ref, lq1_ref, lk1_ref, lq2_ref, lk2_ref, bias_ref, far_ref, lam_ref, *, tq, lam_init):
    n_heads = bias_ref.shape[0]
    strip = 32
    row = lax.broadcasted_iota(jnp.int32, (strip, tq), 0)
    col = lax.broadcasted_iota(jnp.int32, (strip, tq), 1)
    max_exact = N_BUCKETS // 2

    def fill(r, carry):
        r0 = pl.multiple_of(r * strip, strip)
        for which in range(2):
            n = jnp.maximum(row + r0 - col + which * tq, 0)
            nf = jnp.maximum(n, 1).astype(F32)
            large = max_exact + (jnp.log(nf / max_exact) / math.log(MAX_DIST / max_exact)
                                 * (N_BUCKETS - max_exact)).astype(jnp.int32)
            large = jnp.minimum(large, N_BUCKETS - 1)
            bucket = jnp.where(n < max_exact, n, large)
            for h in range(n_heads):
                b = jnp.zeros((strip, tq), F32)
                for kk in range(N_BUCKETS):
                    b = jnp.where(bucket == kk, rb_ref[kk, h], b)
                bias_ref[h, which, pl.ds(r0, strip), :] = (b - rb_ref[N_BUCKETS - 1, h]) * LOG2E
        return carry

    lax.fori_loop(0, tq // strip, fill, 0)
    for h in range(n_heads):
        far_ref[h] = jnp.full((1, LANES), rb_ref[N_BUCKETS - 1, h], F32) * LOG2E
    lam = (jnp.exp(jnp.sum(lq1_ref[...] * lk1_ref[...], axis=1, keepdims=True))
           - jnp.exp(jnp.sum(lq2_ref[...] * lk2_ref[...], axis=1, keepdims=True)) + lam_init)
    lam_ref[...] = jnp.broadcast_to(lam, (1, LANES))


def diff_prep(rel_bias, lq1, lk1, lq2, lk2, *, tq, lam_init):
    n_heads = rel_bias.shape[1]
    vec = lambda a: a.reshape(1, -1)
    vspec = pl.BlockSpec(memory_space=pltpu.VMEM)
    return pl.pallas_call(
        functools.partial(_diff_prep_kernel, tq=tq, lam_init=lam_init),
        in_specs=[pl.BlockSpec(memory_space=pltpu.SMEM), vspec, vspec, vspec, vspec],
        out_specs=[vspec, vspec, vspec],
        out_shape=[
            jax.ShapeDtypeStruct((n_heads, 2, tq, tq), F32),
            jax.ShapeDtypeStruct((n_heads, 1, LANES), F32),
            jax.ShapeDtypeStruct((1, LANES), F32),
        ],
        compiler_params=pltpu.CompilerParams(vmem_limit_bytes=VMEM_LIMIT), name="diff_prep",
    )(rel_bias, vec(lq1), vec(lk1), vec(lq2), vec(lk2))


def _diff_kernel(q_ref, k_ref, v_ref, bias_ref, far_ref, lam_ref, subln_ref, w_ref, o_ref, wcast_ref,
                 ka_s, *, tq, lam_init):
    wcast_ref[...] = w_ref[...].astype(BF16)
    lane = lax.broadcasted_iota(jnp.int32, (1, LANES), 1)
    halves = _halves(lane)
    k = k_ref[...]
    far = _split3(far_ref[0])
    for sub in range(2):
        ka_s[sub] = _augment(k, halves[sub], lane, HEAD_DIM * (1 - sub), far, True)

    zero = jnp.zeros((1, LANES), F32)
    for i in range(q_ref.shape[0] // tq):
        rows = slice(i * tq, (i + 1) * tq)
        outs = []
        for sub in range(2):
            qa = _augment(q_ref[rows, :], halves[sub], lane, HEAD_DIM * (1 - sub), (zero,) * 3, False)
            outs.append(_causal_attention(qa, ka_s.at[sub], v_ref, bias_ref, i, tq))
        o = outs[0] - lam_ref[...] * outs[1]
        o_ref[rows, :] = (_rms(o, subln_ref[...]) * (1.0 - lam_init)).astype(o_ref.dtype)


def diff_attention(main, bias, far, lam, subln, w_ride, *, batch, seq, tq, q_col, k_col, v_col, lam_init):
    t = batch * seq
    n_heads = bias.shape[0]
    grid = (batch, n_heads)
    ride_spec, ride_shape = _rider(w_ride, grid)
    block = lambda col: pl.BlockSpec((seq, LANES), lambda b, h: (b, col + h))
    return pl.pallas_call(
        functools.partial(_diff_kernel, tq=tq, lam_init=lam_init),
        grid=grid,
        in_specs=[
            block(q_col), block(k_col), block(v_col),
            pl.BlockSpec((1, 2, tq, tq), lambda b, h: (h, 0, 0, 0)),
            pl.BlockSpec((1, 1, LANES), lambda b, h: (h, 0, 0)),
            pl.BlockSpec((1, LANES), lambda b, h: (0, 0)),
            pl.BlockSpec((1, LANES), lambda b, h: (0, 0)),
            ride_spec,
        ],
        out_specs=[block(0), ride_spec],
        out_shape=[jax.ShapeDtypeStruct((t, n_heads * LANES), BF16), ride_shape],
        scratch_shapes=[pltpu.VMEM((2, seq, LANES), BF16)],
        compiler_params=_params("parallel", "parallel"), name="diff_attention",
    )(main, main, main, bias, far, lam, subln.reshape(1, LANES), w_ride)


def _router_kernel(x_ref, g_ref, wr_ref, meta_ref, wts_ref, before_ref, cnt_ref, carry_s, *, n_exp, sub):
    @pl.when(pl.program_id(0) == 0)
    def _():
        carry_s[...] = jnp.zeros_like(carry_s)

    lane = lax.broadcasted_iota(jnp.int32, (sub, LANES), 1)
    lane_f = lane.astype(F32)
    row = lax.broadcasted_iota(jnp.int32, (sub, sub), 0)
    col = lax.broadcasted_iota(jnp.int32, (sub, sub), 1)
    earlier = jnp.where(row > col, 1.0, 0.0).astype(BF16)
    pick = lambda sel, val: jnp.sum(jnp.where(sel, val, 0.0), axis=1, keepdims=True)
    carry = carry_s[...]
    for r in range(x_ref.shape[0] // sub):
        rows = slice(r * sub, (r + 1) * sub)
        before_ref[r * SUBLANES:(r + 1) * SUBLANES, :] = jnp.broadcast_to(carry, (SUBLANES, LANES))
        h = _rms(x_ref[rows, :], g_ref[...])
        h_hi = h.astype(BF16)
        h_lo = (h - h_hi.astype(F32)).astype(BF16)
        logits = _dot(h_hi, wr_ref[0]) + (_dot(h_hi, wr_ref[1]) + _dot(h_lo, wr_ref[0]))
        logits = jnp.where(lane < n_exp, logits, -jnp.inf)
        m1 = jnp.max(logits, axis=1, keepdims=True)
        i1 = jnp.min(jnp.where(logits == m1, lane_f, float(LANES)), axis=1, keepdims=True)
        rest = jnp.where(lane_f == i1, -jnp.inf, logits)
        m2 = jnp.max(rest, axis=1, keepdims=True)
        i2 = jnp.min(jnp.where(rest == m2, lane_f, float(LANES)), axis=1, keepdims=True)
        e = jnp.exp(m2 - m1)
        sel1 = lane_f == i1
        sel2 = lane_f == i2
        onehot = jnp.where(sel1 | sel2, 1.0, 0.0)
        local = _dot(earlier, onehot.astype(BF16))
        meta = jnp.zeros((sub, LANES), F32)
        for idx, field in enumerate([i1, i2, pick(sel1, local), pick(sel2, local)]):
            meta = jnp.where(lane == idx, field, meta)
        meta_ref[rows, :] = meta.astype(jnp.int32)
        wts_ref[rows, :] = jnp.where(lane == 0, 1.0 / (1.0 + e), jnp.where(lane == 1, e / (1.0 + e), 0.0))
        carry = carry + jnp.sum(onehot, axis=0, keepdims=True)
    carry_s[...] = carry
    cnt_ref[...] = carry


def route_tokens(x, g, wr, *, tm, sub, n_exp):
    t, d = x.shape
    per_step = tm // sub * SUBLANES
    return pl.pallas_call(
        functools.partial(_router_kernel, n_exp=n_exp, sub=sub),
        grid=(t // tm,),
        in_specs=[
            pl.BlockSpec((tm, d), lambda i: (i, 0)),
            pl.BlockSpec((1, d), lambda i: (0, 0)),
            pl.BlockSpec((2, d, LANES), lambda i: (0, 0, 0)),
        ],
        out_specs=[
            pl.BlockSpec((tm, LANES), lambda i: (i, 0)),
            pl.BlockSpec((tm, LANES), lambda i: (i, 0)),
            pl.BlockSpec((per_step, LANES), lambda i: (i, 0)),
            pl.BlockSpec((1, LANES), lambda i: (0, 0)),
        ],
        out_shape=[
            jax.ShapeDtypeStruct((t, LANES), jnp.int32),
            jax.ShapeDtypeStruct((t, LANES), F32),
            jax.ShapeDtypeStruct((t // tm * per_step, LANES), F32),
            jax.ShapeDtypeStruct((1, LANES), F32),
        ],
        scratch_shapes=[pltpu.VMEM((1, LANES), F32)],
        compiler_params=_params("arbitrary"), name="moe_router",
    )(x, g.reshape(1, d), wr)


def _dispatch_kernel(start_ref, shift_ref, keep_ref, x_ref, g_ref, meta_ref, zeros_hbm, xs_hbm,
                     stage_s, carry_s, sems, *, tm, n_exp):
    del zeros_hbm
    j = pl.program_id(0)
    slot = j % 2
    rows = tm + BF16_ROWS

    def block_copy(step, e, buf):
        first = pl.multiple_of(start_ref[step * n_exp + e], BF16_ROWS)
        return pltpu.make_async_copy(stage_s.at[buf, e], xs_hbm.at[pl.ds(first, rows)], sems.at[buf])

    @pl.when(j == 0)
    def _():
        carry_s[...] = jnp.zeros_like(carry_s)

    h = _rms(x_ref[...], g_ref[...]).astype(BF16)
    fields = meta_ref[...].astype(F32).T
    slot_row = lax.broadcasted_iota(jnp.int32, (rows, tm), 0).astype(F32)
    for e in range(n_exp):
        key = j * n_exp + e
        idx = jnp.where(fields[0:1] == e, fields[2:3], jnp.where(fields[1:2] == e, fields[3:4], -2.0 * rows))
        idx = idx + shift_ref[key].astype(F32)
        onehot = jnp.where(slot_row == idx, 1.0, 0.0).astype(BF16)
        stage_s[slot, e] = _dot(onehot, h).astype(BF16)
        stage_s[slot, e, 0:BF16_ROWS, :] += carry_s[e]
        keep = pl.multiple_of(keep_ref[key], BF16_ROWS)
        carry_s[e] = stage_s[slot, e, pl.ds(keep, BF16_ROWS), :]

    @pl.when(j > 0)
    def _():
        for e in range(n_exp):
            block_copy(j - 1, e, 1 - slot).wait()

    for e in range(n_exp):
        block_copy(j, e, slot).start()

    @pl.when(j == pl.num_programs(0) - 1)
    def _():
        for e in range(n_exp):
            block_copy(j, e, slot).wait()


def moe_dispatch(starts, shifts, keeps, x, g, meta, zeros, *, tm, n_exp):
    t, d = x.shape
    n_rows = zeros.shape[0]
    rows = tm + BF16_ROWS
    return pl.pallas_call(
        functools.partial(_dispatch_kernel, tm=tm, n_exp=n_exp),
        grid_spec=pltpu.PrefetchScalarGridSpec(
            num_scalar_prefetch=3, grid=(t // tm,),
            in_specs=[
                pl.BlockSpec((tm, d), lambda i, *_: (i, 0)),
                pl.BlockSpec((1, d), lambda i, *_: (0, 0)),
                pl.BlockSpec((tm, LANES), lambda i, *_: (i, 0)),
                pl.BlockSpec(memory_space=pl.ANY),
            ],
            out_specs=pl.BlockSpec(memory_space=pl.ANY),
            scratch_shapes=[pltpu.VMEM((2, n_exp, rows, d), BF16), pltpu.VMEM((n_exp, BF16_ROWS, d), BF16),
                            pltpu.SemaphoreType.DMA((2,))],
        ),
        out_shape=jax.ShapeDtypeStruct((n_rows, d), BF16),
        input_output_aliases={6: 0},
        compiler_params=_params("arbitrary"),
        name="moe_dispatch",
    )(starts, shifts, keeps, x, g.reshape(1, d), meta, zeros)


def _expert_kernel(te_ref, used_ref, h_ref, w13_ref, w2_ref, o_ref, gu_s, acc_s):
    del te_ref
    used = used_ref[pl.program_id(0)] != 0

    @pl.when(used)
    def _():
        acc_s[...] = jnp.zeros_like(acc_s)
        _swiglu_chunks(h_ref, w13_ref.at[0], w2_ref.at[0], acc_s, gu_s)
        o_ref[...] = acc_s[...].astype(o_ref.dtype)

    @pl.when(jnp.logical_not(used))
    def _():
        o_ref[...] = jnp.zeros_like(o_ref)


def moe_experts(tile_expert, tile_used, hs, w13, w2, *, tm):
    n_rows, d = hs.shape
    tf = w2.shape[2]
    resident = pl.Buffered(1)
    return pl.pallas_call(
        _expert_kernel,
        grid_spec=pltpu.PrefetchScalarGridSpec(
            num_scalar_prefetch=2, grid=(n_rows // tm,),
            in_specs=[
                pl.BlockSpec((tm, d), lambda i, te, tu: (i, 0)),
                pl.BlockSpec((1,) + w13.shape[1:], lambda i, te, tu: (te[i], 0, 0)),
                pl.BlockSpec((1,) + w2.shape[1:], lambda i, te, tu: (te[i], 0, 0, 0)),
            ],
            out_specs=pl.BlockSpec((tm, d), lambda i, te, tu: (i, 0)),
            scratch_shapes=[pltpu.VMEM((2, 2, tm, tf), F32), pltpu.VMEM((tm, d), F32)],
        ),
        out_shape=jax.ShapeDtypeStruct((n_rows, d), BF16),
        compiler_params=pltpu.CompilerParams(dimension_semantics=("arbitrary",), vmem_limit_bytes=EXPERT_VMEM_LIMIT),
        name="moe_experts",
    )(tile_expert, tile_used, hs, w13, w2)


def _combine_kernel(start_ref, shift_ref, tail_ref, x_ref, wts_ref, meta_ref, g_ref, y_hbm, o_ref, blk_s, sems, *,
                    tm, n_exp):
    j = pl.program_id(0)
    slot = j % 2
    rows = tm + BF16_ROWS

    def block_copy(step, e, buf):
        first = pl.multiple_of(start_ref[step * n_exp + e], BF16_ROWS)
        return pltpu.make_async_copy(y_hbm.at[pl.ds(first, rows)], blk_s.at[buf, e], sems.at[buf])

    def fetch(step, buf):
        for e in range(n_exp):
            block_copy(step, e, buf).start()

    @pl.when(j == 0)
    def _():
        fetch(0, 0)

    @pl.when(j + 1 < pl.num_programs(0))
    def _():
        fetch(j + 1, 1 - slot)

    for e in range(n_exp):
        block_copy(j, e, slot).wait()

    meta = meta_ref[...]
    wts = wts_ref[...]
    col = lax.broadcasted_iota(jnp.int32, (tm, tm), 1)
    tail_col = lax.broadcasted_iota(jnp.int32, (tm, BF16_ROWS), 1) + tm

    def pick(e):
        sel = [meta[:, k:k + 1] == e for k in range(TOP_K)]
        idx = jnp.where(sel[0], meta[:, 2:3], jnp.where(sel[1], meta[:, 3:4], -2 * rows)) + shift_ref[j * n_exp + e]
        w = jnp.where(sel[0], wts[:, 0:1], jnp.where(sel[1], wts[:, 1:2], 0.0))
        return idx, w

    acc = x_ref[...]
    for e in range(n_exp):
        idx, w = pick(e)
        onehot = jnp.where(col == idx, 1.0, 0.0).astype(BF16)
        acc = acc + w * _dot(onehot, blk_s[slot, e, 0:tm, :])
    o_ref[...] = acc

    for e in range(n_exp):
        @pl.when(tail_ref[j * n_exp + e] != 0)
        def _(e=e):
            idx, w = pick(e)
            onehot_tail = jnp.where(tail_col == idx, 1.0, 0.0).astype(BF16)
            o_ref[...] += w * _dot(onehot_tail, blk_s[slot, e, tm:rows, :])

    o_ref[...] = _rms(o_ref[...], g_ref[...])


def moe_combine(starts, shifts, tails, x, wts, meta, g, y, *, tm, n_exp):
    t, d = x.shape
    n_pre = 3
    return pl.pallas_call(
        functools.partial(_combine_kernel, tm=tm, n_exp=n_exp),
        grid_spec=pltpu.PrefetchScalarGridSpec(
            num_scalar_prefetch=n_pre, grid=(t // tm,),
            in_specs=[
                pl.BlockSpec((tm, d), lambda i, *_: (i, 0)),
                pl.BlockSpec((tm, LANES), lambda i, *_: (i, 0)),
                pl.BlockSpec((tm, LANES), lambda i, *_: (i, 0)),
                pl.BlockSpec((1, d), lambda i, *_: (0, 0)),
                pl.BlockSpec(memory_space=pl.ANY),
            ],
            out_specs=pl.BlockSpec((tm, d), lambda i, *_: (i, 0)),
            scratch_shapes=[pltpu.VMEM((2, n_exp, tm + BF16_ROWS, d), BF16), pltpu.SemaphoreType.DMA((2,))],
        ),
        out_shape=jax.ShapeDtypeStruct((t, d), F32),
        compiler_params=_params("arbitrary"), name="moe_combine",
    )(starts, shifts, tails, x, wts, meta, g.reshape(1, d), y)


def _pad_cols(w, n):
    return jnp.pad(w, ((0, 0), (0, n - w.shape[1])))


def _even_mixer(x, norm, w_in, b_f, sgu_norm, w_s, b_s, w_ride, proj_rides, *, batch, seq, tq):
    n_heads = b_f.shape[0]
    a_width = n_heads * HEAD_DIM
    b_width = w_s.shape[0] * LANES
    f0 = 3 * a_width
    w_main = jnp.concatenate([w_in[:, :f0], w_in[:, f0 + n_heads:]], axis=1)
    w_gate = _pad_cols(w_in[:, f0:f0 + n_heads], LANES)
    main, gate, *proj_casts = norm_matmul(x, norm, [w_main.astype(BF16), w_gate.astype(BF16)], [BF16, F32],
                                          tm=min(IN_PROJ_ROWS, x.shape[0]), name="even_in_proj",
                                          scaled=(0, a_width, LOG2E * HEAD_DIM ** -0.5), rides=proj_rides)
    c = gate_cumsum(gate, _pad_cols(b_f.reshape(1, -1), LANES), seq=seq)
    n_pairs = a_width // LANES
    a, w_cast = fox_attention(main, c, w_ride, batch=batch, seq=seq, tq=tq, q_col=0, k_col=n_pairs,
                              v_col=2 * n_pairs, n_pairs=n_pairs)
    u_col = f0 // b_width
    return (a, (main, u_col, u_col + 1, sgu_norm, w_s, b_s)), w_cast, proj_casts


def _odd_mixer(x, norm, w_in, conv_w, lq1, lk1, lq2, lk2, subln, rel_bias, lam_init, w_ride, *, batch, seq, tq):
    c_width = conv_w.shape[1]
    d_width = rel_bias.shape[1] * 2 * HEAD_DIM
    q0 = 3 * c_width
    w_in = w_in.astype(BF16)
    c_out, main = conv_proj(x, norm, w_in[:, :q0], w_in[:, q0:], conv_w, tm=min(IN_PROJ_ROWS, seq), seq=seq,
                            scaled=(0, d_width, LOG2E * HEAD_DIM ** -0.5))
    bias, far, lam = diff_prep(rel_bias, lq1, lk1, lq2, lk2, tq=tq, lam_init=lam_init)
    n_heads = rel_bias.shape[1]
    d_out, w_cast = diff_attention(main, bias, far, lam, subln, w_ride, batch=batch, seq=seq, tq=tq, q_col=0,
                                   k_col=n_heads, v_col=2 * n_heads, lam_init=lam_init)
    return (c_out, d_out), w_cast


def _sorted_rows(t, n_exp):
    block_rows = ROUTE_ROWS + BF16_ROWS
    return TOP_K * t + n_exp * (EXPERT_ROWS + pl.cdiv(block_rows, EXPERT_ROWS) * EXPERT_ROWS)


def _moe_layer(x, norm, w_router, w13, w2, final_norm, zeros):
    t, d = x.shape
    n_exp = w_router.shape[1]
    tm_route = ROUTE_ROWS
    tm_expert = EXPERT_ROWS
    n_rows = zeros.shape[0]
    assert n_rows == _sorted_rows(t, n_exp)
    wr = _pad_cols(w_router, LANES)
    wr_hi = wr.astype(BF16)
    wr_split = jnp.stack([wr_hi, (wr - wr_hi.astype(F32)).astype(BF16)])
    meta, wts, before, counts = route_tokens(x, norm, wr_split, tm=ROUTE_STEP_ROWS, sub=tm_route, n_exp=n_exp)
    counts = counts[0, :n_exp].astype(jnp.int32)
    block_rows = tm_route + BF16_ROWS
    padded = (counts + block_rows + tm_expert - 1) // tm_expert * tm_expert
    ends = jnp.cumsum(padded)
    offsets = ends - padded
    tile_start = jnp.arange(n_rows // tm_expert, dtype=jnp.int32) * tm_expert
    tile_expert = jnp.minimum(jnp.sum(tile_start[:, None] >= ends[None, :], axis=1), n_exp - 1).astype(jnp.int32)
    tile_used = (tile_start < (offsets + counts)[tile_expert]).astype(jnp.int32)
    before = before[::SUBLANES, :n_exp].astype(jnp.int32)
    in_tile = jnp.concatenate([before[1:], counts[None]]) - before
    first = offsets[None, :] + before
    starts = first // BF16_ROWS * BF16_ROWS
    shifts = first - starts
    keeps = (shifts + in_tile) // BF16_ROWS * BF16_ROWS
    tails = (shifts + in_tile > tm_route).astype(jnp.int32)
    flat = lambda a: a.reshape(-1).astype(jnp.int32)
    hs = moe_dispatch(flat(starts), flat(shifts), flat(keeps), x, norm, meta, zeros, tm=tm_route, n_exp=n_exp)
    y = moe_experts(tile_expert, tile_used, hs, *_chunk_weights(w13, w2, EXPERT_CHUNK), tm=tm_expert)
    return moe_combine(flat(starts), flat(shifts), flat(tails), x, wts, meta, final_norm, y, tm=tm_route,
                       n_exp=n_exp)


def kernel(x, mem, rel_bias, mem_norm, final_norm, ev_norm, ev_w_in, ev_b_f, ev_sgu_norm, ev_w_s, ev_b_s, ev_w_out, ffn_w13, ffn_w2, od_norm, od_w_in, od_conv_w, od_lam_q1, od_lam_k1, od_lam_q2, od_lam_k2, od_subln, od_w_out, moe_router, moe_w13, moe_w2, x_norm, x_wq, x_wkv, x_wo, ffn_norm):
    batch, seq, d = x.shape
    mem_len = mem.shape[1]
    depth = x_norm.shape[0]
    assert depth == 2 and ev_norm.shape[0] == 1 and od_norm.shape[0] == 1
    x_heads, x_dh = 4, 128
    xf = x.reshape(batch * seq, d)
    wkv = jnp.concatenate([x_wkv[layer] for layer in range(depth)], axis=1).astype(BF16)
    (kv,) = norm_matmul(mem.reshape(batch * mem_len, d), mem_norm, [wkv], [BF16], tm=MEM_ROWS, name="mem_kv")

    def tail(xf, mixed, w_out, layer, ffn, name, zero_rows=0):
        return layer_tail(xf, *mixed, w_out.astype(BF16), x_norm[layer], x_wq[layer].astype(BF16), kv,
                          x_wo[layer].astype(BF16), ffn, zero_rows, tm=TAIL_ROWS, seq=seq, mem_len=mem_len, kv_col=layer,
                          n_heads=x_heads, dh=x_dh, name=name)

    e13, e2 = moe_w13[0], moe_w2[0]
    mixed, e13_bf16, ffn_bf16 = _even_mixer(xf, ev_norm[0], ev_w_in[0], ev_b_f[0], ev_sgu_norm[0], ev_w_s[0],
                                            ev_b_s[0], e13.reshape(-1, e13.shape[-1]), (ffn_w13[0], ffn_w2[0]),
                                            batch=batch, seq=seq, tq=ATTN_TILE)
    xf = tail(xf, mixed, ev_w_out[0], 0, (ffn_norm[0],) + _chunk_weights(*ffn_bf16, FFN_CHUNK), "even_tail")
    lam_init = 0.8 - 0.6 * math.exp(-0.3 * 1)
    mixed, e2_bf16 = _odd_mixer(xf, od_norm[0], od_w_in[0], od_conv_w[0], od_lam_q1[0], od_lam_k1[0],
                                od_lam_q2[0], od_lam_k2[0], od_subln[0], rel_bias, lam_init,
                                e2.reshape(-1, e2.shape[-1]), batch=batch, seq=seq, tq=ATTN_TILE)
    xf, zeros = tail(xf, mixed, od_w_out[0], 1, None, "odd_tail", _sorted_rows(xf.shape[0], moe_router.shape[-1]))
    out = _moe_layer(xf, ffn_norm[1], moe_router[0], e13_bf16.reshape(e13.shape), e2_bf16.reshape(e2.shape),
                     final_norm, zeros)
    return out.reshape(batch, seq, d)
```

```python
import functools
import math

import jax
import jax.numpy as jnp
from jax import lax
from jax.experimental import pallas as pl
from jax.experimental.pallas import tpu as pltpu

F32 = jnp.float32
BF16 = jnp.bfloat16
EPS = 1e-6
HEAD_DIM = 64
LANES = 128
SUBLANES = 8
BF16_ROWS = 16
N_BUCKETS = 32
MAX_DIST = 128
TOP_K = 2
LOG2E = 1.4426950408889634
ATTN_TILE = 512
IN_PROJ_ROWS = 1024
MEM_ROWS = 512
TAIL_ROWS = 512
ROUTE_ROWS = 256
ROUTE_STEP_ROWS = 512
EXPERT_ROWS = 512
FFN_CHUNK = 256
EXPERT_CHUNK = 512
VMEM_BYTES = 64 * 1024 * 1024
VMEM_LIMIT = VMEM_BYTES - 8 * 1024 * 1024
EXPERT_VMEM_LIMIT = VMEM_BYTES - 2 * 1024 * 1024


def _params(*sem):
    return pltpu.CompilerParams(dimension_semantics=sem, vmem_limit_bytes=VMEM_LIMIT)


def _rms(x, g):
    ms = jnp.mean(x * x, axis=-1, keepdims=True)
    return x * lax.rsqrt(ms + EPS) * g


def _dot(a, b):
    return jnp.dot(a, b, preferred_element_type=F32)


def _dot_nt(a, b):
    return lax.dot_general(a, b, (((1,), (1,)), ((), ())), preferred_element_type=F32)


def _norm_matmul_kernel(x_ref, g_ref, *refs, n_w, n_ride, chunk, scaled):
    w_refs, ride_in, o_refs = refs[:n_w], refs[n_w:n_w + n_ride], refs[n_w + n_ride:2 * n_w + n_ride]
    for src, dst in zip(ride_in, refs[2 * n_w + n_ride:]):
        dst[...] = src[...].astype(BF16)
    h = _rms(x_ref[...], g_ref[...]).astype(BF16)
    s0, s1, scale = scaled
    for k, (w_ref, o_ref) in enumerate(zip(w_refs, o_refs)):
        n = w_ref.shape[1]
        for c0 in range(0, n, chunk):
            c1 = min(c0 + chunk, n)
            y = _dot(h, w_ref[:, c0:c1])
            if k == 0 and s0 <= c0 and c1 <= s1:
                y = y * scale
            o_ref[:, c0:c1] = y.astype(o_ref.dtype)


def norm_matmul(x, g, ws, out_dtypes, *, tm, name, scaled=(0, 0, 1.0), rides=()):
    t, d = x.shape
    chunk = 512
    assert scaled[0] % chunk == 0 and scaled[1] % chunk == 0
    grid = (t // tm,)
    ride_specs = [_rider(w, grid) for w in rides]
    in_specs = [pl.BlockSpec((tm, d), lambda i: (i, 0)), pl.BlockSpec((1, d), lambda i: (0, 0))]
    in_specs += [pl.BlockSpec(w.shape, lambda i: (0, 0)) for w in ws] + [spec for spec, _ in ride_specs]
    out_specs = [pl.BlockSpec((tm, w.shape[1]), lambda i: (i, 0)) for w in ws] + [spec for spec, _ in ride_specs]
    out_shape = [jax.ShapeDtypeStruct((t, w.shape[1]), dt) for w, dt in zip(ws, out_dtypes)]
    out_shape += [shape for _, shape in ride_specs]
    return pl.pallas_call(
        functools.partial(_norm_matmul_kernel, n_w=len(ws), n_ride=len(rides), chunk=chunk, scaled=scaled),
        grid=grid, in_specs=in_specs, out_specs=out_specs, out_shape=out_shape,
        compiler_params=_params("parallel"), name=name,
    )(x, g.reshape(1, d), *ws, *rides)


def _gate_kernel(g_ref, b_ref, c_ref):
    s = g_ref.shape[0]
    row = lax.broadcasted_iota(jnp.int32, (LANES, LANES), 0)
    col = lax.broadcasted_iota(jnp.int32, (LANES, LANES), 1)
    tri = (row >= col).astype(F32)
    carry = jnp.zeros((1, LANES), F32)
    for blk in range(s // LANES):
        z = g_ref[blk * LANES:(blk + 1) * LANES, :] + b_ref[...]
        log_f = jnp.minimum(z, 0.0) - jnp.log1p(jnp.exp(-jnp.abs(z)))
        cs = jnp.dot(tri, log_f, precision=lax.Precision.HIGHEST, preferred_element_type=F32) + carry
        c_ref[blk * LANES:(blk + 1) * LANES, :] = cs
        carry = cs[LANES - 1:LANES, :]


def gate_cumsum(g, b, *, seq):
    t = g.shape[0]
    return pl.pallas_call(
        _gate_kernel, grid=(t // seq,),
        in_specs=[pl.BlockSpec((seq, LANES), lambda i: (i, 0)), pl.BlockSpec((1, LANES), lambda i: (0, 0))],
        out_specs=pl.BlockSpec((seq, LANES), lambda i: (i, 0)),
        out_shape=jax.ShapeDtypeStruct((t, LANES), F32),
        compiler_params=_params("parallel"), name="gate_cumsum",
    )(g, b)


def _split3(x):
    hi = x.astype(BF16).astype(F32)
    rest = x - hi
    mid = rest.astype(BF16).astype(F32)
    lo = (rest - mid).astype(BF16).astype(F32)
    return hi, mid, lo


def _augment(x, in_half, lane, base, pieces, pieces_first):
    n = len(pieces)
    p0, o0 = (base, base + n) if pieces_first else (base + n, base)
    aug = jnp.where((lane >= o0) & (lane < o0 + n), 1.0, 0.0)
    for idx, piece in enumerate(pieces):
        aug = jnp.where(lane == p0 + idx, piece, aug)
    return jnp.where(in_half, x, aug.astype(x.dtype))


def _halves(lane):
    return [(lane >= HEAD_DIM * hh) & (lane < HEAD_DIM * (hh + 1)) for hh in range(2)]


def _causal_attention(qa, ka_s, v_ref, bias_ref, i, tq):
    n_chunks = tq // LANES
    row = lax.broadcasted_iota(jnp.int32, (tq, tq), 0)
    col = lax.broadcasted_iota(jnp.int32, (tq, tq), 1)

    def scores(j):
        s = _dot_nt(qa, ka_s[j * tq:(j + 1) * tq, :])
        if bias_ref is not None and j >= i - 1:
            s = s + bias_ref[0, i - j]
        if j == i:
            s = jnp.where(row >= col, s, -jnp.inf)
        return [s[:, c * LANES:(c + 1) * LANES] for c in range(n_chunks)]

    m = jnp.full((tq, LANES), -jnp.inf, F32)
    for j in range(i + 1):
        for chunk in scores(j):
            m = jnp.maximum(m, chunk)
    m = jnp.broadcast_to(jnp.max(m, axis=1, keepdims=True), (tq, LANES))
    l = jnp.zeros((tq, LANES), F32)
    acc = jnp.zeros((tq, LANES), F32)
    for j in range(i + 1):
        ps = [jnp.exp2(chunk - m) for chunk in scores(j)]
        l = l + functools.reduce(lambda a, b: a + b, ps)
        acc = acc + _dot(jnp.concatenate(ps, axis=1).astype(BF16), v_ref[j * tq:(j + 1) * tq, :])
    return acc / jnp.sum(l, axis=1, keepdims=True)


def _rider(w, grid):
    steps = math.prod(grid)
    rows = w.shape[0] // steps
    assert rows * steps == w.shape[0] and rows % BF16_ROWS == 0

    def index(*ids):
        step = 0
        for n, idx in zip(grid, ids):
            step = step * n + idx
        return step, 0

    return pl.BlockSpec((rows, w.shape[1]), index), jax.ShapeDtypeStruct(w.shape, BF16)


def _fox_kernel(q_ref, k_ref, v_ref, c_ref, w_ref, o_ref, wcast_ref, ka_s, *, tq):
    wcast_ref[...] = w_ref[...].astype(BF16)
    hp = pl.program_id(1)
    lane = lax.broadcasted_iota(jnp.int32, (1, LANES), 1)
    halves = _halves(lane)

    def decay(c, hh):
        return jnp.sum(jnp.where(lane == 2 * hp + hh, c, 0.0), axis=1, keepdims=True) * LOG2E

    k = k_ref[...]
    c_all = c_ref[...]
    for hh in range(2):
        ka_s[hh] = _augment(k, halves[hh], lane, HEAD_DIM * (1 - hh), _split3(-decay(c_all, hh)), True)

    for i in range(q_ref.shape[0] // tq):
        rows = slice(i * tq, (i + 1) * tq)
        outs = []
        for hh in range(2):
            qa = _augment(q_ref[rows, :], halves[hh], lane, HEAD_DIM * (1 - hh), _split3(decay(c_ref[rows, :], hh)),
                          False)
            outs.append(_causal_attention(qa, ka_s.at[hh], v_ref, None, i, tq))
        o_ref[rows, :] = jnp.where(lane < HEAD_DIM, outs[0], outs[1]).astype(o_ref.dtype)


def fox_attention(qkv, c, w_ride, *, batch, seq, tq, q_col, k_col, v_col, n_pairs):
    t = batch * seq
    grid = (batch, n_pairs)
    ride_spec, ride_shape = _rider(w_ride, grid)
    block = lambda col: pl.BlockSpec((seq, LANES), lambda b, h: (b, col + h))
    return pl.pallas_call(
        functools.partial(_fox_kernel, tq=tq),
        grid=grid,
        in_specs=[block(q_col), block(k_col), block(v_col), pl.BlockSpec((seq, LANES), lambda b, h: (b, 0)),
                  ride_spec],
        out_specs=[block(0), ride_spec],
        out_shape=[jax.ShapeDtypeStruct((t, n_pairs * LANES), BF16), ride_shape],
        scratch_shapes=[pltpu.VMEM((2, seq, LANES), BF16)],
        compiler_params=_params("parallel", "parallel"), name="fox_attention",
    )(qkv, qkv, qkv, c, w_ride)


def _spatial_gate(u_ref, v_ref, norm_ref, ws_ref, bs_ref, o_ref):
    tb = u_ref.shape[0]
    n_groups, chunk, _ = ws_ref.shape
    row = lax.broadcasted_iota(jnp.int32, (chunk, chunk), 0)
    col = lax.broadcasted_iota(jnp.int32, (chunk, chunk), 1)
    tri = row >= col
    for g in range(n_groups):
        w = jnp.where(tri, ws_ref[g], 0.0).astype(BF16)
        bias = bs_ref[:, g:g + 1]
        gain = norm_ref[g:g + 1, :]
        for c in range(tb // chunk):
            rs = slice(c * chunk, (c + 1) * chunk)
            cs = slice(g * LANES, (g + 1) * LANES)
            vn = _rms(jax.nn.gelu(v_ref[rs, cs].astype(F32)), gain)
            mixed = _dot(w, vn.astype(BF16)) + bias
            o_ref[rs, cs] = (jax.nn.gelu(u_ref[rs, cs].astype(F32)) * mixed).astype(o_ref.dtype)


def _chunk_weights(w13, w2, tf):
    *lead, ff, d = w2.shape
    return w13.astype(BF16), w2.astype(BF16).reshape(*lead, ff // tf, tf, d)


def _swiglu_chunks(h_s, w13, w2, o_ref, gu_s):
    n, tf, _ = w2.shape

    def project(c, slot):
        for part in range(2):
            cols = pl.ds(pl.multiple_of((part * n + c) * tf, tf), tf)
            gu_s[slot, part] = _dot(h_s[...], w13[:, cols])

    def consume(c, slot):
        gate = gu_s[slot, 0]
        act = (gate * jax.nn.sigmoid(gate) * gu_s[slot, 1]).astype(BF16)
        o_ref[...] += _dot(act, w2[c])

    def pair(k, carry):
        c = 2 * k
        project(c + 1, 1)
        consume(c, 0)
        project(c + 2, 0)
        consume(c + 1, 1)
        return carry

    project(0, 0)
    lax.fori_loop(0, (n - 1) // 2, pair, 0)
    if n % 2 == 0:
        project(n - 1, 1)
        consume(n - 2, 0)
        consume(n - 1, 1)
    else:
        consume(n - 1, 0)


def _swiglu_scratch(tm, d, tf):
    return [pltpu.VMEM((tm, d), BF16), pltpu.VMEM((2, 2, tm, tf), F32)]


def _tail_kernel(*refs, n_heads, dh, gated, ffn, zero_fill):
    refs = list(refs)
    take = lambda n: [refs.pop(0) for _ in range(n)]
    x_ref, a_ref = take(2)
    b_in = take(5 if gated else 1)
    wa_ref, wb_ref, gx_ref, wq_ref, kv_ref, wo_ref = take(6)
    ffn_in = take(3 if ffn else 0)
    (o_ref,) = take(1)
    for z_ref in take(1 if zero_fill else 0):
        z_ref[...] = jnp.zeros_like(z_ref)
    if gated:
        (b_ref,) = take(1)
        _spatial_gate(*b_in, b_ref)
    else:
        (b_ref,) = b_in
    rest = ffn_in + [o_ref] + refs
    x = x_ref[...] + _dot(a_ref[...], wa_ref[...]) + _dot(b_ref[...], wb_ref[...])
    q = _dot(_rms(x, gx_ref[...]).astype(BF16), wq_ref[...]).astype(BF16)
    width = n_heads * dh
    outs = []
    for hd in range(n_heads):
        cs = slice(hd * dh, (hd + 1) * dh)
        s = _dot_nt(q[:, cs], kv_ref[:, cs]) * (dh ** -0.5)
        p = jnp.exp(s - jnp.max(s, axis=1, keepdims=True))
        p = p / jnp.sum(p, axis=1, keepdims=True)
        outs.append(_dot(p.astype(BF16), kv_ref[:, width + hd * dh:width + (hd + 1) * dh]).astype(BF16))
    x = x + _dot(jnp.concatenate(outs, axis=1), wo_ref[...])
    if len(rest) == 1:
        (o_ref,) = rest
        o_ref[...] = x
    else:
        gf_ref, w13_ref, w2_ref, o_ref, h_s, gu_s = rest
        h_s[...] = _rms(x, gf_ref[...]).astype(BF16)
        o_ref[...] = x
        _swiglu_chunks(h_s, w13_ref, w2_ref, o_ref, gu_s)


def layer_tail(x, a, b, w_out, gx, wq, kv, wo, ffn=None, zero_rows=0, *, tm, seq, mem_len, kv_col, n_heads, dh,
               name):
    t, d = x.shape
    per_b = seq // tm
    resident = pl.Buffered(1)
    const = lambda arr: pl.BlockSpec(arr.shape, lambda i: (0,) * arr.ndim, pipeline_mode=resident)
    rows = lambda arr: pl.BlockSpec((tm, arr.shape[1]), lambda i: (i, 0))
    vec = lambda g: g.reshape(1, d)
    gated = isinstance(b, tuple)
    scratch = []
    if gated:
        proj, u_col, v_col, sgu_norm, w_s, b_s = b
        b_width = w_s.shape[0] * LANES
        b_args = [proj, proj, sgu_norm, w_s, b_s.T]
        b_specs = [pl.BlockSpec((tm, b_width), lambda i: (i, u_col)), pl.BlockSpec((tm, b_width), lambda i: (i, v_col)),
                   const(sgu_norm), const(w_s), const(b_s.T)]
        scratch.append(pltpu.VMEM((tm, b_width), BF16))
    else:
        b_width = b.shape[1]
        b_args, b_specs = [b], [rows(b)]
    wa, wb = w_out[:a.shape[1]], w_out[a.shape[1]:]
    assert wb.shape[0] == b_width
    args = [x, a] + b_args + [wa, wb, vec(gx), wq, kv, wo]
    in_specs = [rows(x), rows(a)] + b_specs + [
        const(wa), const(wb), const(vec(gx)), const(wq),
        pl.BlockSpec((mem_len, 2 * n_heads * dh), lambda i: (i // per_b, kv_col)), const(wo)]
    if ffn is not None:
        gf, w13, w2 = ffn
        args += [vec(gf), w13, w2]
        in_specs += [const(vec(gf)), const(w13), const(w2)]
        scratch += _swiglu_scratch(tm, d, w2.shape[1])
    out_specs, out_shape = [rows(x)], [jax.ShapeDtypeStruct((t, d), F32)]
    if zero_rows:
        per_step = zero_rows // (t // tm)
        assert per_step * (t // tm) == zero_rows and per_step % BF16_ROWS == 0
        out_specs.append(pl.BlockSpec((per_step, d), lambda i: (i, 0)))
        out_shape.append(jax.ShapeDtypeStruct((zero_rows, d), BF16))
    outs = pl.pallas_call(
        functools.partial(_tail_kernel, n_heads=n_heads, dh=dh, gated=gated, ffn=ffn is not None,
                          zero_fill=bool(zero_rows)),
        grid=(t // tm,), in_specs=in_specs, out_specs=out_specs, out_shape=out_shape,
        scratch_shapes=scratch, compiler_params=_params("parallel"), name=name,
    )(*args)
    return outs if zero_rows else outs[0]


def _conv_proj_kernel(x_ref, g_ref, wc_ref, wa_ref, cw_ref, c_ref, o_ref, carry_s, *, steps_per_seq, scaled, chunk):
    h = _rms(x_ref[...], g_ref[...]).astype(BF16)
    tm, width = c_ref.shape
    n_taps = cw_ref.shape[0]
    bg, cg, xi = [_dot(h, wc_ref[:, k * width:(k + 1) * width]) for k in range(3)]
    xc = cg * xi
    first = pl.program_id(0) % steps_per_seq == 0
    prev = jnp.where(first, 0.0, carry_s[...])
    row = lax.broadcasted_iota(jnp.int32, (tm, width), 0)
    y = cw_ref[n_taps - 1:n_taps, :] * xc
    for back in range(1, n_taps):
        shifted = pltpu.roll(xc, back, axis=0)
        for r in range(back):
            src = SUBLANES - back + r
            shifted = jnp.where(row == r, prev[src:src + 1, :], shifted)
        y = y + cw_ref[n_taps - 1 - back:n_taps - back, :] * shifted
    c_ref[...] = (bg * y).astype(c_ref.dtype)
    carry_s[...] = xc[tm - SUBLANES:tm, :]
    s0, s1, scale = scaled
    for c0 in range(0, wa_ref.shape[1], chunk):
        y = _dot(h, wa_ref[:, c0:c0 + chunk])
        if s0 <= c0 and c0 + chunk <= s1:
            y = y * scale
        o_ref[:, c0:c0 + chunk] = y.astype(o_ref.dtype)


def conv_proj(x, g, w_conv, w_attn, conv_w, *, tm, seq, scaled):
    t, d = x.shape
    width = conv_w.shape[1]
    chunk = 512
    assert seq % tm == 0 and w_attn.shape[1] % chunk == 0 and conv_w.shape[0] - 1 <= SUBLANES
    const = lambda arr: pl.BlockSpec(arr.shape, lambda i: (0, 0))
    return pl.pallas_call(
        functools.partial(_conv_proj_kernel, steps_per_seq=seq // tm, scaled=scaled, chunk=chunk),
        grid=(t // tm,),
        in_specs=[pl.BlockSpec((tm, d), lambda i: (i, 0)), const(g.reshape(1, d)), const(w_conv), const(w_attn),
                  const(conv_w)],
        out_specs=[pl.BlockSpec((tm, width), lambda i: (i, 0)), pl.BlockSpec((tm, w_attn.shape[1]), lambda i: (i, 0))],
        out_shape=[jax.ShapeDtypeStruct((t, width), BF16), jax.ShapeDtypeStruct((t, w_attn.shape[1]), BF16)],
        scratch_shapes=[pltpu.VMEM((SUBLANES, width), F32)],
        compiler_params=_params("arbitrary"), name="odd_in_proj",
    )(x, g.reshape(1, d), w_conv, w_attn, conv_w)


def _diff_prep_kernel(rb_ref, lq1_ref, lk1_ref, lq2_ref, lk2_ref, bias_ref, far_ref, lam_ref, *, tq, lam_init):
    n_heads = bias_ref.shape[0]
    strip = 32
    row = lax.broadcasted_iota(jnp.int32, (strip, tq), 0)
    col = lax.broadcasted_iota(jnp.int32, (strip, tq), 1)
    max_exact = N_BUCKETS // 2

    def fill(r, carry):
        r0 = pl.multiple_of(r * strip, strip)
        for which in range(2):
            n = jnp.maximum(row + r0 - col + which * tq, 0)
            nf = jnp.maximum(n, 1).astype(F32)
            large = max_exact + (jnp.log(nf / max_exact) / math.log(MAX_DIST / max_exact)
                                 * (N_BUCKETS - max_exact)).astype(jnp.int32)
            large = jnp.minimum(large, N_BUCKETS - 1)
            bucket = jnp.where(n < max_exact, n, large)
            for h in range(n_heads):
                b = jnp.zeros((strip, tq), F32)
                for kk in range(N_BUCKETS):
                    b = jnp.where(bucket == kk, rb_ref[kk, h], b)
                bias_ref[h, which, pl.ds(r0, strip), :] = (b - rb_ref[N_BUCKETS - 1, h]) * LOG2E
        return carry

    lax.fori_loop(0, tq // strip, fill, 0)
    for h in range(n_heads):
        far_ref[h] = jnp.full((1, LANES), rb_ref[N_BUCKETS - 1, h], F32) * LOG2E
    lam = (jnp.exp(jnp.sum(lq1_ref[...] * lk1_ref[...], axis=1, keepdims=True))
           - jnp.exp(jnp.sum(lq2_ref[...] * lk2_ref[...], axis=1, keepdims=True)) + lam_init)
    lam_ref[...] = jnp.broadcast_to(lam, (1, LANES))


def diff_prep(rel_bias, lq1, lk1, lq2, lk2, *, tq, lam_init):
    n_heads = rel_bias.shape[1]
    vec = lambda a: a.reshape(1, -1)
    vspec = pl.BlockSpec(memory_space=pltpu.VMEM)
    return pl.pallas_call(
        functools.partial(_diff_prep_kernel, tq=tq, lam_init=lam_init),
        in_specs=[pl.BlockSpec(memory_space=pltpu.SMEM), vspec, vspec, vspec, vspec],
        out_specs=[vspec, vspec, vspec],
        out_shape=[
            jax.ShapeDtypeStruct((n_heads, 2, tq, tq), F32),
            jax.ShapeDtypeStruct((n_heads, 1, LANES), F32),
            jax.ShapeDtypeStruct((1, LANES), F32),
        ],
        compiler_params=pltpu.CompilerParams(vmem_limit_bytes=VMEM_LIMIT), name="diff_prep",
    )(rel_bias, vec(lq1), vec(lk1), vec(lq2), vec(lk2))


def _diff_kernel(q_ref, k_ref, v_ref, bias_ref, far_ref, lam_ref, subln_ref, w_ref, o_ref, wcast_ref,
                 ka_s, *, tq, lam_init):
    wcast_ref[...] = w_ref[...].astype(BF16)
    lane = lax.broadcasted_iota(jnp.int32, (1, LANES), 1)
    halves = _halves(lane)
    k = k_ref[...]
    far = _split3(far_ref[0])
    for sub in range(2):
        ka_s[sub] = _augment(k, halves[sub], lane, HEAD_DIM * (1 - sub), far, True)

    zero = jnp.zeros((1, LANES), F32)
    for i in range(q_ref.shape[0] // tq):
        rows = slice(i * tq, (i + 1) * tq)
        outs = []
        for sub in range(2):
            qa = _augment(q_ref[rows, :], halves[sub], lane, HEAD_DIM * (1 - sub), (zero,) * 3, False)
            outs.append(_causal_attention(qa, ka_s.at[sub], v_ref, bias_ref, i, tq))
        o = outs[0] - lam_ref[...] * outs[1]
        o_ref[rows, :] = (_rms(o, subln_ref[...]) * (1.0 - lam_init)).astype(o_ref.dtype)


def diff_attention(main, bias, far, lam, subln, w_ride, *, batch, seq, tq, q_col, k_col, v_col, lam_init):
    t = batch * seq
    n_heads = bias.shape[0]
    grid = (batch, n_heads)
    ride_spec, ride_shape = _rider(w_ride, grid)
    block = lambda col: pl.BlockSpec((seq, LANES), lambda b, h: (b, col + h))
    return pl.pallas_call(
        functools.partial(_diff_kernel, tq=tq, lam_init=lam_init),
        grid=grid,
        in_specs=[
            block(q_col), block(k_col), block(v_col),
            pl.BlockSpec((1, 2, tq, tq), lambda b, h: (h, 0, 0, 0)),
            pl.BlockSpec((1, 1, LANES), lambda b, h: (h, 0, 0)),
            pl.BlockSpec((1, LANES), lambda b, h: (0, 0)),
            pl.BlockSpec((1, LANES), lambda b, h: (0, 0)),
            ride_spec,
        ],
        out_specs=[block(0), ride_spec],
        out_shape=[jax.ShapeDtypeStruct((t, n_heads * LANES), BF16), ride_shape],
        scratch_shapes=[pltpu.VMEM((2, seq, LANES), BF16)],
        compiler_params=_params("parallel", "parallel"), name="diff_attention",
    )(main, main, main, bias, far, lam, subln.reshape(1, LANES), w_ride)


def _router_kernel(x_ref, g_ref, wr_ref, meta_ref, wts_ref, before_ref, cnt_ref, carry_s, *, n_exp, sub):
    @pl.when(pl.program_id(0) == 0)
    def _():
        carry_s[...] = jnp.zeros_like(carry_s)

    lane = lax.broadcasted_iota(jnp.int32, (sub, LANES), 1)
    lane_f = lane.astype(F32)
    row = lax.broadcasted_iota(jnp.int32, (sub, sub), 0)
    col = lax.broadcasted_iota(jnp.int32, (sub, sub), 1)
    earlier = jnp.where(row > col, 1.0, 0.0).astype(BF16)
    pick = lambda sel, val: jnp.sum(jnp.where(sel, val, 0.0), axis=1, keepdims=True)
    carry = carry_s[...]
    for r in range(x_ref.shape[0] // sub):
        rows = slice(r * sub, (r + 1) * sub)
        before_ref[r * SUBLANES:(r + 1) * SUBLANES, :] = jnp.broadcast_to(carry, (SUBLANES, LANES))
        h = _rms(x_ref[rows, :], g_ref[...])
        h_hi = h.astype(BF16)
        h_lo = (h - h_hi.astype(F32)).astype(BF16)
        logits = _dot(h_hi, wr_ref[0]) + (_dot(h_hi, wr_ref[1]) + _dot(h_lo, wr_ref[0]))
        logits = jnp.where(lane < n_exp, logits, -jnp.inf)
        m1 = jnp.max(logits, axis=1, keepdims=True)
        i1 = jnp.min(jnp.where(logits == m1, lane_f, float(LANES)), axis=1, keepdims=True)
        rest = jnp.where(lane_f == i1, -jnp.inf, logits)
        m2 = jnp.max(rest, axis=1, keepdims=True)
        i2 = jnp.min(jnp.where(rest == m2, lane_f, float(LANES)), axis=1, keepdims=True)
        e = jnp.exp(m2 - m1)
        sel1 = lane_f == i1
        sel2 = lane_f == i2
        onehot = jnp.where(sel1 | sel2, 1.0, 0.0)
        local = _dot(earlier, onehot.astype(BF16))
        meta = jnp.zeros((sub, LANES), F32)
        for idx, field in enumerate([i1, i2, pick(sel1, local), pick(sel2, local)]):
            meta = jnp.where(lane == idx, field, meta)
        meta_ref[rows, :] = meta.astype(jnp.int32)
        wts_ref[rows, :] = jnp.where(lane == 0, 1.0 / (1.0 + e), jnp.where(lane == 1, e / (1.0 + e), 0.0))
        carry = carry + jnp.sum(onehot, axis=0, keepdims=True)
    carry_s[...] = carry
    cnt_ref[...] = carry


def route_tokens(x, g, wr, *, tm, sub, n_exp):
    t, d = x.shape
    per_step = tm // sub * SUBLANES
    return pl.pallas_call(
        functools.partial(_router_kernel, n_exp=n_exp, sub=sub),
        grid=(t // tm,),
        in_specs=[
            pl.BlockSpec((tm, d), lambda i: (i, 0)),
            pl.BlockSpec((1, d), lambda i: (0, 0)),
            pl.BlockSpec((2, d, LANES), lambda i: (0, 0, 0)),
        ],
        out_specs=[
            pl.BlockSpec((tm, LANES), lambda i: (i, 0)),
            pl.BlockSpec((tm, LANES), lambda i: (i, 0)),
            pl.BlockSpec((per_step, LANES), lambda i: (i, 0)),
            pl.BlockSpec((1, LANES), lambda i: (0, 0)),
        ],
        out_shape=[
            jax.ShapeDtypeStruct((t, LANES), jnp.int32),
            jax.ShapeDtypeStruct((t, LANES), F32),
            jax.ShapeDtypeStruct((t // tm * per_step, LANES), F32),
            jax.ShapeDtypeStruct((1, LANES), F32),
        ],
        scratch_shapes=[pltpu.VMEM((1, LANES), F32)],
        compiler_params=_params("arbitrary"), name="moe_router",
    )(x, g.reshape(1, d), wr)


def _dispatch_kernel(start_ref, shift_ref, keep_ref, x_ref, g_ref, meta_ref, zeros_hbm, xs_hbm,
                     stage_s, carry_s, sems, *, tm, n_exp):
    del zeros_hbm
    j = pl.program_id(0)
    slot = j % 2
    rows = tm + BF16_ROWS

    def block_copy(step, e, buf):
        first = pl.multiple_of(start_ref[step * n_exp + e], BF16_ROWS)
        return pltpu.make_async_copy(stage_s.at[buf, e], xs_hbm.at[pl.ds(first, rows)], sems.at[buf])

    @pl.when(j == 0)
    def _():
        carry_s[...] = jnp.zeros_like(carry_s)

    h = _rms(x_ref[...], g_ref[...]).astype(BF16)
    fields = meta_ref[...].astype(F32).T
    slot_row = lax.broadcasted_iota(jnp.int32, (rows, tm), 0).astype(F32)
    for e in range(n_exp):
        key = j * n_exp + e
        idx = jnp.where(fields[0:1] == e, fields[2:3], jnp.where(fields[1:2] == e, fields[3:4], -2.0 * rows))
        idx = idx + shift_ref[key].astype(F32)
        onehot = jnp.where(slot_row == idx, 1.0, 0.0).astype(BF16)
        stage_s[slot, e] = _dot(onehot, h).astype(BF16)
        stage_s[slot, e, 0:BF16_ROWS, :] += carry_s[e]
        keep = pl.multiple_of(keep_ref[key], BF16_ROWS)
        carry_s[e] = stage_s[slot, e, pl.ds(keep, BF16_ROWS), :]

    @pl.when(j > 0)
    def _():
        for e in range(n_exp):
            block_copy(j - 1, e, 1 - slot).wait()

    for e in range(n_exp):
        block_copy(j, e, slot).start()

    @pl.when(j == pl.num_programs(0) - 1)
    def _():
        for e in range(n_exp):
            block_copy(j, e, slot).wait()


def moe_dispatch(starts, shifts, keeps, x, g, meta, zeros, *, tm, n_exp):
    t, d = x.shape
    n_rows = zeros.shape[0]
    rows = tm + BF16_ROWS
    return pl.pallas_call(
        functools.partial(_dispatch_kernel, tm=tm, n_exp=n_exp),
        grid_spec=pltpu.PrefetchScalarGridSpec(
            num_scalar_prefetch=3, grid=(t // tm,),
            in_specs=[
                pl.BlockSpec((tm, d), lambda i, *_: (i, 0)),
                pl.BlockSpec((1, d), lambda i, *_: (0, 0)),
                pl.BlockSpec((tm, LANES), lambda i, *_: (i, 0)),
                pl.BlockSpec(memory_space=pl.ANY),
            ],
            out_specs=pl.BlockSpec(memory_space=pl.ANY),
            scratch_shapes=[pltpu.VMEM((2, n_exp, rows, d), BF16), pltpu.VMEM((n_exp, BF16_ROWS, d), BF16),
                            pltpu.SemaphoreType.DMA((2,))],
        ),
        out_shape=jax.ShapeDtypeStruct((n_rows, d), BF16),
        input_output_aliases={6: 0},
        compiler_params=_params("arbitrary"),
        name="moe_dispatch",
    )(starts, shifts, keeps, x, g.reshape(1, d), meta, zeros)


def _expert_kernel(te_ref, used_ref, h_ref, w13_ref, w2_ref, o_ref, gu_s, acc_s):
    del te_ref
    used = used_ref[pl.program_id(0)] != 0

    @pl.when(used)
    def _():
        acc_s[...] = jnp.zeros_like(acc_s)
        _swiglu_chunks(h_ref, w13_ref.at[0], w2_ref.at[0], acc_s, gu_s)
        o_ref[...] = acc_s[...].astype(o_ref.dtype)

    @pl.when(jnp.logical_not(used))
    def _():
        o_ref[...] = jnp.zeros_like(o_ref)


def moe_experts(tile_expert, tile_used, hs, w13, w2, *, tm):
    n_rows, d = hs.shape
    tf = w2.shape[2]
    resident = pl.Buffered(1)
    return pl.pallas_call(
        _expert_kernel,
        grid_spec=pltpu.PrefetchScalarGridSpec(
            num_scalar_prefetch=2, grid=(n_rows // tm,),
            in_specs=[
                pl.BlockSpec((tm, d), lambda i, te, tu: (i, 0)),
                pl.BlockSpec((1,) + w13.shape[1:], lambda i, te, tu: (te[i], 0, 0)),
                pl.BlockSpec((1,) + w2.shape[1:], lambda i, te, tu: (te[i], 0, 0, 0)),
            ],
            out_specs=pl.BlockSpec((tm, d), lambda i, te, tu: (i, 0)),
            scratch_shapes=[pltpu.VMEM((2, 2, tm, tf), F32), pltpu.VMEM((tm, d), F32)],
        ),
        out_shape=jax.ShapeDtypeStruct((n_rows, d), BF16),
        compiler_params=pltpu.CompilerParams(dimension_semantics=("arbitrary",), vmem_limit_bytes=EXPERT_VMEM_LIMIT),
        name="moe_experts",
    )(tile_expert, tile_used, hs, w13, w2)


def _combine_kernel(start_ref, shift_ref, tail_ref, small_ref, x_ref, wts_ref, meta_ref, g_ref, y_hbm, o_ref,
                    blk_s, sems, *, tm, n_exp):
    j = pl.program_id(0)
    slot = j % 2
    rows = tm + BF16_ROWS

    def block_copy(step, e, buf, n):
        first = pl.multiple_of(start_ref[step * n_exp + e], BF16_ROWS)
        return pltpu.make_async_copy(y_hbm.at[pl.ds(first, n)], blk_s.at[buf, e, pl.ds(0, n)], sems.at[buf])

    def blocks(step, buf, act):
        for e in range(n_exp):
            small = small_ref[step * n_exp + e] != 0

            @pl.when(small)
            def _(e=e):
                act(block_copy(step, e, buf, tm // 2))

            @pl.when(jnp.logical_not(small))
            def _(e=e):
                act(block_copy(step, e, buf, rows))

    @pl.when(j == 0)
    def _():
        blk_s[...] = jnp.zeros_like(blk_s)
        blocks(0, 0, lambda copy: copy.start())

    @pl.when(j + 1 < pl.num_programs(0))
    def _():
        blocks(j + 1, 1 - slot, lambda copy: copy.start())

    blocks(j, slot, lambda copy: copy.wait())

    meta = meta_ref[...]
    wts = wts_ref[...]
    col = lax.broadcasted_iota(jnp.int32, (tm, tm), 1)
    tail_col = lax.broadcasted_iota(jnp.int32, (tm, BF16_ROWS), 1) + tm

    def pick(e):
        sel = [meta[:, k:k + 1] == e for k in range(TOP_K)]
        idx = jnp.where(sel[0], meta[:, 2:3], jnp.where(sel[1], meta[:, 3:4], -2 * rows)) + shift_ref[j * n_exp + e]
        w = jnp.where(sel[0], wts[:, 0:1], jnp.where(sel[1], wts[:, 1:2], 0.0))
        return idx, w

    acc = x_ref[...]
    for e in range(n_exp):
        idx, w = pick(e)
        onehot = jnp.where(col == idx, 1.0, 0.0).astype(BF16)
        acc = acc + w * _dot(onehot, blk_s[slot, e, 0:tm, :])
    o_ref[...] = acc

    for e in range(n_exp):
        @pl.when(tail_ref[j * n_exp + e] != 0)
        def _(e=e):
            idx, w = pick(e)
            onehot_tail = jnp.where(tail_col == idx, 1.0, 0.0).astype(BF16)
            o_ref[...] += w * _dot(onehot_tail, blk_s[slot, e, tm:rows, :])

    o_ref[...] = _rms(o_ref[...], g_ref[...])


def moe_combine(starts, shifts, tails, smalls, x, wts, meta, g, y, *, tm, n_exp):
    t, d = x.shape
    n_pre = 4
    return pl.pallas_call(
        functools.partial(_combine_kernel, tm=tm, n_exp=n_exp),
        grid_spec=pltpu.PrefetchScalarGridSpec(
            num_scalar_prefetch=n_pre, grid=(t // tm,),
            in_specs=[
                pl.BlockSpec((tm, d), lambda i, *_: (i, 0)),
                pl.BlockSpec((tm, LANES), lambda i, *_: (i, 0)),
                pl.BlockSpec((tm, LANES), lambda i, *_: (i, 0)),
                pl.BlockSpec((1, d), lambda i, *_: (0, 0)),
                pl.BlockSpec(memory_space=pl.ANY),
            ],
            out_specs=pl.BlockSpec((tm, d), lambda i, *_: (i, 0)),
            scratch_shapes=[pltpu.VMEM((2, n_exp, tm + BF16_ROWS, d), BF16), pltpu.SemaphoreType.DMA((2,))],
        ),
        out_shape=jax.ShapeDtypeStruct((t, d), F32),
        compiler_params=_params("arbitrary"), name="moe_combine",
    )(starts, shifts, tails, smalls, x, wts, meta, g.reshape(1, d), y)


def _pad_cols(w, n):
    return jnp.pad(w, ((0, 0), (0, n - w.shape[1])))


def _even_mixer(x, norm, w_in, b_f, sgu_norm, w_s, b_s, w_ride, proj_rides, *, batch, seq, tq):
    n_heads = b_f.shape[0]
    a_width = n_heads * HEAD_DIM
    b_width = w_s.shape[0] * LANES
    f0 = 3 * a_width
    w_main = jnp.concatenate([w_in[:, :f0], w_in[:, f0 + n_heads:]], axis=1)
    w_gate = _pad_cols(w_in[:, f0:f0 + n_heads], LANES)
    main, gate, *proj_casts = norm_matmul(x, norm, [w_main.astype(BF16), w_gate.astype(BF16)], [BF16, F32],
                                          tm=min(IN_PROJ_ROWS, x.shape[0]), name="even_in_proj",
                                          scaled=(0, a_width, LOG2E * HEAD_DIM ** -0.5), rides=proj_rides)
    c = gate_cumsum(gate, _pad_cols(b_f.reshape(1, -1), LANES), seq=seq)
    n_pairs = a_width // LANES
    a, w_cast = fox_attention(main, c, w_ride, batch=batch, seq=seq, tq=tq, q_col=0, k_col=n_pairs,
                              v_col=2 * n_pairs, n_pairs=n_pairs)
    u_col = f0 // b_width
    return (a, (main, u_col, u_col + 1, sgu_norm, w_s, b_s)), w_cast, proj_casts


def _odd_mixer(x, norm, w_in, conv_w, lq1, lk1, lq2, lk2, subln, rel_bias, lam_init, w_ride, *, batch, seq, tq):
    c_width = conv_w.shape[1]
    d_width = rel_bias.shape[1] * 2 * HEAD_DIM
    q0 = 3 * c_width
    w_in = w_in.astype(BF16)
    c_out, main = conv_proj(x, norm, w_in[:, :q0], w_in[:, q0:], conv_w, tm=min(IN_PROJ_ROWS, seq), seq=seq,
                            scaled=(0, d_width, LOG2E * HEAD_DIM ** -0.5))
    bias, far, lam = diff_prep(rel_bias, lq1, lk1, lq2, lk2, tq=tq, lam_init=lam_init)
    n_heads = rel_bias.shape[1]
    d_out, w_cast = diff_attention(main, bias, far, lam, subln, w_ride, batch=batch, seq=seq, tq=tq, q_col=0,
                                   k_col=n_heads, v_col=2 * n_heads, lam_init=lam_init)
    return (c_out, d_out), w_cast


def _sorted_rows(t, n_exp):
    block_rows = ROUTE_ROWS + BF16_ROWS
    return TOP_K * t + n_exp * (EXPERT_ROWS + pl.cdiv(block_rows, EXPERT_ROWS) * EXPERT_ROWS)


def _moe_layer(x, norm, w_router, w13, w2, final_norm, zeros):
    t, d = x.shape
    n_exp = w_router.shape[1]
    tm_route = ROUTE_ROWS
    tm_expert = EXPERT_ROWS
    n_rows = zeros.shape[0]
    assert n_rows == _sorted_rows(t, n_exp)
    wr = _pad_cols(w_router, LANES)
    wr_hi = wr.astype(BF16)
    wr_split = jnp.stack([wr_hi, (wr - wr_hi.astype(F32)).astype(BF16)])
    meta, wts, before, counts = route_tokens(x, norm, wr_split, tm=ROUTE_STEP_ROWS, sub=tm_route, n_exp=n_exp)
    counts = counts[0, :n_exp].astype(jnp.int32)
    block_rows = tm_route + BF16_ROWS
    padded = (counts + block_rows + tm_expert - 1) // tm_expert * tm_expert
    ends = jnp.cumsum(padded)
    offsets = ends - padded
    tile_start = jnp.arange(n_rows // tm_expert, dtype=jnp.int32) * tm_expert
    tile_expert = jnp.minimum(jnp.sum(tile_start[:, None] >= ends[None, :], axis=1), n_exp - 1).astype(jnp.int32)
    tile_used = (tile_start < (offsets + counts)[tile_expert]).astype(jnp.int32)
    before = before[::SUBLANES, :n_exp].astype(jnp.int32)
    in_tile = jnp.concatenate([before[1:], counts[None]]) - before
    first = offsets[None, :] + before
    starts = first // BF16_ROWS * BF16_ROWS
    shifts = first - starts
    keeps = (shifts + in_tile) // BF16_ROWS * BF16_ROWS
    tails = (shifts + in_tile > tm_route).astype(jnp.int32)
    flat = lambda a: a.reshape(-1).astype(jnp.int32)
    hs = moe_dispatch(flat(starts), flat(shifts), flat(keeps), x, norm, meta, zeros, tm=tm_route, n_exp=n_exp)
    y = moe_experts(tile_expert, tile_used, hs, *_chunk_weights(w13, w2, EXPERT_CHUNK), tm=tm_expert)
    smalls = (shifts + in_tile <= tm_route // 2).astype(jnp.int32)
    return moe_combine(flat(starts), flat(shifts), flat(tails), flat(smalls), x, wts, meta, final_norm, y,
                       tm=tm_route, n_exp=n_exp)


def kernel(x, mem, rel_bias, mem_norm, final_norm, ev_norm, ev_w_in, ev_b_f, ev_sgu_norm, ev_w_s, ev_b_s, ev_w_out, ffn_w13, ffn_w2, od_norm, od_w_in, od_conv_w, od_lam_q1, od_lam_k1, od_lam_q2, od_lam_k2, od_subln, od_w_out, moe_router, moe_w13, moe_w2, x_norm, x_wq, x_wkv, x_wo, ffn_norm):
    batch, seq, d = x.shape
    mem_len = mem.shape[1]
    depth = x_norm.shape[0]
    assert depth == 2 and ev_norm.shape[0] == 1 and od_norm.shape[0] == 1
    x_heads, x_dh = 4, 128
    xf = x.reshape(batch * seq, d)
    wkv = jnp.concatenate([x_wkv[layer] for layer in range(depth)], axis=1).astype(BF16)
    (kv,) = norm_matmul(mem.reshape(batch * mem_len, d), mem_norm, [wkv], [BF16], tm=MEM_ROWS, name="mem_kv")

    def tail(xf, mixed, w_out, layer, ffn, name, zero_rows=0):
        return layer_tail(xf, *mixed, w_out.astype(BF16), x_norm[layer], x_wq[layer].astype(BF16), kv,
                          x_wo[layer].astype(BF16), ffn, zero_rows, tm=TAIL_ROWS, seq=seq, mem_len=mem_len, kv_col=layer,
                          n_heads=x_heads, dh=x_dh, name=name)

    e13, e2 = moe_w13[0], moe_w2[0]
    mixed, e13_bf16, ffn_bf16 = _even_mixer(xf, ev_norm[0], ev_w_in[0], ev_b_f[0], ev_sgu_norm[0], ev_w_s[0],
                                            ev_b_s[0], e13.reshape(-1, e13.shape[-1]), (ffn_w13[0], ffn_w2[0]),
                                            batch=batch, seq=seq, tq=ATTN_TILE)
    xf = tail(xf, mixed, ev_w_out[0], 0, (ffn_norm[0],) + _chunk_weights(*ffn_bf16, FFN_CHUNK), "even_tail")
    lam_init = 0.8 - 0.6 * math.exp(-0.3 * 1)
    mixed, e2_bf16 = _odd_mixer(xf, od_norm[0], od_w_in[0], od_conv_w[0], od_lam_q1[0], od_lam_k1[0],
                                od_lam_q2[0], od_lam_k2[0], od_subln[0], rel_bias, lam_init,
                                e2.reshape(-1, e2.shape[-1]), batch=batch, seq=seq, tq=ATTN_TILE)
    xf, zeros = tail(xf, mixed, od_w_out[0], 1, None, "odd_tail", _sorted_rows(xf.shape[0], moe_router.shape[-1]))
    out = _moe_layer(xf, ffn_norm[1], moe_router[0], e13_bf16.reshape(e13.shape), e2_bf16.reshape(e2.shape),
                     final_norm, zeros)
    return out.reshape(batch, seq, d)
```

```python
import functools
import math

import jax
import jax.numpy as jnp
from jax import lax
from jax.experimental import pallas as pl
from jax.experimental.pallas import tpu as pltpu

F32 = jnp.float32
BF16 = jnp.bfloat16
EPS = 1e-6
HEAD_DIM = 64
LANES = 128
SUBLANES = 8
BF16_ROWS = 16
N_BUCKETS = 32
MAX_DIST = 128
TOP_K = 2
LOG2E = 1.4426950408889634
ATTN_TILE = 512
IN_PROJ_ROWS = 1024
MEM_ROWS = 512
TAIL_ROWS = 512
ROUTE_ROWS = 256
ROUTE_STEP_ROWS = 512
EXPERT_ROWS = 512
FFN_CHUNK = 256
EXPERT_CHUNK = 512
VMEM_BYTES = 64 * 1024 * 1024
VMEM_LIMIT = VMEM_BYTES - 8 * 1024 * 1024
EXPERT_VMEM_LIMIT = VMEM_BYTES - 2 * 1024 * 1024


def _params(*sem):
    return pltpu.CompilerParams(dimension_semantics=sem, vmem_limit_bytes=VMEM_LIMIT)


def _rms(x, g):
    ms = jnp.mean(x * x, axis=-1, keepdims=True)
    return x * lax.rsqrt(ms + EPS) * g


def _dot(a, b):
    return jnp.dot(a, b, preferred_element_type=F32)


def _dot_nt(a, b):
    return lax.dot_general(a, b, (((1,), (1,)), ((), ())), preferred_element_type=F32)


def _norm_matmul_kernel(x_ref, g_ref, *refs, n_w, n_ride, chunk, scaled):
    w_refs, ride_in, o_refs = refs[:n_w], refs[n_w:n_w + n_ride], refs[n_w + n_ride:2 * n_w + n_ride]
    for src, dst in zip(ride_in, refs[2 * n_w + n_ride:]):
        dst[...] = src[...].astype(BF16)
    h = _rms(x_ref[...], g_ref[...]).astype(BF16)
    s0, s1, scale = scaled
    for k, (w_ref, o_ref) in enumerate(zip(w_refs, o_refs)):
        n = w_ref.shape[1]
        for c0 in range(0, n, chunk):
            c1 = min(c0 + chunk, n)
            y = _dot(h, w_ref[:, c0:c1])
            if k == 0 and s0 <= c0 and c1 <= s1:
                y = y * scale
            o_ref[:, c0:c1] = y.astype(o_ref.dtype)


def norm_matmul(x, g, ws, out_dtypes, *, tm, name, scaled=(0, 0, 1.0), rides=()):
    t, d = x.shape
    chunk = 512
    assert scaled[0] % chunk == 0 and scaled[1] % chunk == 0
    grid = (t // tm,)
    ride_specs = [_rider(w, grid) for w in rides]
    in_specs = [pl.BlockSpec((tm, d), lambda i: (i, 0)), pl.BlockSpec((1, d), lambda i: (0, 0))]
    in_specs += [pl.BlockSpec(w.shape, lambda i: (0, 0)) for w in ws] + [spec for spec, _ in ride_specs]
    out_specs = [pl.BlockSpec((tm, w.shape[1]), lambda i: (i, 0)) for w in ws] + [spec for spec, _ in ride_specs]
    out_shape = [jax.ShapeDtypeStruct((t, w.shape[1]), dt) for w, dt in zip(ws, out_dtypes)]
    out_shape += [shape for _, shape in ride_specs]
    return pl.pallas_call(
        functools.partial(_norm_matmul_kernel, n_w=len(ws), n_ride=len(rides), chunk=chunk, scaled=scaled),
        grid=grid, in_specs=in_specs, out_specs=out_specs, out_shape=out_shape,
        compiler_params=_params("parallel"), name=name,
    )(x, g.reshape(1, d), *ws, *rides)


def _gate_kernel(g_ref, b_ref, c_ref):
    s = g_ref.shape[0]
    row = lax.broadcasted_iota(jnp.int32, (LANES, LANES), 0)
    col = lax.broadcasted_iota(jnp.int32, (LANES, LANES), 1)
    tri = (row >= col).astype(F32)
    carry = jnp.zeros((1, LANES), F32)
    for blk in range(s // LANES):
        z = g_ref[blk * LANES:(blk + 1) * LANES, :] + b_ref[...]
        log_f = jnp.minimum(z, 0.0) - jnp.log1p(jnp.exp(-jnp.abs(z)))
        cs = jnp.dot(tri, log_f, precision=lax.Precision.HIGHEST, preferred_element_type=F32) + carry
        c_ref[blk * LANES:(blk + 1) * LANES, :] = cs
        carry = cs[LANES - 1:LANES, :]


def gate_cumsum(g, b, *, seq):
    t = g.shape[0]
    return pl.pallas_call(
        _gate_kernel, grid=(t // seq,),
        in_specs=[pl.BlockSpec((seq, LANES), lambda i: (i, 0)), pl.BlockSpec((1, LANES), lambda i: (0, 0))],
        out_specs=pl.BlockSpec((seq, LANES), lambda i: (i, 0)),
        out_shape=jax.ShapeDtypeStruct((t, LANES), F32),
        compiler_params=_params("parallel"), name="gate_cumsum",
    )(g, b)


def _split3(x):
    hi = x.astype(BF16).astype(F32)
    rest = x - hi
    mid = rest.astype(BF16).astype(F32)
    lo = (rest - mid).astype(BF16).astype(F32)
    return hi, mid, lo


def _augment(x, in_half, lane, base, pieces, pieces_first):
    n = len(pieces)
    p0, o0 = (base, base + n) if pieces_first else (base + n, base)
    aug = jnp.where((lane >= o0) & (lane < o0 + n), 1.0, 0.0)
    for idx, piece in enumerate(pieces):
        aug = jnp.where(lane == p0 + idx, piece, aug)
    return jnp.where(in_half, x, aug.astype(x.dtype))


def _halves(lane):
    return [(lane >= HEAD_DIM * hh) & (lane < HEAD_DIM * (hh + 1)) for hh in range(2)]


def _causal_attention(qa, ka_s, v_ref, bias_ref, i, tq):
    n_chunks = tq // LANES
    row = lax.broadcasted_iota(jnp.int32, (tq, tq), 0)
    col = lax.broadcasted_iota(jnp.int32, (tq, tq), 1)

    def scores(j):
        s = _dot_nt(qa, ka_s[j * tq:(j + 1) * tq, :])
        if bias_ref is not None and j >= i - 1:
            s = s + bias_ref[0, i - j]
        if j == i:
            s = jnp.where(row >= col, s, -jnp.inf)
        return [s[:, c * LANES:(c + 1) * LANES] for c in range(n_chunks)]

    m = jnp.full((tq, LANES), -jnp.inf, F32)
    for j in range(i + 1):
        for chunk in scores(j):
            m = jnp.maximum(m, chunk)
    m = jnp.broadcast_to(jnp.max(m, axis=1, keepdims=True), (tq, LANES))
    l = jnp.zeros((tq, LANES), F32)
    acc = jnp.zeros((tq, LANES), F32)
    for j in range(i + 1):
        ps = [jnp.exp2(chunk - m) for chunk in scores(j)]
        l = l + functools.reduce(lambda a, b: a + b, ps)
        acc = acc + _dot(jnp.concatenate(ps, axis=1).astype(BF16), v_ref[j * tq:(j + 1) * tq, :])
    return acc / jnp.sum(l, axis=1, keepdims=True)


def _rider(w, grid):
    steps = math.prod(grid)
    rows = w.shape[0] // steps
    assert rows * steps == w.shape[0] and rows % BF16_ROWS == 0

    def index(*ids):
        step = 0
        for n, idx in zip(grid, ids):
            step = step * n + idx
        return step, 0

    return pl.BlockSpec((rows, w.shape[1]), index), jax.ShapeDtypeStruct(w.shape, BF16)


def _fox_kernel(q_ref, k_ref, v_ref, c_ref, w_ref, o_ref, wcast_ref, ka_s, *, tq):
    wcast_ref[...] = w_ref[...].astype(BF16)
    hp = pl.program_id(1)
    lane = lax.broadcasted_iota(jnp.int32, (1, LANES), 1)
    halves = _halves(lane)

    def decay(c, hh):
        return jnp.sum(jnp.where(lane == 2 * hp + hh, c, 0.0), axis=1, keepdims=True) * LOG2E

    k = k_ref[...]
    c_all = c_ref[...]
    for hh in range(2):
        ka_s[hh] = _augment(k, halves[hh], lane, HEAD_DIM * (1 - hh), _split3(-decay(c_all, hh)), True)

    for i in range(q_ref.shape[0] // tq):
        rows = slice(i * tq, (i + 1) * tq)
        outs = []
        for hh in range(2):
            qa = _augment(q_ref[rows, :], halves[hh], lane, HEAD_DIM * (1 - hh), _split3(decay(c_ref[rows, :], hh)),
                          False)
            outs.append(_causal_attention(qa, ka_s.at[hh], v_ref, None, i, tq))
        o_ref[rows, :] = jnp.where(lane < HEAD_DIM, outs[0], outs[1]).astype(o_ref.dtype)


def fox_attention(qkv, c, w_ride, *, batch, seq, tq, q_col, k_col, v_col, n_pairs):
    t = batch * seq
    grid = (batch, n_pairs)
    ride_spec, ride_shape = _rider(w_ride, grid)
    block = lambda col: pl.BlockSpec((seq, LANES), lambda b, h: (b, col + h))
    return pl.pallas_call(
        functools.partial(_fox_kernel, tq=tq),
        grid=grid,
        in_specs=[block(q_col), block(k_col), block(v_col), pl.BlockSpec((seq, LANES), lambda b, h: (b, 0)),
                  ride_spec],
        out_specs=[block(0), ride_spec],
        out_shape=[jax.ShapeDtypeStruct((t, n_pairs * LANES), BF16), ride_shape],
        scratch_shapes=[pltpu.VMEM((2, seq, LANES), BF16)],
        compiler_params=_params("parallel", "parallel"), name="fox_attention",
    )(qkv, qkv, qkv, c, w_ride)


def _spatial_gate(u_ref, v_ref, norm_ref, ws_ref, bs_ref, o_ref):
    tb = u_ref.shape[0]
    n_groups, chunk, _ = ws_ref.shape
    row = lax.broadcasted_iota(jnp.int32, (chunk, chunk), 0)
    col = lax.broadcasted_iota(jnp.int32, (chunk, chunk), 1)
    tri = row >= col
    for g in range(n_groups):
        w = jnp.where(tri, ws_ref[g], 0.0).astype(BF16)
        bias = bs_ref[:, g:g + 1]
        gain = norm_ref[g:g + 1, :]
        for c in range(tb // chunk):
            rs = slice(c * chunk, (c + 1) * chunk)
            cs = slice(g * LANES, (g + 1) * LANES)
            vn = _rms(jax.nn.gelu(v_ref[rs, cs].astype(F32)), gain)
            mixed = _dot(w, vn.astype(BF16)) + bias
            o_ref[rs, cs] = (jax.nn.gelu(u_ref[rs, cs].astype(F32)) * mixed).astype(o_ref.dtype)


def _chunk_weights(w13, w2, tf):
    *lead, ff, d = w2.shape
    return w13.astype(BF16), w2.astype(BF16).reshape(*lead, ff // tf, tf, d)


def _swiglu_chunks(h_s, w13, w2, o_ref, gu_s):
    n, tf, _ = w2.shape

    def project(c, slot):
        for part in range(2):
            cols = pl.ds(pl.multiple_of((part * n + c) * tf, tf), tf)
            gu_s[slot, part] = _dot(h_s[...], w13[:, cols])

    def consume(c, slot):
        gate = gu_s[slot, 0]
        act = (gate * jax.nn.sigmoid(gate) * gu_s[slot, 1]).astype(BF16)
        o_ref[...] += _dot(act, w2[c])

    def pair(k, carry):
        c = 2 * k
        project(c + 1, 1)
        consume(c, 0)
        project(c + 2, 0)
        consume(c + 1, 1)
        return carry

    project(0, 0)
    lax.fori_loop(0, (n - 1) // 2, pair, 0)
    if n % 2 == 0:
        project(n - 1, 1)
        consume(n - 2, 0)
        consume(n - 1, 1)
    else:
        consume(n - 1, 0)


def _swiglu_scratch(tm, d, tf):
    return [pltpu.VMEM((tm, d), BF16), pltpu.VMEM((2, 2, tm, tf), F32)]


def _tail_kernel(*refs, n_heads, dh, gated, ffn, zero_fill):
    refs = list(refs)
    take = lambda n: [refs.pop(0) for _ in range(n)]
    x_ref, a_ref = take(2)
    b_in = take(5 if gated else 1)
    wa_ref, wb_ref, gx_ref, wq_ref, kv_ref, wo_ref = take(6)
    ffn_in = take(3 if ffn else 0)
    (o_ref,) = take(1)
    for z_ref in take(1 if zero_fill else 0):
        z_ref[...] = jnp.zeros_like(z_ref)
    if gated:
        (b_ref,) = take(1)
        _spatial_gate(*b_in, b_ref)
    else:
        (b_ref,) = b_in
    rest = ffn_in + [o_ref] + refs
    x = x_ref[...] + _dot(a_ref[...], wa_ref[...]) + _dot(b_ref[...], wb_ref[...])
    q = _dot(_rms(x, gx_ref[...]).astype(BF16), wq_ref[...]).astype(BF16)
    width = n_heads * dh
    outs = []
    for hd in range(n_heads):
        cs = slice(hd * dh, (hd + 1) * dh)
        s = _dot_nt(q[:, cs], kv_ref[:, cs]) * (dh ** -0.5)
        p = jnp.exp(s - jnp.max(s, axis=1, keepdims=True))
        p = p / jnp.sum(p, axis=1, keepdims=True)
        outs.append(_dot(p.astype(BF16), kv_ref[:, width + hd * dh:width + (hd + 1) * dh]).astype(BF16))
    x = x + _dot(jnp.concatenate(outs, axis=1), wo_ref[...])
    if len(rest) == 1:
        (o_ref,) = rest
        o_ref[...] = x
    else:
        gf_ref, w13_ref, w2_ref, o_ref, h_s, gu_s = rest
        h_s[...] = _rms(x, gf_ref[...]).astype(BF16)
        o_ref[...] = x
        _swiglu_chunks(h_s, w13_ref, w2_ref, o_ref, gu_s)


def layer_tail(x, a, b, w_out, gx, wq, kv, wo, ffn=None, zero_rows=0, *, tm, seq, mem_len, kv_col, n_heads, dh,
               name):
    t, d = x.shape
    per_b = seq // tm
    resident = pl.Buffered(1)
    const = lambda arr: pl.BlockSpec(arr.shape, lambda i: (0,) * arr.ndim, pipeline_mode=resident)
    rows = lambda arr: pl.BlockSpec((tm, arr.shape[1]), lambda i: (i, 0))
    vec = lambda g: g.reshape(1, d)
    gated = isinstance(b, tuple)
    scratch = []
    if gated:
        proj, u_col, v_col, sgu_norm, w_s, b_s = b
        b_width = w_s.shape[0] * LANES
        b_args = [proj, proj, sgu_norm, w_s, b_s.T]
        b_specs = [pl.BlockSpec((tm, b_width), lambda i: (i, u_col)), pl.BlockSpec((tm, b_width), lambda i: (i, v_col)),
                   const(sgu_norm), const(w_s), const(b_s.T)]
        scratch.append(pltpu.VMEM((tm, b_width), BF16))
    else:
        b_width = b.shape[1]
        b_args, b_specs = [b], [rows(b)]
    wa, wb = w_out[:a.shape[1]], w_out[a.shape[1]:]
    assert wb.shape[0] == b_width
    args = [x, a] + b_args + [wa, wb, vec(gx), wq, kv, wo]
    in_specs = [rows(x), rows(a)] + b_specs + [
        const(wa), const(wb), const(vec(gx)), const(wq),
        pl.BlockSpec((mem_len, 2 * n_heads * dh), lambda i: (i // per_b, kv_col)), const(wo)]
    if ffn is not None:
        gf, w13, w2 = ffn
        args += [vec(gf), w13, w2]
        in_specs += [const(vec(gf)), const(w13), const(w2)]
        scratch += _swiglu_scratch(tm, d, w2.shape[1])
    out_specs, out_shape = [rows(x)], [jax.ShapeDtypeStruct((t, d), F32)]
    if zero_rows:
        per_step = zero_rows // (t // tm)
        assert per_step * (t // tm) == zero_rows and per_step % BF16_ROWS == 0
        out_specs.append(pl.BlockSpec((per_step, d), lambda i: (i, 0)))
        out_shape.append(jax.ShapeDtypeStruct((zero_rows, d), BF16))
    outs = pl.pallas_call(
        functools.partial(_tail_kernel, n_heads=n_heads, dh=dh, gated=gated, ffn=ffn is not None,
                          zero_fill=bool(zero_rows)),
        grid=(t // tm,), in_specs=in_specs, out_specs=out_specs, out_shape=out_shape,
        scratch_shapes=scratch, compiler_params=_params("parallel"), name=name,
    )(*args)
    return outs if zero_rows else outs[0]


def _conv_proj_kernel(x_ref, g_ref, wc_ref, wa_ref, cw_ref, c_ref, o_ref, carry_s, *, steps_per_seq, scaled, chunk):
    h = _rms(x_ref[...], g_ref[...]).astype(BF16)
    tm, width = c_ref.shape
    n_taps = cw_ref.shape[0]
    bg, cg, xi = [_dot(h, wc_ref[:, k * width:(k + 1) * width]) for k in range(3)]
    xc = cg * xi
    first = pl.program_id(0) % steps_per_seq == 0
    prev = jnp.where(first, 0.0, carry_s[...])
    row = lax.broadcasted_iota(jnp.int32, (tm, width), 0)
    y = cw_ref[n_taps - 1:n_taps, :] * xc
    for back in range(1, n_taps):
        shifted = pltpu.roll(xc, back, axis=0)
        for r in range(back):
            src = SUBLANES - back + r
            shifted = jnp.where(row == r, prev[src:src + 1, :], shifted)
        y = y + cw_ref[n_taps - 1 - back:n_taps - back, :] * shifted
    c_ref[...] = (bg * y).astype(c_ref.dtype)
    carry_s[...] = xc[tm - SUBLANES:tm, :]
    s0, s1, scale = scaled
    for c0 in range(0, wa_ref.shape[1], chunk):
        y = _dot(h, wa_ref[:, c0:c0 + chunk])
        if s0 <= c0 and c0 + chunk <= s1:
            y = y * scale
        o_ref[:, c0:c0 + chunk] = y.astype(o_ref.dtype)


def conv_proj(x, g, w_conv, w_attn, conv_w, *, tm, seq, scaled):
    t, d = x.shape
    width = conv_w.shape[1]
    chunk = 512
    assert seq % tm == 0 and w_attn.shape[1] % chunk == 0 and conv_w.shape[0] - 1 <= SUBLANES
    const = lambda arr: pl.BlockSpec(arr.shape, lambda i: (0, 0))
    return pl.pallas_call(
        functools.partial(_conv_proj_kernel, steps_per_seq=seq // tm, scaled=scaled, chunk=chunk),
        grid=(t // tm,),
        in_specs=[pl.BlockSpec((tm, d), lambda i: (i, 0)), const(g.reshape(1, d)), const(w_conv), const(w_attn),
                  const(conv_w)],
        out_specs=[pl.BlockSpec((tm, width), lambda i: (i, 0)), pl.BlockSpec((tm, w_attn.shape[1]), lambda i: (i, 0))],
        out_shape=[jax.ShapeDtypeStruct((t, width), BF16), jax.ShapeDtypeStruct((t, w_attn.shape[1]), BF16)],
        scratch_shapes=[pltpu.VMEM((SUBLANES, width), F32)],
        compiler_params=_params("arbitrary"), name="odd_in_proj",
    )(x, g.reshape(1, d), w_conv, w_attn, conv_w)


def _diff_prep_kernel(rb_ref, lq1_ref, lk1_ref, lq2_ref, lk2_ref, bias_ref, far_ref, lam_ref, *, tq, lam_init):
    n_heads = bias_ref.shape[0]
    strip = 32
    row = lax.broadcasted_iota(jnp.int32, (strip, tq), 0)
    col = lax.broadcasted_iota(jnp.int32, (strip, tq), 1)
    max_exact = N_BUCKETS // 2

    def fill(r, carry):
        r0 = pl.multiple_of(r * strip, strip)
        for which in range(2):
            n = jnp.maximum(row + r0 - col + which * tq, 0)
            nf = jnp.maximum(n, 1).astype(F32)
            large = max_exact + (jnp.log(nf / max_exact) / math.log(MAX_DIST / max_exact)
                                 * (N_BUCKETS - max_exact)).astype(jnp.int32)
            large = jnp.minimum(large, N_BUCKETS - 1)
            bucket = jnp.where(n < max_exact, n, large)
            for h in range(n_heads):
                b = jnp.zeros((strip, tq), F32)
                for kk in range(N_BUCKETS):
                    b = jnp.where(bucket == kk, rb_ref[kk, h], b)
                bias_ref[h, which, pl.ds(r0, strip), :] = (b - rb_ref[N_BUCKETS - 1, h]) * LOG2E
        return carry

    lax.fori_loop(0, tq // strip, fill, 0)
    for h in range(n_heads):
        far_ref[h] = jnp.full((1, LANES), rb_ref[N_BUCKETS - 1, h], F32) * LOG2E
    lam = (jnp.exp(jnp.sum(lq1_ref[...] * lk1_ref[...], axis=1, keepdims=True))
           - jnp.exp(jnp.sum(lq2_ref[...] * lk2_ref[...], axis=1, keepdims=True)) + lam_init)
    lam_ref[...] = jnp.broadcast_to(lam, (1, LANES))


def diff_prep(rel_bias, lq1, lk1, lq2, lk2, *, tq, lam_init):
    n_heads = rel_bias.shape[1]
    vec = lambda a: a.reshape(1, -1)
    vspec = pl.BlockSpec(memory_space=pltpu.VMEM)
    return pl.pallas_call(
        functools.partial(_diff_prep_kernel, tq=tq, lam_init=lam_init),
        in_specs=[pl.BlockSpec(memory_space=pltpu.SMEM), vspec, vspec, vspec, vspec],
        out_specs=[vspec, vspec, vspec],
        out_shape=[
            jax.ShapeDtypeStruct((n_heads, 2, tq, tq), F32),
            jax.ShapeDtypeStruct((n_heads, 1, LANES), F32),
            jax.ShapeDtypeStruct((1, LANES), F32),
        ],
        compiler_params=pltpu.CompilerParams(vmem_limit_bytes=VMEM_LIMIT), name="diff_prep",
    )(rel_bias, vec(lq1), vec(lk1), vec(lq2), vec(lk2))


def _diff_kernel(q_ref, k_ref, v_ref, bias_ref, far_ref, lam_ref, subln_ref, w_ref, o_ref, wcast_ref,
                 ka_s, *, tq, lam_init):
    wcast_ref[...] = w_ref[...].astype(BF16)
    lane = lax.broadcasted_iota(jnp.int32, (1, LANES), 1)
    halves = _halves(lane)
    k = k_ref[...]
    far = _split3(far_ref[0])
    for sub in range(2):
        ka_s[sub] = _augment(k, halves[sub], lane, HEAD_DIM * (1 - sub), far, True)

    zero = jnp.zeros((1, LANES), F32)
    for i in range(q_ref.shape[0] // tq):
        rows = slice(i * tq, (i + 1) * tq)
        outs = []
        for sub in range(2):
            qa = _augment(q_ref[rows, :], halves[sub], lane, HEAD_DIM * (1 - sub), (zero,) * 3, False)
            outs.append(_causal_attention(qa, ka_s.at[sub], v_ref, bias_ref, i, tq))
        o = outs[0] - lam_ref[...] * outs[1]
        o_ref[rows, :] = (_rms(o, subln_ref[...]) * (1.0 - lam_init)).astype(o_ref.dtype)


def diff_attention(main, bias, far, lam, subln, w_ride, *, batch, seq, tq, q_col, k_col, v_col, lam_init):
    t = batch * seq
    n_heads = bias.shape[0]
    grid = (batch, n_heads)
    ride_spec, ride_shape = _rider(w_ride, grid)
    block = lambda col: pl.BlockSpec((seq, LANES), lambda b, h: (b, col + h))
    return pl.pallas_call(
        functools.partial(_diff_kernel, tq=tq, lam_init=lam_init),
        grid=grid,
        in_specs=[
            block(q_col), block(k_col), block(v_col),
            pl.BlockSpec((1, 2, tq, tq), lambda b, h: (h, 0, 0, 0)),
            pl.BlockSpec((1, 1, LANES), lambda b, h: (h, 0, 0)),
            pl.BlockSpec((1, LANES), lambda b, h: (0, 0)),
            pl.BlockSpec((1, LANES), lambda b, h: (0, 0)),
            ride_spec,
        ],
        out_specs=[block(0), ride_spec],
        out_shape=[jax.ShapeDtypeStruct((t, n_heads * LANES), BF16), ride_shape],
        scratch_shapes=[pltpu.VMEM((2, seq, LANES), BF16)],
        compiler_params=_params("parallel", "parallel"), name="diff_attention",
    )(main, main, main, bias, far, lam, subln.reshape(1, LANES), w_ride)


def _router_kernel(x_ref, g_ref, wr_ref, meta_ref, wts_ref, before_ref, cnt_ref, carry_s, *, n_exp, sub):
    @pl.when(pl.program_id(0) == 0)
    def _():
        carry_s[...] = jnp.zeros_like(carry_s)

    lane = lax.broadcasted_iota(jnp.int32, (sub, LANES), 1)
    lane_f = lane.astype(F32)
    row = lax.broadcasted_iota(jnp.int32, (sub, sub), 0)
    col = lax.broadcasted_iota(jnp.int32, (sub, sub), 1)
    earlier = jnp.where(row > col, 1.0, 0.0).astype(BF16)
    pick = lambda sel, val: jnp.sum(jnp.where(sel, val, 0.0), axis=1, keepdims=True)
    carry = carry_s[...]
    for r in range(x_ref.shape[0] // sub):
        rows = slice(r * sub, (r + 1) * sub)
        before_ref[r * SUBLANES:(r + 1) * SUBLANES, :] = jnp.broadcast_to(carry, (SUBLANES, LANES))
        h = _rms(x_ref[rows, :], g_ref[...])
        h_hi = h.astype(BF16)
        h_lo = (h - h_hi.astype(F32)).astype(BF16)
        logits = _dot(h_hi, wr_ref[0]) + (_dot(h_hi, wr_ref[1]) + _dot(h_lo, wr_ref[0]))
        logits = jnp.where(lane < n_exp, logits, -jnp.inf)
        m1 = jnp.max(logits, axis=1, keepdims=True)
        i1 = jnp.min(jnp.where(logits == m1, lane_f, float(LANES)), axis=1, keepdims=True)
        rest = jnp.where(lane_f == i1, -jnp.inf, logits)
        m2 = jnp.max(rest, axis=1, keepdims=True)
        i2 = jnp.min(jnp.where(rest == m2, lane_f, float(LANES)), axis=1, keepdims=True)
        e = jnp.exp(m2 - m1)
        sel1 = lane_f == i1
        sel2 = lane_f == i2
        onehot = jnp.where(sel1 | sel2, 1.0, 0.0)
        local = _dot(earlier, onehot.astype(BF16))
        meta = jnp.zeros((sub, LANES), F32)
        for idx, field in enumerate([i1, i2, pick(sel1, local), pick(sel2, local)]):
            meta = jnp.where(lane == idx, field, meta)
        meta_ref[rows, :] = meta.astype(jnp.int32)
        wts_ref[rows, :] = jnp.where(lane == 0, 1.0 / (1.0 + e), jnp.where(lane == 1, e / (1.0 + e), 0.0))
        carry = carry + jnp.sum(onehot, axis=0, keepdims=True)
    carry_s[...] = carry
    cnt_ref[...] = carry


def route_tokens(x, g, wr, *, tm, sub, n_exp):
    t, d = x.shape
    per_step = tm // sub * SUBLANES
    return pl.pallas_call(
        functools.partial(_router_kernel, n_exp=n_exp, sub=sub),
        grid=(t // tm,),
        in_specs=[
            pl.BlockSpec((tm, d), lambda i: (i, 0)),
            pl.BlockSpec((1, d), lambda i: (0, 0)),
            pl.BlockSpec((2, d, LANES), lambda i: (0, 0, 0)),
        ],
        out_specs=[
            pl.BlockSpec((tm, LANES), lambda i: (i, 0)),
            pl.BlockSpec((tm, LANES), lambda i: (i, 0)),
            pl.BlockSpec((per_step, LANES), lambda i: (i, 0)),
            pl.BlockSpec((1, LANES), lambda i: (0, 0)),
        ],
        out_shape=[
            jax.ShapeDtypeStruct((t, LANES), jnp.int32),
            jax.ShapeDtypeStruct((t, LANES), F32),
            jax.ShapeDtypeStruct((t // tm * per_step, LANES), F32),
            jax.ShapeDtypeStruct((1, LANES), F32),
        ],
        scratch_shapes=[pltpu.VMEM((1, LANES), F32)],
        compiler_params=_params("arbitrary"), name="moe_router",
    )(x, g.reshape(1, d), wr)


def _dispatch_kernel(start_ref, shift_ref, keep_ref, small_ref, x_ref, g_ref, meta_ref, zeros_hbm, xs_hbm,
                     stage_s, carry_s, sems, *, tm, n_exp):
    del zeros_hbm
    j = pl.program_id(0)
    slot = j % 2
    rows = tm + BF16_ROWS

    def block_copy(step, e, buf, n):
        first = pl.multiple_of(start_ref[step * n_exp + e], BF16_ROWS)
        return pltpu.make_async_copy(stage_s.at[buf, e, pl.ds(0, n)], xs_hbm.at[pl.ds(first, n)], sems.at[buf])

    def blocks(step, buf, act):
        for e in range(n_exp):
            small = small_ref[step * n_exp + e] != 0

            @pl.when(small)
            def _(e=e):
                act(block_copy(step, e, buf, tm // 2 + BF16_ROWS))

            @pl.when(jnp.logical_not(small))
            def _(e=e):
                act(block_copy(step, e, buf, rows))

    @pl.when(j == 0)
    def _():
        carry_s[...] = jnp.zeros_like(carry_s)

    h = _rms(x_ref[...], g_ref[...]).astype(BF16)
    fields = meta_ref[...].astype(F32).T
    slot_row = lax.broadcasted_iota(jnp.int32, (rows, tm), 0).astype(F32)
    for e in range(n_exp):
        key = j * n_exp + e
        idx = jnp.where(fields[0:1] == e, fields[2:3], jnp.where(fields[1:2] == e, fields[3:4], -2.0 * rows))
        idx = idx + shift_ref[key].astype(F32)
        onehot = jnp.where(slot_row == idx, 1.0, 0.0).astype(BF16)
        stage_s[slot, e] = _dot(onehot, h).astype(BF16)
        stage_s[slot, e, 0:BF16_ROWS, :] += carry_s[e]
        keep = pl.multiple_of(keep_ref[key], BF16_ROWS)
        carry_s[e] = stage_s[slot, e, pl.ds(keep, BF16_ROWS), :]

    @pl.when(j > 0)
    def _():
        blocks(j - 1, 1 - slot, lambda copy: copy.wait())

    blocks(j, slot, lambda copy: copy.start())

    @pl.when(j == pl.num_programs(0) - 1)
    def _():
        blocks(j, slot, lambda copy: copy.wait())


def moe_dispatch(starts, shifts, keeps, smalls, x, g, meta, zeros, *, tm, n_exp):
    t, d = x.shape
    n_rows = zeros.shape[0]
    rows = tm + BF16_ROWS
    return pl.pallas_call(
        functools.partial(_dispatch_kernel, tm=tm, n_exp=n_exp),
        grid_spec=pltpu.PrefetchScalarGridSpec(
            num_scalar_prefetch=4, grid=(t // tm,),
            in_specs=[
                pl.BlockSpec((tm, d), lambda i, *_: (i, 0)),
                pl.BlockSpec((1, d), lambda i, *_: (0, 0)),
                pl.BlockSpec((tm, LANES), lambda i, *_: (i, 0)),
                pl.BlockSpec(memory_space=pl.ANY),
            ],
            out_specs=pl.BlockSpec(memory_space=pl.ANY),
            scratch_shapes=[pltpu.VMEM((2, n_exp, rows, d), BF16), pltpu.VMEM((n_exp, BF16_ROWS, d), BF16),
                            pltpu.SemaphoreType.DMA((2,))],
        ),
        out_shape=jax.ShapeDtypeStruct((n_rows, d), BF16),
        input_output_aliases={7: 0},
        compiler_params=_params("arbitrary"),
        name="moe_dispatch",
    )(starts, shifts, keeps, smalls, x, g.reshape(1, d), meta, zeros)


def _expert_kernel(te_ref, used_ref, h_ref, w13_ref, w2_ref, o_ref, gu_s, acc_s):
    del te_ref
    used = used_ref[pl.program_id(0)] != 0

    @pl.when(used)
    def _():
        acc_s[...] = jnp.zeros_like(acc_s)
        _swiglu_chunks(h_ref, w13_ref.at[0], w2_ref.at[0], acc_s, gu_s)
        o_ref[...] = acc_s[...].astype(o_ref.dtype)

    @pl.when(jnp.logical_not(used))
    def _():
        o_ref[...] = jnp.zeros_like(o_ref)


def moe_experts(tile_expert, tile_used, hs, w13, w2, *, tm):
    n_rows, d = hs.shape
    tf = w2.shape[2]
    resident = pl.Buffered(1)
    return pl.pallas_call(
        _expert_kernel,
        grid_spec=pltpu.PrefetchScalarGridSpec(
            num_scalar_prefetch=2, grid=(n_rows // tm,),
            in_specs=[
                pl.BlockSpec((tm, d), lambda i, te, tu: (i, 0)),
                pl.BlockSpec((1,) + w13.shape[1:], lambda i, te, tu: (te[i], 0, 0)),
                pl.BlockSpec((1,) + w2.shape[1:], lambda i, te, tu: (te[i], 0, 0, 0)),
            ],
            out_specs=pl.BlockSpec((tm, d), lambda i, te, tu: (i, 0)),
            scratch_shapes=[pltpu.VMEM((2, 2, tm, tf), F32), pltpu.VMEM((tm, d), F32)],
        ),
        out_shape=jax.ShapeDtypeStruct((n_rows, d), BF16),
        compiler_params=pltpu.CompilerParams(dimension_semantics=("arbitrary",), vmem_limit_bytes=EXPERT_VMEM_LIMIT),
        name="moe_experts",
    )(tile_expert, tile_used, hs, w13, w2)


def _combine_kernel(start_ref, shift_ref, tail_ref, small_ref, x_ref, wts_ref, meta_ref, g_ref, y_hbm, o_ref,
                    blk_s, sems, *, tm, n_exp):
    j = pl.program_id(0)
    slot = j % 2
    rows = tm + BF16_ROWS

    def block_copy(step, e, buf, n):
        first = pl.multiple_of(start_ref[step * n_exp + e], BF16_ROWS)
        return pltpu.make_async_copy(y_hbm.at[pl.ds(first, n)], blk_s.at[buf, e, pl.ds(0, n)], sems.at[buf])

    def blocks(step, buf, act):
        for e in range(n_exp):
            small = small_ref[step * n_exp + e] != 0

            @pl.when(small)
            def _(e=e):
                act(block_copy(step, e, buf, tm // 2))

            @pl.when(jnp.logical_not(small))
            def _(e=e):
                act(block_copy(step, e, buf, rows))

    @pl.when(j == 0)
    def _():
        blk_s[...] = jnp.zeros_like(blk_s)
        blocks(0, 0, lambda copy: copy.start())

    @pl.when(j + 1 < pl.num_programs(0))
    def _():
        blocks(j + 1, 1 - slot, lambda copy: copy.start())

    blocks(j, slot, lambda copy: copy.wait())

    meta = meta_ref[...]
    wts = wts_ref[...]
    col = lax.broadcasted_iota(jnp.int32, (tm, tm), 1)
    tail_col = lax.broadcasted_iota(jnp.int32, (tm, BF16_ROWS), 1) + tm

    def pick(e):
        sel = [meta[:, k:k + 1] == e for k in range(TOP_K)]
        idx = jnp.where(sel[0], meta[:, 2:3], jnp.where(sel[1], meta[:, 3:4], -2 * rows)) + shift_ref[j * n_exp + e]
        w = jnp.where(sel[0], wts[:, 0:1], jnp.where(sel[1], wts[:, 1:2], 0.0))
        return idx, w

    acc = x_ref[...]
    for e in range(n_exp):
        idx, w = pick(e)
        onehot = jnp.where(col == idx, 1.0, 0.0).astype(BF16)
        acc = acc + w * _dot(onehot, blk_s[slot, e, 0:tm, :])
    o_ref[...] = acc

    for e in range(n_exp):
        @pl.when(tail_ref[j * n_exp + e] != 0)
        def _(e=e):
            idx, w = pick(e)
            onehot_tail = jnp.where(tail_col == idx, 1.0, 0.0).astype(BF16)
            o_ref[...] += w * _dot(onehot_tail, blk_s[slot, e, tm:rows, :])

    o_ref[...] = _rms(o_ref[...], g_ref[...])


def moe_combine(starts, shifts, tails, smalls, x, wts, meta, g, y, *, tm, n_exp):
    t, d = x.shape
    n_pre = 4
    return pl.pallas_call(
        functools.partial(_combine_kernel, tm=tm, n_exp=n_exp),
        grid_spec=pltpu.PrefetchScalarGridSpec(
            num_scalar_prefetch=n_pre, grid=(t // tm,),
            in_specs=[
                pl.BlockSpec((tm, d), lambda i, *_: (i, 0)),
                pl.BlockSpec((tm, LANES), lambda i, *_: (i, 0)),
                pl.BlockSpec((tm, LANES), lambda i, *_: (i, 0)),
                pl.BlockSpec((1, d), lambda i, *_: (0, 0)),
                pl.BlockSpec(memory_space=pl.ANY),
            ],
            out_specs=pl.BlockSpec((tm, d), lambda i, *_: (i, 0)),
            scratch_shapes=[pltpu.VMEM((2, n_exp, tm + BF16_ROWS, d), BF16), pltpu.SemaphoreType.DMA((2,))],
        ),
        out_shape=jax.ShapeDtypeStruct((t, d), F32),
        compiler_params=_params("arbitrary"), name="moe_combine",
    )(starts, shifts, tails, smalls, x, wts, meta, g.reshape(1, d), y)


def _pad_cols(w, n):
    return jnp.pad(w, ((0, 0), (0, n - w.shape[1])))


def _even_mixer(x, norm, w_in, b_f, sgu_norm, w_s, b_s, w_ride, proj_rides, *, batch, seq, tq):
    n_heads = b_f.shape[0]
    a_width = n_heads * HEAD_DIM
    b_width = w_s.shape[0] * LANES
    f0 = 3 * a_width
    w_main = jnp.concatenate([w_in[:, :f0], w_in[:, f0 + n_heads:]], axis=1)
    w_gate = _pad_cols(w_in[:, f0:f0 + n_heads], LANES)
    main, gate, *proj_casts = norm_matmul(x, norm, [w_main.astype(BF16), w_gate.astype(BF16)], [BF16, F32],
                                          tm=min(IN_PROJ_ROWS, x.shape[0]), name="even_in_proj",
                                          scaled=(0, a_width, LOG2E * HEAD_DIM ** -0.5), rides=proj_rides)
    c = gate_cumsum(gate, _pad_cols(b_f.reshape(1, -1), LANES), seq=seq)
    n_pairs = a_width // LANES
    a, w_cast = fox_attention(main, c, w_ride, batch=batch, seq=seq, tq=tq, q_col=0, k_col=n_pairs,
                              v_col=2 * n_pairs, n_pairs=n_pairs)
    u_col = f0 // b_width
    return (a, (main, u_col, u_col + 1, sgu_norm, w_s, b_s)), w_cast, proj_casts


def _odd_mixer(x, norm, w_in, conv_w, lq1, lk1, lq2, lk2, subln, rel_bias, lam_init, w_ride, *, batch, seq, tq):
    c_width = conv_w.shape[1]
    d_width = rel_bias.shape[1] * 2 * HEAD_DIM
    q0 = 3 * c_width
    w_in = w_in.astype(BF16)
    c_out, main = conv_proj(x, norm, w_in[:, :q0], w_in[:, q0:], conv_w, tm=min(IN_PROJ_ROWS, seq), seq=seq,
                            scaled=(0, d_width, LOG2E * HEAD_DIM ** -0.5))
    bias, far, lam = diff_prep(rel_bias, lq1, lk1, lq2, lk2, tq=tq, lam_init=lam_init)
    n_heads = rel_bias.shape[1]
    d_out, w_cast = diff_attention(main, bias, far, lam, subln, w_ride, batch=batch, seq=seq, tq=tq, q_col=0,
                                   k_col=n_heads, v_col=2 * n_heads, lam_init=lam_init)
    return (c_out, d_out), w_cast


def _sorted_rows(t, n_exp):
    block_rows = ROUTE_ROWS + BF16_ROWS
    return TOP_K * t + n_exp * (EXPERT_ROWS + pl.cdiv(block_rows, EXPERT_ROWS) * EXPERT_ROWS)


def _moe_layer(x, norm, w_router, w13, w2, final_norm, zeros):
    t, d = x.shape
    n_exp = w_router.shape[1]
    tm_route = ROUTE_ROWS
    tm_expert = EXPERT_ROWS
    n_rows = zeros.shape[0]
    assert n_rows == _sorted_rows(t, n_exp)
    wr = _pad_cols(w_router, LANES)
    wr_hi = wr.astype(BF16)
    wr_split = jnp.stack([wr_hi, (wr - wr_hi.astype(F32)).astype(BF16)])
    meta, wts, before, counts = route_tokens(x, norm, wr_split, tm=ROUTE_STEP_ROWS, sub=tm_route, n_exp=n_exp)
    counts = counts[0, :n_exp].astype(jnp.int32)
    block_rows = tm_route + BF16_ROWS
    padded = (counts + block_rows + tm_expert - 1) // tm_expert * tm_expert
    ends = jnp.cumsum(padded)
    offsets = ends - padded
    tile_start = jnp.arange(n_rows // tm_expert, dtype=jnp.int32) * tm_expert
    tile_expert = jnp.minimum(jnp.sum(tile_start[:, None] >= ends[None, :], axis=1), n_exp - 1).astype(jnp.int32)
    tile_used = (tile_start < (offsets + counts)[tile_expert]).astype(jnp.int32)
    before = before[::SUBLANES, :n_exp].astype(jnp.int32)
    in_tile = jnp.concatenate([before[1:], counts[None]]) - before
    first = offsets[None, :] + before
    starts = first // BF16_ROWS * BF16_ROWS
    shifts = first - starts
    keeps = (shifts + in_tile) // BF16_ROWS * BF16_ROWS
    tails = (shifts + in_tile > tm_route).astype(jnp.int32)
    flat = lambda a: a.reshape(-1).astype(jnp.int32)
    smalls = (shifts + in_tile <= tm_route // 2).astype(jnp.int32)
    hs = moe_dispatch(flat(starts), flat(shifts), flat(keeps), flat(smalls), x, norm, meta, zeros, tm=tm_route,
                      n_exp=n_exp)
    y = moe_experts(tile_expert, tile_used, hs, *_chunk_weights(w13, w2, EXPERT_CHUNK), tm=tm_expert)
    return moe_combine(flat(starts), flat(shifts), flat(tails), flat(smalls), x, wts, meta, final_norm, y,
                       tm=tm_route, n_exp=n_exp)


def kernel(x, mem, rel_bias, mem_norm, final_norm, ev_norm, ev_w_in, ev_b_f, ev_sgu_norm, ev_w_s, ev_b_s, ev_w_out, ffn_w13, ffn_w2, od_norm, od_w_in, od_conv_w, od_lam_q1, od_lam_k1, od_lam_q2, od_lam_k2, od_subln, od_w_out, moe_router, moe_w13, moe_w2, x_norm, x_wq, x_wkv, x_wo, ffn_norm):
    batch, seq, d = x.shape
    mem_len = mem.shape[1]
    depth = x_norm.shape[0]
    assert depth == 2 and ev_norm.shape[0] == 1 and od_norm.shape[0] == 1
    x_heads, x_dh = 4, 128
    xf = x.reshape(batch * seq, d)
    wkv = jnp.concatenate([x_wkv[layer] for layer in range(depth)], axis=1).astype(BF16)
    (kv,) = norm_matmul(mem.reshape(batch * mem_len, d), mem_norm, [wkv], [BF16], tm=MEM_ROWS, name="mem_kv")

    def tail(xf, mixed, w_out, layer, ffn, name, zero_rows=0):
        return layer_tail(xf, *mixed, w_out.astype(BF16), x_norm[layer], x_wq[layer].astype(BF16), kv,
                          x_wo[layer].astype(BF16), ffn, zero_rows, tm=TAIL_ROWS, seq=seq, mem_len=mem_len, kv_col=layer,
                          n_heads=x_heads, dh=x_dh, name=name)

    e13, e2 = moe_w13[0], moe_w2[0]
    mixed, e13_bf16, ffn_bf16 = _even_mixer(xf, ev_norm[0], ev_w_in[0], ev_b_f[0], ev_sgu_norm[0], ev_w_s[0],
                                            ev_b_s[0], e13.reshape(-1, e13.shape[-1]), (ffn_w13[0], ffn_w2[0]),
                                            batch=batch, seq=seq, tq=ATTN_TILE)
    xf = tail(xf, mixed, ev_w_out[0], 0, (ffn_norm[0],) + _chunk_weights(*ffn_bf16, FFN_CHUNK), "even_tail")
    lam_init = 0.8 - 0.6 * math.exp(-0.3 * 1)
    mixed, e2_bf16 = _odd_mixer(xf, od_norm[0], od_w_in[0], od_conv_w[0], od_lam_q1[0], od_lam_k1[0],
                                od_lam_q2[0], od_lam_k2[0], od_subln[0], rel_bias, lam_init,
                                e2.reshape(-1, e2.shape[-1]), batch=batch, seq=seq, tq=ATTN_TILE)
    xf, zeros = tail(xf, mixed, od_w_out[0], 1, None, "odd_tail", _sorted_rows(xf.shape[0], moe_router.shape[-1]))
    out = _moe_layer(xf, ffn_norm[1], moe_router[0], e13_bf16.reshape(e13.shape), e2_bf16.reshape(e2.shape),
                     final_norm, zeros)
    return out.reshape(batch, seq, d)
```

```python
import functools
import math

import jax
import jax.numpy as jnp
from jax import lax
from jax.experimental import pallas as pl
from jax.experimental.pallas import tpu as pltpu

F32 = jnp.float32
BF16 = jnp.bfloat16
EPS = 1e-6
HEAD_DIM = 64
LANES = 128
SUBLANES = 8
BF16_ROWS = 16
N_BUCKETS = 32
MAX_DIST = 128
TOP_K = 2
LOG2E = 1.4426950408889634
ATTN_TILE = 512
IN_PROJ_ROWS = 1024
MEM_ROWS = 512
TAIL_ROWS = 512
ROUTE_ROWS = 256
ROUTE_STEP_ROWS = 512
SMALL_BLOCK_ROWS = 96
EXPERT_ROWS = 512
FFN_CHUNK = 256
EXPERT_CHUNK = 512
VMEM_BYTES = 64 * 1024 * 1024
VMEM_LIMIT = VMEM_BYTES - 8 * 1024 * 1024
EXPERT_VMEM_LIMIT = VMEM_BYTES - 2 * 1024 * 1024


def _params(*sem):
    return pltpu.CompilerParams(dimension_semantics=sem, vmem_limit_bytes=VMEM_LIMIT)


def _rms(x, g):
    ms = jnp.mean(x * x, axis=-1, keepdims=True)
    return x * lax.rsqrt(ms + EPS) * g


def _dot(a, b):
    return jnp.dot(a, b, preferred_element_type=F32)


def _dot_nt(a, b):
    return lax.dot_general(a, b, (((1,), (1,)), ((), ())), preferred_element_type=F32)


def _norm_matmul_kernel(x_ref, g_ref, *refs, n_w, n_ride, chunk, scaled):
    w_refs, ride_in, o_refs = refs[:n_w], refs[n_w:n_w + n_ride], refs[n_w + n_ride:2 * n_w + n_ride]
    for src, dst in zip(ride_in, refs[2 * n_w + n_ride:]):
        dst[...] = src[...].astype(BF16)
    h = _rms(x_ref[...], g_ref[...]).astype(BF16)
    s0, s1, scale = scaled
    for k, (w_ref, o_ref) in enumerate(zip(w_refs, o_refs)):
        n = w_ref.shape[1]
        for c0 in range(0, n, chunk):
            c1 = min(c0 + chunk, n)
            y = _dot(h, w_ref[:, c0:c1])
            if k == 0 and s0 <= c0 and c1 <= s1:
                y = y * scale
            o_ref[:, c0:c1] = y.astype(o_ref.dtype)


def norm_matmul(x, g, ws, out_dtypes, *, tm, name, scaled=(0, 0, 1.0), rides=()):
    t, d = x.shape
    chunk = 512
    assert scaled[0] % chunk == 0 and scaled[1] % chunk == 0
    grid = (t // tm,)
    ride_specs = [_rider(w, grid) for w in rides]
    in_specs = [pl.BlockSpec((tm, d), lambda i: (i, 0)), pl.BlockSpec((1, d), lambda i: (0, 0))]
    in_specs += [pl.BlockSpec(w.shape, lambda i: (0, 0)) for w in ws] + [spec for spec, _ in ride_specs]
    out_specs = [pl.BlockSpec((tm, w.shape[1]), lambda i: (i, 0)) for w in ws] + [spec for spec, _ in ride_specs]
    out_shape = [jax.ShapeDtypeStruct((t, w.shape[1]), dt) for w, dt in zip(ws, out_dtypes)]
    out_shape += [shape for _, shape in ride_specs]
    return pl.pallas_call(
        functools.partial(_norm_matmul_kernel, n_w=len(ws), n_ride=len(rides), chunk=chunk, scaled=scaled),
        grid=grid, in_specs=in_specs, out_specs=out_specs, out_shape=out_shape,
        compiler_params=_params("parallel"), name=name,
    )(x, g.reshape(1, d), *ws, *rides)


def _gate_kernel(g_ref, b_ref, c_ref):
    s = g_ref.shape[0]
    row = lax.broadcasted_iota(jnp.int32, (LANES, LANES), 0)
    col = lax.broadcasted_iota(jnp.int32, (LANES, LANES), 1)
    tri = (row >= col).astype(F32)
    carry = jnp.zeros((1, LANES), F32)
    for blk in range(s // LANES):
        z = g_ref[blk * LANES:(blk + 1) * LANES, :] + b_ref[...]
        log_f = jnp.minimum(z, 0.0) - jnp.log1p(jnp.exp(-jnp.abs(z)))
        cs = jnp.dot(tri, log_f, precision=lax.Precision.HIGHEST, preferred_element_type=F32) + carry
        c_ref[blk * LANES:(blk + 1) * LANES, :] = cs
        carry = cs[LANES - 1:LANES, :]


def gate_cumsum(g, b, *, seq):
    t = g.shape[0]
    return pl.pallas_call(
        _gate_kernel, grid=(t // seq,),
        in_specs=[pl.BlockSpec((seq, LANES), lambda i: (i, 0)), pl.BlockSpec((1, LANES), lambda i: (0, 0))],
        out_specs=pl.BlockSpec((seq, LANES), lambda i: (i, 0)),
        out_shape=jax.ShapeDtypeStruct((t, LANES), F32),
        compiler_params=_params("parallel"), name="gate_cumsum",
    )(g, b)


def _split3(x):
    hi = x.astype(BF16).astype(F32)
    rest = x - hi
    mid = rest.astype(BF16).astype(F32)
    lo = (rest - mid).astype(BF16).astype(F32)
    return hi, mid, lo


def _augment(x, in_half, lane, base, pieces, pieces_first):
    n = len(pieces)
    p0, o0 = (base, base + n) if pieces_first else (base + n, base)
    aug = jnp.where((lane >= o0) & (lane < o0 + n), 1.0, 0.0)
    for idx, piece in enumerate(pieces):
        aug = jnp.where(lane == p0 + idx, piece, aug)
    return jnp.where(in_half, x, aug.astype(x.dtype))


def _halves(lane):
    return [(lane >= HEAD_DIM * hh) & (lane < HEAD_DIM * (hh + 1)) for hh in range(2)]


def _causal_attention(qa, ka_s, v_ref, bias_ref, i, tq):
    n_chunks = tq // LANES
    row = lax.broadcasted_iota(jnp.int32, (tq, tq), 0)
    col = lax.broadcasted_iota(jnp.int32, (tq, tq), 1)

    def scores(j):
        s = _dot_nt(qa, ka_s[j * tq:(j + 1) * tq, :])
        if bias_ref is not None and j >= i - 1:
            s = s + bias_ref[0, i - j]
        if j == i:
            s = jnp.where(row >= col, s, -jnp.inf)
        return [s[:, c * LANES:(c + 1) * LANES] for c in range(n_chunks)]

    m = jnp.full((tq, LANES), -jnp.inf, F32)
    for j in range(i + 1):
        for chunk in scores(j):
            m = jnp.maximum(m, chunk)
    m = jnp.broadcast_to(jnp.max(m, axis=1, keepdims=True), (tq, LANES))
    l = jnp.zeros((tq, LANES), F32)
    acc = jnp.zeros((tq, LANES), F32)
    for j in range(i + 1):
        ps = [jnp.exp2(chunk - m) for chunk in scores(j)]
        l = l + functools.reduce(lambda a, b: a + b, ps)
        acc = acc + _dot(jnp.concatenate(ps, axis=1).astype(BF16), v_ref[j * tq:(j + 1) * tq, :])
    return acc / jnp.sum(l, axis=1, keepdims=True)


def _rider(w, grid):
    steps = math.prod(grid)
    rows = w.shape[0] // steps
    assert rows * steps == w.shape[0] and rows % BF16_ROWS == 0

    def index(*ids):
        step = 0
        for n, idx in zip(grid, ids):
            step = step * n + idx
        return step, 0

    return pl.BlockSpec((rows, w.shape[1]), index), jax.ShapeDtypeStruct(w.shape, BF16)


def _fox_kernel(q_ref, k_ref, v_ref, c_ref, w_ref, o_ref, wcast_ref, ka_s, *, tq):
    wcast_ref[...] = w_ref[...].astype(BF16)
    hp = pl.program_id(1)
    lane = lax.broadcasted_iota(jnp.int32, (1, LANES), 1)
    halves = _halves(lane)

    def decay(c, hh):
        return jnp.sum(jnp.where(lane == 2 * hp + hh, c, 0.0), axis=1, keepdims=True) * LOG2E

    k = k_ref[...]
    c_all = c_ref[...]
    for hh in range(2):
        ka_s[hh] = _augment(k, halves[hh], lane, HEAD_DIM * (1 - hh), _split3(-decay(c_all, hh)), True)

    for i in range(q_ref.shape[0] // tq):
        rows = slice(i * tq, (i + 1) * tq)
        outs = []
        for hh in range(2):
            qa = _augment(q_ref[rows, :], halves[hh], lane, HEAD_DIM * (1 - hh), _split3(decay(c_ref[rows, :], hh)),
                          False)
            outs.append(_causal_attention(qa, ka_s.at[hh], v_ref, None, i, tq))
        o_ref[rows, :] = jnp.where(lane < HEAD_DIM, outs[0], outs[1]).astype(o_ref.dtype)


def fox_attention(qkv, c, w_ride, *, batch, seq, tq, q_col, k_col, v_col, n_pairs):
    t = batch * seq
    grid = (batch, n_pairs)
    ride_spec, ride_shape = _rider(w_ride, grid)
    block = lambda col: pl.BlockSpec((seq, LANES), lambda b, h: (b, col + h))
    return pl.pallas_call(
        functools.partial(_fox_kernel, tq=tq),
        grid=grid,
        in_specs=[block(q_col), block(k_col), block(v_col), pl.BlockSpec((seq, LANES), lambda b, h: (b, 0)),
                  ride_spec],
        out_specs=[block(0), ride_spec],
        out_shape=[jax.ShapeDtypeStruct((t, n_pairs * LANES), BF16), ride_shape],
        scratch_shapes=[pltpu.VMEM((2, seq, LANES), BF16)],
        compiler_params=_params("parallel", "parallel"), name="fox_attention",
    )(qkv, qkv, qkv, c, w_ride)


def _spatial_gate(u_ref, v_ref, norm_ref, ws_ref, bs_ref, o_ref):
    tb = u_ref.shape[0]
    n_groups, chunk, _ = ws_ref.shape
    row = lax.broadcasted_iota(jnp.int32, (chunk, chunk), 0)
    col = lax.broadcasted_iota(jnp.int32, (chunk, chunk), 1)
    tri = row >= col
    for g in range(n_groups):
        w = jnp.where(tri, ws_ref[g], 0.0).astype(BF16)
        bias = bs_ref[:, g:g + 1]
        gain = norm_ref[g:g + 1, :]
        for c in range(tb // chunk):
            rs = slice(c * chunk, (c + 1) * chunk)
            cs = slice(g * LANES, (g + 1) * LANES)
            vn = _rms(jax.nn.gelu(v_ref[rs, cs].astype(F32)), gain)
            mixed = _dot(w, vn.astype(BF16)) + bias
            o_ref[rs, cs] = (jax.nn.gelu(u_ref[rs, cs].astype(F32)) * mixed).astype(o_ref.dtype)


def _chunk_weights(w13, w2, tf):
    *lead, ff, d = w2.shape
    return w13.astype(BF16), w2.astype(BF16).reshape(*lead, ff // tf, tf, d)


def _swiglu_chunks(h_s, w13, w2, o_ref, gu_s):
    n, tf, _ = w2.shape

    def project(c, slot):
        for part in range(2):
            cols = pl.ds(pl.multiple_of((part * n + c) * tf, tf), tf)
            gu_s[slot, part] = _dot(h_s[...], w13[:, cols])

    def consume(c, slot):
        gate = gu_s[slot, 0]
        act = (gate * jax.nn.sigmoid(gate) * gu_s[slot, 1]).astype(BF16)
        o_ref[...] += _dot(act, w2[c])

    def pair(k, carry):
        c = 2 * k
        project(c + 1, 1)
        consume(c, 0)
        project(c + 2, 0)
        consume(c + 1, 1)
        return carry

    project(0, 0)
    lax.fori_loop(0, (n - 1) // 2, pair, 0)
    if n % 2 == 0:
        project(n - 1, 1)
        consume(n - 2, 0)
        consume(n - 1, 1)
    else:
        consume(n - 1, 0)


def _swiglu_scratch(tm, d, tf):
    return [pltpu.VMEM((tm, d), BF16), pltpu.VMEM((2, 2, tm, tf), F32)]


def _tail_kernel(*refs, n_heads, dh, gated, ffn, zero_fill):
    refs = list(refs)
    take = lambda n: [refs.pop(0) for _ in range(n)]
    x_ref, a_ref = take(2)
    b_in = take(5 if gated else 1)
    wa_ref, wb_ref, gx_ref, wq_ref, kv_ref, wo_ref = take(6)
    ffn_in = take(3 if ffn else 0)
    (o_ref,) = take(1)
    for z_ref in take(1 if zero_fill else 0):
        z_ref[...] = jnp.zeros_like(z_ref)
    if gated:
        (b_ref,) = take(1)
        _spatial_gate(*b_in, b_ref)
    else:
        (b_ref,) = b_in
    rest = ffn_in + [o_ref] + refs
    x = x_ref[...] + _dot(a_ref[...], wa_ref[...]) + _dot(b_ref[...], wb_ref[...])
    q = _dot(_rms(x, gx_ref[...]).astype(BF16), wq_ref[...]).astype(BF16)
    width = n_heads * dh
    outs = []
    for hd in range(n_heads):
        cs = slice(hd * dh, (hd + 1) * dh)
        s = _dot_nt(q[:, cs], kv_ref[:, cs]) * (dh ** -0.5)
        p = jnp.exp(s - jnp.max(s, axis=1, keepdims=True))
        p = p / jnp.sum(p, axis=1, keepdims=True)
        outs.append(_dot(p.astype(BF16), kv_ref[:, width + hd * dh:width + (hd + 1) * dh]).astype(BF16))
    x = x + _dot(jnp.concatenate(outs, axis=1), wo_ref[...])
    if len(rest) == 1:
        (o_ref,) = rest
        o_ref[...] = x
    else:
        gf_ref, w13_ref, w2_ref, o_ref, h_s, gu_s = rest
        h_s[...] = _rms(x, gf_ref[...]).astype(BF16)
        o_ref[...] = x
        _swiglu_chunks(h_s, w13_ref, w2_ref, o_ref, gu_s)


def layer_tail(x, a, b, w_out, gx, wq, kv, wo, ffn=None, zero_rows=0, *, tm, seq, mem_len, kv_col, n_heads, dh,
               name):
    t, d = x.shape
    per_b = seq // tm
    resident = pl.Buffered(1)
    const = lambda arr: pl.BlockSpec(arr.shape, lambda i: (0,) * arr.ndim, pipeline_mode=resident)
    rows = lambda arr: pl.BlockSpec((tm, arr.shape[1]), lambda i: (i, 0))
    vec = lambda g: g.reshape(1, d)
    gated = isinstance(b, tuple)
    scratch = []
    if gated:
        proj, u_col, v_col, sgu_norm, w_s, b_s = b
        b_width = w_s.shape[0] * LANES
        b_args = [proj, proj, sgu_norm, w_s, b_s.T]
        b_specs = [pl.BlockSpec((tm, b_width), lambda i: (i, u_col)), pl.BlockSpec((tm, b_width), lambda i: (i, v_col)),
                   const(sgu_norm), const(w_s), const(b_s.T)]
        scratch.append(pltpu.VMEM((tm, b_width), BF16))
    else:
        b_width = b.shape[1]
        b_args, b_specs = [b], [rows(b)]
    wa, wb = w_out[:a.shape[1]], w_out[a.shape[1]:]
    assert wb.shape[0] == b_width
    args = [x, a] + b_args + [wa, wb, vec(gx), wq, kv, wo]
    in_specs = [rows(x), rows(a)] + b_specs + [
        const(wa), const(wb), const(vec(gx)), const(wq),
        pl.BlockSpec((mem_len, 2 * n_heads * dh), lambda i: (i // per_b, kv_col)), const(wo)]
    if ffn is not None:
        gf, w13, w2 = ffn
        args += [vec(gf), w13, w2]
        in_specs += [const(vec(gf)), const(w13), const(w2)]
        scratch += _swiglu_scratch(tm, d, w2.shape[1])
    out_specs, out_shape = [rows(x)], [jax.ShapeDtypeStruct((t, d), F32)]
    if zero_rows:
        per_step = zero_rows // (t // tm)
        assert per_step * (t // tm) == zero_rows and per_step % BF16_ROWS == 0
        out_specs.append(pl.BlockSpec((per_step, d), lambda i: (i, 0)))
        out_shape.append(jax.ShapeDtypeStruct((zero_rows, d), BF16))
    outs = pl.pallas_call(
        functools.partial(_tail_kernel, n_heads=n_heads, dh=dh, gated=gated, ffn=ffn is not None,
                          zero_fill=bool(zero_rows)),
        grid=(t // tm,), in_specs=in_specs, out_specs=out_specs, out_shape=out_shape,
        scratch_shapes=scratch, compiler_params=_params("parallel"), name=name,
    )(*args)
    return outs if zero_rows else outs[0]


def _conv_proj_kernel(x_ref, g_ref, wc_ref, wa_ref, cw_ref, c_ref, o_ref, carry_s, *, steps_per_seq, scaled, chunk):
    h = _rms(x_ref[...], g_ref[...]).astype(BF16)
    tm, width = c_ref.shape
    n_taps = cw_ref.shape[0]
    bg, cg, xi = [_dot(h, wc_ref[:, k * width:(k + 1) * width]) for k in range(3)]
    xc = cg * xi
    first = pl.program_id(0) % steps_per_seq == 0
    prev = jnp.where(first, 0.0, carry_s[...])
    row = lax.broadcasted_iota(jnp.int32, (tm, width), 0)
    y = cw_ref[n_taps - 1:n_taps, :] * xc
    for back in range(1, n_taps):
        shifted = pltpu.roll(xc, back, axis=0)
        for r in range(back):
            src = SUBLANES - back + r
            shifted = jnp.where(row == r, prev[src:src + 1, :], shifted)
        y = y + cw_ref[n_taps - 1 - back:n_taps - back, :] * shifted
    c_ref[...] = (bg * y).astype(c_ref.dtype)
    carry_s[...] = xc[tm - SUBLANES:tm, :]
    s0, s1, scale = scaled
    for c0 in range(0, wa_ref.shape[1], chunk):
        y = _dot(h, wa_ref[:, c0:c0 + chunk])
        if s0 <= c0 and c0 + chunk <= s1:
            y = y * scale
        o_ref[:, c0:c0 + chunk] = y.astype(o_ref.dtype)


def conv_proj(x, g, w_conv, w_attn, conv_w, *, tm, seq, scaled):
    t, d = x.shape
    width = conv_w.shape[1]
    chunk = 512
    assert seq % tm == 0 and w_attn.shape[1] % chunk == 0 and conv_w.shape[0] - 1 <= SUBLANES
    const = lambda arr: pl.BlockSpec(arr.shape, lambda i: (0, 0))
    return pl.pallas_call(
        functools.partial(_conv_proj_kernel, steps_per_seq=seq // tm, scaled=scaled, chunk=chunk),
        grid=(t // tm,),
        in_specs=[pl.BlockSpec((tm, d), lambda i: (i, 0)), const(g.reshape(1, d)), const(w_conv), const(w_attn),
                  const(conv_w)],
        out_specs=[pl.BlockSpec((tm, width), lambda i: (i, 0)), pl.BlockSpec((tm, w_attn.shape[1]), lambda i: (i, 0))],
        out_shape=[jax.ShapeDtypeStruct((t, width), BF16), jax.ShapeDtypeStruct((t, w_attn.shape[1]), BF16)],
        scratch_shapes=[pltpu.VMEM((SUBLANES, width), F32)],
        compiler_params=_params("arbitrary"), name="odd_in_proj",
    )(x, g.reshape(1, d), w_conv, w_attn, conv_w)


def _diff_prep_kernel(rb_ref, lq1_ref, lk1_ref, lq2_ref, lk2_ref, bias_ref, far_ref, lam_ref, *, tq, lam_init):
    n_heads = bias_ref.shape[0]
    strip = 32
    row = lax.broadcasted_iota(jnp.int32, (strip, tq), 0)
    col = lax.broadcasted_iota(jnp.int32, (strip, tq), 1)
    max_exact = N_BUCKETS // 2

    def fill(r, carry):
        r0 = pl.multiple_of(r * strip, strip)
        for which in range(2):
            n = jnp.maximum(row + r0 - col + which * tq, 0)
            nf = jnp.maximum(n, 1).astype(F32)
            large = max_exact + (jnp.log(nf / max_exact) / math.log(MAX_DIST / max_exact)
                                 * (N_BUCKETS - max_exact)).astype(jnp.int32)
            large = jnp.minimum(large, N_BUCKETS - 1)
            bucket = jnp.where(n < max_exact, n, large)
            for h in range(n_heads):
                b = jnp.zeros((strip, tq), F32)
                for kk in range(N_BUCKETS):
                    b = jnp.where(bucket == kk, rb_ref[kk, h], b)
                bias_ref[h, which, pl.ds(r0, strip), :] = (b - rb_ref[N_BUCKETS - 1, h]) * LOG2E
        return carry

    lax.fori_loop(0, tq // strip, fill, 0)
    for h in range(n_heads):
        far_ref[h] = jnp.full((1, LANES), rb_ref[N_BUCKETS - 1, h], F32) * LOG2E
    lam = (jnp.exp(jnp.sum(lq1_ref[...] * lk1_ref[...], axis=1, keepdims=True))
           - jnp.exp(jnp.sum(lq2_ref[...] * lk2_ref[...], axis=1, keepdims=True)) + lam_init)
    lam_ref[...] = jnp.broadcast_to(lam, (1, LANES))


def diff_prep(rel_bias, lq1, lk1, lq2, lk2, *, tq, lam_init):
    n_heads = rel_bias.shape[1]
    vec = lambda a: a.reshape(1, -1)
    vspec = pl.BlockSpec(memory_space=pltpu.VMEM)
    return pl.pallas_call(
        functools.partial(_diff_prep_kernel, tq=tq, lam_init=lam_init),
        in_specs=[pl.BlockSpec(memory_space=pltpu.SMEM), vspec, vspec, vspec, vspec],
        out_specs=[vspec, vspec, vspec],
        out_shape=[
            jax.ShapeDtypeStruct((n_heads, 2, tq, tq), F32),
            jax.ShapeDtypeStruct((n_heads, 1, LANES), F32),
            jax.ShapeDtypeStruct((1, LANES), F32),
        ],
        compiler_params=pltpu.CompilerParams(vmem_limit_bytes=VMEM_LIMIT), name="diff_prep",
    )(rel_bias, vec(lq1), vec(lk1), vec(lq2), vec(lk2))


def _diff_kernel(q_ref, k_ref, v_ref, bias_ref, far_ref, lam_ref, subln_ref, w_ref, o_ref, wcast_ref,
                 ka_s, *, tq, lam_init):
    wcast_ref[...] = w_ref[...].astype(BF16)
    lane = lax.broadcasted_iota(jnp.int32, (1, LANES), 1)
    halves = _halves(lane)
    k = k_ref[...]
    far = _split3(far_ref[0])
    for sub in range(2):
        ka_s[sub] = _augment(k, halves[sub], lane, HEAD_DIM * (1 - sub), far, True)

    zero = jnp.zeros((1, LANES), F32)
    for i in range(q_ref.shape[0] // tq):
        rows = slice(i * tq, (i + 1) * tq)
        outs = []
        for sub in range(2):
            qa = _augment(q_ref[rows, :], halves[sub], lane, HEAD_DIM * (1 - sub), (zero,) * 3, False)
            outs.append(_causal_attention(qa, ka_s.at[sub], v_ref, bias_ref, i, tq))
        o = outs[0] - lam_ref[...] * outs[1]
        o_ref[rows, :] = (_rms(o, subln_ref[...]) * (1.0 - lam_init)).astype(o_ref.dtype)


def diff_attention(main, bias, far, lam, subln, w_ride, *, batch, seq, tq, q_col, k_col, v_col, lam_init):
    t = batch * seq
    n_heads = bias.shape[0]
    grid = (batch, n_heads)
    ride_spec, ride_shape = _rider(w_ride, grid)
    block = lambda col: pl.BlockSpec((seq, LANES), lambda b, h: (b, col + h))
    return pl.pallas_call(
        functools.partial(_diff_kernel, tq=tq, lam_init=lam_init),
        grid=grid,
        in_specs=[
            block(q_col), block(k_col), block(v_col),
            pl.BlockSpec((1, 2, tq, tq), lambda b, h: (h, 0, 0, 0)),
            pl.BlockSpec((1, 1, LANES), lambda b, h: (h, 0, 0)),
            pl.BlockSpec((1, LANES), lambda b, h: (0, 0)),
            pl.BlockSpec((1, LANES), lambda b, h: (0, 0)),
            ride_spec,
        ],
        out_specs=[block(0), ride_spec],
        out_shape=[jax.ShapeDtypeStruct((t, n_heads * LANES), BF16), ride_shape],
        scratch_shapes=[pltpu.VMEM((2, seq, LANES), BF16)],
        compiler_params=_params("parallel", "parallel"), name="diff_attention",
    )(main, main, main, bias, far, lam, subln.reshape(1, LANES), w_ride)


def _router_kernel(x_ref, g_ref, wr_ref, meta_ref, wts_ref, before_ref, cnt_ref, carry_s, *, n_exp, sub):
    @pl.when(pl.program_id(0) == 0)
    def _():
        carry_s[...] = jnp.zeros_like(carry_s)

    lane = lax.broadcasted_iota(jnp.int32, (sub, LANES), 1)
    lane_f = lane.astype(F32)
    row = lax.broadcasted_iota(jnp.int32, (sub, sub), 0)
    col = lax.broadcasted_iota(jnp.int32, (sub, sub), 1)
    earlier = jnp.where(row > col, 1.0, 0.0).astype(BF16)
    pick = lambda sel, val: jnp.sum(jnp.where(sel, val, 0.0), axis=1, keepdims=True)
    carry = carry_s[...]
    for r in range(x_ref.shape[0] // sub):
        rows = slice(r * sub, (r + 1) * sub)
        before_ref[r * SUBLANES:(r + 1) * SUBLANES, :] = jnp.broadcast_to(carry, (SUBLANES, LANES))
        h = _rms(x_ref[rows, :], g_ref[...])
        h_hi = h.astype(BF16)
        h_lo = (h - h_hi.astype(F32)).astype(BF16)
        logits = _dot(h_hi, wr_ref[0]) + (_dot(h_hi, wr_ref[1]) + _dot(h_lo, wr_ref[0]))
        logits = jnp.where(lane < n_exp, logits, -jnp.inf)
        m1 = jnp.max(logits, axis=1, keepdims=True)
        i1 = jnp.min(jnp.where(logits == m1, lane_f, float(LANES)), axis=1, keepdims=True)
        rest = jnp.where(lane_f == i1, -jnp.inf, logits)
        m2 = jnp.max(rest, axis=1, keepdims=True)
        i2 = jnp.min(jnp.where(rest == m2, lane_f, float(LANES)), axis=1, keepdims=True)
        e = jnp.exp(m2 - m1)
        sel1 = lane_f == i1
        sel2 = lane_f == i2
        onehot = jnp.where(sel1 | sel2, 1.0, 0.0)
        local = _dot(earlier, onehot.astype(BF16))
        meta = jnp.zeros((sub, LANES), F32)
        for idx, field in enumerate([i1, i2, pick(sel1, local), pick(sel2, local)]):
            meta = jnp.where(lane == idx, field, meta)
        meta_ref[rows, :] = meta.astype(jnp.int32)
        wts_ref[rows, :] = jnp.where(lane == 0, 1.0 / (1.0 + e), jnp.where(lane == 1, e / (1.0 + e), 0.0))
        carry = carry + jnp.sum(onehot, axis=0, keepdims=True)
    carry_s[...] = carry
    cnt_ref[...] = carry


def route_tokens(x, g, wr, *, tm, sub, n_exp):
    t, d = x.shape
    per_step = tm // sub * SUBLANES
    return pl.pallas_call(
        functools.partial(_router_kernel, n_exp=n_exp, sub=sub),
        grid=(t // tm,),
        in_specs=[
            pl.BlockSpec((tm, d), lambda i: (i, 0)),
            pl.BlockSpec((1, d), lambda i: (0, 0)),
            pl.BlockSpec((2, d, LANES), lambda i: (0, 0, 0)),
        ],
        out_specs=[
            pl.BlockSpec((tm, LANES), lambda i: (i, 0)),
            pl.BlockSpec((tm, LANES), lambda i: (i, 0)),
            pl.BlockSpec((per_step, LANES), lambda i: (i, 0)),
            pl.BlockSpec((1, LANES), lambda i: (0, 0)),
        ],
        out_shape=[
            jax.ShapeDtypeStruct((t, LANES), jnp.int32),
            jax.ShapeDtypeStruct((t, LANES), F32),
            jax.ShapeDtypeStruct((t // tm * per_step, LANES), F32),
            jax.ShapeDtypeStruct((1, LANES), F32),
        ],
        scratch_shapes=[pltpu.VMEM((1, LANES), F32)],
        compiler_params=_params("arbitrary"), name="moe_router",
    )(x, g.reshape(1, d), wr)


def _dispatch_kernel(start_ref, shift_ref, keep_ref, small_ref, x_ref, g_ref, meta_ref, zeros_hbm, xs_hbm,
                     stage_s, carry_s, sems, *, tm, n_exp):
    del zeros_hbm
    j = pl.program_id(0)
    slot = j % 2
    rows = tm + BF16_ROWS

    def block_copy(step, e, buf, n):
        first = pl.multiple_of(start_ref[step * n_exp + e], BF16_ROWS)
        return pltpu.make_async_copy(stage_s.at[buf, e, pl.ds(0, n)], xs_hbm.at[pl.ds(first, n)], sems.at[buf])

    def blocks(step, buf, act):
        for e in range(n_exp):
            small = small_ref[step * n_exp + e] != 0

            @pl.when(small)
            def _(e=e):
                act(block_copy(step, e, buf, SMALL_BLOCK_ROWS + BF16_ROWS))

            @pl.when(jnp.logical_not(small))
            def _(e=e):
                act(block_copy(step, e, buf, rows))

    @pl.when(j == 0)
    def _():
        carry_s[...] = jnp.zeros_like(carry_s)

    h = _rms(x_ref[...], g_ref[...]).astype(BF16)
    fields = meta_ref[...].astype(F32).T
    slot_row = lax.broadcasted_iota(jnp.int32, (rows, tm), 0).astype(F32)
    for e in range(n_exp):
        key = j * n_exp + e
        idx = jnp.where(fields[0:1] == e, fields[2:3], jnp.where(fields[1:2] == e, fields[3:4], -2.0 * rows))
        idx = idx + shift_ref[key].astype(F32)
        onehot = jnp.where(slot_row == idx, 1.0, 0.0).astype(BF16)
        stage_s[slot, e] = _dot(onehot, h).astype(BF16)
        stage_s[slot, e, 0:BF16_ROWS, :] += carry_s[e]
        keep = pl.multiple_of(keep_ref[key], BF16_ROWS)
        carry_s[e] = stage_s[slot, e, pl.ds(keep, BF16_ROWS), :]

    @pl.when(j > 0)
    def _():
        blocks(j - 1, 1 - slot, lambda copy: copy.wait())

    blocks(j, slot, lambda copy: copy.start())

    @pl.when(j == pl.num_programs(0) - 1)
    def _():
        blocks(j, slot, lambda copy: copy.wait())


def moe_dispatch(starts, shifts, keeps, smalls, x, g, meta, zeros, *, tm, n_exp):
    t, d = x.shape
    n_rows = zeros.shape[0]
    rows = tm + BF16_ROWS
    return pl.pallas_call(
        functools.partial(_dispatch_kernel, tm=tm, n_exp=n_exp),
        grid_spec=pltpu.PrefetchScalarGridSpec(
            num_scalar_prefetch=4, grid=(t // tm,),
            in_specs=[
                pl.BlockSpec((tm, d), lambda i, *_: (i, 0)),
                pl.BlockSpec((1, d), lambda i, *_: (0, 0)),
                pl.BlockSpec((tm, LANES), lambda i, *_: (i, 0)),
                pl.BlockSpec(memory_space=pl.ANY),
            ],
            out_specs=pl.BlockSpec(memory_space=pl.ANY),
            scratch_shapes=[pltpu.VMEM((2, n_exp, rows, d), BF16), pltpu.VMEM((n_exp, BF16_ROWS, d), BF16),
                            pltpu.SemaphoreType.DMA((2,))],
        ),
        out_shape=jax.ShapeDtypeStruct((n_rows, d), BF16),
        input_output_aliases={7: 0},
        compiler_params=_params("arbitrary"),
        name="moe_dispatch",
    )(starts, shifts, keeps, smalls, x, g.reshape(1, d), meta, zeros)


def _expert_kernel(te_ref, used_ref, h_ref, w13_ref, w2_ref, o_ref, gu_s, acc_s):
    del te_ref
    used = used_ref[pl.program_id(0)] != 0

    @pl.when(used)
    def _():
        acc_s[...] = jnp.zeros_like(acc_s)
        _swiglu_chunks(h_ref, w13_ref.at[0], w2_ref.at[0], acc_s, gu_s)
        o_ref[...] = acc_s[...].astype(o_ref.dtype)

    @pl.when(jnp.logical_not(used))
    def _():
        o_ref[...] = jnp.zeros_like(o_ref)


def moe_experts(tile_expert, tile_used, hs, w13, w2, *, tm):
    n_rows, d = hs.shape
    tf = w2.shape[2]
    resident = pl.Buffered(1)
    return pl.pallas_call(
        _expert_kernel,
        grid_spec=pltpu.PrefetchScalarGridSpec(
            num_scalar_prefetch=2, grid=(n_rows // tm,),
            in_specs=[
                pl.BlockSpec((tm, d), lambda i, te, tu: (i, 0)),
                pl.BlockSpec((1,) + w13.shape[1:], lambda i, te, tu: (te[i], 0, 0)),
                pl.BlockSpec((1,) + w2.shape[1:], lambda i, te, tu: (te[i], 0, 0, 0)),
            ],
            out_specs=pl.BlockSpec((tm, d), lambda i, te, tu: (i, 0)),
            scratch_shapes=[pltpu.VMEM((2, 2, tm, tf), F32), pltpu.VMEM((tm, d), F32)],
        ),
        out_shape=jax.ShapeDtypeStruct((n_rows, d), BF16),
        compiler_params=pltpu.CompilerParams(dimension_semantics=("arbitrary",), vmem_limit_bytes=EXPERT_VMEM_LIMIT),
        name="moe_experts",
    )(tile_expert, tile_used, hs, w13, w2)


def _combine_kernel(start_ref, shift_ref, tail_ref, small_ref, x_ref, wts_ref, meta_ref, g_ref, y_hbm, o_ref,
                    blk_s, sems, *, tm, n_exp):
    j = pl.program_id(0)
    slot = j % 2
    rows = tm + BF16_ROWS

    def block_copy(step, e, buf, n):
        first = pl.multiple_of(start_ref[step * n_exp + e], BF16_ROWS)
        return pltpu.make_async_copy(y_hbm.at[pl.ds(first, n)], blk_s.at[buf, e, pl.ds(0, n)], sems.at[buf])

    def blocks(step, buf, act):
        for e in range(n_exp):
            small = small_ref[step * n_exp + e] != 0

            @pl.when(small)
            def _(e=e):
                act(block_copy(step, e, buf, SMALL_BLOCK_ROWS))

            @pl.when(jnp.logical_not(small))
            def _(e=e):
                act(block_copy(step, e, buf, rows))

    @pl.when(j == 0)
    def _():
        blk_s[...] = jnp.zeros_like(blk_s)
        blocks(0, 0, lambda copy: copy.start())

    @pl.when(j + 1 < pl.num_programs(0))
    def _():
        blocks(j + 1, 1 - slot, lambda copy: copy.start())

    blocks(j, slot, lambda copy: copy.wait())

    meta = meta_ref[...]
    wts = wts_ref[...]
    col = lax.broadcasted_iota(jnp.int32, (tm, tm), 1)
    tail_col = lax.broadcasted_iota(jnp.int32, (tm, BF16_ROWS), 1) + tm

    def pick(e):
        sel = [meta[:, k:k + 1] == e for k in range(TOP_K)]
        idx = jnp.where(sel[0], meta[:, 2:3], jnp.where(sel[1], meta[:, 3:4], -2 * rows)) + shift_ref[j * n_exp + e]
        w = jnp.where(sel[0], wts[:, 0:1], jnp.where(sel[1], wts[:, 1:2], 0.0))
        return idx, w

    acc = x_ref[...]
    for e in range(n_exp):
        idx, w = pick(e)
        onehot = jnp.where(col == idx, 1.0, 0.0).astype(BF16)
        acc = acc + w * _dot(onehot, blk_s[slot, e, 0:tm, :])
    o_ref[...] = acc

    for e in range(n_exp):
        @pl.when(tail_ref[j * n_exp + e] != 0)
        def _(e=e):
            idx, w = pick(e)
            onehot_tail = jnp.where(tail_col == idx, 1.0, 0.0).astype(BF16)
            o_ref[...] += w * _dot(onehot_tail, blk_s[slot, e, tm:rows, :])

    o_ref[...] = _rms(o_ref[...], g_ref[...])


def moe_combine(starts, shifts, tails, smalls, x, wts, meta, g, y, *, tm, n_exp):
    t, d = x.shape
    n_pre = 4
    return pl.pallas_call(
        functools.partial(_combine_kernel, tm=tm, n_exp=n_exp),
        grid_spec=pltpu.PrefetchScalarGridSpec(
            num_scalar_prefetch=n_pre, grid=(t // tm,),
            in_specs=[
                pl.BlockSpec((tm, d), lambda i, *_: (i, 0)),
                pl.BlockSpec((tm, LANES), lambda i, *_: (i, 0)),
                pl.BlockSpec((tm, LANES), lambda i, *_: (i, 0)),
                pl.BlockSpec((1, d), lambda i, *_: (0, 0)),
                pl.BlockSpec(memory_space=pl.ANY),
            ],
            out_specs=pl.BlockSpec((tm, d), lambda i, *_: (i, 0)),
            scratch_shapes=[pltpu.VMEM((2, n_exp, tm + BF16_ROWS, d), BF16), pltpu.SemaphoreType.DMA((2,))],
        ),
        out_shape=jax.ShapeDtypeStruct((t, d), F32),
        compiler_params=_params("arbitrary"), name="moe_combine",
    )(starts, shifts, tails, smalls, x, wts, meta, g.reshape(1, d), y)


def _pad_cols(w, n):
    return jnp.pad(w, ((0, 0), (0, n - w.shape[1])))


def _even_mixer(x, norm, w_in, b_f, sgu_norm, w_s, b_s, w_ride, proj_rides, *, batch, seq, tq):
    n_heads = b_f.shape[0]
    a_width = n_heads * HEAD_DIM
    b_width = w_s.shape[0] * LANES
    f0 = 3 * a_width
    w_main = jnp.concatenate([w_in[:, :f0], w_in[:, f0 + n_heads:]], axis=1)
    w_gate = _pad_cols(w_in[:, f0:f0 + n_heads], LANES)
    main, gate, *proj_casts = norm_matmul(x, norm, [w_main.astype(BF16), w_gate.astype(BF16)], [BF16, F32],
                                          tm=min(IN_PROJ_ROWS, x.shape[0]), name="even_in_proj",
                                          scaled=(0, a_width, LOG2E * HEAD_DIM ** -0.5), rides=proj_rides)
    c = gate_cumsum(gate, _pad_cols(b_f.reshape(1, -1), LANES), seq=seq)
    n_pairs = a_width // LANES
    a, w_cast = fox_attention(main, c, w_ride, batch=batch, seq=seq, tq=tq, q_col=0, k_col=n_pairs,
                              v_col=2 * n_pairs, n_pairs=n_pairs)
    u_col = f0 // b_width
    return (a, (main, u_col, u_col + 1, sgu_norm, w_s, b_s)), w_cast, proj_casts


def _odd_mixer(x, norm, w_in, conv_w, lq1, lk1, lq2, lk2, subln, rel_bias, lam_init, w_ride, *, batch, seq, tq):
    c_width = conv_w.shape[1]
    d_width = rel_bias.shape[1] * 2 * HEAD_DIM
    q0 = 3 * c_width
    w_in = w_in.astype(BF16)
    c_out, main = conv_proj(x, norm, w_in[:, :q0], w_in[:, q0:], conv_w, tm=min(IN_PROJ_ROWS, seq), seq=seq,
                            scaled=(0, d_width, LOG2E * HEAD_DIM ** -0.5))
    bias, far, lam = diff_prep(rel_bias, lq1, lk1, lq2, lk2, tq=tq, lam_init=lam_init)
    n_heads = rel_bias.shape[1]
    d_out, w_cast = diff_attention(main, bias, far, lam, subln, w_ride, batch=batch, seq=seq, tq=tq, q_col=0,
                                   k_col=n_heads, v_col=2 * n_heads, lam_init=lam_init)
    return (c_out, d_out), w_cast


def _sorted_rows(t, n_exp):
    block_rows = ROUTE_ROWS + BF16_ROWS
    return TOP_K * t + n_exp * (EXPERT_ROWS + pl.cdiv(block_rows, EXPERT_ROWS) * EXPERT_ROWS)


def _moe_layer(x, norm, w_router, w13, w2, final_norm, zeros):
    t, d = x.shape
    n_exp = w_router.shape[1]
    tm_route = ROUTE_ROWS
    tm_expert = EXPERT_ROWS
    n_rows = zeros.shape[0]
    assert n_rows == _sorted_rows(t, n_exp)
    wr = _pad_cols(w_router, LANES)
    wr_hi = wr.astype(BF16)
    wr_split = jnp.stack([wr_hi, (wr - wr_hi.astype(F32)).astype(BF16)])
    meta, wts, before, counts = route_tokens(x, norm, wr_split, tm=ROUTE_STEP_ROWS, sub=tm_route, n_exp=n_exp)
    counts = counts[0, :n_exp].astype(jnp.int32)
    block_rows = tm_route + BF16_ROWS
    padded = (counts + block_rows + tm_expert - 1) // tm_expert * tm_expert
    ends = jnp.cumsum(padded)
    offsets = ends - padded
    tile_start = jnp.arange(n_rows // tm_expert, dtype=jnp.int32) * tm_expert
    tile_expert = jnp.minimum(jnp.sum(tile_start[:, None] >= ends[None, :], axis=1), n_exp - 1).astype(jnp.int32)
    tile_used = (tile_start < (offsets + counts)[tile_expert]).astype(jnp.int32)
    before = before[::SUBLANES, :n_exp].astype(jnp.int32)
    in_tile = jnp.concatenate([before[1:], counts[None]]) - before
    first = offsets[None, :] + before
    starts = first // BF16_ROWS * BF16_ROWS
    shifts = first - starts
    keeps = (shifts + in_tile) // BF16_ROWS * BF16_ROWS
    tails = (shifts + in_tile > tm_route).astype(jnp.int32)
    flat = lambda a: a.reshape(-1).astype(jnp.int32)
    smalls = (shifts + in_tile <= SMALL_BLOCK_ROWS).astype(jnp.int32)
    hs = moe_dispatch(flat(starts), flat(shifts), flat(keeps), flat(smalls), x, norm, meta, zeros, tm=tm_route,
                      n_exp=n_exp)
    y = moe_experts(tile_expert, tile_used, hs, *_chunk_weights(w13, w2, EXPERT_CHUNK), tm=tm_expert)
    return moe_combine(flat(starts), flat(shifts), flat(tails), flat(smalls), x, wts, meta, final_norm, y,
                       tm=tm_route, n_exp=n_exp)


def kernel(x, mem, rel_bias, mem_norm, final_norm, ev_norm, ev_w_in, ev_b_f, ev_sgu_norm, ev_w_s, ev_b_s, ev_w_out, ffn_w13, ffn_w2, od_norm, od_w_in, od_conv_w, od_lam_q1, od_lam_k1, od_lam_q2, od_lam_k2, od_subln, od_w_out, moe_router, moe_w13, moe_w2, x_norm, x_wq, x_wkv, x_wo, ffn_norm):
    batch, seq, d = x.shape
    mem_len = mem.shape[1]
    depth = x_norm.shape[0]
    assert depth == 2 and ev_norm.shape[0] == 1 and od_norm.shape[0] == 1
    x_heads, x_dh = 4, 128
    xf = x.reshape(batch * seq, d)
    wkv = jnp.concatenate([x_wkv[layer] for layer in range(depth)], axis=1).astype(BF16)
    (kv,) = norm_matmul(mem.reshape(batch * mem_len, d), mem_norm, [wkv], [BF16], tm=MEM_ROWS, name="mem_kv")

    def tail(xf, mixed, w_out, layer, ffn, name, zero_rows=0):
        return layer_tail(xf, *mixed, w_out.astype(BF16), x_norm[layer], x_wq[layer].astype(BF16), kv,
                          x_wo[layer].astype(BF16), ffn, zero_rows, tm=TAIL_ROWS, seq=seq, mem_len=mem_len, kv_col=layer,
                          n_heads=x_heads, dh=x_dh, name=name)

    e13, e2 = moe_w13[0], moe_w2[0]
    mixed, e13_bf16, ffn_bf16 = _even_mixer(xf, ev_norm[0], ev_w_in[0], ev_b_f[0], ev_sgu_norm[0], ev_w_s[0],
                                            ev_b_s[0], e13.reshape(-1, e13.shape[-1]), (ffn_w13[0], ffn_w2[0]),
                                            batch=batch, seq=seq, tq=ATTN_TILE)
    xf = tail(xf, mixed, ev_w_out[0], 0, (ffn_norm[0],) + _chunk_weights(*ffn_bf16, FFN_CHUNK), "even_tail")
    lam_init = 0.8 - 0.6 * math.exp(-0.3 * 1)
    mixed, e2_bf16 = _odd_mixer(xf, od_norm[0], od_w_in[0], od_conv_w[0], od_lam_q1[0], od_lam_k1[0],
                                od_lam_q2[0], od_lam_k2[0], od_subln[0], rel_bias, lam_init,
                                e2.reshape(-1, e2.shape[-1]), batch=batch, seq=seq, tq=ATTN_TILE)
    xf, zeros = tail(xf, mixed, od_w_out[0], 1, None, "odd_tail", _sorted_rows(xf.shape[0], moe_router.shape[-1]))
    out = _moe_layer(xf, ffn_norm[1], moe_router[0], e13_bf16.reshape(e13.shape), e2_bf16.reshape(e2.shape),
                     final_norm, zeros)
    return out.reshape(batch, seq, d)
```

```python
import functools
import math

import jax
import jax.numpy as jnp
from jax import lax
from jax.experimental import pallas as pl
from jax.experimental.pallas import tpu as pltpu

F32 = jnp.float32
BF16 = jnp.bfloat16
EPS = 1e-6
HEAD_DIM = 64
LANES = 128
SUBLANES = 8
BF16_ROWS = 16
N_BUCKETS = 32
MAX_DIST = 128
TOP_K = 2
LOG2E = 1.4426950408889634
ATTN_TILE = 512
IN_PROJ_ROWS = 1024
MEM_ROWS = 512
TAIL_ROWS = 512
ROUTE_ROWS = 256
ROUTE_STEP_ROWS = 512
EXPERT_ROWS = 512
FFN_CHUNK = 256
EXPERT_CHUNK = 512
VMEM_BYTES = 64 * 1024 * 1024
VMEM_LIMIT = VMEM_BYTES - 8 * 1024 * 1024
EXPERT_VMEM_LIMIT = VMEM_BYTES - 2 * 1024 * 1024


def _params(*sem):
    return pltpu.CompilerParams(dimension_semantics=sem, vmem_limit_bytes=VMEM_LIMIT)


def _rms(x, g):
    ms = jnp.mean(x * x, axis=-1, keepdims=True)
    return x * lax.rsqrt(ms + EPS) * g


def _dot(a, b):
    return jnp.dot(a, b, preferred_element_type=F32)


def _dot_nt(a, b):
    return lax.dot_general(a, b, (((1,), (1,)), ((), ())), preferred_element_type=F32)


def _norm_matmul_kernel(x_ref, g_ref, *refs, n_w, n_ride, chunk, scaled):
    w_refs, ride_in, o_refs = refs[:n_w], refs[n_w:n_w + n_ride], refs[n_w + n_ride:2 * n_w + n_ride]
    for src, dst in zip(ride_in, refs[2 * n_w + n_ride:]):
        dst[...] = src[...].astype(BF16)
    h = _rms(x_ref[...], g_ref[...]).astype(BF16)
    s0, s1, scale = scaled
    for k, (w_ref, o_ref) in enumerate(zip(w_refs, o_refs)):
        n = w_ref.shape[1]
        for c0 in range(0, n, chunk):
            c1 = min(c0 + chunk, n)
            y = _dot(h, w_ref[:, c0:c1])
            if k == 0 and s0 <= c0 and c1 <= s1:
                y = y * scale
            o_ref[:, c0:c1] = y.astype(o_ref.dtype)


def norm_matmul(x, g, ws, out_dtypes, *, tm, name, scaled=(0, 0, 1.0), rides=()):
    t, d = x.shape
    chunk = 512
    assert scaled[0] % chunk == 0 and scaled[1] % chunk == 0
    grid = (t // tm,)
    ride_specs = [_rider(w, grid) for w in rides]
    in_specs = [pl.BlockSpec((tm, d), lambda i: (i, 0)), pl.BlockSpec((1, d), lambda i: (0, 0))]
    in_specs += [pl.BlockSpec(w.shape, lambda i: (0, 0)) for w in ws] + [spec for spec, _ in ride_specs]
    out_specs = [pl.BlockSpec((tm, w.shape[1]), lambda i: (i, 0)) for w in ws] + [spec for spec, _ in ride_specs]
    out_shape = [jax.ShapeDtypeStruct((t, w.shape[1]), dt) for w, dt in zip(ws, out_dtypes)]
    out_shape += [shape for _, shape in ride_specs]
    return pl.pallas_call(
        functools.partial(_norm_matmul_kernel, n_w=len(ws), n_ride=len(rides), chunk=chunk, scaled=scaled),
        grid=grid, in_specs=in_specs, out_specs=out_specs, out_shape=out_shape,
        compiler_params=_params("parallel"), name=name,
    )(x, g.reshape(1, d), *ws, *rides)


def _gate_kernel(g_ref, b_ref, c_ref):
    s = g_ref.shape[0]
    row = lax.broadcasted_iota(jnp.int32, (LANES, LANES), 0)
    col = lax.broadcasted_iota(jnp.int32, (LANES, LANES), 1)
    tri = (row >= col).astype(F32)
    carry = jnp.zeros((1, LANES), F32)
    for blk in range(s // LANES):
        z = g_ref[blk * LANES:(blk + 1) * LANES, :] + b_ref[...]
        log_f = jnp.minimum(z, 0.0) - jnp.log1p(jnp.exp(-jnp.abs(z)))
        cs = jnp.dot(tri, log_f, precision=lax.Precision.HIGHEST, preferred_element_type=F32) + carry
        c_ref[blk * LANES:(blk + 1) * LANES, :] = cs
        carry = cs[LANES - 1:LANES, :]


def gate_cumsum(g, b, *, seq):
    t = g.shape[0]
    return pl.pallas_call(
        _gate_kernel, grid=(t // seq,),
        in_specs=[pl.BlockSpec((seq, LANES), lambda i: (i, 0)), pl.BlockSpec((1, LANES), lambda i: (0, 0))],
        out_specs=pl.BlockSpec((seq, LANES), lambda i: (i, 0)),
        out_shape=jax.ShapeDtypeStruct((t, LANES), F32),
        compiler_params=_params("parallel"), name="gate_cumsum",
    )(g, b)


def _split3(x):
    hi = x.astype(BF16).astype(F32)
    rest = x - hi
    mid = rest.astype(BF16).astype(F32)
    lo = (rest - mid).astype(BF16).astype(F32)
    return hi, mid, lo


def _augment(x, in_half, lane, base, pieces, pieces_first):
    n = len(pieces)
    p0, o0 = (base, base + n) if pieces_first else (base + n, base)
    aug = jnp.where((lane >= o0) & (lane < o0 + n), 1.0, 0.0)
    for idx, piece in enumerate(pieces):
        aug = jnp.where(lane == p0 + idx, piece, aug)
    return jnp.where(in_half, x, aug.astype(x.dtype))


def _halves(lane):
    return [(lane >= HEAD_DIM * hh) & (lane < HEAD_DIM * (hh + 1)) for hh in range(2)]


def _causal_attention(qa, ka_s, v_ref, bias_ref, i, tq):
    n_chunks = tq // LANES
    row = lax.broadcasted_iota(jnp.int32, (tq, tq), 0)
    col = lax.broadcasted_iota(jnp.int32, (tq, tq), 1)

    def scores(j):
        s = _dot_nt(qa, ka_s[j * tq:(j + 1) * tq, :])
        if bias_ref is not None and j >= i - 1:
            s = s + bias_ref[0, i - j]
        if j == i:
            s = jnp.where(row >= col, s, -jnp.inf)
        return [s[:, c * LANES:(c + 1) * LANES] for c in range(n_chunks)]

    m = jnp.full((tq, LANES), -jnp.inf, F32)
    for j in range(i + 1):
        for chunk in scores(j):
            m = jnp.maximum(m, chunk)
    m = jnp.broadcast_to(jnp.max(m, axis=1, keepdims=True), (tq, LANES))
    l = jnp.zeros((tq, LANES), F32)
    acc = jnp.zeros((tq, LANES), F32)
    for j in range(i + 1):
        ps = [jnp.exp2(chunk - m) for chunk in scores(j)]
        l = l + functools.reduce(lambda a, b: a + b, ps)
        acc = acc + _dot(jnp.concatenate(ps, axis=1).astype(BF16), v_ref[j * tq:(j + 1) * tq, :])
    return acc / jnp.sum(l, axis=1, keepdims=True)


def _rider(w, grid):
    steps = math.prod(grid)
    rows = w.shape[0] // steps
    assert rows * steps == w.shape[0] and rows % BF16_ROWS == 0

    def index(*ids):
        step = 0
        for n, idx in zip(grid, ids):
            step = step * n + idx
        return step, 0

    return pl.BlockSpec((rows, w.shape[1]), index), jax.ShapeDtypeStruct(w.shape, BF16)


def _fox_kernel(q_ref, k_ref, v_ref, c_ref, w_ref, o_ref, wcast_ref, ka_s, *, tq):
    wcast_ref[...] = w_ref[...].astype(BF16)
    hp = pl.program_id(1)
    lane = lax.broadcasted_iota(jnp.int32, (1, LANES), 1)
    halves = _halves(lane)

    def decay(c, hh):
        return jnp.sum(jnp.where(lane == 2 * hp + hh, c, 0.0), axis=1, keepdims=True) * LOG2E

    k = k_ref[...]
    c_all = c_ref[...]
    for hh in range(2):
        ka_s[hh] = _augment(k, halves[hh], lane, HEAD_DIM * (1 - hh), _split3(-decay(c_all, hh)), True)

    for i in range(q_ref.shape[0] // tq):
        rows = slice(i * tq, (i + 1) * tq)
        outs = []
        for hh in range(2):
            qa = _augment(q_ref[rows, :], halves[hh], lane, HEAD_DIM * (1 - hh), _split3(decay(c_ref[rows, :], hh)),
                          False)
            outs.append(_causal_attention(qa, ka_s.at[hh], v_ref, None, i, tq))
        o_ref[rows, :] = jnp.where(lane < HEAD_DIM, outs[0], outs[1]).astype(o_ref.dtype)


def fox_attention(qkv, c, w_ride, *, batch, seq, tq, q_col, k_col, v_col, n_pairs):
    t = batch * seq
    grid = (batch, n_pairs)
    ride_spec, ride_shape = _rider(w_ride, grid)
    block = lambda col: pl.BlockSpec((seq, LANES), lambda b, h: (b, col + h))
    return pl.pallas_call(
        functools.partial(_fox_kernel, tq=tq),
        grid=grid,
        in_specs=[block(q_col), block(k_col), block(v_col), pl.BlockSpec((seq, LANES), lambda b, h: (b, 0)),
                  ride_spec],
        out_specs=[block(0), ride_spec],
        out_shape=[jax.ShapeDtypeStruct((t, n_pairs * LANES), BF16), ride_shape],
        scratch_shapes=[pltpu.VMEM((2, seq, LANES), BF16)],
        compiler_params=_params("parallel", "parallel"), name="fox_attention",
    )(qkv, qkv, qkv, c, w_ride)


def _spatial_gate(u_ref, v_ref, norm_ref, ws_ref, bs_ref, o_ref):
    tb = u_ref.shape[0]
    n_groups, chunk, _ = ws_ref.shape
    row = lax.broadcasted_iota(jnp.int32, (chunk, chunk), 0)
    col = lax.broadcasted_iota(jnp.int32, (chunk, chunk), 1)
    tri = row >= col
    for g in range(n_groups):
        w = jnp.where(tri, ws_ref[g], 0.0).astype(BF16)
        bias = bs_ref[:, g:g + 1]
        gain = norm_ref[g:g + 1, :]
        for c in range(tb // chunk):
            rs = slice(c * chunk, (c + 1) * chunk)
            cs = slice(g * LANES, (g + 1) * LANES)
            vn = _rms(jax.nn.gelu(v_ref[rs, cs].astype(F32)), gain)
            mixed = _dot(w, vn.astype(BF16)) + bias
            o_ref[rs, cs] = (jax.nn.gelu(u_ref[rs, cs].astype(F32)) * mixed).astype(o_ref.dtype)


def _chunk_weights(w13, w2, tf):
    *lead, ff, d = w2.shape
    return w13.astype(BF16), w2.astype(BF16).reshape(*lead, ff // tf, tf, d)


def _swiglu_chunks(h_s, w13, w2, o_ref, gu_s):
    n, tf, _ = w2.shape

    def project(c, slot):
        for part in range(2):
            cols = pl.ds(pl.multiple_of((part * n + c) * tf, tf), tf)
            gu_s[slot, part] = _dot(h_s[...], w13[:, cols])

    def consume(c, slot):
        gate = gu_s[slot, 0]
        act = (gate * jax.nn.sigmoid(gate) * gu_s[slot, 1]).astype(BF16)
        o_ref[...] += _dot(act, w2[c])

    def pair(k, carry):
        c = 2 * k
        project(c + 1, 1)
        consume(c, 0)
        project(c + 2, 0)
        consume(c + 1, 1)
        return carry

    project(0, 0)
    lax.fori_loop(0, (n - 1) // 2, pair, 0)
    if n % 2 == 0:
        project(n - 1, 1)
        consume(n - 2, 0)
        consume(n - 1, 1)
    else:
        consume(n - 1, 0)


def _swiglu_scratch(tm, d, tf):
    return [pltpu.VMEM((tm, d), BF16), pltpu.VMEM((2, 2, tm, tf), F32)]


def _tail_kernel(*refs, n_heads, dh, gated, ffn, zero_fill):
    refs = list(refs)
    take = lambda n: [refs.pop(0) for _ in range(n)]
    x_ref, a_ref = take(2)
    b_in = take(5 if gated else 1)
    wa_ref, wb_ref, gx_ref, wq_ref, kv_ref, wo_ref = take(6)
    ffn_in = take(3 if ffn else 0)
    (o_ref,) = take(1)
    for z_ref in take(1 if zero_fill else 0):
        z_ref[...] = jnp.zeros_like(z_ref)
    if gated:
        (b_ref,) = take(1)
        _spatial_gate(*b_in, b_ref)
    else:
        (b_ref,) = b_in
    rest = ffn_in + [o_ref] + refs
    x = x_ref[...] + _dot(a_ref[...], wa_ref[...]) + _dot(b_ref[...], wb_ref[...])
    q = _dot(_rms(x, gx_ref[...]).astype(BF16), wq_ref[...]).astype(BF16)
    width = n_heads * dh
    outs = []
    for hd in range(n_heads):
        cs = slice(hd * dh, (hd + 1) * dh)
        s = _dot_nt(q[:, cs], kv_ref[:, cs]) * (dh ** -0.5)
        p = jnp.exp(s - jnp.max(s, axis=1, keepdims=True))
        p = p / jnp.sum(p, axis=1, keepdims=True)
        outs.append(_dot(p.astype(BF16), kv_ref[:, width + hd * dh:width + (hd + 1) * dh]).astype(BF16))
    x = x + _dot(jnp.concatenate(outs, axis=1), wo_ref[...])
    if len(rest) == 1:
        (o_ref,) = rest
        o_ref[...] = x
    else:
        gf_ref, w13_ref, w2_ref, o_ref, h_s, gu_s = rest
        h_s[...] = _rms(x, gf_ref[...]).astype(BF16)
        o_ref[...] = x
        _swiglu_chunks(h_s, w13_ref, w2_ref, o_ref, gu_s)


def layer_tail(x, a, b, w_out, gx, wq, kv, wo, ffn=None, zero_rows=0, *, tm, seq, mem_len, kv_col, n_heads, dh,
               name):
    t, d = x.shape
    per_b = seq // tm
    resident = pl.Buffered(1)
    const = lambda arr: pl.BlockSpec(arr.shape, lambda i: (0,) * arr.ndim, pipeline_mode=resident)
    rows = lambda arr: pl.BlockSpec((tm, arr.shape[1]), lambda i: (i, 0))
    vec = lambda g: g.reshape(1, d)
    gated = isinstance(b, tuple)
    scratch = []
    if gated:
        proj, u_col, v_col, sgu_norm, w_s, b_s = b
        b_width = w_s.shape[0] * LANES
        b_args = [proj, proj, sgu_norm, w_s, b_s.T]
        b_specs = [pl.BlockSpec((tm, b_width), lambda i: (i, u_col)), pl.BlockSpec((tm, b_width), lambda i: (i, v_col)),
                   const(sgu_norm), const(w_s), const(b_s.T)]
        scratch.append(pltpu.VMEM((tm, b_width), BF16))
    else:
        b_width = b.shape[1]
        b_args, b_specs = [b], [rows(b)]
    wa, wb = w_out[:a.shape[1]], w_out[a.shape[1]:]
    assert wb.shape[0] == b_width
    args = [x, a] + b_args + [wa, wb, vec(gx), wq, kv, wo]
    in_specs = [rows(x), rows(a)] + b_specs + [
        const(wa), const(wb), const(vec(gx)), const(wq),
        pl.BlockSpec((mem_len, 2 * n_heads * dh), lambda i: (i // per_b, kv_col)), const(wo)]
    if ffn is not None:
        gf, w13, w2 = ffn
        args += [vec(gf), w13, w2]
        in_specs += [const(vec(gf)), const(w13), const(w2)]
        scratch += _swiglu_scratch(tm, d, w2.shape[1])
    out_specs, out_shape = [rows(x)], [jax.ShapeDtypeStruct((t, d), F32)]
    if zero_rows:
        per_step = zero_rows // (t // tm)
        assert per_step * (t // tm) == zero_rows and per_step % BF16_ROWS == 0
        out_specs.append(pl.BlockSpec((per_step, d), lambda i: (i, 0)))
        out_shape.append(jax.ShapeDtypeStruct((zero_rows, d), BF16))
    outs = pl.pallas_call(
        functools.partial(_tail_kernel, n_heads=n_heads, dh=dh, gated=gated, ffn=ffn is not None,
                          zero_fill=bool(zero_rows)),
        grid=(t // tm,), in_specs=in_specs, out_specs=out_specs, out_shape=out_shape,
        scratch_shapes=scratch, compiler_params=_params("parallel"), name=name,
    )(*args)
    return outs if zero_rows else outs[0]


def _conv_proj_kernel(x_ref, g_ref, wc_ref, wa_ref, cw_ref, c_ref, o_ref, carry_s, *, steps_per_seq, scaled, chunk):
    h = _rms(x_ref[...], g_ref[...]).astype(BF16)
    tm, width = c_ref.shape
    n_taps = cw_ref.shape[0]
    bg, cg, xi = [_dot(h, wc_ref[:, k * width:(k + 1) * width]) for k in range(3)]
    xc = cg * xi
    first = pl.program_id(0) % steps_per_seq == 0
    prev = jnp.where(first, 0.0, carry_s[...])
    row = lax.broadcasted_iota(jnp.int32, (tm, width), 0)
    y = cw_ref[n_taps - 1:n_taps, :] * xc
    for back in range(1, n_taps):
        shifted = pltpu.roll(xc, back, axis=0)
        for r in range(back):
            src = SUBLANES - back + r
            shifted = jnp.where(row == r, prev[src:src + 1, :], shifted)
        y = y + cw_ref[n_taps - 1 - back:n_taps - back, :] * shifted
    c_ref[...] = (bg * y).astype(c_ref.dtype)
    carry_s[...] = xc[tm - SUBLANES:tm, :]
    s0, s1, scale = scaled
    for c0 in range(0, wa_ref.shape[1], chunk):
        y = _dot(h, wa_ref[:, c0:c0 + chunk])
        if s0 <= c0 and c0 + chunk <= s1:
            y = y * scale
        o_ref[:, c0:c0 + chunk] = y.astype(o_ref.dtype)


def conv_proj(x, g, w_conv, w_attn, conv_w, *, tm, seq, scaled):
    t, d = x.shape
    width = conv_w.shape[1]
    chunk = 512
    assert seq % tm == 0 and w_attn.shape[1] % chunk == 0 and conv_w.shape[0] - 1 <= SUBLANES
    const = lambda arr: pl.BlockSpec(arr.shape, lambda i: (0, 0))
    return pl.pallas_call(
        functools.partial(_conv_proj_kernel, steps_per_seq=seq // tm, scaled=scaled, chunk=chunk),
        grid=(t // tm,),
        in_specs=[pl.BlockSpec((tm, d), lambda i: (i, 0)), const(g.reshape(1, d)), const(w_conv), const(w_attn),
                  const(conv_w)],
        out_specs=[pl.BlockSpec((tm, width), lambda i: (i, 0)), pl.BlockSpec((tm, w_attn.shape[1]), lambda i: (i, 0))],
        out_shape=[jax.ShapeDtypeStruct((t, width), BF16), jax.ShapeDtypeStruct((t, w_attn.shape[1]), BF16)],
        scratch_shapes=[pltpu.VMEM((SUBLANES, width), F32)],
        compiler_params=_params("arbitrary"), name="odd_in_proj",
    )(x, g.reshape(1, d), w_conv, w_attn, conv_w)


def _diff_prep_kernel(rb_ref, lq1_ref, lk1_ref, lq2_ref, lk2_ref, bias_ref, far_ref, lam_ref, *, tq, lam_init):
    n_heads = bias_ref.shape[0]
    strip = 32
    row = lax.broadcasted_iota(jnp.int32, (strip, tq), 0)
    col = lax.broadcasted_iota(jnp.int32, (strip, tq), 1)
    max_exact = N_BUCKETS // 2

    def fill(r, carry):
        r0 = pl.multiple_of(r * strip, strip)
        for which in range(2):
            n = jnp.maximum(row + r0 - col + which * tq, 0)
            nf = jnp.maximum(n, 1).astype(F32)
            large = max_exact + (jnp.log(nf / max_exact) / math.log(MAX_DIST / max_exact)
                                 * (N_BUCKETS - max_exact)).astype(jnp.int32)
            large = jnp.minimum(large, N_BUCKETS - 1)
            bucket = jnp.where(n < max_exact, n, large)
            for h in range(n_heads):
                b = jnp.zeros((strip, tq), F32)
                for kk in range(N_BUCKETS):
                    b = jnp.where(bucket == kk, rb_ref[kk, h], b)
                bias_ref[h, which, pl.ds(r0, strip), :] = (b - rb_ref[N_BUCKETS - 1, h]) * LOG2E
        return carry

    lax.fori_loop(0, tq // strip, fill, 0)
    for h in range(n_heads):
        far_ref[h] = jnp.full((1, LANES), rb_ref[N_BUCKETS - 1, h], F32) * LOG2E
    lam = (jnp.exp(jnp.sum(lq1_ref[...] * lk1_ref[...], axis=1, keepdims=True))
           - jnp.exp(jnp.sum(lq2_ref[...] * lk2_ref[...], axis=1, keepdims=True)) + lam_init)
    lam_ref[...] = jnp.broadcast_to(lam, (1, LANES))


def diff_prep(rel_bias, lq1, lk1, lq2, lk2, *, tq, lam_init):
    n_heads = rel_bias.shape[1]
    vec = lambda a: a.reshape(1, -1)
    vspec = pl.BlockSpec(memory_space=pltpu.VMEM)
    return pl.pallas_call(
        functools.partial(_diff_prep_kernel, tq=tq, lam_init=lam_init),
        in_specs=[pl.BlockSpec(memory_space=pltpu.SMEM), vspec, vspec, vspec, vspec],
        out_specs=[vspec, vspec, vspec],
        out_shape=[
            jax.ShapeDtypeStruct((n_heads, 2, tq, tq), F32),
            jax.ShapeDtypeStruct((n_heads, 1, LANES), F32),
            jax.ShapeDtypeStruct((1, LANES), F32),
        ],
        compiler_params=pltpu.CompilerParams(vmem_limit_bytes=VMEM_LIMIT), name="diff_prep",
    )(rel_bias, vec(lq1), vec(lk1), vec(lq2), vec(lk2))


def _diff_kernel(q_ref, k_ref, v_ref, bias_ref, far_ref, lam_ref, subln_ref, w_ref, o_ref, wcast_ref,
                 ka_s, *, tq, lam_init):
    wcast_ref[...] = w_ref[...].astype(BF16)
    lane = lax.broadcasted_iota(jnp.int32, (1, LANES), 1)
    halves = _halves(lane)
    k = k_ref[...]
    far = _split3(far_ref[0])
    for sub in range(2):
        ka_s[sub] = _augment(k, halves[sub], lane, HEAD_DIM * (1 - sub), far, True)

    zero = jnp.zeros((1, LANES), F32)
    for i in range(q_ref.shape[0] // tq):
        rows = slice(i * tq, (i + 1) * tq)
        outs = []
        for sub in range(2):
            qa = _augment(q_ref[rows, :], halves[sub], lane, HEAD_DIM * (1 - sub), (zero,) * 3, False)
            outs.append(_causal_attention(qa, ka_s.at[sub], v_ref, bias_ref, i, tq))
        o = outs[0] - lam_ref[...] * outs[1]
        o_ref[rows, :] = (_rms(o, subln_ref[...]) * (1.0 - lam_init)).astype(o_ref.dtype)


def diff_attention(main, bias, far, lam, subln, w_ride, *, batch, seq, tq, q_col, k_col, v_col, lam_init):
    t = batch * seq
    n_heads = bias.shape[0]
    grid = (batch, n_heads)
    ride_spec, ride_shape = _rider(w_ride, grid)
    block = lambda col: pl.BlockSpec((seq, LANES), lambda b, h: (b, col + h))
    return pl.pallas_call(
        functools.partial(_diff_kernel, tq=tq, lam_init=lam_init),
        grid=grid,
        in_specs=[
            block(q_col), block(k_col), block(v_col),
            pl.BlockSpec((1, 2, tq, tq), lambda b, h: (h, 0, 0, 0)),
            pl.BlockSpec((1, 1, LANES), lambda b, h: (h, 0, 0)),
            pl.BlockSpec((1, LANES), lambda b, h: (0, 0)),
            pl.BlockSpec((1, LANES), lambda b, h: (0, 0)),
            ride_spec,
        ],
        out_specs=[block(0), ride_spec],
        out_shape=[jax.ShapeDtypeStruct((t, n_heads * LANES), BF16), ride_shape],
        scratch_shapes=[pltpu.VMEM((2, seq, LANES), BF16)],
        compiler_params=_params("parallel", "parallel"), name="diff_attention",
    )(main, main, main, bias, far, lam, subln.reshape(1, LANES), w_ride)


def _router_kernel(x_ref, g_ref, wr_ref, meta_ref, wts_ref, before_ref, cnt_ref, carry_s, *, n_exp, sub):
    @pl.when(pl.program_id(0) == 0)
    def _():
        carry_s[...] = jnp.zeros_like(carry_s)

    lane = lax.broadcasted_iota(jnp.int32, (sub, LANES), 1)
    lane_f = lane.astype(F32)
    row = lax.broadcasted_iota(jnp.int32, (sub, sub), 0)
    col = lax.broadcasted_iota(jnp.int32, (sub, sub), 1)
    earlier = jnp.where(row > col, 1.0, 0.0).astype(BF16)
    pick = lambda sel, val: jnp.sum(jnp.where(sel, val, 0.0), axis=1, keepdims=True)
    carry = carry_s[...]
    for r in range(x_ref.shape[0] // sub):
        rows = slice(r * sub, (r + 1) * sub)
        before_ref[r * SUBLANES:(r + 1) * SUBLANES, :] = jnp.broadcast_to(carry, (SUBLANES, LANES))
        h = _rms(x_ref[rows, :], g_ref[...])
        h_hi = h.astype(BF16)
        h_lo = (h - h_hi.astype(F32)).astype(BF16)
        logits = _dot(h_hi, wr_ref[0]) + (_dot(h_hi, wr_ref[1]) + _dot(h_lo, wr_ref[0]))
        logits = jnp.where(lane < n_exp, logits, -jnp.inf)
        m1 = jnp.max(logits, axis=1, keepdims=True)
        i1 = jnp.min(jnp.where(logits == m1, lane_f, float(LANES)), axis=1, keepdims=True)
        rest = jnp.where(lane_f == i1, -jnp.inf, logits)
        m2 = jnp.max(rest, axis=1, keepdims=True)
        i2 = jnp.min(jnp.where(rest == m2, lane_f, float(LANES)), axis=1, keepdims=True)
        e = jnp.exp(m2 - m1)
        sel1 = lane_f == i1
        sel2 = lane_f == i2
        onehot = jnp.where(sel1 | sel2, 1.0, 0.0)
        local = _dot(earlier, onehot.astype(BF16))
        meta = jnp.zeros((sub, LANES), F32)
        for idx, field in enumerate([i1, i2, pick(sel1, local), pick(sel2, local)]):
            meta = jnp.where(lane == idx, field, meta)
        meta_ref[rows, :] = meta.astype(jnp.int32)
        wts_ref[rows, :] = jnp.where(lane == 0, 1.0 / (1.0 + e), jnp.where(lane == 1, e / (1.0 + e), 0.0))
        carry = carry + jnp.sum(onehot, axis=0, keepdims=True)
    carry_s[...] = carry
    cnt_ref[...] = carry


def route_tokens(x, g, wr, *, tm, sub, n_exp):
    t, d = x.shape
    per_step = tm // sub * SUBLANES
    return pl.pallas_call(
        functools.partial(_router_kernel, n_exp=n_exp, sub=sub),
        grid=(t // tm,),
        in_specs=[
            pl.BlockSpec((tm, d), lambda i: (i, 0)),
            pl.BlockSpec((1, d), lambda i: (0, 0)),
            pl.BlockSpec((2, d, LANES), lambda i: (0, 0, 0)),
        ],
        out_specs=[
            pl.BlockSpec((tm, LANES), lambda i: (i, 0)),
            pl.BlockSpec((tm, LANES), lambda i: (i, 0)),
            pl.BlockSpec((per_step, LANES), lambda i: (i, 0)),
            pl.BlockSpec((1, LANES), lambda i: (0, 0)),
        ],
        out_shape=[
            jax.ShapeDtypeStruct((t, LANES), jnp.int32),
            jax.ShapeDtypeStruct((t, LANES), F32),
            jax.ShapeDtypeStruct((t // tm * per_step, LANES), F32),
            jax.ShapeDtypeStruct((1, LANES), F32),
        ],
        scratch_shapes=[pltpu.VMEM((1, LANES), F32)],
        compiler_params=_params("arbitrary"), name="moe_router",
    )(x, g.reshape(1, d), wr)


def _dispatch_kernel(start_ref, shift_ref, keep_ref, small_ref, x_ref, g_ref, meta_ref, zeros_hbm, xs_hbm,
                     stage_s, carry_s, sems, *, tm, n_exp):
    del zeros_hbm
    j = pl.program_id(0)
    slot = j % 2
    rows = tm + BF16_ROWS

    def block_copy(step, e, buf, n):
        first = pl.multiple_of(start_ref[step * n_exp + e], BF16_ROWS)
        return pltpu.make_async_copy(stage_s.at[buf, e, pl.ds(0, n)], xs_hbm.at[pl.ds(first, n)], sems.at[buf])

    def blocks(step, buf, act):
        for e in range(n_exp):
            small = small_ref[step * n_exp + e] != 0

            @pl.when(small)
            def _(e=e):
                act(block_copy(step, e, buf, tm // 2 + BF16_ROWS), e)

            @pl.when(jnp.logical_not(small))
            def _(e=e):
                act(block_copy(step, e, buf, rows), e)

    @pl.when(j == 0)
    def _():
        carry_s[...] = jnp.zeros_like(carry_s)

    h = _rms(x_ref[...], g_ref[...]).astype(BF16)
    fields = meta_ref[...].astype(F32).T
    slot_row = lax.broadcasted_iota(jnp.int32, (rows, tm), 0).astype(F32)
    for e in range(n_exp):
        key = j * n_exp + e
        idx = jnp.where(fields[0:1] == e, fields[2:3], jnp.where(fields[1:2] == e, fields[3:4], -2.0 * rows))
        idx = idx + shift_ref[key].astype(F32)
        onehot = jnp.where(slot_row == idx, 1.0, 0.0).astype(BF16)
        stage_s[slot, e] = _dot(onehot, h).astype(BF16)
        stage_s[slot, e, 0:BF16_ROWS, :] += carry_s[e]
        keep = pl.multiple_of(keep_ref[key], BF16_ROWS)
        carry_s[e] = stage_s[slot, e, pl.ds(keep, BF16_ROWS), :]

    @pl.when(j > 0)
    def _():
        blocks(j - 1, 1 - slot, lambda copy, e: copy.wait())

    blocks(j, slot, lambda copy, e: copy.start(priority=e % 2))

    @pl.when(j == pl.num_programs(0) - 1)
    def _():
        blocks(j, slot, lambda copy, e: copy.wait())


def moe_dispatch(starts, shifts, keeps, smalls, x, g, meta, zeros, *, tm, n_exp):
    t, d = x.shape
    n_rows = zeros.shape[0]
    rows = tm + BF16_ROWS
    return pl.pallas_call(
        functools.partial(_dispatch_kernel, tm=tm, n_exp=n_exp),
        grid_spec=pltpu.PrefetchScalarGridSpec(
            num_scalar_prefetch=4, grid=(t // tm,),
            in_specs=[
                pl.BlockSpec((tm, d), lambda i, *_: (i, 0)),
                pl.BlockSpec((1, d), lambda i, *_: (0, 0)),
                pl.BlockSpec((tm, LANES), lambda i, *_: (i, 0)),
                pl.BlockSpec(memory_space=pl.ANY),
            ],
            out_specs=pl.BlockSpec(memory_space=pl.ANY),
            scratch_shapes=[pltpu.VMEM((2, n_exp, rows, d), BF16), pltpu.VMEM((n_exp, BF16_ROWS, d), BF16),
                            pltpu.SemaphoreType.DMA((2,))],
        ),
        out_shape=jax.ShapeDtypeStruct((n_rows, d), BF16),
        input_output_aliases={7: 0},
        compiler_params=_params("arbitrary"),
        name="moe_dispatch",
    )(starts, shifts, keeps, smalls, x, g.reshape(1, d), meta, zeros)


def _expert_kernel(te_ref, used_ref, h_ref, w13_ref, w2_ref, o_ref, gu_s, acc_s):
    del te_ref
    used = used_ref[pl.program_id(0)] != 0

    @pl.when(used)
    def _():
        acc_s[...] = jnp.zeros_like(acc_s)
        _swiglu_chunks(h_ref, w13_ref.at[0], w2_ref.at[0], acc_s, gu_s)
        o_ref[...] = acc_s[...].astype(o_ref.dtype)

    @pl.when(jnp.logical_not(used))
    def _():
        o_ref[...] = jnp.zeros_like(o_ref)


def moe_experts(tile_expert, tile_used, hs, w13, w2, *, tm):
    n_rows, d = hs.shape
    tf = w2.shape[2]
    resident = pl.Buffered(1)
    return pl.pallas_call(
        _expert_kernel,
        grid_spec=pltpu.PrefetchScalarGridSpec(
            num_scalar_prefetch=2, grid=(n_rows // tm,),
            in_specs=[
                pl.BlockSpec((tm, d), lambda i, te, tu: (i, 0)),
                pl.BlockSpec((1,) + w13.shape[1:], lambda i, te, tu: (te[i], 0, 0)),
                pl.BlockSpec((1,) + w2.shape[1:], lambda i, te, tu: (te[i], 0, 0, 0)),
            ],
            out_specs=pl.BlockSpec((tm, d), lambda i, te, tu: (i, 0)),
            scratch_shapes=[pltpu.VMEM((2, 2, tm, tf), F32), pltpu.VMEM((tm, d), F32)],
        ),
        out_shape=jax.ShapeDtypeStruct((n_rows, d), BF16),
        compiler_params=pltpu.CompilerParams(dimension_semantics=("arbitrary",), vmem_limit_bytes=EXPERT_VMEM_LIMIT),
        name="moe_experts",
    )(tile_expert, tile_used, hs, w13, w2)


def _combine_kernel(start_ref, shift_ref, tail_ref, small_ref, x_ref, wts_ref, meta_ref, g_ref, y_hbm, o_ref,
                    blk_s, sems, *, tm, n_exp):
    j = pl.program_id(0)
    slot = j % 2
    rows = tm + BF16_ROWS

    def block_copy(step, e, buf, n):
        first = pl.multiple_of(start_ref[step * n_exp + e], BF16_ROWS)
        return pltpu.make_async_copy(y_hbm.at[pl.ds(first, n)], blk_s.at[buf, e, pl.ds(0, n)], sems.at[buf])

    def blocks(step, buf, act):
        for e in range(n_exp):
            small = small_ref[step * n_exp + e] != 0

            @pl.when(small)
            def _(e=e):
                act(block_copy(step, e, buf, tm // 2), e)

            @pl.when(jnp.logical_not(small))
            def _(e=e):
                act(block_copy(step, e, buf, rows), e)

    @pl.when(j == 0)
    def _():
        blk_s[...] = jnp.zeros_like(blk_s)
        blocks(0, 0, lambda copy, e: copy.start(priority=e % 2))

    @pl.when(j + 1 < pl.num_programs(0))
    def _():
        blocks(j + 1, 1 - slot, lambda copy, e: copy.start(priority=e % 2))

    blocks(j, slot, lambda copy, e: copy.wait())

    meta = meta_ref[...]
    wts = wts_ref[...]
    col = lax.broadcasted_iota(jnp.int32, (tm, tm), 1)
    tail_col = lax.broadcasted_iota(jnp.int32, (tm, BF16_ROWS), 1) + tm

    def pick(e):
        sel = [meta[:, k:k + 1] == e for k in range(TOP_K)]
        idx = jnp.where(sel[0], meta[:, 2:3], jnp.where(sel[1], meta[:, 3:4], -2 * rows)) + shift_ref[j * n_exp + e]
        w = jnp.where(sel[0], wts[:, 0:1], jnp.where(sel[1], wts[:, 1:2], 0.0))
        return idx, w

    acc = x_ref[...]
    for e in range(n_exp):
        idx, w = pick(e)
        onehot = jnp.where(col == idx, 1.0, 0.0).astype(BF16)
        acc = acc + w * _dot(onehot, blk_s[slot, e, 0:tm, :])
    o_ref[...] = acc

    for e in range(n_exp):
        @pl.when(tail_ref[j * n_exp + e] != 0)
        def _(e=e):
            idx, w = pick(e)
            onehot_tail = jnp.where(tail_col == idx, 1.0, 0.0).astype(BF16)
            o_ref[...] += w * _dot(onehot_tail, blk_s[slot, e, tm:rows, :])

    o_ref[...] = _rms(o_ref[...], g_ref[...])


def moe_combine(starts, shifts, tails, smalls, x, wts, meta, g, y, *, tm, n_exp):
    t, d = x.shape
    n_pre = 4
    return pl.pallas_call(
        functools.partial(_combine_kernel, tm=tm, n_exp=n_exp),
        grid_spec=pltpu.PrefetchScalarGridSpec(
            num_scalar_prefetch=n_pre, grid=(t // tm,),
            in_specs=[
                pl.BlockSpec((tm, d), lambda i, *_: (i, 0)),
                pl.BlockSpec((tm, LANES), lambda i, *_: (i, 0)),
                pl.BlockSpec((tm, LANES), lambda i, *_: (i, 0)),
                pl.BlockSpec((1, d), lambda i, *_: (0, 0)),
                pl.BlockSpec(memory_space=pl.ANY),
            ],
            out_specs=pl.BlockSpec((tm, d), lambda i, *_: (i, 0)),
            scratch_shapes=[pltpu.VMEM((2, n_exp, tm + BF16_ROWS, d), BF16), pltpu.SemaphoreType.DMA((2,))],
        ),
        out_shape=jax.ShapeDtypeStruct((t, d), F32),
        compiler_params=_params("arbitrary"), name="moe_combine",
    )(starts, shifts, tails, smalls, x, wts, meta, g.reshape(1, d), y)


def _pad_cols(w, n):
    return jnp.pad(w, ((0, 0), (0, n - w.shape[1])))


def _even_mixer(x, norm, w_in, b_f, sgu_norm, w_s, b_s, w_ride, proj_rides, *, batch, seq, tq):
    n_heads = b_f.shape[0]
    a_width = n_heads * HEAD_DIM
    b_width = w_s.shape[0] * LANES
    f0 = 3 * a_width
    w_main = jnp.concatenate([w_in[:, :f0], w_in[:, f0 + n_heads:]], axis=1)
    w_gate = _pad_cols(w_in[:, f0:f0 + n_heads], LANES)
    main, gate, *proj_casts = norm_matmul(x, norm, [w_main.astype(BF16), w_gate.astype(BF16)], [BF16, F32],
                                          tm=min(IN_PROJ_ROWS, x.shape[0]), name="even_in_proj",
                                          scaled=(0, a_width, LOG2E * HEAD_DIM ** -0.5), rides=proj_rides)
    c = gate_cumsum(gate, _pad_cols(b_f.reshape(1, -1), LANES), seq=seq)
    n_pairs = a_width // LANES
    a, w_cast = fox_attention(main, c, w_ride, batch=batch, seq=seq, tq=tq, q_col=0, k_col=n_pairs,
                              v_col=2 * n_pairs, n_pairs=n_pairs)
    u_col = f0 // b_width
    return (a, (main, u_col, u_col + 1, sgu_norm, w_s, b_s)), w_cast, proj_casts


def _odd_mixer(x, norm, w_in, conv_w, lq1, lk1, lq2, lk2, subln, rel_bias, lam_init, w_ride, *, batch, seq, tq):
    c_width = conv_w.shape[1]
    d_width = rel_bias.shape[1] * 2 * HEAD_DIM
    q0 = 3 * c_width
    w_in = w_in.astype(BF16)
    c_out, main = conv_proj(x, norm, w_in[:, :q0], w_in[:, q0:], conv_w, tm=min(IN_PROJ_ROWS, seq), seq=seq,
                            scaled=(0, d_width, LOG2E * HEAD_DIM ** -0.5))
    bias, far, lam = diff_prep(rel_bias, lq1, lk1, lq2, lk2, tq=tq, lam_init=lam_init)
    n_heads = rel_bias.shape[1]
    d_out, w_cast = diff_attention(main, bias, far, lam, subln, w_ride, batch=batch, seq=seq, tq=tq, q_col=0,
                                   k_col=n_heads, v_col=2 * n_heads, lam_init=lam_init)
    return (c_out, d_out), w_cast


def _sorted_rows(t, n_exp):
    block_rows = ROUTE_ROWS + BF16_ROWS
    return TOP_K * t + n_exp * (EXPERT_ROWS + pl.cdiv(block_rows, EXPERT_ROWS) * EXPERT_ROWS)


def _moe_layer(x, norm, w_router, w13, w2, final_norm, zeros):
    t, d = x.shape
    n_exp = w_router.shape[1]
    tm_route = ROUTE_ROWS
    tm_expert = EXPERT_ROWS
    n_rows = zeros.shape[0]
    assert n_rows == _sorted_rows(t, n_exp)
    wr = _pad_cols(w_router, LANES)
    wr_hi = wr.astype(BF16)
    wr_split = jnp.stack([wr_hi, (wr - wr_hi.astype(F32)).astype(BF16)])
    meta, wts, before, counts = route_tokens(x, norm, wr_split, tm=ROUTE_STEP_ROWS, sub=tm_route, n_exp=n_exp)
    counts = counts[0, :n_exp].astype(jnp.int32)
    block_rows = tm_route + BF16_ROWS
    padded = (counts + block_rows + tm_expert - 1) // tm_expert * tm_expert
    ends = jnp.cumsum(padded)
    offsets = ends - padded
    tile_start = jnp.arange(n_rows // tm_expert, dtype=jnp.int32) * tm_expert
    tile_expert = jnp.minimum(jnp.sum(tile_start[:, None] >= ends[None, :], axis=1), n_exp - 1).astype(jnp.int32)
    tile_used = (tile_start < (offsets + counts)[tile_expert]).astype(jnp.int32)
    before = before[::SUBLANES, :n_exp].astype(jnp.int32)
    in_tile = jnp.concatenate([before[1:], counts[None]]) - before
    first = offsets[None, :] + before
    starts = first // BF16_ROWS * BF16_ROWS
    shifts = first - starts
    keeps = (shifts + in_tile) // BF16_ROWS * BF16_ROWS
    tails = (shifts + in_tile > tm_route).astype(jnp.int32)
    flat = lambda a: a.reshape(-1).astype(jnp.int32)
    smalls = (shifts + in_tile <= tm_route // 2).astype(jnp.int32)
    hs = moe_dispatch(flat(starts), flat(shifts), flat(keeps), flat(smalls), x, norm, meta, zeros, tm=tm_route,
                      n_exp=n_exp)
    y = moe_experts(tile_expert, tile_used, hs, *_chunk_weights(w13, w2, EXPERT_CHUNK), tm=tm_expert)
    return moe_combine(flat(starts), flat(shifts), flat(tails), flat(smalls), x, wts, meta, final_norm, y,
                       tm=tm_route, n_exp=n_exp)


def kernel(x, mem, rel_bias, mem_norm, final_norm, ev_norm, ev_w_in, ev_b_f, ev_sgu_norm, ev_w_s, ev_b_s, ev_w_out, ffn_w13, ffn_w2, od_norm, od_w_in, od_conv_w, od_lam_q1, od_lam_k1, od_lam_q2, od_lam_k2, od_subln, od_w_out, moe_router, moe_w13, moe_w2, x_norm, x_wq, x_wkv, x_wo, ffn_norm):
    batch, seq, d = x.shape
    mem_len = mem.shape[1]
    depth = x_norm.shape[0]
    assert depth == 2 and ev_norm.shape[0] == 1 and od_norm.shape[0] == 1
    x_heads, x_dh = 4, 128
    xf = x.reshape(batch * seq, d)
    wkv = jnp.concatenate([x_wkv[layer] for layer in range(depth)], axis=1).astype(BF16)
    (kv,) = norm_matmul(mem.reshape(batch * mem_len, d), mem_norm, [wkv], [BF16], tm=MEM_ROWS, name="mem_kv")

    def tail(xf, mixed, w_out, layer, ffn, name, zero_rows=0):
        return layer_tail(xf, *mixed, w_out.astype(BF16), x_norm[layer], x_wq[layer].astype(BF16), kv,
                          x_wo[layer].astype(BF16), ffn, zero_rows, tm=TAIL_ROWS, seq=seq, mem_len=mem_len, kv_col=layer,
                          n_heads=x_heads, dh=x_dh, name=name)

    e13, e2 = moe_w13[0], moe_w2[0]
    mixed, e13_bf16, ffn_bf16 = _even_mixer(xf, ev_norm[0], ev_w_in[0], ev_b_f[0], ev_sgu_norm[0], ev_w_s[0],
                                            ev_b_s[0], e13.reshape(-1, e13.shape[-1]), (ffn_w13[0], ffn_w2[0]),
                                            batch=batch, seq=seq, tq=ATTN_TILE)
    xf = tail(xf, mixed, ev_w_out[0], 0, (ffn_norm[0],) + _chunk_weights(*ffn_bf16, FFN_CHUNK), "even_tail")
    lam_init = 0.8 - 0.6 * math.exp(-0.3 * 1)
    mixed, e2_bf16 = _odd_mixer(xf, od_norm[0], od_w_in[0], od_conv_w[0], od_lam_q1[0], od_lam_k1[0],
                                od_lam_q2[0], od_lam_k2[0], od_subln[0], rel_bias, lam_init,
                                e2.reshape(-1, e2.shape[-1]), batch=batch, seq=seq, tq=ATTN_TILE)
    xf, zeros = tail(xf, mixed, od_w_out[0], 1, None, "odd_tail", _sorted_rows(xf.shape[0], moe_router.shape[-1]))
    out = _moe_layer(xf, ffn_norm[1], moe_router[0], e13_bf16.reshape(e13.shape), e2_bf16.reshape(e2.shape),
                     final_norm, zeros)
    return out.reshape(batch, seq, d)
```
